```python
import math
import jax
import jax.numpy as jnp
from jax import lax
import numpy as np

D_MODEL = 1024
BATCH = 4
SEQ = 8192
DEPTH = 2

GRID_W = 64
CTX_LEN = 256

ATT_HEAD_DIM = 64
ATT_HEADS = D_MODEL // (2 * ATT_HEAD_DIM)
D_ATT = ATT_HEADS * ATT_HEAD_DIM
WIN_ROWS_MAX = 8
WIN_COLS = 16
D_HYENA = D_MODEL // 4
HYENA_ORDER = 2
FILTER_EMB = 33
FILTER_BANDS = (FILTER_EMB - 1) // 2
FILTER_WIDTH = 64
DECAY_TARGET = 1e-2
FAST_DECAY_PCT = 0.3
SLOW_DECAY_PCT = 1.5
MIN_DECAY = math.log(DECAY_TARGET) / SLOW_DECAY_PCT
MAX_DECAY = math.log(DECAY_TARGET) / FAST_DECAY_PCT
SHORT_CONV = 3
D_SSM = D_MODEL // 4
SSM_GROUP_WIDTH = 16
SSM_GROUPS = D_SSM // SSM_GROUP_WIDTH
SSM_STATE = 64
D_MIX = D_ATT + D_HYENA + D_SSM
COL_K = D_ATT
COL_V = 2 * D_ATT
COL_HY = 3 * D_ATT
COL_SSM = COL_HY + (HYENA_ORDER + 1) * D_HYENA
D_IN = COL_SSM + D_SSM
D_FF = 2816
N_EXPERTS = 8
TOP_K = 2
D_FF_EXPERT = 2816
N_DENSE = (DEPTH + 1) // 2
N_MOE = DEPTH // 2
EPS = 1e-6

kernel_name = 'hybrid_dit_natten_hyena_s5_moe'


def rms_norm(x, g):
    xf = x.astype(jnp.float32)
    y = xf * lax.rsqrt(jnp.mean(xf * xf, axis=-1, keepdims=True) + EPS)
    return (y * g.astype(jnp.float32)).astype(x.dtype)


def modulate(h, shift, scale):
    return h * (1.0 + scale) + shift


def short_conv(u, w, b):
    L = u.shape[1]
    pad = SHORT_CONV // 2
    up = jnp.pad(u, ((0, 0), (pad, pad), (0, 0)))
    y = b
    for j in range(SHORT_CONV):
        y = y + up[:, j:j + L] * w[j]
    return y


def neighbourhood_attention(q, k, v, k_ctx, v_ctx, rpb):
    B, L, H, dh = q.shape
    rows = L // GRID_W
    kr = min(WIN_ROWS_MAX, rows)
    kc = WIN_COLS
    n_loc = kr * kc
    scale = dh ** -0.5
    kg = k.reshape(B, rows, GRID_W, H, dh)
    vg = v.reshape(B, rows, GRID_W, H, dh)
    q_rows = q.reshape(B, rows, GRID_W, H, dh).transpose(1, 0, 2, 3, 4)
    row_ids = jnp.arange(rows)
    row_start = jnp.clip(row_ids - kr // 2, 0, rows - kr)
    col_ids = jnp.arange(GRID_W)
    col_start = jnp.clip(col_ids - kc // 2, 0, GRID_W - kc)
    col_idx = col_start[:, None] + jnp.arange(kc)[None, :]
    col_bias = rpb[:, :, col_idx - col_ids[:, None] + WIN_COLS - 1]

    def one_row(args):
        q_r, r0, r = args
        k_win = lax.dynamic_slice_in_dim(kg, r0, kr, axis=1)[:, :, col_idx]
        v_win = lax.dynamic_slice_in_dim(vg, r0, kr, axis=1)[:, :, col_idx]
        row_off = r0 + jnp.arange(kr) - r + WIN_ROWS_MAX - 1
        bias = col_bias[:, row_off].transpose(0, 2, 1, 3)
        s_loc = (jnp.einsum('bwhd,brwkhd->bhwrk', q_r, k_win).astype(jnp.float32) * scale
                 + bias.astype(jnp.float32))
        s_ctx = jnp.einsum('bwhd,bchd->bhwc', q_r, k_ctx).astype(jnp.float32) * scale
        s = jnp.concatenate([s_loc.reshape(B, H, GRID_W, n_loc), s_ctx], axis=-1)
        p = jax.nn.softmax(s, axis=-1).astype(v_win.dtype)
        p_loc = p[..., :n_loc].reshape(B, H, GRID_W, kr, kc)
        return (jnp.einsum('bhwrk,brwkhd->bwhd', p_loc, v_win)
                + jnp.einsum('bhwc,bchd->bwhd', p[..., n_loc:], v_ctx))

    out = lax.map(one_row, (q_rows, row_start, row_ids))
    return out.transpose(1, 0, 2, 3, 4).reshape(B, L, H * dh)


def context_attention(q, k, v):
    B, Lc, H, dh = q.shape
    s = jnp.einsum('bqhd,bkhd->bhqk', q, k).astype(jnp.float32) * dh ** -0.5
    p = jax.nn.softmax(s, axis=-1).astype(v.dtype)
    return jnp.einsum('bhqk,bkhd->bqhd', p, v).reshape(B, Lc, H * dh)


def hyena_filters(L, w1, b1, w2, b2, w3, b3, freq, w_out):
    f32 = jnp.float32
    t = jnp.linspace(0.0, 1.0, L, dtype=f32)[:, None]
    w = (2.0 * math.pi / L) * jnp.arange(L, dtype=f32)[:, None]
    f = jnp.linspace(1e-4, FILTER_BANDS - 1, FILTER_BANDS, dtype=f32)[None, :]
    z = jnp.concatenate([t, jnp.cos(f * w), -jnp.sin(f * w)], axis=-1)
    fr = freq.astype(f32)
    h = jnp.sin(fr * (z @ w1.astype(f32) + b1.astype(f32)))
    h = jnp.sin(fr * (h @ w2.astype(f32) + b2.astype(f32)))
    h = jnp.sin(fr * (h @ w3.astype(f32) + b3.astype(f32)))
    h = (h @ w_out.astype(f32)).reshape(L, HYENA_ORDER, 2, D_HYENA)
    deltas = jnp.abs(jnp.linspace(MIN_DECAY, MAX_DECAY, D_HYENA, dtype=f32))
    decay = jnp.exp(-t * deltas[None, :])
    h = h * decay[:, None, None, :]
    return h.transpose(1, 2, 0, 3)


def bidir_long_conv(u, k_fwd, k_rev, bias):
    L = u.shape[1]
    k = jnp.concatenate([k_fwd, k_rev[::-1]], axis=0)
    k_f = jnp.fft.rfft(k, n=2 * L, axis=0)
    u_f = jnp.fft.rfft(u, n=2 * L, axis=1)
    y = jnp.fft.irfft(u_f * k_f[None], n=2 * L, axis=1)[:, :L]
    return y + u * bias


def hyena_mixer(z, conv_w, conv_b, filters, bias):
    zc = short_conv(z, conv_w, conv_b).astype(jnp.float32)
    parts = jnp.split(zc, HYENA_ORDER + 1, axis=-1)
    y = parts[0]
    for n in range(HYENA_ORDER):
        y = parts[n + 1] * bidir_long_conv(y, filters[n, 0], filters[n, 1], bias[n].astype(jnp.float32))
    return y.astype(z.dtype)


def _scan_combine(e_i, e_j):
    a_i, b_i = e_i
    a_j, b_j = e_j
    return a_j * a_i, a_j * b_i + b_j


def s5_states(u, lam_re, lam_im, log_step, b_re, b_im, reverse, h0):
    f32 = jnp.float32
    B, L, _ = u.shape
    lam = lax.complex(lam_re.astype(f32), lam_im.astype(f32))
    lam_dt = lam * jnp.exp(log_step.astype(f32))[:, None]
    lam_bar = jnp.exp(lam_dt)
    b_bar = ((lam_bar - 1.0) / lam)[..., None] * lax.complex(b_re.astype(f32), b_im.astype(f32))
    u_g = u.astype(f32).reshape(B, L, SSM_GROUPS, SSM_GROUP_WIDTH).astype(jnp.complex64)
    bu = jnp.einsum('blgh,gph->lbgp', u_g, b_bar)
    a = jnp.broadcast_to(lam_bar, (L, 1) + lam_bar.shape)
    _, xs = lax.associative_scan(_scan_combine, (a, bu), reverse=reverse, axis=0)
    if h0 is not None:
        steps = jnp.arange(1, L + 1, dtype=f32)
        if reverse:
            steps = steps[::-1]
        xs = xs + jnp.exp(steps[:, None, None] * lam_dt)[:, None] * h0[None]
    return xs


def s5_readout(u, xs_f, xs_b, c_re, c_im, d_skip, w_glu, b_glu):
    f32 = jnp.float32
    B, L, _ = u.shape
    c_f = lax.complex(c_re[0].astype(f32), c_im[0].astype(f32))
    c_b = lax.complex(c_re[1].astype(f32), c_im[1].astype(f32))
    y = jnp.real(jnp.einsum('lbgp,ghp->blgh', xs_f, c_f) + jnp.einsum('lbgp,ghp->blgh', xs_b, c_b))
    y = y.reshape(B, L, D_SSM) + u.astype(f32) * d_skip.astype(f32)
    y = jax.nn.gelu(y)
    y = y * jax.nn.sigmoid(y @ w_glu.astype(f32) + b_glu.astype(f32))
    return y.astype(u.dtype)


def swiglu(h, w1, w3, w2):
    return (jax.nn.silu(h @ w1) * (h @ w3)) @ w2


def moe_swiglu(h, router_w, w1, w3, w2):
    logits = (h @ router_w).astype(jnp.float32)
    top_v, top_i = lax.top_k(logits, TOP_K)
    top_w = jax.nn.softmax(top_v, axis=-1)
    gates = jnp.sum(jax.nn.one_hot(top_i, N_EXPERTS, dtype=jnp.float32) * top_w[..., None], axis=-2)
    out = jnp.zeros_like(h)
    for e in range(N_EXPERTS):
        out = out + gates[..., e:e + 1].astype(h.dtype) * swiglu(h, w1[e], w3[e], w2[e])
    return out


def channel_mixer(layer, h, ffn_w1, ffn_w3, ffn_w2, router_w, moe_w1, moe_w3, moe_w2):
    i = layer // 2
    if layer % 2 == 0:
        return swiglu(h, ffn_w1[i], ffn_w3[i], ffn_w2[i])
    return moe_swiglu(h, router_w[i], moe_w1[i], moe_w3[i], moe_w2[i])


def setup_inputs(seed: int = 0) -> dict:
    key = jax.random.key(seed)
    keys = jax.random.split(key, 48)
    counter = [0]

    def nxt():
        kk = keys[counter[0]]
        counter[0] += 1
        return kk

    def nrm(shape, std):
        return std * jax.random.normal(nxt(), shape, jnp.float32)

    G, P, Hs = SSM_GROUPS, SSM_STATE, SSM_GROUP_WIDTH
    return {
        'x': nrm((BATCH, SEQ, D_MODEL), 1.0),
        'c': nrm((BATCH, D_MODEL), 1.0),
        'ctx': nrm((BATCH, CTX_LEN, D_MODEL), 1.0),
        'c_ctx': nrm((D_MODEL,), 1.0),
        'w_ada': nrm((DEPTH, D_MODEL, 6 * D_MODEL), 0.5 * D_MODEL ** -0.5),
        'b_ada': nrm((DEPTH, 6 * D_MODEL), 0.02),
        'g_norm1': 1.0 + nrm((DEPTH, D_MODEL), 0.02),
        'g_norm2': 1.0 + nrm((DEPTH, D_MODEL), 0.02),
        'w_in': nrm((DEPTH, D_MODEL, D_IN), D_MODEL ** -0.5),
        'w_out': nrm((DEPTH, D_MIX, D_MODEL), D_MIX ** -0.5),
        'g_q': 1.0 + nrm((DEPTH, ATT_HEAD_DIM), 0.02),
        'g_k': 1.0 + nrm((DEPTH, ATT_HEAD_DIM), 0.02),
        'rpb': nrm((DEPTH, ATT_HEADS, 2 * WIN_ROWS_MAX - 1, 2 * WIN_COLS - 1), 0.02),
        'hy_conv_w': nrm((DEPTH, SHORT_CONV, (HYENA_ORDER + 1) * D_HYENA), SHORT_CONV ** -0.5),
        'hy_conv_b': nrm((DEPTH, (HYENA_ORDER + 1) * D_HYENA), 0.02),
        'filt_w1': nrm((DEPTH, FILTER_EMB, FILTER_WIDTH), FILTER_EMB ** -0.5),
        'filt_b1': nrm((DEPTH, FILTER_WIDTH), 0.1),
        'filt_w2': nrm((DEPTH, FILTER_WIDTH, FILTER_WIDTH), FILTER_WIDTH ** -0.5),
        'filt_b2': nrm((DEPTH, FILTER_WIDTH), 0.1),
        'filt_w3': nrm((DEPTH, FILTER_WIDTH, FILTER_WIDTH), FILTER_WIDTH ** -0.5),
        'filt_b3': nrm((DEPTH, FILTER_WIDTH), 0.1),
        'filt_freq': 1.0 + nrm((DEPTH, FILTER_WIDTH), 0.02),
        'filt_w_out': nrm((DEPTH, FILTER_WIDTH, HYENA_ORDER * 2 * D_HYENA), 0.03 * FILTER_WIDTH ** -0.5),
        'hy_bias': nrm((DEPTH, HYENA_ORDER, D_HYENA), 0.5),
        'lam_re': -0.5 + nrm((DEPTH, 2, G, P), 0.01),
        'lam_im': math.pi * jnp.arange(P, dtype=jnp.float32) + nrm((DEPTH, 2, G, P), 0.01),
        'log_step': jax.random.uniform(nxt(), (DEPTH, 2, G), jnp.float32, math.log(1e-3), math.log(1e-1)),
        'b_re': nrm((DEPTH, 2, G, P, Hs), (2 * Hs) ** -0.5),
        'b_im': nrm((DEPTH, 2, G, P, Hs), (2 * Hs) ** -0.5),
        'c_re': nrm((DEPTH, 2, G, Hs, P), (2 * P) ** -0.5),
        'c_im': nrm((DEPTH, 2, G, Hs, P), (2 * P) ** -0.5),
        'd_skip': nrm((DEPTH, D_SSM), 1.0),
        'w_glu': nrm((DEPTH, D_SSM, D_SSM), D_SSM ** -0.5),
        'b_glu': nrm((DEPTH, D_SSM), 0.02),
        'ffn_w1': nrm((N_DENSE, D_MODEL, D_FF), D_MODEL ** -0.5),
        'ffn_w3': nrm((N_DENSE, D_MODEL, D_FF), D_MODEL ** -0.5),
        'ffn_w2': nrm((N_DENSE, D_FF, D_MODEL), D_FF ** -0.5),
        'router_w': nrm((N_MOE, D_MODEL, N_EXPERTS), D_MODEL ** -0.5),
        'moe_w1': nrm((N_MOE, N_EXPERTS, D_MODEL, D_FF_EXPERT), D_MODEL ** -0.5),
        'moe_w3': nrm((N_MOE, N_EXPERTS, D_MODEL, D_FF_EXPERT), D_MODEL ** -0.5),
        'moe_w2': nrm((N_MOE, N_EXPERTS, D_FF_EXPERT, D_MODEL), D_FF_EXPERT ** -0.5),
    }


def reference(x, c, ctx, c_ctx, w_ada, b_ada, g_norm1, g_norm2, w_in, w_out, g_q, g_k, rpb,
              hy_conv_w, hy_conv_b, filt_w1, filt_b1, filt_w2, filt_b2, filt_w3, filt_b3, filt_freq,
              filt_w_out, hy_bias, lam_re, lam_im, log_step, b_re, b_im, c_re, c_im, d_skip, w_glu, b_glu,
              ffn_w1, ffn_w3, ffn_w2, router_w, moe_w1, moe_w3, moe_w2):
    B, L, _ = x.shape
    Lc = ctx.shape[1]
    H, dh = ATT_HEADS, ATT_HEAD_DIM
    for layer in range(DEPTH):
        last = layer == DEPTH - 1
        mod_x = jax.nn.silu(c) @ w_ada[layer] + b_ada[layer]
        mod_c = jax.nn.silu(c_ctx) @ w_ada[layer] + b_ada[layer]
        sh1, sc1, gt1, sh2, sc2, gt2 = jnp.split(mod_x[:, None, :], 6, axis=-1)
        csh1, csc1, cgt1, csh2, csc2, cgt2 = jnp.split(mod_c, 6, axis=-1)
        w_l = w_in[layer]
        filt_p = (filt_w1[layer], filt_b1[layer], filt_w2[layer], filt_b2[layer],
                  filt_w3[layer], filt_b3[layer], filt_freq[layer], filt_w_out[layer])
        ssm_f = (lam_re[layer, 0], lam_im[layer, 0], log_step[layer, 0], b_re[layer, 0], b_im[layer, 0])
        ssm_b = (lam_re[layer, 1], lam_im[layer, 1], log_step[layer, 1], b_re[layer, 1], b_im[layer, 1])

        hc = modulate(rms_norm(ctx, g_norm1[layer]), csh1, csc1)
        k_c, v_c = jnp.split(hc @ w_l[:, COL_K:COL_HY], 2, axis=-1)
        k_c = rms_norm(k_c.reshape(B, Lc, H, dh), g_k[layer])
        v_c = v_c.reshape(B, Lc, H, dh)
        u_c = hc @ w_l[:, COL_SSM:]
        xs_cf = s5_states(u_c, *ssm_f, reverse=False, h0=None)
        xs_cb = s5_states(u_c, *ssm_b, reverse=True, h0=None)

        hx = modulate(rms_norm(x, g_norm1[layer]), sh1, sc1)
        zx = hx @ w_l
        q = rms_norm(zx[..., :COL_K].reshape(B, L, H, dh), g_q[layer])
        k = rms_norm(zx[..., COL_K:COL_V].reshape(B, L, H, dh), g_k[layer])
        v = zx[..., COL_V:COL_HY].reshape(B, L, H, dh)
        att = neighbourhood_attention(q, k, v, k_c, v_c, rpb[layer])
        hy = hyena_mixer(zx[..., COL_HY:COL_SSM], hy_conv_w[layer], hy_conv_b[layer],
                         hyena_filters(L, *filt_p), hy_bias[layer])
        u_x = zx[..., COL_SSM:]
        xs_f = s5_states(u_x, *ssm_f, reverse=False, h0=xs_cf[-1])
        xs_b = s5_states(u_x, *ssm_b, reverse=True, h0=xs_cb[0])
        ss = s5_readout(u_x, xs_f, xs_b, c_re[layer], c_im[layer], d_skip[layer], w_glu[layer], b_glu[layer])
        mix = jnp.concatenate([att.astype(x.dtype), hy.astype(x.dtype), ss.astype(x.dtype)], axis=-1) @ w_out[layer]
        x = x + gt1 * mix
        h2 = modulate(rms_norm(x, g_norm2[layer]), sh2, sc2)
        x = x + gt2 * channel_mixer(layer, h2, ffn_w1, ffn_w3, ffn_w2, router_w, moe_w1, moe_w3, moe_w2)

        if not last:
            q_c = rms_norm((hc @ w_l[:, :COL_K]).reshape(B, Lc, H, dh), g_q[layer])
            att_c = context_attention(q_c, k_c, v_c)
            hy_c = hyena_mixer(hc @ w_l[:, COL_HY:COL_SSM], hy_conv_w[layer], hy_conv_b[layer],
                               hyena_filters(Lc, *filt_p), hy_bias[layer])
            ss_c = s5_readout(u_c, xs_cf, xs_cb, c_re[layer], c_im[layer], d_skip[layer], w_glu[layer], b_glu[layer])
            mix_c = jnp.concatenate([att_c.astype(ctx.dtype), hy_c.astype(ctx.dtype), ss_c.astype(ctx.dtype)], axis=-1) @ w_out[layer]
            ctx = ctx + cgt1 * mix_c
            hc2 = modulate(rms_norm(ctx, g_norm2[layer]), csh2, csc2)
            ctx = ctx + cgt2 * channel_mixer(layer, hc2, ffn_w1, ffn_w3, ffn_w2, router_w, moe_w1, moe_w3, moe_w2)
    return x
```

```python
import functools
import math

import numpy as np
import jax
import jax.numpy as jnp
from jax import lax
from jax.experimental import pallas as pl
from jax.experimental.pallas import tpu as pltpu

F32 = jnp.float32
BF16 = jnp.bfloat16
HIGHEST = lax.Precision.HIGHEST

D_MODEL = 1024
DEPTH = 2
GRID_W = 64
ATT_HEAD_DIM = 64
ATT_HEADS = 8
D_ATT = 512
WIN_ROWS = 8
WIN_COLS = 16
D_HYENA = 256
HYENA_ORDER = 2
FILTER_EMB = 33
FILTER_BANDS = 16
FILTER_WIDTH = 64
MIN_DECAY = math.log(1e-2) / 1.5
MAX_DECAY = math.log(1e-2) / 0.3
D_SSM = 256
SSM_GROUP_WIDTH = 16
SSM_GROUPS = 16
SSM_STATE = 64
COL_K = D_ATT
COL_V = 2 * D_ATT
COL_HY = 3 * D_ATT
COL_SSM = COL_HY + 3 * D_HYENA
D_IN = COL_SSM + D_SSM
D_FF = 2816
N_EXPERTS = 8
EPS = 1e-6
NEG = -1e30

VMEM_LIMIT = 56 * 1024 * 1024


def _cparams(sem, vmem=None):
    return pltpu.CompilerParams(dimension_semantics=sem, vmem_limit_bytes=vmem)


def _dot(a, b):
    return jnp.dot(a, b, preferred_element_type=F32)


def _dot_nt(a, b):
    return lax.dot_general(a, b, (((1,), (1,)), ((), ())), preferred_element_type=F32)


def _mod_kernel(c_ref, w_ref, b_ref, o_ref):
    c = c_ref[...]
    s = c * jax.nn.sigmoid(c)
    o_ref[...] = jnp.dot(s, w_ref[...], precision=HIGHEST, preferred_element_type=F32) + b_ref[...]


def adaln_mod(c8, w_ada, b_ada, tn=512):
    depth, d, n = w_ada.shape
    return pl.pallas_call(
        _mod_kernel,
        grid=(depth, n // tn),
        in_specs=[pl.BlockSpec((8, d), lambda l, j: (0, 0)),
                  pl.BlockSpec((None, d, tn), lambda l, j: (l, 0, j)),
                  pl.BlockSpec((None, 1, tn), lambda l, j: (l, 0, j))],
        out_specs=pl.BlockSpec((None, 8, tn), lambda l, j: (l, 0, j)),
        out_shape=jax.ShapeDtypeStruct((depth, 8, n), F32),
        compiler_params=_cparams(("parallel", "parallel")),
        name="adaln_mod",
    )(c8, w_ada, b_ada.reshape(depth, 1, n))


def _rms_mod(x, g, shift, scale):
    ms = jnp.mean(x * x, axis=-1, keepdims=True)
    h = x * lax.rsqrt(ms + EPS) * g
    return h * (1.0 + scale) + shift


def _head_norm(z, a, g):
    zz = z * z
    hi = zz.astype(BF16)
    lo = (zz - hi.astype(F32)).astype(BF16)
    m = _dot(hi, a) + _dot(lo, a)
    return z * lax.rsqrt(m + EPS) * g


def _inproj_kernel(x_ref, mod_ref, g_ref, w_ref, a_ref, gq_ref, gk_ref,
                   q_ref, k_ref, v_ref, hz_ref, u_ref):
    h = _rms_mod(x_ref[...], g_ref[...], mod_ref[0:1, :], mod_ref[1:2, :]).astype(BF16)
    a = a_ref[...]
    q = _dot(h, w_ref[:, 0:COL_K])
    q_ref[...] = (_head_norm(q, a, gq_ref[...]) * (ATT_HEAD_DIM ** -0.5)).astype(BF16)
    k = _dot(h, w_ref[:, COL_K:COL_V])
    k_ref[...] = _head_norm(k, a, gk_ref[...]).astype(BF16)
    v_ref[...] = _dot(h, w_ref[:, COL_V:COL_HY]).astype(BF16)
    hz_ref[...] = _dot(h, w_ref[:, COL_HY:COL_SSM])
    u_ref[...] = _dot(h, w_ref[:, COL_SSM:D_IN])


def in_proj(x, mod, g1, w_in_bf, gq, gk, tm=256):
    B, T, D = x.shape
    head_avg = jnp.asarray(np.kron(np.eye(ATT_HEADS), np.full((ATT_HEAD_DIM, ATT_HEAD_DIM), 1.0 / ATT_HEAD_DIM)), BF16)
    tok = lambda n: pl.BlockSpec((None, tm, n), lambda b, i: (b, i, 0))
    const = lambda shape: pl.BlockSpec(shape, lambda b, i: (0,) * len(shape))
    return pl.pallas_call(
        _inproj_kernel,
        grid=(B, T // tm),
        in_specs=[tok(D), pl.BlockSpec((None, 6, D), lambda b, i: (b, 0, 0)), const((1, D)),
                  const((D, D_IN)), const((D_ATT, D_ATT)), const((1, D_ATT)), const((1, D_ATT))],
        out_specs=[tok(D_ATT), tok(D_ATT), tok(D_ATT), tok(3 * D_HYENA), tok(D_SSM)],
        out_shape=[jax.ShapeDtypeStruct((B, T, D_ATT), BF16)] * 3
                  + [jax.ShapeDtypeStruct((B, T, 3 * D_HYENA), F32), jax.ShapeDtypeStruct((B, T, D_SSM), F32)],
        compiler_params=_cparams(("parallel", "parallel"), VMEM_LIMIT),
        name="in_proj",
    )(x, mod, g1.reshape(1, D), w_in_bf, head_avg,
      jnp.tile(gq, ATT_HEADS).reshape(1, D_ATT), jnp.tile(gk, ATT_HEADS).reshape(1, D_ATT))


ATT_R = 4
ATT_KROWS = ATT_R + WIN_ROWS - 1
ATT_LANES = 256


def _softmax_heads(q, parts, bias_ref, o_ref):
    lane = lax.broadcasted_iota(jnp.int32, (1, ATT_LANES), 1)
    acc = jnp.zeros((q.shape[0], ATT_LANES), F32)
    for h in range(ATT_LANES // ATT_HEAD_DIM):
        hm = (lane >= ATT_HEAD_DIM * h) & (lane < ATT_HEAD_DIM * (h + 1))
        qh = jnp.where(hm, q, jnp.zeros_like(q))
        ss = [_dot_nt(qh, kk) for kk, _ in parts]
        if bias_ref is not None:
            ss[0] = ss[0] + bias_ref[h].astype(F32)
        m = ss[0].max(axis=-1, keepdims=True)
        for s in ss[1:]:
            m = jnp.maximum(m, s.max(axis=-1, keepdims=True))
        ps = [jnp.exp(s - m) for s in ss]
        l = ps[0].sum(axis=-1, keepdims=True)
        for p in ps[1:]:
            l = l + p.sum(axis=-1, keepdims=True)
        o = _dot(ps[0].astype(BF16), parts[0][1])
        for p, (_, vv) in zip(ps[1:], parts[1:]):
            o = o + _dot(p.astype(BF16), vv)
        acc = jnp.where(hm, o / l, acc)
    o_ref[...] = acc.astype(o_ref.dtype)


def _nattn_kernel(q_ref, k_ref, v_ref, kc_ref, vc_ref, bias_ref, o_ref, *, rows):
    j = pl.program_id(2)
    ks = jnp.clip(j * ATT_R - WIN_ROWS // 2, 0, rows - ATT_KROWS)
    start = pl.multiple_of(ks * GRID_W, GRID_W)
    kl = k_ref[pl.ds(start, ATT_KROWS * GRID_W), :]
    vl = v_ref[pl.ds(start, ATT_KROWS * GRID_W), :]
    _softmax_heads(q_ref[...], [(kl, vl), (kc_ref[...], vc_ref[...])], bias_ref, o_ref)


def _attn_bias_tables(rpb, rows):
    nblk = rows // ATT_R
    qc = np.arange(GRID_W)
    c0 = np.clip(qc - WIN_COLS // 2, 0, GRID_W - WIN_COLS)
    kc = np.arange(GRID_W)
    col_ok = (kc[None, :] >= c0[:, None]) & (kc[None, :] < c0[:, None] + WIN_COLS)
    col_off = np.clip(kc[None, :] - qc[:, None] + WIN_COLS - 1, 0, 2 * WIN_COLS - 2)
    tabs = []
    for blk in (0, 1, nblk - 1):
        r = blk * ATT_R + np.arange(ATT_R)
        ks = int(np.clip(blk * ATT_R - WIN_ROWS // 2, 0, rows - ATT_KROWS))
        kk = ks + np.arange(ATT_KROWS)
        r0 = np.clip(r - WIN_ROWS // 2, 0, rows - WIN_ROWS)
        row_ok = (kk[None, :] >= r0[:, None]) & (kk[None, :] < r0[:, None] + WIN_ROWS)
        row_off = np.clip(kk[None, :] - r[:, None] + WIN_ROWS - 1, 0, 2 * WIN_ROWS - 2)
        vals = rpb[:, row_off[:, None, :, None], col_off[None, :, None, :]]
        ok = row_ok[:, None, :, None] & col_ok[None, :, None, :]
        t = jnp.where(ok[None], vals, NEG)
        tabs.append(t.reshape(ATT_HEADS, ATT_R * GRID_W, ATT_KROWS * GRID_W))
    return jnp.stack(tabs).astype(BF16)


def neighbourhood_attention(q, k, v, kc, vc, rpb):
    B, L, _ = q.shape
    Lc = kc.shape[1]
    rows = L // GRID_W
    nblk = rows // ATT_R
    tq = ATT_R * GRID_W
    nkl = ATT_KROWS * GRID_W
    bias = _attn_bias_tables(rpb, rows)
    hpb = ATT_LANES // ATT_HEAD_DIM
    variant = lambda j: jnp.where(j == 0, 0, jnp.where(j == nblk - 1, 2, 1))
    return pl.pallas_call(
        functools.partial(_nattn_kernel, rows=rows),
        grid=(B, D_ATT // ATT_LANES, nblk),
        in_specs=[pl.BlockSpec((None, tq, ATT_LANES), lambda b, g, j: (b, j, g)),
                  pl.BlockSpec((None, L, ATT_LANES), lambda b, g, j: (b, 0, g)),
                  pl.BlockSpec((None, L, ATT_LANES), lambda b, g, j: (b, 0, g)),
                  pl.BlockSpec((None, Lc, ATT_LANES), lambda b, g, j: (b, 0, g)),
                  pl.BlockSpec((None, Lc, ATT_LANES), lambda b, g, j: (b, 0, g)),
                  pl.BlockSpec((None, hpb, tq, nkl), lambda b, g, j: (variant(j), g, 0, 0))],
        out_specs=pl.BlockSpec((None, tq, ATT_LANES), lambda b, g, j: (b, j, g)),
        out_shape=jax.ShapeDtypeStruct((B, L, D_ATT), BF16),
        compiler_params=_cparams(("parallel", "parallel", "arbitrary"), VMEM_LIMIT),
        name="nattn",
    )(q, k, v, kc, vc, bias)


def _cattn_kernel(q_ref, k_ref, v_ref, o_ref):
    _softmax_heads(q_ref[...], [(k_ref[...], v_ref[...])], None, o_ref)


def context_attention(q, k, v):
    B, Lc, _ = q.shape
    spec = pl.BlockSpec((None, Lc, ATT_LANES), lambda b, g: (b, 0, g))
    return pl.pallas_call(
        _cattn_kernel,
        grid=(B, D_ATT // ATT_LANES),
        in_specs=[spec, spec, spec],
        out_specs=spec,
        out_shape=jax.ShapeDtypeStruct((B, Lc, D_ATT), BF16),
        compiler_params=_cparams(("parallel", "parallel")),
        name="cattn",
    )(q, k, v)


FFT_N = 16384
FFT_R = 128
FFT_K1 = FFT_R // 2 + 1
FFT_K1P = 72


def _filter_kernel(w1_ref, b1_ref, w2_ref, b2_ref, w3_ref, b3_ref, fr_ref, wo_ref, o_ref, *, lf, tile):
    i = pl.program_id(0)
    half = FFT_N // 2
    j = i * tile + lax.broadcasted_iota(jnp.int32, (tile, 1), 0)
    pos_i = jnp.where(j < half, j, FFT_N - 1 - j)
    valid = pos_i < lf
    pos = pos_i.astype(F32)
    t = pos / (lf - 1.0)
    w = (2.0 * math.pi / lf) * pos
    lane = lax.broadcasted_iota(jnp.int32, (1, 128), 1)
    band = jnp.where(lane <= FILTER_BANDS, lane - 1, lane - 1 - FILTER_BANDS).astype(F32)
    f = 1e-4 + band * ((FILTER_BANDS - 1 - 1e-4) / (FILTER_BANDS - 1))
    ang = f * w
    z = jnp.where(lane == 0, t,
                  jnp.where(lane <= FILTER_BANDS, jnp.cos(ang),
                            jnp.where(lane <= 2 * FILTER_BANDS, -jnp.sin(ang), 0.0)))
    fr = fr_ref[...]
    hdot = lambda a, b: jnp.dot(a, b, precision=HIGHEST, preferred_element_type=F32)
    h = jnp.sin(fr * (hdot(z, w1_ref[...]) + b1_ref[...]))
    h = jnp.sin(fr * (hdot(h, w2_ref[...]) + b2_ref[...]))
    h = jnp.sin(fr * (hdot(h, w3_ref[...]) + b3_ref[...]))
    y = hdot(h, wo_ref[...])
    ch = lax.broadcasted_iota(jnp.int32, (1, HYENA_ORDER * D_HYENA), 1) % D_HYENA
    delta = jnp.abs(MIN_DECAY + ch.astype(F32) * ((MAX_DECAY - MIN_DECAY) / (D_HYENA - 1)))
    y = y * jnp.exp(-t * delta)
    o_ref[...] = jnp.where(valid, y, 0.0)


def hyena_filter_taps(lf, w1, b1, w2, b2, w3, b3, freq, w_out, tile=1024):
    w1p = jnp.zeros((128, FILTER_WIDTH), F32).at[:FILTER_EMB].set(w1)
    wo = w_out.reshape(FILTER_WIDTH, HYENA_ORDER, 2, D_HYENA).transpose(2, 0, 1, 3).reshape(2, FILTER_WIDTH, HYENA_ORDER * D_HYENA)
    nt = FFT_N // tile
    const = lambda shape: pl.BlockSpec(shape, lambda i: (0,) * len(shape))
    row = lambda a: a.reshape(1, FILTER_WIDTH)
    return pl.pallas_call(
        functools.partial(_filter_kernel, lf=lf, tile=tile),
        grid=(nt,),
        in_specs=[const((128, FILTER_WIDTH)), const((1, FILTER_WIDTH)), const((FILTER_WIDTH, FILTER_WIDTH)),
                  const((1, FILTER_WIDTH)), const((FILTER_WIDTH, FILTER_WIDTH)), const((1, FILTER_WIDTH)),
                  const((1, FILTER_WIDTH)),
                  pl.BlockSpec((None, FILTER_WIDTH, HYENA_ORDER * D_HYENA), lambda i: (i // (nt // 2), 0, 0))],
        out_specs=pl.BlockSpec((tile, HYENA_ORDER * D_HYENA), lambda i: (i, 0)),
        out_shape=jax.ShapeDtypeStruct((FFT_N, HYENA_ORDER * D_HYENA), F32),
        compiler_params=_cparams(("parallel",)),
        name="hyena_filter",
    )(w1p, row(b1), w2, row(b2), w3, row(b3), row(freq), wo)


def _shortconv_kernel(z_ref, w_ref, b_ref, o_ref, zp_ref, *, L):
    C = z_ref.shape[-1]
    zp_ref[0:8, :] = jnp.zeros((8, C), F32)
    zp_ref[L + 8:L + 16, :] = jnp.zeros((8, C), F32)
    zp_ref[8:L + 8, :] = z_ref[...]
    ch = min(L, 512)
    for c in range(L // ch):
        s = c * ch
        o_ref[s:s + ch, :] = (b_ref[...] + zp_ref[s + 7:s + 7 + ch, :] * w_ref[0:1, :]
                              + zp_ref[s + 8:s + 8 + ch, :] * w_ref[1:2, :]
                              + zp_ref[s + 9:s + 9 + ch, :] * w_ref[2:3, :])


def short_conv(z, w, b, cb=128):
    B, L, C = z.shape
    return pl.pallas_call(
        functools.partial(_shortconv_kernel, L=L),
        grid=(B, C // cb),
        in_specs=[pl.BlockSpec((None, L, cb), lambda b_, c: (b_, 0, c)),
                  pl.BlockSpec((3, cb), lambda b_, c: (0, c)),
                  pl.BlockSpec((1, cb), lambda b_, c: (0, c))],
        out_specs=pl.BlockSpec((None, L, cb), lambda b_, c: (b_, 0, c)),
        out_shape=jax.ShapeDtypeStruct((B, L, C), F32),
        scratch_shapes=[pltpu.VMEM((L + 16, cb), F32)],
        compiler_params=_cparams(("parallel", "parallel")),
        name="short_conv",
    )(z, w, b.reshape(1, C))


@functools.lru_cache(maxsize=None)
def _fft_tables(n1_in):
    n2 = np.arange(FFT_R)[:, None, None]
    k1 = np.arange(FFT_K1)[None, :, None]
    n1 = np.arange(n1_in)[None, None, :]
    th = 2.0 * np.pi * ((k1 * (FFT_R * n1 + n2)) % FFT_N) / FFT_N
    ga = np.zeros((FFT_R, 2 * FFT_K1P, n1_in))
    ga[:, :FFT_K1] = np.cos(th)
    ga[:, FFT_K1P:FFT_K1P + FFT_K1] = -np.sin(th)
    wk = np.where((np.arange(FFT_K1) == 0) | (np.arange(FFT_K1) == FFT_R // 2), 1.0, 2.0)[None, None, :]
    thd = th.transpose(0, 2, 1)
    gd = np.zeros((FFT_R, n1_in, 2 * FFT_K1P))
    gd[:, :, :FFT_K1] = wk * np.cos(thd)
    gd[:, :, FFT_K1P:FFT_K1P + FFT_K1] = -wk * np.sin(thd)
    a = np.arange(FFT_R)
    ph = 2.0 * np.pi * ((a[:, None] * a[None, :]) % FFT_R) / FFT_R
    c, s = np.cos(ph), np.sin(ph)
    fb = np.block([[c, s], [-s, c]])
    fc = np.block([[c, -s], [s, c]])
    return ga, gd, fb, fc


def _spectrum_slab(ya_ref, fb, k1):
    re = ya_ref[pl.ds(k1, FFT_R, stride=2 * FFT_K1P), :]
    im = ya_ref[pl.ds(FFT_K1P + k1, FFT_R, stride=2 * FFT_K1P), :]
    return _dot(fb, jnp.concatenate([re, im], axis=0).astype(BF16))


def _stage_a(x_ref, ga_ref, ya_ref, n1_in):
    def body(n2, carry):
        xs = x_ref[pl.ds(n2, n1_in, stride=FFT_R), :].astype(BF16)
        ya_ref[pl.ds(pl.multiple_of(n2 * 2 * FFT_K1P, 8), 2 * FFT_K1P), :] = _dot(ga_ref[n2], xs)
        return carry
    lax.fori_loop(0, FFT_R, body, 0)


def _spectrum_kernel(x_ref, ga_ref, fb_ref, o_ref, ya_ref):
    _stage_a(x_ref, ga_ref, ya_ref, FFT_R)
    fb = fb_ref[...]

    def body(k1, carry):
        o_ref[k1] = _spectrum_slab(ya_ref, fb, k1)
        return carry
    lax.fori_loop(0, FFT_K1, body, 0)


def filter_spectrum(taps, cb=128):
    C = taps.shape[1]
    ga, _, fb, _ = _fft_tables(FFT_R)
    return pl.pallas_call(
        _spectrum_kernel,
        grid=(C // cb,),
        in_specs=[pl.BlockSpec((FFT_N, cb), lambda c: (0, c)),
                  pl.BlockSpec((FFT_R, 2 * FFT_K1P, FFT_R), lambda c: (0, 0, 0)),
                  pl.BlockSpec((2 * FFT_R, 2 * FFT_R), lambda c: (0, 0))],
        out_specs=pl.BlockSpec((FFT_K1, 2 * FFT_R, cb), lambda c: (0, 0, c)),
        out_shape=jax.ShapeDtypeStruct((FFT_K1, 2 * FFT_R, C), F32),
        scratch_shapes=[pltpu.VMEM((FFT_R * 2 * FFT_K1P, cb), F32)],
        compiler_params=_cparams(("parallel",), VMEM_LIMIT),
        name="filter_spectrum",
    )(taps, jnp.asarray(ga, BF16), jnp.asarray(fb, BF16))


def _longconv_kernel(u_ref, gate_ref, bias_ref, spec_ref, ga_ref, gd_ref, fb_ref, fc_ref, o_ref, ya_ref):
    n1_in = FFT_R // 2
    _stage_a(u_ref, ga_ref, ya_ref, n1_in)
    fb = fb_ref[...]
    fc = fc_ref[...]

    def mid(k1, carry):
        x = _spectrum_slab(ya_ref, fb, k1)
        s = spec_ref[k1]
        xr, xi = x[:FFT_R], x[FFT_R:]
        sr, si = s[:FFT_R], s[FFT_R:]
        z = jnp.concatenate([xr * sr - xi * si, xr * si + xi * sr], axis=0).astype(BF16)
        v = _dot(fc, z)
        ya_ref[pl.ds(k1, FFT_R, stride=2 * FFT_K1P), :] = v[:FFT_R]
        ya_ref[pl.ds(FFT_K1P + k1, FFT_R, stride=2 * FFT_K1P), :] = v[FFT_R:]
        return carry
    lax.fori_loop(0, FFT_K1, mid, 0)

    bias = bias_ref[...]

    def last(n2, carry):
        slab = ya_ref[pl.ds(pl.multiple_of(n2 * 2 * FFT_K1P, 8), 2 * FFT_K1P), :].astype(BF16)
        y = _dot(gd_ref[n2], slab) * (1.0 / FFT_N)
        rows = pl.ds(n2, n1_in, stride=FFT_R)
        o_ref[rows, :] = gate_ref[rows, :] * (y + u_ref[rows, :] * bias)
        return carry
    lax.fori_loop(0, FFT_R, last, 0)


def gated_long_conv(u, u_col, gate, gate_col, bias, spec, spec_col, C=D_HYENA, cb=128):
    B, L, _ = u.shape
    n1_in = FFT_R // 2
    ga, gd, fb, fc = _fft_tables(n1_in)
    once = pl.Buffered(1)
    seq = lambda col: pl.BlockSpec((None, L, cb), lambda c, b: (b, 0, c + col // cb))
    return pl.pallas_call(
        _longconv_kernel,
        grid=(C // cb, B),
        in_specs=[seq(u_col), seq(gate_col),
                  pl.BlockSpec((1, cb), lambda c, b: (0, c)),
                  pl.BlockSpec((FFT_K1, 2 * FFT_R, cb), lambda c, b: (0, 0, c + spec_col // cb), pipeline_mode=once),
                  pl.BlockSpec((FFT_R, 2 * FFT_K1P, n1_in), lambda c, b: (0, 0, 0), pipeline_mode=once),
                  pl.BlockSpec((FFT_R, n1_in, 2 * FFT_K1P), lambda c, b: (0, 0, 0), pipeline_mode=once),
                  pl.BlockSpec((2 * FFT_R, 2 * FFT_R), lambda c, b: (0, 0)),
                  pl.BlockSpec((2 * FFT_R, 2 * FFT_R), lambda c, b: (0, 0))],
        out_specs=seq(0),
        out_shape=jax.ShapeDtypeStruct((B, L, C), F32),
        scratch_shapes=[pltpu.VMEM((FFT_R * 2 * FFT_K1P, cb), F32)],
        compiler_params=_cparams(("parallel", "arbitrary"), VMEM_LIMIT),
        name="long_conv",
    )(u, gate, bias.reshape(1, C), spec, jnp.asarray(ga, BF16), jnp.asarray(gd, BF16),
      jnp.asarray(fb, BF16), jnp.asarray(fc, BF16))


SSM_N = SSM_GROUPS * SSM_STATE
S5_SEQS = 8
S5_CHUNK = 128


def _s5_kernel(u_ref, bf_ref, bb_ref, lam_ref, c_ref, o_ref, bu_ref, xs_ref, st_ref):
    rows = S5_CHUNK * S5_SEQS

    @pl.when(pl.program_id(0) == 0)
    def _():
        st_ref[...] = jnp.zeros_like(st_ref)

    u = u_ref[...]
    fwd = (lax.broadcasted_iota(jnp.int32, (rows, 1), 0) % S5_SEQS) < (S5_SEQS // 2)
    bu_ref[...] = jnp.where(fwd, _dot(u, bf_ref[...]), _dot(u, bb_ref[...]))

    def step(t2, carry):
        xr, xi = carry
        out = []
        for half in range(2):
            r0 = pl.multiple_of(t2 * 2 * S5_SEQS + half * S5_SEQS, S5_SEQS)
            br = bu_ref[pl.ds(r0, S5_SEQS), 0:SSM_N]
            bi = bu_ref[pl.ds(r0, S5_SEQS), SSM_N:2 * SSM_N]
            lr = lam_ref[:, 0:SSM_N]
            li = lam_ref[:, SSM_N:2 * SSM_N]
            xr, xi = lr * xr - li * xi + br, lr * xi + li * xr + bi
            out.append(jnp.concatenate([xr, xi], axis=1))
        xs_ref[pl.ds(pl.multiple_of(t2 * 2 * S5_SEQS, 2 * S5_SEQS), 2 * S5_SEQS), :] = (
            jnp.concatenate(out, axis=0).astype(BF16))
        return xr, xi

    xr, xi = lax.fori_loop(0, S5_CHUNK // 2, step, (st_ref[:, 0:SSM_N], st_ref[:, SSM_N:2 * SSM_N]))
    st_ref[:, 0:SSM_N] = xr
    st_ref[:, SSM_N:2 * SSM_N] = xi
    y = _dot(xs_ref[...], c_ref[...])
    o_ref[...] = jnp.where(fwd, y[:, 0:D_SSM], y[:, D_SSM:2 * D_SSM])


def _s5_matrices(lam_re, lam_im, log_step, b_re, b_im, c_re, c_im):
    G, P, Hs = SSM_GROUPS, SSM_STATE, SSM_GROUP_WIDTH
    dt = jnp.exp(log_step)[:, :, None]
    mag = jnp.exp(lam_re * dt)
    ar, ai = mag * jnp.cos(lam_im * dt), mag * jnp.sin(lam_im * dt)
    er, ei = ar - 1.0, ai
    den = lam_re * lam_re + lam_im * lam_im
    cr, ci = (er * lam_re + ei * lam_im) / den, (ei * lam_re - er * lam_im) / den
    bbr = cr[..., None] * b_re - ci[..., None] * b_im
    bbi = cr[..., None] * b_im + ci[..., None] * b_re
    eye = jnp.eye(G, dtype=F32)
    blk_in = lambda m: jnp.einsum('gq,qph->ghqp', eye, m).reshape(G * Hs, G * P)
    bmat = [jnp.concatenate([blk_in(bbr[d]), blk_in(bbi[d])], axis=1).astype(BF16) for d in range(2)]
    blk_out = lambda m: jnp.einsum('gq,qhp->gpqh', eye, m).reshape(G * P, G * Hs)
    cmat = jnp.concatenate(
        [jnp.concatenate([blk_out(c_re[d]), -blk_out(c_im[d])], axis=0) for d in range(2)], axis=1).astype(BF16)
    lam8 = jnp.concatenate([jnp.repeat(ar.reshape(2, 1, G * P), S5_SEQS // 2, axis=1).reshape(S5_SEQS, G * P),
                            jnp.repeat(ai.reshape(2, 1, G * P), S5_SEQS // 2, axis=1).reshape(S5_SEQS, G * P)], axis=1)
    return bmat[0], bmat[1], lam8, cmat


def s5_scan(u_c, u_x, lam_re, lam_im, log_step, b_re, b_im, c_re, c_im):
    B, Lc, _ = u_c.shape
    L = u_x.shape[1]
    T = Lc + L
    assert 2 * B == S5_SEQS and T % S5_CHUNK == 0
    bf, bb, lam8, cmat = _s5_matrices(lam_re, lam_im, log_step, b_re, b_im, c_re, c_im)
    seq_f = jnp.concatenate([u_c, u_x], axis=1)
    seq_b = jnp.concatenate([u_c[:, ::-1], u_x[:, ::-1]], axis=1)
    u8 = jnp.stack([seq_f, seq_b]).transpose(2, 0, 1, 3).reshape(T * S5_SEQS, D_SSM).astype(BF16)
    rows = S5_CHUNK * S5_SEQS
    const = lambda shape: pl.BlockSpec(shape, lambda i: (0,) * len(shape))
    y8 = pl.pallas_call(
        _s5_kernel,
        grid=(T // S5_CHUNK,),
        in_specs=[pl.BlockSpec((rows, D_SSM), lambda i: (i, 0)),
                  const((D_SSM, 2 * SSM_N)), const((D_SSM, 2 * SSM_N)), const((S5_SEQS, 2 * SSM_N)),
                  const((2 * SSM_N, 2 * D_SSM))],
        out_specs=pl.BlockSpec((rows, D_SSM), lambda i: (i, 0)),
        out_shape=jax.ShapeDtypeStruct((T * S5_SEQS, D_SSM), F32),
        scratch_shapes=[pltpu.VMEM((rows, 2 * SSM_N), F32), pltpu.VMEM((rows, 2 * SSM_N), BF16),
                        pltpu.VMEM((S5_SEQS, 2 * SSM_N), F32)],
        compiler_params=_cparams(("arbitrary",), VMEM_LIMIT),
        name="s5_scan",
    )(u8, bf, bb, lam8, cmat)
    y8 = y8.reshape(T, 2, B, D_SSM).transpose(1, 2, 0, 3)
    yf, yb = y8[0], y8[1]
    return (yf[:, :Lc], yb[:, :Lc][:, ::-1]), (yf[:, Lc:], yb[:, Lc:][:, ::-1])


def _gelu_tanh(y):
    return 0.5 * y * (1.0 + jnp.tanh(math.sqrt(2.0 / math.pi) * (y + 0.044715 * (y * y * y))))


def _outproj_kernel(*refs, route):
    (x_ref, mod_ref, att_ref, hy_ref, yf_ref, yb_ref, u_ref, dsk_ref, wg_ref, bg_ref, wo_ref, g2_ref) = refs[:12]
    if route:
        wr_ref, x1_ref, h2_ref, rt_ref = refs[12:]
    else:
        x1_ref, h2_ref = refs[12:]
    y = yf_ref[...] + yb_ref[...] + u_ref[...] * dsk_ref[...]
    y = _gelu_tanh(y)
    ss = y * jax.nn.sigmoid(_dot(y.astype(BF16), wg_ref[...]) + bg_ref[...])
    mix = (_dot(att_ref[...], wo_ref[0:D_ATT, :])
           + _dot(hy_ref[...].astype(BF16), wo_ref[D_ATT:D_ATT + D_HYENA, :])
           + _dot(ss.astype(BF16), wo_ref[D_ATT + D_HYENA:, :]))
    x1 = x_ref[...] + mod_ref[2:3, :] * mix
    x1_ref[...] = x1
    h2 = _rms_mod(x1, g2_ref[...], mod_ref[3:4, :], mod_ref[4:5, :])
    h2_ref[...] = h2.astype(h2_ref.dtype)
    if route:
        lane = lax.broadcasted_iota(jnp.int32, (1, 128), 1)
        logits = jnp.dot(h2, wr_ref[...], precision=HIGHEST, preferred_element_type=F32)
        lg = jnp.where(lane < N_EXPERTS, logits, -jnp.inf)
        v1 = lg.max(axis=-1, keepdims=True)
        i1 = jnp.where(lg == v1, lane, 128).min(axis=-1, keepdims=True)
        lg2 = jnp.where(lane == i1, -jnp.inf, lg)
        v2 = lg2.max(axis=-1, keepdims=True)
        i2 = jnp.where(lg2 == v2, lane, 128).min(axis=-1, keepdims=True)
        e = jnp.exp(v2 - v1)
        w1 = 1.0 / (1.0 + e)
        rt_ref[...] = jnp.where(lane == 0, i1.astype(F32),
                                jnp.where(lane == 1, i2.astype(F32),
                                          jnp.where(lane == 2, w1, jnp.where(lane == 3, e * w1, 0.0))))


def out_proj(x, mod, att, hy, yf, yb, u, d_skip, w_glu, b_glu, w_out_bf, g2, router_w=None, tm=256):
    B, T, D = x.shape
    route = router_w is not None
    tok = lambda n: pl.BlockSpec((None, tm, n), lambda b, i: (b, i, 0))
    const = lambda shape: pl.BlockSpec(shape, lambda b, i: (0,) * len(shape))
    in_specs = [tok(D), pl.BlockSpec((None, 6, D), lambda b, i: (b, 0, 0)), tok(D_ATT), tok(D_HYENA),
                tok(D_SSM), tok(D_SSM), tok(D_SSM), const((1, D_SSM)), const((D_SSM, D_SSM)), const((1, D_SSM)),
                const((D, D)), const((1, D))]
    args = [x, mod, att, hy, yf, yb, u, d_skip.reshape(1, D_SSM), w_glu.astype(BF16), b_glu.reshape(1, D_SSM),
            w_out_bf, g2.reshape(1, D)]
    out_specs = [tok(D), tok(D)]
    out_shape = [jax.ShapeDtypeStruct((B, T, D), F32), jax.ShapeDtypeStruct((B, T, D), BF16)]
    if route:
        in_specs.append(const((D, 128)))
        args.append(jnp.zeros((D, 128), F32).at[:, :N_EXPERTS].set(router_w))
        out_specs.append(tok(128))
        out_shape.append(jax.ShapeDtypeStruct((B, T, 128), F32))
    return pl.pallas_call(
        functools.partial(_outproj_kernel, route=route),
        grid=(B, T // tm),
        in_specs=in_specs, out_specs=out_specs, out_shape=out_shape,
        compiler_params=_cparams(("parallel", "parallel"), VMEM_LIMIT),
        name="out_proj",
    )(*args)


FF_SPLIT = 1280


def _swiglu_tile(h, w1_ref, w3_ref, w2_ref):
    acc = None
    for lo, hi in ((0, FF_SPLIT), (FF_SPLIT, D_FF)):
        a = _dot(h, w1_ref[:, lo:hi])
        g = (a * jax.nn.sigmoid(a) * _dot(h, w3_ref[:, lo:hi])).astype(BF16)
        part = _dot(g, w2_ref[lo:hi, :])
        acc = part if acc is None else acc + part
    return acc


def _ffn_kernel(h_ref, x_ref, mod_ref, w1_ref, w3_ref, w2_ref, o_ref):
    o_ref[...] = x_ref[...] + mod_ref[5:6, :] * _swiglu_tile(h_ref[...], w1_ref, w3_ref, w2_ref)


def ffn_dense(h2, x1, mod, w1, w3, w2, tm=256):
    B, T, D = x1.shape
    tm = min(tm, T)
    tok = pl.BlockSpec((None, tm, D), lambda b, i: (b, i, 0))
    once = pl.Buffered(1)
    return pl.pallas_call(
        _ffn_kernel,
        grid=(B, T // tm),
        in_specs=[tok, tok, pl.BlockSpec((None, 6, D), lambda b, i: (b, 0, 0)),
                  pl.BlockSpec((D, D_FF), lambda b, i: (0, 0), pipeline_mode=once),
                  pl.BlockSpec((D, D_FF), lambda b, i: (0, 0), pipeline_mode=once),
                  pl.BlockSpec((D_FF, D), lambda b, i: (0, 0), pipeline_mode=once)],
        out_specs=tok,
        out_shape=jax.ShapeDtypeStruct((B, T, D), F32),
        compiler_params=_cparams(("parallel", "parallel"), VMEM_LIMIT),
        name="ffn_dense",
    )(h2, x1, mod, w1, w3, w2)


MOE_TM = 512


def _moe_kernel(te_ref, tv_ref, h_ref, w1_ref, w3_ref, w2_ref, o_ref):
    @pl.when(tv_ref[pl.program_id(0)] > 0)
    def _():
        o_ref[...] = _swiglu_tile(h_ref[...], w1_ref, w3_ref, w2_ref)


def moe_experts(hs, tile_expert, tile_valid, w1, w3, w2):
    P, D = hs.shape
    nt = P // MOE_TM
    tok = pl.BlockSpec((MOE_TM, D), lambda i, te, tv: (i, 0))
    return pl.pallas_call(
        _moe_kernel,
        grid_spec=pltpu.PrefetchScalarGridSpec(
            num_scalar_prefetch=2,
            grid=(nt,),
            in_specs=[tok,
                      pl.BlockSpec((None, D, D_FF), lambda i, te, tv: (te[i], 0, 0)),
                      pl.BlockSpec((None, D, D_FF), lambda i, te, tv: (te[i], 0, 0)),
                      pl.BlockSpec((None, D_FF, D), lambda i, te, tv: (te[i], 0, 0))],
            out_specs=tok),
        out_shape=jax.ShapeDtypeStruct((P, D), F32),
        compiler_params=_cparams(("arbitrary",), VMEM_LIMIT),
        name="moe_experts",
    )(tile_expert, tile_valid, hs, w1, w3, w2)


def _combine_kernel(x_ref, mod_ref, ya_ref, yb_ref, rt_ref, o_ref):
    wa = rt_ref[:, 2:3]
    wb = rt_ref[:, 3:4]
    o_ref[...] = x_ref[...] + mod_ref[5:6, :] * (wa * ya_ref[...] + wb * yb_ref[...])


def moe_combine(x1, mod, ya, yb, rt, tm=512):
    B, T, D = x1.shape
    tok = lambda n: pl.BlockSpec((None, tm, n), lambda b, i: (b, i, 0))
    return pl.pallas_call(
        _combine_kernel,
        grid=(B, T // tm),
        in_specs=[tok(D), pl.BlockSpec((None, 6, D), lambda b, i: (b, 0, 0)), tok(D), tok(D), tok(128)],
        out_specs=tok(D),
        out_shape=jax.ShapeDtypeStruct((B, T, D), F32),
        compiler_params=_cparams(("parallel", "parallel")),
        name="moe_combine",
    )(x1, mod, ya, yb, rt)


def moe_layer(h2, x1, mod, rt, w1, w3, w2):
    B, T, D = x1.shape
    n = B * T
    e_flat = jnp.concatenate([rt[..., 0].reshape(n), rt[..., 1].reshape(n)]).astype(jnp.int32)
    tok = jnp.concatenate([jnp.arange(n, dtype=jnp.int32)] * 2)
    order = jnp.argsort(e_flat, stable=True)
    sorted_e = e_flat[order]
    counts = jnp.sum(jax.nn.one_hot(e_flat, N_EXPERTS, dtype=jnp.int32), axis=0)
    padded = ((counts + MOE_TM - 1) // MOE_TM) * MOE_TM
    pend = jnp.cumsum(padded)
    pstart = pend - padded
    start = jnp.cumsum(counts) - counts
    dest = pstart[sorted_e] + jnp.arange(2 * n, dtype=jnp.int32) - start[sorted_e]
    P = 2 * n + N_EXPERTS * MOE_TM
    nt = P // MOE_TM
    src = jnp.zeros((P,), jnp.int32).at[dest].set(tok[order])
    slot = jnp.zeros((2 * n,), jnp.int32).at[order].set(dest)
    tile_row = jnp.arange(nt, dtype=jnp.int32) * MOE_TM
    tile_expert = jnp.minimum(jnp.searchsorted(pend, tile_row, side='right'), N_EXPERTS - 1).astype(jnp.int32)
    tile_valid = (tile_row < pend[-1]).astype(jnp.int32)
    hs = jnp.take(h2.reshape(n, D), src, axis=0)
    ys = moe_experts(hs, tile_expert, tile_valid, w1, w3, w2)
    ya = jnp.take(ys, slot[:n], axis=0).reshape(B, T, D)
    yb = jnp.take(ys, slot[n:], axis=0).reshape(B, T, D)
    return moe_combine(x1, mod, ya, yb, rt)


def _hyena(hz, conv_w, conv_b, spec, bias):
    zc = short_conv(hz, conv_w, conv_b)
    L = zc.shape[1]
    if L < FFT_N // 2:
        zc = jnp.pad(zc, ((0, 0), (0, FFT_N // 2 - L), (0, 0)))
    y = gated_long_conv(zc, 0, zc, D_HYENA, bias[0], spec, 0)
    y = gated_long_conv(y, 0, zc, 2 * D_HYENA, bias[1], spec, D_HYENA)
    return y[:, :L]


def _split_route(outs, mod):
    x1, h2 = outs[0], outs[1]
    return h2, x1, mod, (outs[2] if len(outs) > 2 else None)


def kernel(x, c, ctx, c_ctx, w_ada, b_ada, g_norm1, g_norm2, w_in, w_out, g_q, g_k, rpb, hy_conv_w, hy_conv_b, filt_w1, filt_b1, filt_w2, filt_b2, filt_w3, filt_b3, filt_freq, filt_w_out, hy_bias, lam_re, lam_im, log_step, b_re, b_im, c_re, c_im, d_skip, w_glu, b_glu, ffn_w1, ffn_w3, ffn_w2, router_w, moe_w1, moe_w3, moe_w2):
    B, L, D = x.shape
    Lc = ctx.shape[1]
    c8 = jnp.zeros((8, D), F32).at[:B].set(c).at[B].set(c_ctx)
    mod_all = adaln_mod(c8, w_ada, b_ada)
    for layer in range(DEPTH):
        last = layer == DEPTH - 1
        m = mod_all[layer].reshape(8, 6, D)
        mod_x = m[:B]
        mod_c = jnp.broadcast_to(m[B], (B, 6, D))
        w_l = w_in[layer].astype(BF16)
        wo_l = w_out[layer].astype(BF16)
        filt = (filt_w1[layer], filt_b1[layer], filt_w2[layer], filt_b2[layer],
                filt_w3[layer], filt_b3[layer], filt_freq[layer], filt_w_out[layer])
        ssm = (lam_re[layer], lam_im[layer], log_step[layer], b_re[layer], b_im[layer], c_re[layer], c_im[layer])

        q_c, k_c, v_c, hz_c, u_c = in_proj(ctx, mod_c, g_norm1[layer], w_l, g_q[layer], g_k[layer])
        q, k, v, hz, u = in_proj(x, mod_x, g_norm1[layer], w_l, g_q[layer], g_k[layer])
        att = neighbourhood_attention(q, k, v, k_c, v_c, rpb[layer])
        spec = filter_spectrum(hyena_filter_taps(L, *filt))
        hy = _hyena(hz, hy_conv_w[layer], hy_conv_b[layer], spec, hy_bias[layer])
        (yf_c, yb_c), (yf, yb) = s5_scan(u_c, u, *ssm)

        tail = (d_skip[layer], w_glu[layer], b_glu[layer], wo_l, g_norm2[layer])
        i = layer // 2
        if layer % 2 == 0:
            ffw = (ffn_w1[i].astype(BF16), ffn_w3[i].astype(BF16), ffn_w2[i].astype(BF16))
            mixer = lambda h2, x1, mod, rt: ffn_dense(h2, x1, mod, *ffw)
            rw = None
        else:
            mow = (moe_w1[i].astype(BF16), moe_w3[i].astype(BF16), moe_w2[i].astype(BF16))
            mixer = lambda h2, x1, mod, rt: moe_layer(h2, x1, mod, rt, *mow)
            rw = router_w[i]
        x = mixer(*_split_route(out_proj(x, mod_x, att, hy, yf, yb, u, *tail, router_w=rw), mod_x))

        if not last:
            att_c = context_attention(q_c, k_c, v_c)
            spec_c = filter_spectrum(hyena_filter_taps(Lc, *filt))
            hy_c = _hyena(hz_c, hy_conv_w[layer], hy_conv_b[layer], spec_c, hy_bias[layer])
            ctx = mixer(*_split_route(out_proj(ctx, mod_c, att_c, hy_c, yf_c, yb_c, u_c, *tail, router_w=rw), mod_c))
    return x
```

```python
import functools
import math

import numpy as np
import jax
import jax.numpy as jnp
from jax import lax
from jax.experimental import pallas as pl
from jax.experimental.pallas import tpu as pltpu

F32 = jnp.float32
BF16 = jnp.bfloat16
HIGHEST = lax.Precision.HIGHEST

D_MODEL = 1024
DEPTH = 2
GRID_W = 64
ATT_HEAD_DIM = 64
ATT_HEADS = 8
D_ATT = 512
WIN_ROWS = 8
WIN_COLS = 16
D_HYENA = 256
HYENA_ORDER = 2
FILTER_EMB = 33
FILTER_BANDS = 16
FILTER_WIDTH = 64
MIN_DECAY = math.log(1e-2) / 1.5
MAX_DECAY = math.log(1e-2) / 0.3
D_SSM = 256
SSM_GROUP_WIDTH = 16
SSM_GROUPS = 16
SSM_STATE = 64
COL_K = D_ATT
COL_V = 2 * D_ATT
COL_HY = 3 * D_ATT
COL_SSM = COL_HY + 3 * D_HYENA
D_IN = COL_SSM + D_SSM
D_FF = 2816
N_EXPERTS = 8
EPS = 1e-6
NEG = -1e30

VMEM_LIMIT = 56 * 1024 * 1024


def _cparams(sem, vmem=None):
    return pltpu.CompilerParams(dimension_semantics=sem, vmem_limit_bytes=vmem)


def _dot(a, b):
    return jnp.dot(a, b, preferred_element_type=F32)


def _dot_nt(a, b):
    return lax.dot_general(a, b, (((1,), (1,)), ((), ())), preferred_element_type=F32)


def _mod_kernel(c_ref, w_ref, b_ref, o_ref):
    c = c_ref[...]
    s = c * jax.nn.sigmoid(c)
    o_ref[...] = jnp.dot(s, w_ref[...], precision=HIGHEST, preferred_element_type=F32) + b_ref[...]


def adaln_mod(c8, w_ada, b_ada, tn=512):
    depth, d, n = w_ada.shape
    return pl.pallas_call(
        _mod_kernel,
        grid=(depth, n // tn),
        in_specs=[pl.BlockSpec((8, d), lambda l, j: (0, 0)),
                  pl.BlockSpec((None, d, tn), lambda l, j: (l, 0, j)),
                  pl.BlockSpec((None, 1, tn), lambda l, j: (l, 0, j))],
        out_specs=pl.BlockSpec((None, 8, tn), lambda l, j: (l, 0, j)),
        out_shape=jax.ShapeDtypeStruct((depth, 8, n), F32),
        compiler_params=_cparams(("parallel", "parallel")),
        name="adaln_mod",
    )(c8, w_ada, b_ada.reshape(depth, 1, n))


def _rms_mod(x, g, shift, scale):
    ms = jnp.mean(x * x, axis=-1, keepdims=True)
    h = x * lax.rsqrt(ms + EPS) * g
    return h * (1.0 + scale) + shift


def _head_norm(z, a, g):
    zz = z * z
    hi = zz.astype(BF16)
    lo = (zz - hi.astype(F32)).astype(BF16)
    m = _dot(hi, a) + _dot(lo, a)
    return z * lax.rsqrt(m + EPS) * g


def _inproj_kernel(x_ref, mod_ref, g_ref, w_ref, a_ref, gq_ref, gk_ref,
                   q_ref, k_ref, v_ref, hz_ref, u_ref):
    h = _rms_mod(x_ref[...], g_ref[...], mod_ref[0:1, :], mod_ref[1:2, :]).astype(BF16)
    a = a_ref[...]
    q = _dot(h, w_ref[:, 0:COL_K])
    q_ref[...] = (_head_norm(q, a, gq_ref[...]) * (ATT_HEAD_DIM ** -0.5)).astype(BF16)
    k = _dot(h, w_ref[:, COL_K:COL_V])
    k_ref[...] = _head_norm(k, a, gk_ref[...]).astype(BF16)
    v_ref[...] = _dot(h, w_ref[:, COL_V:COL_HY]).astype(BF16)
    hz_ref[...] = _dot(h, w_ref[:, COL_HY:COL_SSM])
    u_ref[...] = _dot(h, w_ref[:, COL_SSM:D_IN])


def in_proj(x, mod, g1, w_in_bf, gq, gk, tm=256):
    B, T, D = x.shape
    head_avg = jnp.asarray(np.kron(np.eye(ATT_HEADS), np.full((ATT_HEAD_DIM, ATT_HEAD_DIM), 1.0 / ATT_HEAD_DIM)), BF16)
    tok = lambda n: pl.BlockSpec((None, tm, n), lambda b, i: (b, i, 0))
    const = lambda shape: pl.BlockSpec(shape, lambda b, i: (0,) * len(shape))
    return pl.pallas_call(
        _inproj_kernel,
        grid=(B, T // tm),
        in_specs=[tok(D), pl.BlockSpec((None, 6, D), lambda b, i: (b, 0, 0)), const((1, D)),
                  const((D, D_IN)), const((D_ATT, D_ATT)), const((1, D_ATT)), const((1, D_ATT))],
        out_specs=[tok(D_ATT), tok(D_ATT), tok(D_ATT), tok(3 * D_HYENA), tok(D_SSM)],
        out_shape=[jax.ShapeDtypeStruct((B, T, D_ATT), BF16)] * 3
                  + [jax.ShapeDtypeStruct((B, T, 3 * D_HYENA), F32), jax.ShapeDtypeStruct((B, T, D_SSM), F32)],
        compiler_params=_cparams(("parallel", "parallel"), VMEM_LIMIT),
        name="in_proj",
    )(x, mod, g1.reshape(1, D), w_in_bf, head_avg,
      jnp.tile(gq, ATT_HEADS).reshape(1, D_ATT), jnp.tile(gk, ATT_HEADS).reshape(1, D_ATT))


ATT_R = 4
ATT_KROWS = ATT_R + WIN_ROWS - 1
ATT_LANES = 256


def _softmax_heads(q, parts, bias_ref, o_ref):
    lane = lax.broadcasted_iota(jnp.int32, (1, ATT_LANES), 1)
    acc = jnp.zeros((q.shape[0], ATT_LANES), F32)
    for h in range(ATT_LANES // ATT_HEAD_DIM):
        hm = (lane >= ATT_HEAD_DIM * h) & (lane < ATT_HEAD_DIM * (h + 1))
        qh = jnp.where(hm, q, jnp.zeros_like(q))
        ss = [_dot_nt(qh, kk) for kk, _ in parts]
        if bias_ref is not None:
            ss[0] = ss[0] + bias_ref[h].astype(F32)
        m = ss[0].max(axis=-1, keepdims=True)
        for s in ss[1:]:
            m = jnp.maximum(m, s.max(axis=-1, keepdims=True))
        ps = [jnp.exp(s - m) for s in ss]
        l = ps[0].sum(axis=-1, keepdims=True)
        for p in ps[1:]:
            l = l + p.sum(axis=-1, keepdims=True)
        o = _dot(ps[0].astype(BF16), parts[0][1])
        for p, (_, vv) in zip(ps[1:], parts[1:]):
            o = o + _dot(p.astype(BF16), vv)
        acc = jnp.where(hm, o / l, acc)
    o_ref[...] = acc.astype(o_ref.dtype)


def _nattn_kernel(q_ref, k_ref, v_ref, kc_ref, vc_ref, bias_ref, o_ref, *, rows):
    j = pl.program_id(2)
    ks = jnp.clip(j * ATT_R - WIN_ROWS // 2, 0, rows - ATT_KROWS)
    start = pl.multiple_of(ks * GRID_W, GRID_W)
    kl = k_ref[pl.ds(start, ATT_KROWS * GRID_W), :]
    vl = v_ref[pl.ds(start, ATT_KROWS * GRID_W), :]
    _softmax_heads(q_ref[...], [(kl, vl), (kc_ref[...], vc_ref[...])], bias_ref, o_ref)


def _attn_bias_tables(rpb, rows):
    nblk = rows // ATT_R
    qc = np.arange(GRID_W)
    c0 = np.clip(qc - WIN_COLS // 2, 0, GRID_W - WIN_COLS)
    kc = np.arange(GRID_W)
    col_ok = (kc[None, :] >= c0[:, None]) & (kc[None, :] < c0[:, None] + WIN_COLS)
    col_off = kc[None, :] - qc[:, None] + WIN_COLS - 1
    col_sel = (col_off[:, :, None] == np.arange(2 * WIN_COLS - 1)) & col_ok[:, :, None]
    band = jnp.einsum('hij,qkj->hiqk', rpb, jnp.asarray(col_sel, F32), precision=HIGHEST)
    band = jnp.where(jnp.asarray(col_ok)[None, None], band, NEG)
    tabs = []
    for blk in (0, 1, nblk - 1):
        r = blk * ATT_R + np.arange(ATT_R)
        ks = int(np.clip(blk * ATT_R - WIN_ROWS // 2, 0, rows - ATT_KROWS))
        kk = ks + np.arange(ATT_KROWS)
        r0 = np.clip(r - WIN_ROWS // 2, 0, rows - WIN_ROWS)
        row_ok = (kk[None, :] >= r0[:, None]) & (kk[None, :] < r0[:, None] + WIN_ROWS)
        row_off = kk[None, :] - r[:, None] + WIN_ROWS - 1
        row_sel = (row_off[:, :, None] == np.arange(2 * WIN_ROWS - 1)) & row_ok[:, :, None]
        t = jnp.einsum('rsi,hiqk->hrqsk', jnp.asarray(row_sel, F32), band, precision=HIGHEST)
        t = t + jnp.asarray(np.where(row_ok, 0.0, NEG), F32)[None, :, None, :, None]
        tabs.append(t.reshape(ATT_HEADS, ATT_R * GRID_W, ATT_KROWS * GRID_W))
    return jnp.stack(tabs).astype(BF16)


def neighbourhood_attention(q, k, v, kc, vc, rpb):
    B, L, _ = q.shape
    Lc = kc.shape[1]
    rows = L // GRID_W
    nblk = rows // ATT_R
    tq = ATT_R * GRID_W
    nkl = ATT_KROWS * GRID_W
    bias = _attn_bias_tables(rpb, rows)
    hpb = ATT_LANES // ATT_HEAD_DIM
    variant = lambda j: jnp.where(j == 0, 0, jnp.where(j == nblk - 1, 2, 1))
    return pl.pallas_call(
        functools.partial(_nattn_kernel, rows=rows),
        grid=(B, D_ATT // ATT_LANES, nblk),
        in_specs=[pl.BlockSpec((None, tq, ATT_LANES), lambda b, g, j: (b, j, g)),
                  pl.BlockSpec((None, L, ATT_LANES), lambda b, g, j: (b, 0, g)),
                  pl.BlockSpec((None, L, ATT_LANES), lambda b, g, j: (b, 0, g)),
                  pl.BlockSpec((None, Lc, ATT_LANES), lambda b, g, j: (b, 0, g)),
                  pl.BlockSpec((None, Lc, ATT_LANES), lambda b, g, j: (b, 0, g)),
                  pl.BlockSpec((None, hpb, tq, nkl), lambda b, g, j: (variant(j), g, 0, 0))],
        out_specs=pl.BlockSpec((None, tq, ATT_LANES), lambda b, g, j: (b, j, g)),
        out_shape=jax.ShapeDtypeStruct((B, L, D_ATT), BF16),
        compiler_params=_cparams(("parallel", "parallel", "arbitrary"), VMEM_LIMIT),
        name="nattn",
    )(q, k, v, kc, vc, bias)


def _cattn_kernel(q_ref, k_ref, v_ref, o_ref):
    _softmax_heads(q_ref[...], [(k_ref[...], v_ref[...])], None, o_ref)


def context_attention(q, k, v):
    B, Lc, _ = q.shape
    spec = pl.BlockSpec((None, Lc, ATT_LANES), lambda b, g: (b, 0, g))
    return pl.pallas_call(
        _cattn_kernel,
        grid=(B, D_ATT // ATT_LANES),
        in_specs=[spec, spec, spec],
        out_specs=spec,
        out_shape=jax.ShapeDtypeStruct((B, Lc, D_ATT), BF16),
        compiler_params=_cparams(("parallel", "parallel")),
        name="cattn",
    )(q, k, v)


FFT_N = 16384
FFT_R = 128
FFT_K1 = FFT_R // 2 + 1
FFT_K1P = 72


def _filter_kernel(w1_ref, b1_ref, w2_ref, b2_ref, w3_ref, b3_ref, fr_ref, wo_ref, o_ref, *, lf, tile):
    i = pl.program_id(0)
    half = FFT_N // 2
    j = i * tile + lax.broadcasted_iota(jnp.int32, (tile, 1), 0)
    pos_i = jnp.where(j < half, j, FFT_N - 1 - j)
    valid = pos_i < lf
    pos = pos_i.astype(F32)
    t = pos / (lf - 1.0)
    w = (2.0 * math.pi / lf) * pos
    lane = lax.broadcasted_iota(jnp.int32, (1, 128), 1)
    band = jnp.where(lane <= FILTER_BANDS, lane - 1, lane - 1 - FILTER_BANDS).astype(F32)
    f = 1e-4 + band * ((FILTER_BANDS - 1 - 1e-4) / (FILTER_BANDS - 1))
    ang = f * w
    z = jnp.where(lane == 0, t,
                  jnp.where(lane <= FILTER_BANDS, jnp.cos(ang),
                            jnp.where(lane <= 2 * FILTER_BANDS, -jnp.sin(ang), 0.0)))
    fr = fr_ref[...]
    hdot = lambda a, b: jnp.dot(a, b, precision=HIGHEST, preferred_element_type=F32)
    h = jnp.sin(fr * (hdot(z, w1_ref[...]) + b1_ref[...]))
    h = jnp.sin(fr * (hdot(h, w2_ref[...]) + b2_ref[...]))
    h = jnp.sin(fr * (hdot(h, w3_ref[...]) + b3_ref[...]))
    y = hdot(h, wo_ref[...])
    ch = lax.broadcasted_iota(jnp.int32, (1, HYENA_ORDER * D_HYENA), 1) % D_HYENA
    delta = jnp.abs(MIN_DECAY + ch.astype(F32) * ((MAX_DECAY - MIN_DECAY) / (D_HYENA - 1)))
    y = y * jnp.exp(-t * delta)
    o_ref[...] = jnp.where(valid, y, 0.0)


def hyena_filter_taps(lf, w1, b1, w2, b2, w3, b3, freq, w_out, tile=1024):
    w1p = jnp.zeros((128, FILTER_WIDTH), F32).at[:FILTER_EMB].set(w1)
    wo = w_out.reshape(FILTER_WIDTH, HYENA_ORDER, 2, D_HYENA).transpose(2, 0, 1, 3).reshape(2, FILTER_WIDTH, HYENA_ORDER * D_HYENA)
    nt = FFT_N // tile
    const = lambda shape: pl.BlockSpec(shape, lambda i: (0,) * len(shape))
    row = lambda a: a.reshape(1, FILTER_WIDTH)
    return pl.pallas_call(
        functools.partial(_filter_kernel, lf=lf, tile=tile),
        grid=(nt,),
        in_specs=[const((128, FILTER_WIDTH)), const((1, FILTER_WIDTH)), const((FILTER_WIDTH, FILTER_WIDTH)),
                  const((1, FILTER_WIDTH)), const((FILTER_WIDTH, FILTER_WIDTH)), const((1, FILTER_WIDTH)),
                  const((1, FILTER_WIDTH)),
                  pl.BlockSpec((None, FILTER_WIDTH, HYENA_ORDER * D_HYENA), lambda i: (i // (nt // 2), 0, 0))],
        out_specs=pl.BlockSpec((tile, HYENA_ORDER * D_HYENA), lambda i: (i, 0)),
        out_shape=jax.ShapeDtypeStruct((FFT_N, HYENA_ORDER * D_HYENA), F32),
        compiler_params=_cparams(("parallel",)),
        name="hyena_filter",
    )(w1p, row(b1), w2, row(b2), w3, row(b3), row(freq), wo)


def _shortconv_kernel(z_ref, w_ref, b_ref, o_ref, zp_ref, *, L):
    C = z_ref.shape[-1]
    zp_ref[0:8, :] = jnp.zeros((8, C), F32)
    zp_ref[L + 8:L + 16, :] = jnp.zeros((8, C), F32)
    zp_ref[8:L + 8, :] = z_ref[...]
    ch = min(L, 512)
    for c in range(L // ch):
        s = c * ch
        o_ref[s:s + ch, :] = (b_ref[...] + zp_ref[s + 7:s + 7 + ch, :] * w_ref[0:1, :]
                              + zp_ref[s + 8:s + 8 + ch, :] * w_ref[1:2, :]
                              + zp_ref[s + 9:s + 9 + ch, :] * w_ref[2:3, :])


def short_conv(z, w, b, cb=128):
    B, L, C = z.shape
    return pl.pallas_call(
        functools.partial(_shortconv_kernel, L=L),
        grid=(B, C // cb),
        in_specs=[pl.BlockSpec((None, L, cb), lambda b_, c: (b_, 0, c)),
                  pl.BlockSpec((3, cb), lambda b_, c: (0, c)),
                  pl.BlockSpec((1, cb), lambda b_, c: (0, c))],
        out_specs=pl.BlockSpec((None, L, cb), lambda b_, c: (b_, 0, c)),
        out_shape=jax.ShapeDtypeStruct((B, L, C), F32),
        scratch_shapes=[pltpu.VMEM((L + 16, cb), F32)],
        compiler_params=_cparams(("parallel", "parallel")),
        name="short_conv",
    )(z, w, b.reshape(1, C))


@functools.lru_cache(maxsize=None)
def _fft_tables(n1_in):
    n2 = np.arange(FFT_R)[:, None, None]
    k1 = np.arange(FFT_K1)[None, :, None]
    n1 = np.arange(n1_in)[None, None, :]
    th = 2.0 * np.pi * ((k1 * (FFT_R * n1 + n2)) % FFT_N) / FFT_N
    ga = np.zeros((FFT_R, 2 * FFT_K1P, n1_in))
    ga[:, :FFT_K1] = np.cos(th)
    ga[:, FFT_K1P:FFT_K1P + FFT_K1] = -np.sin(th)
    wk = np.where((np.arange(FFT_K1) == 0) | (np.arange(FFT_K1) == FFT_R // 2), 1.0, 2.0)[None, None, :]
    thd = th.transpose(0, 2, 1)
    gd = np.zeros((FFT_R, n1_in, 2 * FFT_K1P))
    gd[:, :, :FFT_K1] = wk * np.cos(thd)
    gd[:, :, FFT_K1P:FFT_K1P + FFT_K1] = -wk * np.sin(thd)
    a = np.arange(FFT_R)
    ph = 2.0 * np.pi * ((a[:, None] * a[None, :]) % FFT_R) / FFT_R
    c, s = np.cos(ph), np.sin(ph)
    fb = np.block([[c, s], [-s, c]])
    fc = np.block([[c, -s], [s, c]])
    return ga, gd, fb, fc


FFT_KG = 5
FFT_UNROLL = 8


def _spectrum_slabs(ya_ref, fb, k0):
    cols = []
    for g in range(FFT_KG):
        re = ya_ref[pl.ds(k0 + g, FFT_R, stride=2 * FFT_K1P), :]
        im = ya_ref[pl.ds(FFT_K1P + k0 + g, FFT_R, stride=2 * FFT_K1P), :]
        cols.append(jnp.concatenate([re, im], axis=0).astype(BF16))
    return _dot(fb, jnp.concatenate(cols, axis=1))


def _stage_a(x_ref, ga_ref, ya_ref, n1_in):
    def body(n2, carry):
        xs = x_ref[pl.ds(n2, n1_in, stride=FFT_R), :].astype(BF16)
        ya_ref[pl.ds(pl.multiple_of(n2 * 2 * FFT_K1P, 8), 2 * FFT_K1P), :] = _dot(ga_ref[n2], xs)
        return carry
    lax.fori_loop(0, FFT_R, body, 0, unroll=FFT_UNROLL)


def _spectrum_kernel(x_ref, ga_ref, fb_ref, o_ref, ya_ref):
    _stage_a(x_ref, ga_ref, ya_ref, FFT_R)
    fb = fb_ref[...]
    cb = ya_ref.shape[1]

    def body(kg, carry):
        x = _spectrum_slabs(ya_ref, fb, kg * FFT_KG)
        for g in range(FFT_KG):
            o_ref[kg * FFT_KG + g] = x[:, g * cb:(g + 1) * cb]
        return carry
    lax.fori_loop(0, FFT_K1 // FFT_KG, body, 0)


def filter_spectrum(taps, cb=128):
    C = taps.shape[1]
    ga, _, fb, _ = _fft_tables(FFT_R)
    return pl.pallas_call(
        _spectrum_kernel,
        grid=(C // cb,),
        in_specs=[pl.BlockSpec((FFT_N, cb), lambda c: (0, c)),
                  pl.BlockSpec((FFT_R, 2 * FFT_K1P, FFT_R), lambda c: (0, 0, 0)),
                  pl.BlockSpec((2 * FFT_R, 2 * FFT_R), lambda c: (0, 0))],
        out_specs=pl.BlockSpec((FFT_K1, 2 * FFT_R, cb), lambda c: (0, 0, c)),
        out_shape=jax.ShapeDtypeStruct((FFT_K1, 2 * FFT_R, C), F32),
        scratch_shapes=[pltpu.VMEM((FFT_R * 2 * FFT_K1P, cb), F32)],
        compiler_params=_cparams(("parallel",), VMEM_LIMIT),
        name="filter_spectrum",
    )(taps, jnp.asarray(ga, BF16), jnp.asarray(fb, BF16))


def _longconv_kernel(u_ref, gate_ref, bias_ref, spec_ref, ga_ref, gd_ref, fb_ref, fc_ref, o_ref, ya_ref):
    n1_in = FFT_R // 2
    _stage_a(u_ref, ga_ref, ya_ref, n1_in)
    fb = fb_ref[...]
    fc = fc_ref[...]

    cb = ya_ref.shape[1]

    def mid(kg, carry):
        k0 = kg * FFT_KG
        x = _spectrum_slabs(ya_ref, fb, k0)
        s = jnp.concatenate([spec_ref[k0 + g] for g in range(FFT_KG)], axis=1)
        xr, xi = x[:FFT_R], x[FFT_R:]
        sr, si = s[:FFT_R], s[FFT_R:]
        z = jnp.concatenate([xr * sr - xi * si, xr * si + xi * sr], axis=0).astype(BF16)
        v = _dot(fc, z)
        for g in range(FFT_KG):
            ya_ref[pl.ds(k0 + g, FFT_R, stride=2 * FFT_K1P), :] = v[:FFT_R, g * cb:(g + 1) * cb]
            ya_ref[pl.ds(FFT_K1P + k0 + g, FFT_R, stride=2 * FFT_K1P), :] = v[FFT_R:, g * cb:(g + 1) * cb]
        return carry
    lax.fori_loop(0, FFT_K1 // FFT_KG, mid, 0)

    bias = bias_ref[...]

    def last(n2, carry):
        slab = ya_ref[pl.ds(pl.multiple_of(n2 * 2 * FFT_K1P, 8), 2 * FFT_K1P), :].astype(BF16)
        y = _dot(gd_ref[n2], slab) * (1.0 / FFT_N)
        rows = pl.ds(n2, n1_in, stride=FFT_R)
        o_ref[rows, :] = gate_ref[rows, :] * (y + u_ref[rows, :] * bias)
        return carry
    lax.fori_loop(0, FFT_R, last, 0, unroll=FFT_UNROLL)


def gated_long_conv(u, u_col, gate, gate_col, bias, spec, spec_col, C=D_HYENA, cb=128):
    B, L, _ = u.shape
    n1_in = FFT_R // 2
    ga, gd, fb, fc = _fft_tables(n1_in)
    once = pl.Buffered(1)
    seq = lambda col: pl.BlockSpec((None, L, cb), lambda c, b: (b, 0, c + col // cb))
    return pl.pallas_call(
        _longconv_kernel,
        grid=(C // cb, B),
        in_specs=[seq(u_col), seq(gate_col),
                  pl.BlockSpec((1, cb), lambda c, b: (0, c)),
                  pl.BlockSpec((FFT_K1, 2 * FFT_R, cb), lambda c, b: (0, 0, c + spec_col // cb), pipeline_mode=once),
                  pl.BlockSpec((FFT_R, 2 * FFT_K1P, n1_in), lambda c, b: (0, 0, 0), pipeline_mode=once),
                  pl.BlockSpec((FFT_R, n1_in, 2 * FFT_K1P), lambda c, b: (0, 0, 0), pipeline_mode=once),
                  pl.BlockSpec((2 * FFT_R, 2 * FFT_R), lambda c, b: (0, 0)),
                  pl.BlockSpec((2 * FFT_R, 2 * FFT_R), lambda c, b: (0, 0))],
        out_specs=seq(0),
        out_shape=jax.ShapeDtypeStruct((B, L, C), F32),
        scratch_shapes=[pltpu.VMEM((FFT_R * 2 * FFT_K1P, cb), F32)],
        compiler_params=_cparams(("parallel", "arbitrary"), VMEM_LIMIT),
        name="long_conv",
    )(u, gate, bias.reshape(1, C), spec, jnp.asarray(ga, BF16), jnp.asarray(gd, BF16),
      jnp.asarray(fb, BF16), jnp.asarray(fc, BF16))


SSM_N = SSM_GROUPS * SSM_STATE
S5_SEQS = 8
S5_CHUNK = 128


def _s5_kernel(uf_ref, ub_ref, bf_ref, bb_ref, lam_ref, cf_ref, cb_ref, yf_ref, yb_ref,
               buf_ref, bub_ref, xsf_ref, xsb_ref, st_ref):
    half = S5_SEQS // 2
    tiles = S5_CHUNK // 2

    @pl.when(pl.program_id(0) == 0)
    def _():
        st_ref[...] = jnp.zeros_like(st_ref)

    buf_ref[...] = _dot(uf_ref[...], bf_ref[...])
    bub_ref[...] = _dot(ub_ref[...], bb_ref[...])
    lo = lax.broadcasted_iota(jnp.int32, (S5_SEQS, 1), 0) < half

    def advance(x, lam, b):
        xr, xi = x
        lr, li = lam[:, 0:SSM_N], lam[:, SSM_N:2 * SSM_N]
        return lr * xr - li * xi + b[:, 0:SSM_N], lr * xi + li * xr + b[:, SSM_N:2 * SSM_N]

    def swap(x):
        return pltpu.roll(x[0], half, axis=0), pltpu.roll(x[1], half, axis=0)

    def step(t4, x):
        f_tiles, b_tiles = [], []
        for p in range(2):
            tf = buf_ref[pl.ds(pl.multiple_of((2 * t4 + p) * S5_SEQS, S5_SEQS), S5_SEQS), :]
            tb = bub_ref[pl.ds(pl.multiple_of((tiles - 1 - 2 * t4 - p) * S5_SEQS, S5_SEQS), S5_SEQS), :]
            xe = advance(x, lam_ref[0], jnp.where(lo, tf, tb))
            xo = advance(swap(xe), lam_ref[1], jnp.where(lo, tb, tf))
            x = swap(xo)
            xe = jnp.concatenate(xe, axis=1)
            xo = jnp.concatenate(xo, axis=1)
            f_tiles.append(jnp.where(lo, xe, xo))
            b_tiles.append(jnp.where(lo, xo, xe))
        xsf_ref[pl.ds(pl.multiple_of(t4 * 2 * S5_SEQS, 2 * S5_SEQS), 2 * S5_SEQS), :] = (
            jnp.concatenate(f_tiles, axis=0).astype(BF16))
        xsb_ref[pl.ds(pl.multiple_of((tiles - 2 - 2 * t4) * S5_SEQS, 2 * S5_SEQS), 2 * S5_SEQS), :] = (
            jnp.concatenate(b_tiles[::-1], axis=0).astype(BF16))
        return x

    xr, xi = lax.fori_loop(0, S5_CHUNK // 4, step, (st_ref[:, 0:SSM_N], st_ref[:, SSM_N:2 * SSM_N]))
    st_ref[:, 0:SSM_N] = xr
    st_ref[:, SSM_N:2 * SSM_N] = xi
    yf_ref[...] = _dot(xsf_ref[...], cf_ref[...])
    yb_ref[...] = _dot(xsb_ref[...], cb_ref[...])


def _s5_matrices(lam_re, lam_im, log_step, b_re, b_im, c_re, c_im):
    G, P, Hs = SSM_GROUPS, SSM_STATE, SSM_GROUP_WIDTH
    dt = jnp.exp(log_step)[:, :, None]
    mag = jnp.exp(lam_re * dt)
    ar, ai = mag * jnp.cos(lam_im * dt), mag * jnp.sin(lam_im * dt)
    er, ei = ar - 1.0, ai
    den = lam_re * lam_re + lam_im * lam_im
    cr, ci = (er * lam_re + ei * lam_im) / den, (ei * lam_re - er * lam_im) / den
    bbr = cr[..., None] * b_re - ci[..., None] * b_im
    bbi = cr[..., None] * b_im + ci[..., None] * b_re
    eye = jnp.eye(G, dtype=F32)
    blk_in = lambda m: jnp.einsum('gq,qph->ghqp', eye, m).reshape(G * Hs, G * P)
    bmat = [jnp.concatenate([blk_in(bbr[d]), blk_in(bbi[d])], axis=1).astype(BF16) for d in range(2)]
    blk_out = lambda m: jnp.einsum('gq,qhp->gpqh', eye, m).reshape(G * P, G * Hs)
    cmat = [jnp.concatenate([blk_out(c_re[d]), -blk_out(c_im[d])], axis=0).astype(BF16) for d in range(2)]
    lam = jnp.concatenate([ar.reshape(2, 1, G * P), ai.reshape(2, 1, G * P)], axis=2)
    lam = jnp.broadcast_to(lam, (2, S5_SEQS // 2, 2 * G * P))
    lam8 = jnp.stack([lam.reshape(S5_SEQS, 2 * G * P), lam[::-1].reshape(S5_SEQS, 2 * G * P)])
    return bmat[0], bmat[1], lam8, cmat[0], cmat[1]


def s5_scan(u_c, u_x, lam_re, lam_im, log_step, b_re, b_im, c_re, c_im):
    B, Lc, _ = u_c.shape
    L = u_x.shape[1]
    T = Lc + L
    assert 2 * B == S5_SEQS and T % S5_CHUNK == 0
    assert Lc % S5_CHUNK == 0
    bf, bb, lam8, cf, cb = _s5_matrices(lam_re, lam_im, log_step, b_re, b_im, c_re, c_im)
    half = S5_SEQS // 2
    un = jnp.concatenate([u_c, u_x], axis=1).astype(BF16).transpose(1, 0, 2).reshape(T * half, D_SSM)
    rows = S5_CHUNK * half
    n, nc = T // S5_CHUNK, Lc // S5_CHUNK
    fwd = pl.BlockSpec((rows, D_SSM), lambda i: (i, 0))
    bwd = pl.BlockSpec((rows, D_SSM), lambda i: (jnp.where(i < nc, nc - 1 - i, n + nc - 1 - i), 0))
    const = lambda shape: pl.BlockSpec(shape, lambda i: (0,) * len(shape))
    yf, yb = pl.pallas_call(
        _s5_kernel,
        grid=(n,),
        in_specs=[fwd, bwd, const((D_SSM, 2 * SSM_N)), const((D_SSM, 2 * SSM_N)),
                  const((2, S5_SEQS, 2 * SSM_N)), const((2 * SSM_N, D_SSM)), const((2 * SSM_N, D_SSM))],
        out_specs=[fwd, bwd],
        out_shape=[jax.ShapeDtypeStruct((T * half, D_SSM), F32)] * 2,
        scratch_shapes=[pltpu.VMEM((rows, 2 * SSM_N), F32), pltpu.VMEM((rows, 2 * SSM_N), F32),
                        pltpu.VMEM((rows, 2 * SSM_N), BF16), pltpu.VMEM((rows, 2 * SSM_N), BF16),
                        pltpu.VMEM((S5_SEQS, 2 * SSM_N), F32)],
        compiler_params=_cparams(("arbitrary",), VMEM_LIMIT),
        name="s5_scan",
    )(un, un, bf, bb, lam8, cf, cb)
    yf = yf.reshape(T, B, D_SSM).transpose(1, 0, 2)
    yb = yb.reshape(T, B, D_SSM).transpose(1, 0, 2)
    return (yf[:, :Lc], yb[:, :Lc]), (yf[:, Lc:], yb[:, Lc:])


def _gelu_tanh(y):
    return 0.5 * y * (1.0 + jnp.tanh(math.sqrt(2.0 / math.pi) * (y + 0.044715 * (y * y * y))))


def _outproj_kernel(*refs, route):
    (x_ref, mod_ref, att_ref, hy_ref, yf_ref, yb_ref, u_ref, dsk_ref, wg_ref, bg_ref, wo_ref, g2_ref) = refs[:12]
    if route:
        wr_ref, x1_ref, h2_ref, rt_ref = refs[12:]
    else:
        x1_ref, h2_ref = refs[12:]
    y = yf_ref[...] + yb_ref[...] + u_ref[...] * dsk_ref[...]
    y = _gelu_tanh(y)
    ss = y * jax.nn.sigmoid(_dot(y.astype(BF16), wg_ref[...]) + bg_ref[...])
    mix = (_dot(att_ref[...], wo_ref[0:D_ATT, :])
           + _dot(hy_ref[...].astype(BF16), wo_ref[D_ATT:D_ATT + D_HYENA, :])
           + _dot(ss.astype(BF16), wo_ref[D_ATT + D_HYENA:, :]))
    x1 = x_ref[...] + mod_ref[2:3, :] * mix
    x1_ref[...] = x1
    h2 = _rms_mod(x1, g2_ref[...], mod_ref[3:4, :], mod_ref[4:5, :])
    h2_ref[...] = h2.astype(h2_ref.dtype)
    if route:
        lane = lax.broadcasted_iota(jnp.int32, (1, 128), 1)
        logits = jnp.dot(h2, wr_ref[...], precision=HIGHEST, preferred_element_type=F32)
        lg = jnp.where(lane < N_EXPERTS, logits, -jnp.inf)
        v1 = lg.max(axis=-1, keepdims=True)
        i1 = jnp.where(lg == v1, lane, 128).min(axis=-1, keepdims=True)
        lg2 = jnp.where(lane == i1, -jnp.inf, lg)
        v2 = lg2.max(axis=-1, keepdims=True)
        i2 = jnp.where(lg2 == v2, lane, 128).min(axis=-1, keepdims=True)
        e = jnp.exp(v2 - v1)
        w1 = 1.0 / (1.0 + e)
        rt_ref[...] = jnp.where(lane == 0, i1.astype(F32),
                                jnp.where(lane == 1, i2.astype(F32),
                                          jnp.where(lane == 2, w1, jnp.where(lane == 3, e * w1, 0.0))))


def out_proj(x, mod, att, hy, yf, yb, u, d_skip, w_glu, b_glu, w_out_bf, g2, router_w=None, tm=256):
    B, T, D = x.shape
    route = router_w is not None
    tok = lambda n: pl.BlockSpec((None, tm, n), lambda b, i: (b, i, 0))
    const = lambda shape: pl.BlockSpec(shape, lambda b, i: (0,) * len(shape))
    in_specs = [tok(D), pl.BlockSpec((None, 6, D), lambda b, i: (b, 0, 0)), tok(D_ATT), tok(D_HYENA),
                tok(D_SSM), tok(D_SSM), tok(D_SSM), const((1, D_SSM)), const((D_SSM, D_SSM)), const((1, D_SSM)),
                const((D, D)), const((1, D))]
    args = [x, mod, att, hy, yf, yb, u, d_skip.reshape(1, D_SSM), w_glu.astype(BF16), b_glu.reshape(1, D_SSM),
            w_out_bf, g2.reshape(1, D)]
    out_specs = [tok(D), tok(D)]
    out_shape = [jax.ShapeDtypeStruct((B, T, D), F32), jax.ShapeDtypeStruct((B, T, D), BF16)]
    if route:
        in_specs.append(const((D, 128)))
        args.append(jnp.zeros((D, 128), F32).at[:, :N_EXPERTS].set(router_w))
        out_specs.append(tok(128))
        out_shape.append(jax.ShapeDtypeStruct((B, T, 128), F32))
    return pl.pallas_call(
        functools.partial(_outproj_kernel, route=route),
        grid=(B, T // tm),
        in_specs=in_specs, out_specs=out_specs, out_shape=out_shape,
        compiler_params=_cparams(("parallel", "parallel"), VMEM_LIMIT),
        name="out_proj",
    )(*args)


FF_SPLIT = 1280


def _swiglu_tile(h, w1_ref, w3_ref, w2_ref):
    acc = None
    for lo, hi in ((0, FF_SPLIT), (FF_SPLIT, D_FF)):
        a = _dot(h, w1_ref[:, lo:hi])
        g = (a * jax.nn.sigmoid(a) * _dot(h, w3_ref[:, lo:hi])).astype(BF16)
        part = _dot(g, w2_ref[lo:hi, :])
        acc = part if acc is None else acc + part
    return acc


def _ffn_kernel(h_ref, x_ref, mod_ref, w1_ref, w3_ref, w2_ref, o_ref):
    o_ref[...] = x_ref[...] + mod_ref[5:6, :] * _swiglu_tile(h_ref[...], w1_ref, w3_ref, w2_ref)


def ffn_dense(h2, x1, mod, w1, w3, w2, tm=256):
    B, T, D = x1.shape
    tm = min(tm, T)
    tok = pl.BlockSpec((None, tm, D), lambda b, i: (b, i, 0))
    once = pl.Buffered(1)
    return pl.pallas_call(
        _ffn_kernel,
        grid=(B, T // tm),
        in_specs=[tok, tok, pl.BlockSpec((None, 6, D), lambda b, i: (b, 0, 0)),
                  pl.BlockSpec((D, D_FF), lambda b, i: (0, 0), pipeline_mode=once),
                  pl.BlockSpec((D, D_FF), lambda b, i: (0, 0), pipeline_mode=once),
                  pl.BlockSpec((D_FF, D), lambda b, i: (0, 0), pipeline_mode=once)],
        out_specs=tok,
        out_shape=jax.ShapeDtypeStruct((B, T, D), F32),
        compiler_params=_cparams(("parallel", "parallel"), VMEM_LIMIT),
        name="ffn_dense",
    )(h2, x1, mod, w1, w3, w2)


MOE_TM = 512


def _moe_kernel(te_ref, tv_ref, h_ref, w1_ref, w3_ref, w2_ref, o_ref):
    @pl.when(tv_ref[pl.program_id(0)] > 0)
    def _():
        o_ref[...] = _swiglu_tile(h_ref[...], w1_ref, w3_ref, w2_ref)


def moe_experts(hs, tile_expert, tile_valid, w1, w3, w2):
    P, D = hs.shape
    nt = P // MOE_TM
    tok = pl.BlockSpec((MOE_TM, D), lambda i, te, tv: (i, 0))
    return pl.pallas_call(
        _moe_kernel,
        grid_spec=pltpu.PrefetchScalarGridSpec(
            num_scalar_prefetch=2,
            grid=(nt,),
            in_specs=[tok,
                      pl.BlockSpec((None, D, D_FF), lambda i, te, tv: (te[i], 0, 0)),
                      pl.BlockSpec((None, D, D_FF), lambda i, te, tv: (te[i], 0, 0)),
                      pl.BlockSpec((None, D_FF, D), lambda i, te, tv: (te[i], 0, 0))],
            out_specs=tok),
        out_shape=jax.ShapeDtypeStruct((P, D), F32),
        compiler_params=_cparams(("arbitrary",), VMEM_LIMIT),
        name="moe_experts",
    )(tile_expert, tile_valid, hs, w1, w3, w2)


def _combine_kernel(x_ref, mod_ref, ya_ref, yb_ref, rt_ref, o_ref):
    wa = rt_ref[:, 2:3]
    wb = rt_ref[:, 3:4]
    o_ref[...] = x_ref[...] + mod_ref[5:6, :] * (wa * ya_ref[...] + wb * yb_ref[...])


def moe_combine(x1, mod, ya, yb, rt, tm=512):
    B, T, D = x1.shape
    tok = lambda n: pl.BlockSpec((None, tm, n), lambda b, i: (b, i, 0))
    return pl.pallas_call(
        _combine_kernel,
        grid=(B, T // tm),
        in_specs=[tok(D), pl.BlockSpec((None, 6, D), lambda b, i: (b, 0, 0)), tok(D), tok(D), tok(128)],
        out_specs=tok(D),
        out_shape=jax.ShapeDtypeStruct((B, T, D), F32),
        compiler_params=_cparams(("parallel", "parallel")),
        name="moe_combine",
    )(x1, mod, ya, yb, rt)


def moe_layer(h2, x1, mod, rt, w1, w3, w2):
    B, T, D = x1.shape
    n = B * T
    e_flat = jnp.concatenate([rt[..., 0].reshape(n), rt[..., 1].reshape(n)]).astype(jnp.int32)
    tok = jnp.concatenate([jnp.arange(n, dtype=jnp.int32)] * 2)
    order = jnp.argsort(e_flat, stable=True)
    sorted_e = e_flat[order]
    counts = jnp.sum(jax.nn.one_hot(e_flat, N_EXPERTS, dtype=jnp.int32), axis=0)
    padded = ((counts + MOE_TM - 1) // MOE_TM) * MOE_TM
    pend = jnp.cumsum(padded)
    pstart = pend - padded
    start = jnp.cumsum(counts) - counts
    dest = pstart[sorted_e] + jnp.arange(2 * n, dtype=jnp.int32) - start[sorted_e]
    P = 2 * n + N_EXPERTS * MOE_TM
    nt = P // MOE_TM
    src = jnp.zeros((P,), jnp.int32).at[dest].set(tok[order])
    slot = jnp.zeros((2 * n,), jnp.int32).at[order].set(dest)
    tile_row = jnp.arange(nt, dtype=jnp.int32) * MOE_TM
    tile_expert = jnp.minimum(jnp.searchsorted(pend, tile_row, side='right'), N_EXPERTS - 1).astype(jnp.int32)
    tile_valid = (tile_row < pend[-1]).astype(jnp.int32)
    hs = jnp.take(h2.reshape(n, D), src, axis=0)
    ys = moe_experts(hs, tile_expert, tile_valid, w1, w3, w2)
    ya = jnp.take(ys, slot[:n], axis=0).reshape(B, T, D)
    yb = jnp.take(ys, slot[n:], axis=0).reshape(B, T, D)
    return moe_combine(x1, mod, ya, yb, rt)


def _hyena(hz, conv_w, conv_b, spec, bias):
    zc = short_conv(hz, conv_w, conv_b)
    B, L, C = zc.shape
    lmax = FFT_N // 2
    if L < lmax:
        slot = lmax // B
        assert slot >= 2 * L
        zc = jnp.pad(zc, ((0, 0), (0, slot - L), (0, 0))).reshape(1, lmax, C)
    y = gated_long_conv(zc, 0, zc, D_HYENA, bias[0], spec, 0)
    y = gated_long_conv(y, 0, zc, 2 * D_HYENA, bias[1], spec, D_HYENA)
    if L < lmax:
        y = y.reshape(B, lmax // B, D_HYENA)[:, :L]
    return y


def _split_route(outs, mod):
    x1, h2 = outs[0], outs[1]
    return h2, x1, mod, (outs[2] if len(outs) > 2 else None)


def kernel(x, c, ctx, c_ctx, w_ada, b_ada, g_norm1, g_norm2, w_in, w_out, g_q, g_k, rpb, hy_conv_w, hy_conv_b, filt_w1, filt_b1, filt_w2, filt_b2, filt_w3, filt_b3, filt_freq, filt_w_out, hy_bias, lam_re, lam_im, log_step, b_re, b_im, c_re, c_im, d_skip, w_glu, b_glu, ffn_w1, ffn_w3, ffn_w2, router_w, moe_w1, moe_w3, moe_w2):
    B, L, D = x.shape
    Lc = ctx.shape[1]
    c8 = jnp.zeros((8, D), F32).at[:B].set(c).at[B].set(c_ctx)
    mod_all = adaln_mod(c8, w_ada, b_ada)
    for layer in range(DEPTH):
        last = layer == DEPTH - 1
        m = mod_all[layer].reshape(8, 6, D)
        mod_x = m[:B]
        mod_c = jnp.broadcast_to(m[B], (B, 6, D))
        w_l = w_in[layer].astype(BF16)
        wo_l = w_out[layer].astype(BF16)
        filt = (filt_w1[layer], filt_b1[layer], filt_w2[layer], filt_b2[layer],
                filt_w3[layer], filt_b3[layer], filt_freq[layer], filt_w_out[layer])
        ssm = (lam_re[layer], lam_im[layer], log_step[layer], b_re[layer], b_im[layer], c_re[layer], c_im[layer])

        q_c, k_c, v_c, hz_c, u_c = in_proj(ctx, mod_c, g_norm1[layer], w_l, g_q[layer], g_k[layer])
        q, k, v, hz, u = in_proj(x, mod_x, g_norm1[layer], w_l, g_q[layer], g_k[layer])
        att = neighbourhood_attention(q, k, v, k_c, v_c, rpb[layer])
        spec = filter_spectrum(hyena_filter_taps(L, *filt))
        hy = _hyena(hz, hy_conv_w[layer], hy_conv_b[layer], spec, hy_bias[layer])
        (yf_c, yb_c), (yf, yb) = s5_scan(u_c, u, *ssm)

        tail = (d_skip[layer], w_glu[layer], b_glu[layer], wo_l, g_norm2[layer])
        i = layer // 2
        if layer % 2 == 0:
            ffw = (ffn_w1[i].astype(BF16), ffn_w3[i].astype(BF16), ffn_w2[i].astype(BF16))
            mixer = lambda h2, x1, mod, rt: ffn_dense(h2, x1, mod, *ffw)
            rw = None
        else:
            mow = (moe_w1[i].astype(BF16), moe_w3[i].astype(BF16), moe_w2[i].astype(BF16))
            mixer = lambda h2, x1, mod, rt: moe_layer(h2, x1, mod, rt, *mow)
            rw = router_w[i]
        x = mixer(*_split_route(out_proj(x, mod_x, att, hy, yf, yb, u, *tail, router_w=rw), mod_x))

        if not last:
            att_c = context_attention(q_c, k_c, v_c)
            spec_c = filter_spectrum(hyena_filter_taps(Lc, *filt))
            hy_c = _hyena(hz_c, hy_conv_w[layer], hy_conv_b[layer], spec_c, hy_bias[layer])
            ctx = mixer(*_split_route(out_proj(ctx, mod_c, att_c, hy_c, yf_c, yb_c, u_c, *tail, router_w=rw), mod_c))
    return x
```

```python
import functools
import math

import numpy as np
import jax
import jax.numpy as jnp
from jax import lax
from jax.experimental import pallas as pl
from jax.experimental.pallas import tpu as pltpu

F32 = jnp.float32
BF16 = jnp.bfloat16
HIGHEST = lax.Precision.HIGHEST

D_MODEL = 1024
DEPTH = 2
GRID_W = 64
ATT_HEAD_DIM = 64
ATT_HEADS = 8
D_ATT = 512
WIN_ROWS = 8
WIN_COLS = 16
D_HYENA = 256
HYENA_ORDER = 2
FILTER_EMB = 33
FILTER_BANDS = 16
FILTER_WIDTH = 64
MIN_DECAY = math.log(1e-2) / 1.5
MAX_DECAY = math.log(1e-2) / 0.3
D_SSM = 256
SSM_GROUP_WIDTH = 16
SSM_GROUPS = 16
SSM_STATE = 64
COL_K = D_ATT
COL_V = 2 * D_ATT
COL_HY = 3 * D_ATT
COL_SSM = COL_HY + 3 * D_HYENA
D_IN = COL_SSM + D_SSM
D_FF = 2816
N_EXPERTS = 8
EPS = 1e-6
NEG = -1e30

VMEM_LIMIT = 56 * 1024 * 1024


def _cparams(sem, vmem=None):
    return pltpu.CompilerParams(dimension_semantics=sem, vmem_limit_bytes=vmem)


def _dot(a, b):
    return jnp.dot(a, b, preferred_element_type=F32)


def _dot_nt(a, b):
    return lax.dot_general(a, b, (((1,), (1,)), ((), ())), preferred_element_type=F32)


def _mod_kernel(c_ref, w_ref, b_ref, o_ref):
    c = c_ref[...]
    s = c * jax.nn.sigmoid(c)
    o_ref[...] = jnp.dot(s, w_ref[...], precision=HIGHEST, preferred_element_type=F32) + b_ref[...]


def adaln_mod(c8, w_ada, b_ada, tn=512):
    depth, d, n = w_ada.shape
    return pl.pallas_call(
        _mod_kernel,
        grid=(depth, n // tn),
        in_specs=[pl.BlockSpec((8, d), lambda l, j: (0, 0)),
                  pl.BlockSpec((None, d, tn), lambda l, j: (l, 0, j)),
                  pl.BlockSpec((None, 1, tn), lambda l, j: (l, 0, j))],
        out_specs=pl.BlockSpec((None, 8, tn), lambda l, j: (l, 0, j)),
        out_shape=jax.ShapeDtypeStruct((depth, 8, n), F32),
        compiler_params=_cparams(("parallel", "parallel")),
        name="adaln_mod",
    )(c8, w_ada, b_ada.reshape(depth, 1, n))


def _rms_mod(x, g, shift, scale):
    ms = jnp.mean(x * x, axis=-1, keepdims=True)
    h = x * lax.rsqrt(ms + EPS) * g
    return h * (1.0 + scale) + shift


def _head_norm(z, a, g):
    zz = (z * z).astype(BF16)
    m = jnp.concatenate([_dot(zz[:, c:c + ATT_LANES], a) for c in range(0, z.shape[1], ATT_LANES)], axis=1)
    return z * lax.rsqrt(m + EPS) * g


def _inproj_kernel(x_ref, mod_ref, g_ref, w_ref, a_ref, gq_ref, gk_ref,
                   q_ref, k_ref, v_ref, hz_ref, u_ref):
    h = _rms_mod(x_ref[...], g_ref[...], mod_ref[0:1, :], mod_ref[1:2, :]).astype(BF16)
    a = a_ref[...]
    q = _dot(h, w_ref[:, 0:COL_K])
    q_ref[...] = (_head_norm(q, a, gq_ref[...]) * (ATT_HEAD_DIM ** -0.5)).astype(BF16)
    k = _dot(h, w_ref[:, COL_K:COL_V])
    k_ref[...] = _head_norm(k, a, gk_ref[...]).astype(BF16)
    v_ref[...] = _dot(h, w_ref[:, COL_V:COL_HY]).astype(BF16)
    hz_ref[...] = _dot(h, w_ref[:, COL_HY:COL_SSM])
    u_ref[...] = _dot(h, w_ref[:, COL_SSM:D_IN])


def in_proj(x, mod, g1, w_in_bf, gq, gk, tm=512):
    B, T, D = x.shape
    tm = min(tm, T)
    head_avg = jnp.asarray(np.kron(np.eye(ATT_LANES // ATT_HEAD_DIM),
                                   np.full((ATT_HEAD_DIM, ATT_HEAD_DIM), 1.0 / ATT_HEAD_DIM)), BF16)
    tok = lambda n: pl.BlockSpec((None, tm, n), lambda b, i: (b, i, 0))
    const = lambda shape: pl.BlockSpec(shape, lambda b, i: (0,) * len(shape))
    return pl.pallas_call(
        _inproj_kernel,
        grid=(B, T // tm),
        in_specs=[tok(D), pl.BlockSpec((None, 6, D), lambda b, i: (b, 0, 0)), const((1, D)),
                  const((D, D_IN)), const((ATT_LANES, ATT_LANES)), const((1, D_ATT)), const((1, D_ATT))],
        out_specs=[tok(D_ATT), tok(D_ATT), tok(D_ATT), tok(3 * D_HYENA), tok(D_SSM)],
        out_shape=[jax.ShapeDtypeStruct((B, T, D_ATT), BF16)] * 3
                  + [jax.ShapeDtypeStruct((B, T, 3 * D_HYENA), F32), jax.ShapeDtypeStruct((B, T, D_SSM), F32)],
        compiler_params=_cparams(("parallel", "parallel"), VMEM_LIMIT),
        name="in_proj",
    )(x, mod, g1.reshape(1, D), w_in_bf, head_avg,
      jnp.tile(gq, ATT_HEADS).reshape(1, D_ATT), jnp.tile(gk, ATT_HEADS).reshape(1, D_ATT))


ATT_R = 4
ATT_KROWS = ATT_R + WIN_ROWS - 1
ATT_LANES = 256


def _softmax_heads(q, parts, bias_ref, o_ref):
    lane = lax.broadcasted_iota(jnp.int32, (1, ATT_LANES), 1)
    acc = jnp.zeros((q.shape[0], ATT_LANES), F32)
    for h in range(ATT_LANES // ATT_HEAD_DIM):
        hm = (lane >= ATT_HEAD_DIM * h) & (lane < ATT_HEAD_DIM * (h + 1))
        qh = jnp.where(hm, q, jnp.zeros_like(q))
        ss = [_dot_nt(qh, kk) for kk, _ in parts]
        if bias_ref is not None:
            ss[0] = ss[0] + bias_ref[h].astype(F32)
        m = ss[0].max(axis=-1, keepdims=True)
        for s in ss[1:]:
            m = jnp.maximum(m, s.max(axis=-1, keepdims=True))
        ps = [jnp.exp(s - m) for s in ss]
        l = ps[0].sum(axis=-1, keepdims=True)
        for p in ps[1:]:
            l = l + p.sum(axis=-1, keepdims=True)
        o = _dot(ps[0].astype(BF16), parts[0][1])
        for p, (_, vv) in zip(ps[1:], parts[1:]):
            o = o + _dot(p.astype(BF16), vv)
        acc = jnp.where(hm, o / l, acc)
    o_ref[...] = acc.astype(o_ref.dtype)


def _nattn_kernel(q_ref, k_ref, v_ref, kc_ref, vc_ref, bias_ref, o_ref, *, rows):
    j = pl.program_id(2)
    ks = jnp.clip(j * ATT_R - WIN_ROWS // 2, 0, rows - ATT_KROWS)
    start = pl.multiple_of(ks * GRID_W, GRID_W)
    kl = k_ref[pl.ds(start, ATT_KROWS * GRID_W), :]
    vl = v_ref[pl.ds(start, ATT_KROWS * GRID_W), :]
    _softmax_heads(q_ref[...], [(kl, vl), (kc_ref[...], vc_ref[...])], bias_ref, o_ref)


def _attn_bias_tables(rpb, rows):
    nblk = rows // ATT_R
    qc = np.arange(GRID_W)
    c0 = np.clip(qc - WIN_COLS // 2, 0, GRID_W - WIN_COLS)
    kc = np.arange(GRID_W)
    col_ok = (kc[None, :] >= c0[:, None]) & (kc[None, :] < c0[:, None] + WIN_COLS)
    col_off = kc[None, :] - qc[:, None] + WIN_COLS - 1
    col_sel = (col_off[:, :, None] == np.arange(2 * WIN_COLS - 1)) & col_ok[:, :, None]
    band = jnp.einsum('hij,qkj->hiqk', rpb, jnp.asarray(col_sel, F32), precision=HIGHEST)
    band = jnp.where(jnp.asarray(col_ok)[None, None], band, NEG)
    tabs = []
    for blk in (0, 1, nblk - 1):
        r = blk * ATT_R + np.arange(ATT_R)
        ks = int(np.clip(blk * ATT_R - WIN_ROWS // 2, 0, rows - ATT_KROWS))
        kk = ks + np.arange(ATT_KROWS)
        r0 = np.clip(r - WIN_ROWS // 2, 0, rows - WIN_ROWS)
        row_ok = (kk[None, :] >= r0[:, None]) & (kk[None, :] < r0[:, None] + WIN_ROWS)
        row_off = kk[None, :] - r[:, None] + WIN_ROWS - 1
        row_sel = (row_off[:, :, None] == np.arange(2 * WIN_ROWS - 1)) & row_ok[:, :, None]
        t = jnp.einsum('rsi,hiqk->hrqsk', jnp.asarray(row_sel, F32), band, precision=HIGHEST)
        t = t + jnp.asarray(np.where(row_ok, 0.0, NEG), F32)[None, :, None, :, None]
        tabs.append(t.reshape(ATT_HEADS, ATT_R * GRID_W, ATT_KROWS * GRID_W))
    return jnp.stack(tabs).astype(BF16)


def neighbourhood_attention(q, k, v, kc, vc, rpb):
    B, L, _ = q.shape
    Lc = kc.shape[1]
    rows = L // GRID_W
    nblk = rows // ATT_R
    tq = ATT_R * GRID_W
    nkl = ATT_KROWS * GRID_W
    bias = _attn_bias_tables(rpb, rows)
    hpb = ATT_LANES // ATT_HEAD_DIM
    variant = lambda j: jnp.where(j == 0, 0, jnp.where(j == nblk - 1, 2, 1))
    return pl.pallas_call(
        functools.partial(_nattn_kernel, rows=rows),
        grid=(B, D_ATT // ATT_LANES, nblk),
        in_specs=[pl.BlockSpec((None, tq, ATT_LANES), lambda b, g, j: (b, j, g)),
                  pl.BlockSpec((None, L, ATT_LANES), lambda b, g, j: (b, 0, g)),
                  pl.BlockSpec((None, L, ATT_LANES), lambda b, g, j: (b, 0, g)),
                  pl.BlockSpec((None, Lc, ATT_LANES), lambda b, g, j: (b, 0, g)),
                  pl.BlockSpec((None, Lc, ATT_LANES), lambda b, g, j: (b, 0, g)),
                  pl.BlockSpec((None, hpb, tq, nkl), lambda b, g, j: (variant(j), g, 0, 0))],
        out_specs=pl.BlockSpec((None, tq, ATT_LANES), lambda b, g, j: (b, j, g)),
        out_shape=jax.ShapeDtypeStruct((B, L, D_ATT), BF16),
        compiler_params=_cparams(("parallel", "parallel", "arbitrary"), VMEM_LIMIT),
        name="nattn",
    )(q, k, v, kc, vc, bias)


def _cattn_kernel(q_ref, k_ref, v_ref, o_ref):
    _softmax_heads(q_ref[...], [(k_ref[...], v_ref[...])], None, o_ref)


def context_attention(q, k, v):
    B, Lc, _ = q.shape
    spec = pl.BlockSpec((None, Lc, ATT_LANES), lambda b, g: (b, 0, g))
    return pl.pallas_call(
        _cattn_kernel,
        grid=(B, D_ATT // ATT_LANES),
        in_specs=[spec, spec, spec],
        out_specs=spec,
        out_shape=jax.ShapeDtypeStruct((B, Lc, D_ATT), BF16),
        compiler_params=_cparams(("parallel", "parallel")),
        name="cattn",
    )(q, k, v)


FFT_N = 16384
FFT_R = 128
FFT_K1 = FFT_R // 2 + 1
FFT_K1P = 66
FFT_PITCH = 2 * FFT_K1P


def _filter_kernel(w1_ref, b1_ref, w2_ref, b2_ref, w3_ref, b3_ref, fr_ref, wo_ref, o_ref, *, lf, tile):
    i = pl.program_id(0)
    half = FFT_N // 2
    row0 = i * tile
    first_pos = jnp.where(row0 < half, row0, FFT_N - row0 - tile)

    @pl.when(first_pos >= lf)
    def _():
        o_ref[...] = jnp.zeros_like(o_ref)

    @pl.when(first_pos < lf)
    def _():
        j = row0 + lax.broadcasted_iota(jnp.int32, (tile, 1), 0)
        pos_i = jnp.where(j < half, j, FFT_N - 1 - j)
        valid = pos_i < lf
        pos = pos_i.astype(F32)
        t = pos / (lf - 1.0)
        w = (2.0 * math.pi / lf) * pos
        lane = lax.broadcasted_iota(jnp.int32, (1, 128), 1)
        band = jnp.where(lane <= FILTER_BANDS, lane - 1, lane - 1 - FILTER_BANDS).astype(F32)
        f = 1e-4 + band * ((FILTER_BANDS - 1 - 1e-4) / (FILTER_BANDS - 1))
        ang = f * w + jnp.where(lane <= FILTER_BANDS, 0.0, 0.5 * math.pi)
        z = jnp.where(lane == 0, t, jnp.where(lane <= 2 * FILTER_BANDS, jnp.cos(ang), 0.0))
        fr = fr_ref[...]
        hdot = lambda a, b: jnp.dot(a, b, precision=HIGHEST, preferred_element_type=F32)
        h = jnp.sin(fr * (hdot(z, w1_ref[...]) + b1_ref[...]))
        h = jnp.sin(fr * (hdot(h, w2_ref[...]) + b2_ref[...]))
        h = jnp.sin(fr * (hdot(h, w3_ref[...]) + b3_ref[...]))
        y = hdot(h, wo_ref[...])
        ch = lax.broadcasted_iota(jnp.int32, (1, HYENA_ORDER * D_HYENA), 1) % D_HYENA
        delta = jnp.abs(MIN_DECAY + ch.astype(F32) * ((MAX_DECAY - MIN_DECAY) / (D_HYENA - 1)))
        y = y * jnp.exp(-t * delta)
        o_ref[...] = jnp.where(valid, y, 0.0)


def hyena_filter_taps(lf, w1, b1, w2, b2, w3, b3, freq, w_out, tile=1024):
    w1p = jnp.zeros((128, FILTER_WIDTH), F32).at[:FILTER_EMB].set(w1)
    wo = w_out.reshape(FILTER_WIDTH, HYENA_ORDER, 2, D_HYENA).transpose(2, 0, 1, 3).reshape(2, FILTER_WIDTH, HYENA_ORDER * D_HYENA)
    nt = FFT_N // tile
    const = lambda shape: pl.BlockSpec(shape, lambda i: (0,) * len(shape))
    row = lambda a: a.reshape(1, FILTER_WIDTH)
    return pl.pallas_call(
        functools.partial(_filter_kernel, lf=lf, tile=tile),
        grid=(nt,),
        in_specs=[const((128, FILTER_WIDTH)), const((1, FILTER_WIDTH)), const((FILTER_WIDTH, FILTER_WIDTH)),
                  const((1, FILTER_WIDTH)), const((FILTER_WIDTH, FILTER_WIDTH)), const((1, FILTER_WIDTH)),
                  const((1, FILTER_WIDTH)),
                  pl.BlockSpec((None, FILTER_WIDTH, HYENA_ORDER * D_HYENA), lambda i: (i // (nt // 2), 0, 0))],
        out_specs=pl.BlockSpec((tile, HYENA_ORDER * D_HYENA), lambda i: (i, 0)),
        out_shape=jax.ShapeDtypeStruct((FFT_N, HYENA_ORDER * D_HYENA), F32),
        compiler_params=_cparams(("parallel",)),
        name="hyena_filter",
    )(w1p, row(b1), w2, row(b2), w3, row(b3), row(freq), wo)


def _shortconv_kernel(z_ref, w_ref, b_ref, o_ref, zp_ref, *, L):
    C = z_ref.shape[-1]
    zp_ref[0:8, :] = jnp.zeros((8, C), F32)
    zp_ref[L + 8:L + 16, :] = jnp.zeros((8, C), F32)
    zp_ref[8:L + 8, :] = z_ref[...]
    ch = min(L, 512)
    for c in range(L // ch):
        s = c * ch
        o_ref[s:s + ch, :] = (b_ref[...] + zp_ref[s + 7:s + 7 + ch, :] * w_ref[0:1, :]
                              + zp_ref[s + 8:s + 8 + ch, :] * w_ref[1:2, :]
                              + zp_ref[s + 9:s + 9 + ch, :] * w_ref[2:3, :])


def short_conv(z, w, b, cb=128):
    B, L, C = z.shape
    return pl.pallas_call(
        functools.partial(_shortconv_kernel, L=L),
        grid=(B, C // cb),
        in_specs=[pl.BlockSpec((None, L, cb), lambda b_, c: (b_, 0, c)),
                  pl.BlockSpec((3, cb), lambda b_, c: (0, c)),
                  pl.BlockSpec((1, cb), lambda b_, c: (0, c))],
        out_specs=pl.BlockSpec((None, L, cb), lambda b_, c: (b_, 0, c)),
        out_shape=jax.ShapeDtypeStruct((B, L, C), F32),
        scratch_shapes=[pltpu.VMEM((L + 16, cb), F32)],
        compiler_params=_cparams(("parallel", "parallel")),
        name="short_conv",
    )(z, w, b.reshape(1, C))


@functools.lru_cache(maxsize=None)
def _fft_tables(n1_in):
    n2 = np.arange(FFT_R)[:, None, None]
    k1 = np.arange(FFT_K1)[None, :, None]
    n1 = np.arange(n1_in)[None, None, :]
    th = 2.0 * np.pi * ((k1 * (FFT_R * n1 + n2)) % FFT_N) / FFT_N
    ga = np.zeros((FFT_R, 2 * FFT_K1P, n1_in))
    ga[:, :FFT_K1] = np.cos(th)
    ga[:, FFT_K1P:FFT_K1P + FFT_K1] = -np.sin(th)
    wk = np.where((np.arange(FFT_K1) == 0) | (np.arange(FFT_K1) == FFT_R // 2), 1.0, 2.0)[None, None, :]
    thd = th.transpose(0, 2, 1)
    gd = np.zeros((FFT_R, n1_in, 2 * FFT_K1P))
    gd[:, :, :FFT_K1] = wk * np.cos(thd)
    gd[:, :, FFT_K1P:FFT_K1P + FFT_K1] = -wk * np.sin(thd)
    a = np.arange(FFT_R)
    ph = 2.0 * np.pi * ((a[:, None] * a[None, :]) % FFT_R) / FFT_R
    c, s = np.cos(ph), np.sin(ph)
    fb = np.block([[c, s], [-s, c]])
    fc = np.block([[c, -s], [s, c]])
    return ga, gd, fb, fc


FFT_KG = 5
FFT_UNROLL = 8


def _spectrum_slabs(ya_ref, fb, k0):
    cols = []
    for g in range(FFT_KG):
        re = ya_ref[pl.ds(k0 + g, FFT_R, stride=FFT_PITCH), :]
        im = ya_ref[pl.ds(FFT_K1P + k0 + g, FFT_R, stride=FFT_PITCH), :]
        cols.append(jnp.concatenate([re, im], axis=0).astype(BF16))
    return _dot(fb, jnp.concatenate(cols, axis=1))


def _stage_a(x_ref, x_pitch, ga_ref, ya_ref, n1_in):
    def body(n2, carry):
        xs = x_ref[pl.ds(n2, n1_in, stride=x_pitch), :].astype(BF16)
        ya_ref[pl.ds(n2 * FFT_PITCH, FFT_PITCH), :] = _dot(ga_ref[n2], xs)
        return carry
    lax.fori_loop(0, FFT_R, body, 0, unroll=FFT_UNROLL)


def _spectrum_kernel(x_ref, ga_ref, fb_ref, o_ref, ya_ref):
    _stage_a(x_ref, FFT_R, ga_ref, ya_ref, FFT_R)
    fb = fb_ref[...]
    cb = ya_ref.shape[1]

    def body(kg, carry):
        x = _spectrum_slabs(ya_ref, fb, kg * FFT_KG)
        for g in range(FFT_KG):
            o_ref[kg * FFT_KG + g] = x[:, g * cb:(g + 1) * cb]
        return carry
    lax.fori_loop(0, FFT_K1 // FFT_KG, body, 0)


def filter_spectrum(taps, cb=128):
    C = taps.shape[1]
    ga, _, fb, _ = _fft_tables(FFT_R)
    return pl.pallas_call(
        _spectrum_kernel,
        grid=(C // cb,),
        in_specs=[pl.BlockSpec((FFT_N, cb), lambda c: (0, c)),
                  pl.BlockSpec((FFT_R, FFT_PITCH, FFT_R), lambda c: (0, 0, 0)),
                  pl.BlockSpec((2 * FFT_R, 2 * FFT_R), lambda c: (0, 0))],
        out_specs=pl.BlockSpec((FFT_K1, 2 * FFT_R, cb), lambda c: (0, 0, c)),
        out_shape=jax.ShapeDtypeStruct((FFT_K1, 2 * FFT_R, C), F32),
        scratch_shapes=[pltpu.VMEM((FFT_R * FFT_PITCH, cb), F32)],
        compiler_params=_cparams(("parallel",), VMEM_LIMIT),
        name="filter_spectrum",
    )(taps, _bf16_table(ga), _bf16_table(fb))


def _bf16_table(t):
    return jnp.asarray(t, F32).astype(BF16)


def _longconv_kernel(u_ref, gate_ref, bias_ref, spec_ref, ga_ref, gd_ref, fb_ref, fc_ref, o_ref, ya_ref, xp_ref):
    n1_in = FFT_R // 2

    def repitch(n1, carry):
        xp_ref[pl.ds(n1 * FFT_PITCH, FFT_R), :] = u_ref[pl.ds(pl.multiple_of(n1 * FFT_R, FFT_R), FFT_R), :]
        return carry
    lax.fori_loop(0, n1_in, repitch, 0, unroll=FFT_UNROLL)

    _stage_a(xp_ref, FFT_PITCH, ga_ref, ya_ref, n1_in)
    fb = fb_ref[...]
    fc = fc_ref[...]

    cb = ya_ref.shape[1]

    def mid(kg, carry):
        k0 = kg * FFT_KG
        x = _spectrum_slabs(ya_ref, fb, k0)
        s = jnp.concatenate([spec_ref[k0 + g] for g in range(FFT_KG)], axis=1)
        xr, xi = x[:FFT_R], x[FFT_R:]
        sr, si = s[:FFT_R], s[FFT_R:]
        z = jnp.concatenate([xr * sr - xi * si, xr * si + xi * sr], axis=0).astype(BF16)
        v = _dot(fc, z)
        for g in range(FFT_KG):
            ya_ref[pl.ds(k0 + g, FFT_R, stride=FFT_PITCH), :] = v[:FFT_R, g * cb:(g + 1) * cb]
            ya_ref[pl.ds(FFT_K1P + k0 + g, FFT_R, stride=FFT_PITCH), :] = v[FFT_R:, g * cb:(g + 1) * cb]
        return carry
    lax.fori_loop(0, FFT_K1 // FFT_KG, mid, 0)

    def last(n2, carry):
        slab = ya_ref[pl.ds(n2 * FFT_PITCH, FFT_PITCH), :].astype(BF16)
        xp_ref[pl.ds(n2, n1_in, stride=FFT_PITCH), :] = _dot(gd_ref[n2], slab)
        return carry
    lax.fori_loop(0, FFT_R, last, 0, unroll=FFT_UNROLL)

    bias = bias_ref[...]

    def gate(n1, carry):
        rows = pl.ds(pl.multiple_of(n1 * FFT_R, FFT_R), FFT_R)
        y = xp_ref[pl.ds(n1 * FFT_PITCH, FFT_R), :] * (1.0 / FFT_N)
        o_ref[rows, :] = gate_ref[rows, :] * (y + u_ref[rows, :] * bias)
        return carry
    lax.fori_loop(0, n1_in, gate, 0, unroll=FFT_UNROLL)


def gated_long_conv(u, u_col, gate, gate_col, bias, spec, spec_col, C=D_HYENA, cb=128):
    B, L, _ = u.shape
    n1_in = FFT_R // 2
    ga, gd, fb, fc = _fft_tables(n1_in)
    once = pl.Buffered(1)
    seq = lambda col: pl.BlockSpec((None, L, cb), lambda c, b: (b, 0, c + col // cb))
    return pl.pallas_call(
        _longconv_kernel,
        grid=(C // cb, B),
        in_specs=[seq(u_col), seq(gate_col),
                  pl.BlockSpec((1, cb), lambda c, b: (0, c)),
                  pl.BlockSpec((FFT_K1, 2 * FFT_R, cb), lambda c, b: (0, 0, c + spec_col // cb), pipeline_mode=once),
                  pl.BlockSpec((FFT_R, FFT_PITCH, n1_in), lambda c, b: (0, 0, 0), pipeline_mode=once),
                  pl.BlockSpec((FFT_R, n1_in, FFT_PITCH), lambda c, b: (0, 0, 0), pipeline_mode=once),
                  pl.BlockSpec((2 * FFT_R, 2 * FFT_R), lambda c, b: (0, 0)),
                  pl.BlockSpec((2 * FFT_R, 2 * FFT_R), lambda c, b: (0, 0))],
        out_specs=seq(0),
        out_shape=jax.ShapeDtypeStruct((B, L, C), F32),
        scratch_shapes=[pltpu.VMEM((FFT_R * FFT_PITCH, cb), F32), pltpu.VMEM((n1_in * FFT_PITCH, cb), F32)],
        compiler_params=_cparams(("parallel", "arbitrary"), VMEM_LIMIT),
        name="long_conv",
    )(u, gate, bias.reshape(1, C), spec, _bf16_table(ga), _bf16_table(gd), _bf16_table(fb), _bf16_table(fc))


SSM_N = SSM_GROUPS * SSM_STATE
S5_SEQS = 8
S5_CHUNK = 128


def _s5_kernel(uf_ref, ub_ref, bf_ref, bb_ref, lam_ref, cf_ref, cb_ref, yf_ref, yb_ref,
               buf_ref, bub_ref, xsf_ref, xsb_ref, st_ref):
    half = S5_SEQS // 2
    tiles = S5_CHUNK // 2

    @pl.when(pl.program_id(0) == 0)
    def _():
        st_ref[...] = jnp.zeros_like(st_ref)

    buf_ref[...] = _dot(uf_ref[...], bf_ref[...])
    bub_ref[...] = _dot(ub_ref[...], bb_ref[...])
    lo = lax.broadcasted_iota(jnp.int32, (S5_SEQS, 1), 0) < half

    def advance(x, lam, b):
        xr, xi = x
        lr, li = lam[:, 0:SSM_N], lam[:, SSM_N:2 * SSM_N]
        return lr * xr - li * xi + b[:, 0:SSM_N], lr * xi + li * xr + b[:, SSM_N:2 * SSM_N]

    def swap(x):
        return pltpu.roll(x[0], half, axis=0), pltpu.roll(x[1], half, axis=0)

    def step(t4, x):
        f_tiles, b_tiles = [], []
        for p in range(2):
            tf = buf_ref[pl.ds(pl.multiple_of((2 * t4 + p) * S5_SEQS, S5_SEQS), S5_SEQS), :]
            tb = bub_ref[pl.ds(pl.multiple_of((tiles - 1 - 2 * t4 - p) * S5_SEQS, S5_SEQS), S5_SEQS), :]
            xe = advance(x, lam_ref[0], jnp.where(lo, tf, tb))
            xo = advance(swap(xe), lam_ref[1], jnp.where(lo, tb, tf))
            x = swap(xo)
            xe = jnp.concatenate(xe, axis=1)
            xo = jnp.concatenate(xo, axis=1)
            f_tiles.append(jnp.where(lo, xe, xo))
            b_tiles.append(jnp.where(lo, xo, xe))
        xsf_ref[pl.ds(pl.multiple_of(t4 * 2 * S5_SEQS, 2 * S5_SEQS), 2 * S5_SEQS), :] = (
            jnp.concatenate(f_tiles, axis=0).astype(BF16))
        xsb_ref[pl.ds(pl.multiple_of((tiles - 2 - 2 * t4) * S5_SEQS, 2 * S5_SEQS), 2 * S5_SEQS), :] = (
            jnp.concatenate(b_tiles[::-1], axis=0).astype(BF16))
        return x

    xr, xi = lax.fori_loop(0, S5_CHUNK // 4, step, (st_ref[:, 0:SSM_N], st_ref[:, SSM_N:2 * SSM_N]))
    st_ref[:, 0:SSM_N] = xr
    st_ref[:, SSM_N:2 * SSM_N] = xi
    yf_ref[...] = _dot(xsf_ref[...], cf_ref[...])
    yb_ref[...] = _dot(xsb_ref[...], cb_ref[...])


def _s5_matrices(lam_re, lam_im, log_step, b_re, b_im, c_re, c_im):
    G, P, Hs = SSM_GROUPS, SSM_STATE, SSM_GROUP_WIDTH
    dt = jnp.exp(log_step)[:, :, None]
    mag = jnp.exp(lam_re * dt)
    ar, ai = mag * jnp.cos(lam_im * dt), mag * jnp.sin(lam_im * dt)
    er, ei = ar - 1.0, ai
    den = lam_re * lam_re + lam_im * lam_im
    cr, ci = (er * lam_re + ei * lam_im) / den, (ei * lam_re - er * lam_im) / den
    bbr = cr[..., None] * b_re - ci[..., None] * b_im
    bbi = cr[..., None] * b_im + ci[..., None] * b_re
    eye = jnp.eye(G, dtype=F32)
    blk_in = lambda m: jnp.einsum('gq,qph->ghqp', eye, m).reshape(G * Hs, G * P)
    bmat = [jnp.concatenate([blk_in(bbr[d]), blk_in(bbi[d])], axis=1).astype(BF16) for d in range(2)]
    blk_out = lambda m: jnp.einsum('gq,qhp->gpqh', eye, m).reshape(G * P, G * Hs)
    cmat = [jnp.concatenate([blk_out(c_re[d]), -blk_out(c_im[d])], axis=0).astype(BF16) for d in range(2)]
    lam = jnp.concatenate([ar.reshape(2, 1, G * P), ai.reshape(2, 1, G * P)], axis=2)
    lam = jnp.broadcast_to(lam, (2, S5_SEQS // 2, 2 * G * P))
    lam8 = jnp.stack([lam.reshape(S5_SEQS, 2 * G * P), lam[::-1].reshape(S5_SEQS, 2 * G * P)])
    return bmat[0], bmat[1], lam8, cmat[0], cmat[1]


def s5_scan(u_c, u_x, lam_re, lam_im, log_step, b_re, b_im, c_re, c_im):
    B, Lc, _ = u_c.shape
    L = u_x.shape[1]
    T = Lc + L
    assert 2 * B == S5_SEQS and T % S5_CHUNK == 0
    assert Lc % S5_CHUNK == 0
    bf, bb, lam8, cf, cb = _s5_matrices(lam_re, lam_im, log_step, b_re, b_im, c_re, c_im)
    half = S5_SEQS // 2
    un = jnp.concatenate([u_c, u_x], axis=1).astype(BF16).transpose(1, 0, 2).reshape(T * half, D_SSM)
    rows = S5_CHUNK * half
    n, nc = T // S5_CHUNK, Lc // S5_CHUNK
    fwd = pl.BlockSpec((rows, D_SSM), lambda i: (i, 0))
    bwd = pl.BlockSpec((rows, D_SSM), lambda i: (jnp.where(i < nc, nc - 1 - i, n + nc - 1 - i), 0))
    const = lambda shape: pl.BlockSpec(shape, lambda i: (0,) * len(shape))
    yf, yb = pl.pallas_call(
        _s5_kernel,
        grid=(n,),
        in_specs=[fwd, bwd, const((D_SSM, 2 * SSM_N)), const((D_SSM, 2 * SSM_N)),
                  const((2, S5_SEQS, 2 * SSM_N)), const((2 * SSM_N, D_SSM)), const((2 * SSM_N, D_SSM))],
        out_specs=[fwd, bwd],
        out_shape=[jax.ShapeDtypeStruct((T * half, D_SSM), F32)] * 2,
        scratch_shapes=[pltpu.VMEM((rows, 2 * SSM_N), F32), pltpu.VMEM((rows, 2 * SSM_N), F32),
                        pltpu.VMEM((rows, 2 * SSM_N), BF16), pltpu.VMEM((rows, 2 * SSM_N), BF16),
                        pltpu.VMEM((S5_SEQS, 2 * SSM_N), F32)],
        compiler_params=_cparams(("arbitrary",), VMEM_LIMIT),
        name="s5_scan",
    )(un, un, bf, bb, lam8, cf, cb)
    yf = yf.reshape(T, B, D_SSM).transpose(1, 0, 2)
    yb = yb.reshape(T, B, D_SSM).transpose(1, 0, 2)
    return (yf[:, :Lc], yb[:, :Lc]), (yf[:, Lc:], yb[:, Lc:])


def _gelu_tanh(y):
    return 0.5 * y * (1.0 + jnp.tanh(math.sqrt(2.0 / math.pi) * (y + 0.044715 * (y * y * y))))


def _outproj_kernel(*refs, route):
    (x_ref, mod_ref, att_ref, hy_ref, yf_ref, yb_ref, u_ref, dsk_ref, wg_ref, bg_ref, wo_ref, g2_ref) = refs[:12]
    if route:
        wr_ref, x1_ref, h2_ref, rt_ref = refs[12:]
    else:
        x1_ref, h2_ref = refs[12:]
    y = yf_ref[...] + yb_ref[...] + u_ref[...] * dsk_ref[...]
    y = _gelu_tanh(y)
    ss = y * jax.nn.sigmoid(_dot(y.astype(BF16), wg_ref[...]) + bg_ref[...])
    mix = (_dot(att_ref[...], wo_ref[0:D_ATT, :])
           + _dot(hy_ref[...].astype(BF16), wo_ref[D_ATT:D_ATT + D_HYENA, :])
           + _dot(ss.astype(BF16), wo_ref[D_ATT + D_HYENA:, :]))
    x1 = x_ref[...] + mod_ref[2:3, :] * mix
    x1_ref[...] = x1
    h2 = _rms_mod(x1, g2_ref[...], mod_ref[3:4, :], mod_ref[4:5, :])
    h2_ref[...] = h2.astype(h2_ref.dtype)
    if route:
        lane = lax.broadcasted_iota(jnp.int32, (1, 128), 1)
        h_hi = h2.astype(BF16)
        h_lo = (h2 - h_hi.astype(F32)).astype(BF16)
        logits = _dot(h_hi, wr_ref[0]) + _dot(h_lo, wr_ref[0]) + _dot(h_hi, wr_ref[1])
        lg = jnp.where(lane < N_EXPERTS, logits, -jnp.inf)
        v1 = lg.max(axis=-1, keepdims=True)
        i1 = jnp.where(lg == v1, lane, 128).min(axis=-1, keepdims=True)
        lg2 = jnp.where(lane == i1, -jnp.inf, lg)
        v2 = lg2.max(axis=-1, keepdims=True)
        i2 = jnp.where(lg2 == v2, lane, 128).min(axis=-1, keepdims=True)
        e = jnp.exp(v2 - v1)
        w1 = 1.0 / (1.0 + e)
        rt_ref[...] = jnp.where(lane == 0, i1.astype(F32),
                                jnp.where(lane == 1, i2.astype(F32),
                                          jnp.where(lane == 2, w1, jnp.where(lane == 3, e * w1, 0.0))))


def out_proj(x, mod, att, hy, yf, yb, u, d_skip, w_glu, b_glu, w_out_bf, g2, router_w=None, tm=512):
    B, T, D = x.shape
    tm = min(tm, T)
    route = router_w is not None
    tok = lambda n: pl.BlockSpec((None, tm, n), lambda b, i: (b, i, 0))
    const = lambda shape: pl.BlockSpec(shape, lambda b, i: (0,) * len(shape))
    in_specs = [tok(D), pl.BlockSpec((None, 6, D), lambda b, i: (b, 0, 0)), tok(D_ATT), tok(D_HYENA),
                tok(D_SSM), tok(D_SSM), tok(D_SSM), const((1, D_SSM)), const((D_SSM, D_SSM)), const((1, D_SSM)),
                const((D, D)), const((1, D))]
    args = [x, mod, att, hy, yf, yb, u, d_skip.reshape(1, D_SSM), w_glu.astype(BF16), b_glu.reshape(1, D_SSM),
            w_out_bf, g2.reshape(1, D)]
    out_specs = [tok(D), tok(D)]
    out_shape = [jax.ShapeDtypeStruct((B, T, D), F32), jax.ShapeDtypeStruct((B, T, D), BF16)]
    if route:
        in_specs.append(const((2, D, 128)))
        wr = jnp.pad(router_w.astype(F32), ((0, 0), (0, 128 - N_EXPERTS)))
        wr_hi = wr.astype(BF16)
        args.append(jnp.stack([wr_hi, (wr - wr_hi.astype(F32)).astype(BF16)]))
        out_specs.append(tok(128))
        out_shape.append(jax.ShapeDtypeStruct((B, T, 128), F32))
    return pl.pallas_call(
        functools.partial(_outproj_kernel, route=route),
        grid=(B, T // tm),
        in_specs=in_specs, out_specs=out_specs, out_shape=out_shape,
        compiler_params=_cparams(("parallel", "parallel"), VMEM_LIMIT),
        name="out_proj",
    )(*args)


FF_SPLIT = 1280


def _swiglu_tile(h, w1_ref, w3_ref, w2_ref):
    acc = None
    for lo, hi in ((0, FF_SPLIT), (FF_SPLIT, D_FF)):
        a = _dot(h, w1_ref[:, lo:hi])
        g = (a * jax.nn.sigmoid(a) * _dot(h, w3_ref[:, lo:hi])).astype(BF16)
        part = _dot(g, w2_ref[lo:hi, :])
        acc = part if acc is None else acc + part
    return acc


def _ffn_kernel(h_ref, x_ref, mod_ref, w1_ref, w3_ref, w2_ref, o_ref):
    o_ref[...] = x_ref[...] + mod_ref[5:6, :] * _swiglu_tile(h_ref[...], w1_ref, w3_ref, w2_ref)


def ffn_dense(h2, x1, mod, w1, w3, w2, tm=256):
    B, T, D = x1.shape
    tm = min(tm, T)
    tok = pl.BlockSpec((None, tm, D), lambda b, i: (b, i, 0))
    once = pl.Buffered(1)
    return pl.pallas_call(
        _ffn_kernel,
        grid=(B, T // tm),
        in_specs=[tok, tok, pl.BlockSpec((None, 6, D), lambda b, i: (b, 0, 0)),
                  pl.BlockSpec((D, D_FF), lambda b, i: (0, 0), pipeline_mode=once),
                  pl.BlockSpec((D, D_FF), lambda b, i: (0, 0), pipeline_mode=once),
                  pl.BlockSpec((D_FF, D), lambda b, i: (0, 0), pipeline_mode=once)],
        out_specs=tok,
        out_shape=jax.ShapeDtypeStruct((B, T, D), F32),
        compiler_params=_cparams(("parallel", "parallel"), VMEM_LIMIT),
        name="ffn_dense",
    )(h2, x1, mod, w1, w3, w2)


MOE_TM = 512


def _moe_kernel(te_ref, tv_ref, h_ref, w1_ref, w3_ref, w2_ref, o_ref):
    valid = tv_ref[pl.program_id(0)] > 0

    @pl.when(valid)
    def _():
        o_ref[...] = _swiglu_tile(h_ref[...], w1_ref, w3_ref, w2_ref).astype(o_ref.dtype)

    @pl.when(jnp.logical_not(valid))
    def _():
        o_ref[...] = jnp.zeros_like(o_ref)


def moe_experts(hs, tile_expert, tile_valid, w1, w3, w2):
    P, D = hs.shape
    nt = P // MOE_TM
    tok = pl.BlockSpec((MOE_TM, D), lambda i, te, tv: (i, 0))
    return pl.pallas_call(
        _moe_kernel,
        grid_spec=pltpu.PrefetchScalarGridSpec(
            num_scalar_prefetch=2,
            grid=(nt,),
            in_specs=[tok,
                      pl.BlockSpec((None, D, D_FF), lambda i, te, tv: (te[i], 0, 0)),
                      pl.BlockSpec((None, D, D_FF), lambda i, te, tv: (te[i], 0, 0)),
                      pl.BlockSpec((None, D_FF, D), lambda i, te, tv: (te[i], 0, 0))],
            out_specs=tok),
        out_shape=jax.ShapeDtypeStruct((P, D), BF16),
        compiler_params=_cparams(("arbitrary",), VMEM_LIMIT),
        name="moe_experts",
    )(tile_expert, tile_valid, hs, w1, w3, w2)


def _combine_kernel(x_ref, mod_ref, ya_ref, yb_ref, rt_ref, o_ref):
    wa = rt_ref[:, 2:3]
    wb = rt_ref[:, 3:4]
    o_ref[...] = x_ref[...] + mod_ref[5:6, :] * (wa * ya_ref[...] + wb * yb_ref[...])


def moe_combine(x1, mod, ya, yb, rt, tm=512):
    B, T, D = x1.shape
    tok = lambda n: pl.BlockSpec((None, tm, n), lambda b, i: (b, i, 0))
    return pl.pallas_call(
        _combine_kernel,
        grid=(B, T // tm),
        in_specs=[tok(D), pl.BlockSpec((None, 6, D), lambda b, i: (b, 0, 0)), tok(D), tok(D), tok(128)],
        out_specs=tok(D),
        out_shape=jax.ShapeDtypeStruct((B, T, D), F32),
        compiler_params=_cparams(("parallel", "parallel")),
        name="moe_combine",
    )(x1, mod, ya, yb, rt)


def moe_layer(h2, x1, mod, rt, w1, w3, w2):
    B, T, D = x1.shape
    n = B * T
    e_flat = jnp.concatenate([rt[..., 0].reshape(n), rt[..., 1].reshape(n)]).astype(jnp.int32)
    order = jnp.argsort(e_flat, stable=True).astype(jnp.int32)
    rank = jnp.argsort(order).astype(jnp.int32)
    onehot = jax.nn.one_hot(e_flat, N_EXPERTS, dtype=jnp.int32)
    counts = jnp.sum(onehot, axis=0)
    padded = ((counts + MOE_TM - 1) // MOE_TM) * MOE_TM
    pend = jnp.cumsum(padded)
    shift = (pend - padded) - (jnp.cumsum(counts) - counts)
    P = 2 * n + N_EXPERTS * MOE_TM
    nt = P // MOE_TM
    tile_row = jnp.arange(nt, dtype=jnp.int32) * MOE_TM
    tile_expert = jnp.sum((tile_row[:, None] >= pend[None, :]).astype(jnp.int32), axis=1)
    tile_valid = (tile_expert < N_EXPERTS).astype(jnp.int32)
    tile_expert = jnp.minimum(tile_expert, N_EXPERTS - 1)
    slot = rank + jnp.sum(onehot * shift[None, :], axis=1)
    tile_shift = jnp.sum(jax.nn.one_hot(tile_expert, N_EXPERTS, dtype=jnp.int32) * shift[None, :], axis=1)
    row_rank = jnp.arange(P, dtype=jnp.int32) - jnp.repeat(tile_shift, MOE_TM)
    src = order[jnp.clip(row_rank, 0, 2 * n - 1)] % n
    take = lambda a, idx: a.at[idx].get(mode='promise_in_bounds')
    hs = take(h2.reshape(n, D), src)
    ys = moe_experts(hs, tile_expert, tile_valid, w1, w3, w2)
    ya = take(ys, slot[:n]).reshape(B, T, D)
    yb = take(ys, slot[n:]).reshape(B, T, D)
    return moe_combine(x1, mod, ya, yb, rt)


def _hyena(hz, conv_w, conv_b, spec, bias):
    zc = short_conv(hz, conv_w, conv_b)
    B, L, C = zc.shape
    lmax = FFT_N // 2
    if L < lmax:
        slot = lmax // B
        assert slot >= 2 * L
        zc = jnp.pad(zc, ((0, 0), (0, slot - L), (0, 0))).reshape(1, lmax, C)
    y = gated_long_conv(zc, 0, zc, D_HYENA, bias[0], spec, 0)
    y = gated_long_conv(y, 0, zc, 2 * D_HYENA, bias[1], spec, D_HYENA)
    if L < lmax:
        y = y.reshape(B, lmax // B, D_HYENA)[:, :L]
    return y


def _split_route(outs, mod):
    x1, h2 = outs[0], outs[1]
    return h2, x1, mod, (outs[2] if len(outs) > 2 else None)


def kernel(x, c, ctx, c_ctx, w_ada, b_ada, g_norm1, g_norm2, w_in, w_out, g_q, g_k, rpb, hy_conv_w, hy_conv_b, filt_w1, filt_b1, filt_w2, filt_b2, filt_w3, filt_b3, filt_freq, filt_w_out, hy_bias, lam_re, lam_im, log_step, b_re, b_im, c_re, c_im, d_skip, w_glu, b_glu, ffn_w1, ffn_w3, ffn_w2, router_w, moe_w1, moe_w3, moe_w2):
    B, L, D = x.shape
    Lc = ctx.shape[1]
    c8 = jnp.zeros((8, D), F32).at[:B].set(c).at[B].set(c_ctx)
    mod_all = adaln_mod(c8, w_ada, b_ada)
    for layer in range(DEPTH):
        last = layer == DEPTH - 1
        m = mod_all[layer].reshape(8, 6, D)
        mod_x = m[:B]
        mod_c = jnp.broadcast_to(m[B], (B, 6, D))
        w_l = w_in[layer].astype(BF16)
        wo_l = w_out[layer].astype(BF16)
        filt = (filt_w1[layer], filt_b1[layer], filt_w2[layer], filt_b2[layer],
                filt_w3[layer], filt_b3[layer], filt_freq[layer], filt_w_out[layer])
        ssm = (lam_re[layer], lam_im[layer], log_step[layer], b_re[layer], b_im[layer], c_re[layer], c_im[layer])

        q_c, k_c, v_c, hz_c, u_c = in_proj(ctx, mod_c, g_norm1[layer], w_l, g_q[layer], g_k[layer])
        q, k, v, hz, u = in_proj(x, mod_x, g_norm1[layer], w_l, g_q[layer], g_k[layer])
        att = neighbourhood_attention(q, k, v, k_c, v_c, rpb[layer])
        spec = filter_spectrum(hyena_filter_taps(L, *filt))
        hy = _hyena(hz, hy_conv_w[layer], hy_conv_b[layer], spec, hy_bias[layer])
        (yf_c, yb_c), (yf, yb) = s5_scan(u_c, u, *ssm)

        tail = (d_skip[layer], w_glu[layer], b_glu[layer], wo_l, g_norm2[layer])
        i = layer // 2
        if layer % 2 == 0:
            ffw = (ffn_w1[i].astype(BF16), ffn_w3[i].astype(BF16), ffn_w2[i].astype(BF16))
            mixer = lambda h2, x1, mod, rt: ffn_dense(h2, x1, mod, *ffw)
            rw = None
        else:
            mow = (moe_w1[i].astype(BF16), moe_w3[i].astype(BF16), moe_w2[i].astype(BF16))
            mixer = lambda h2, x1, mod, rt: moe_layer(h2, x1, mod, rt, *mow)
            rw = router_w[i]
        x = mixer(*_split_route(out_proj(x, mod_x, att, hy, yf, yb, u, *tail, router_w=rw), mod_x))

        if not last:
            att_c = context_attention(q_c, k_c, v_c)
            spec_c = filter_spectrum(hyena_filter_taps(Lc, *filt))
            hy_c = _hyena(hz_c, hy_conv_w[layer], hy_conv_b[layer], spec_c, hy_bias[layer])
            ctx = mixer(*_split_route(out_proj(ctx, mod_c, att_c, hy_c, yf_c, yb_c, u_c, *tail, router_w=rw), mod_c))
    return x
```

```python
import functools
import math

import numpy as np
import jax
import jax.numpy as jnp
from jax import lax
from jax.experimental import pallas as pl
from jax.experimental.pallas import tpu as pltpu

F32 = jnp.float32
BF16 = jnp.bfloat16
HIGHEST = lax.Precision.HIGHEST

D_MODEL = 1024
DEPTH = 2
GRID_W = 64
ATT_HEAD_DIM = 64
ATT_HEADS = 8
D_ATT = 512
WIN_ROWS = 8
WIN_COLS = 16
D_HYENA = 256
HYENA_ORDER = 2
FILTER_EMB = 33
FILTER_BANDS = 16
FILTER_WIDTH = 64
MIN_DECAY = math.log(1e-2) / 1.5
MAX_DECAY = math.log(1e-2) / 0.3
D_SSM = 256
SSM_GROUP_WIDTH = 16
SSM_GROUPS = 16
SSM_STATE = 64
COL_K = D_ATT
COL_V = 2 * D_ATT
COL_HY = 3 * D_ATT
COL_SSM = COL_HY + 3 * D_HYENA
D_IN = COL_SSM + D_SSM
D_FF = 2816
N_EXPERTS = 8
EPS = 1e-6
NEG = -1e30

VMEM_LIMIT = 56 * 1024 * 1024


def _cparams(sem, vmem=None):
    return pltpu.CompilerParams(dimension_semantics=sem, vmem_limit_bytes=vmem)


def _dot(a, b):
    return jnp.dot(a, b, preferred_element_type=F32)


def _dot_nt(a, b):
    return lax.dot_general(a, b, (((1,), (1,)), ((), ())), preferred_element_type=F32)


def _mod_kernel(c_ref, w_ref, b_ref, o_ref):
    c = c_ref[...]
    s = c * jax.nn.sigmoid(c)
    o_ref[...] = jnp.dot(s, w_ref[...], precision=HIGHEST, preferred_element_type=F32) + b_ref[...]


def adaln_mod(c8, w_ada, b_ada, tn=512):
    depth, d, n = w_ada.shape
    return pl.pallas_call(
        _mod_kernel,
        grid=(depth, n // tn),
        in_specs=[pl.BlockSpec((8, d), lambda l, j: (0, 0)),
                  pl.BlockSpec((None, d, tn), lambda l, j: (l, 0, j)),
                  pl.BlockSpec((None, 1, tn), lambda l, j: (l, 0, j))],
        out_specs=pl.BlockSpec((None, 8, tn), lambda l, j: (l, 0, j)),
        out_shape=jax.ShapeDtypeStruct((depth, 8, n), F32),
        compiler_params=_cparams(("parallel", "parallel")),
        name="adaln_mod",
    )(c8, w_ada, b_ada.reshape(depth, 1, n))


def _rms_mod(x, g, shift, scale):
    ms = jnp.mean(x * x, axis=-1, keepdims=True)
    h = x * lax.rsqrt(ms + EPS) * g
    return h * (1.0 + scale) + shift


def _head_norm(z, a, g):
    zz = (z * z).astype(BF16)
    m = jnp.concatenate([_dot(zz[:, c:c + ATT_LANES], a) for c in range(0, z.shape[1], ATT_LANES)], axis=1)
    return z * lax.rsqrt(m + EPS) * g


TOK_TB = 128


def _inproj_kernel(x_ref, mod_ref, g_ref, w_ref, a_ref, gq_ref, gk_ref,
                   q_ref, k_ref, v_ref, hz_ref, u_ref):
    B, tb, _ = x_ref.shape
    h = jnp.concatenate([_rms_mod(x_ref[b], g_ref[...], mod_ref[b, 0:1, :], mod_ref[b, 1:2, :]).astype(BF16)
                         for b in range(B)], axis=0)
    a = a_ref[...]
    q = (_head_norm(_dot(h, w_ref[:, 0:COL_K]), a, gq_ref[...]) * (ATT_HEAD_DIM ** -0.5)).astype(BF16)
    k = _head_norm(_dot(h, w_ref[:, COL_K:COL_V]), a, gk_ref[...]).astype(BF16)
    v = _dot(h, w_ref[:, COL_V:COL_HY]).astype(BF16)
    hz = _dot(h, w_ref[:, COL_HY:COL_SSM])
    u = _dot(h, w_ref[:, COL_SSM:D_IN])
    for b in range(B):
        rows = slice(b * tb, (b + 1) * tb)
        q_ref[b] = q[rows]
        k_ref[b] = k[rows]
        v_ref[b] = v[rows]
        hz_ref[b] = hz[rows]
        for s in range(D_SSM // 128):
            u_ref[s, pl.ds(b, tb, stride=B), :] = u[rows, s * 128:(s + 1) * 128]


def in_proj(x, mod, g1, w_in_bf, gq, gk):
    B, T, D = x.shape
    tb = TOK_TB
    head_avg = jnp.asarray(np.kron(np.eye(ATT_LANES // ATT_HEAD_DIM),
                                   np.full((ATT_HEAD_DIM, ATT_HEAD_DIM), 1.0 / ATT_HEAD_DIM)), BF16)
    tok = lambda n: pl.BlockSpec((B, tb, n), lambda i: (0, i, 0))
    const = lambda shape: pl.BlockSpec(shape, lambda i: (0,) * len(shape))
    return pl.pallas_call(
        _inproj_kernel,
        grid=(T // tb,),
        in_specs=[tok(D), const((B, 6, D)), const((1, D)),
                  const((D, D_IN)), const((ATT_LANES, ATT_LANES)), const((1, D_ATT)), const((1, D_ATT))],
        out_specs=[tok(D_ATT), tok(D_ATT), tok(D_ATT), tok(3 * D_HYENA),
                   pl.BlockSpec((D_SSM // 128, tb * B, 128), lambda i: (0, i, 0))],
        out_shape=[jax.ShapeDtypeStruct((B, T, D_ATT), BF16)] * 3
                  + [jax.ShapeDtypeStruct((B, T, 3 * D_HYENA), F32),
                     jax.ShapeDtypeStruct((D_SSM // 128, T * B, 128), F32)],
        compiler_params=_cparams(("parallel",), VMEM_LIMIT),
        name="in_proj",
    )(x, mod, g1.reshape(1, D), w_in_bf, head_avg,
      jnp.tile(gq, ATT_HEADS).reshape(1, D_ATT), jnp.tile(gk, ATT_HEADS).reshape(1, D_ATT))


ATT_R = 4
ATT_KROWS = ATT_R + WIN_ROWS - 1
ATT_LANES = 256


def _softmax_heads(q, parts, bias_ref, o_ref):
    nh = ATT_LANES // ATT_HEAD_DIM
    lane = lax.broadcasted_iota(jnp.int32, (1, ATT_LANES), 1)
    head = lambda h: (lane >= ATT_HEAD_DIM * h) & (lane < ATT_HEAD_DIM * (h + 1))
    acc = jnp.zeros((q.shape[0], ATT_LANES), F32)
    for h in range(nh):
        hm = head(h)
        qh = jnp.where(hm, q, jnp.zeros_like(q))
        ss = [_dot_nt(qh, kk).astype(BF16) for kk, _ in parts]
        if bias_ref is not None:
            ss[0] = ss[0] + bias_ref[h]
        m = ss[0].max(axis=-1, keepdims=True)
        for s in ss[1:]:
            m = jnp.maximum(m, s.max(axis=-1, keepdims=True))
        om = head((h + 1) % nh)
        o = None
        for s, (_, vv) in zip(ss, parts):
            part = _dot(jnp.exp(s - m), jnp.where(om, jnp.ones_like(vv), vv))
            o = part if o is None else o + part
        l = pltpu.roll(o, ATT_LANES - ATT_HEAD_DIM, axis=1)
        acc = jnp.where(hm, o / l, acc)
    o_ref[...] = acc.astype(o_ref.dtype)


def _nattn_kernel(q_ref, k_ref, v_ref, kc_ref, vc_ref, bias_ref, o_ref, *, rows):
    j = pl.program_id(2)
    ks = jnp.clip(j * ATT_R - WIN_ROWS // 2, 0, rows - ATT_KROWS)
    start = pl.multiple_of(ks * GRID_W, GRID_W)
    kl = k_ref[pl.ds(start, ATT_KROWS * GRID_W), :]
    vl = v_ref[pl.ds(start, ATT_KROWS * GRID_W), :]
    _softmax_heads(q_ref[...], [(kl, vl), (kc_ref[...], vc_ref[...])], bias_ref, o_ref)


def _attn_bias_tables(rpb, rows):
    nblk = rows // ATT_R
    qc = np.arange(GRID_W)
    c0 = np.clip(qc - WIN_COLS // 2, 0, GRID_W - WIN_COLS)
    kc = np.arange(GRID_W)
    col_ok = (kc[None, :] >= c0[:, None]) & (kc[None, :] < c0[:, None] + WIN_COLS)
    col_off = kc[None, :] - qc[:, None] + WIN_COLS - 1
    col_sel = (col_off[:, :, None] == np.arange(2 * WIN_COLS - 1)) & col_ok[:, :, None]
    band = jnp.einsum('hij,qkj->hiqk', rpb, jnp.asarray(col_sel, F32), precision=HIGHEST)
    band = jnp.where(jnp.asarray(col_ok)[None, None], band, NEG)
    tabs = []
    for blk in (0, 1, nblk - 1):
        r = blk * ATT_R + np.arange(ATT_R)
        ks = int(np.clip(blk * ATT_R - WIN_ROWS // 2, 0, rows - ATT_KROWS))
        kk = ks + np.arange(ATT_KROWS)
        r0 = np.clip(r - WIN_ROWS // 2, 0, rows - WIN_ROWS)
        row_ok = (kk[None, :] >= r0[:, None]) & (kk[None, :] < r0[:, None] + WIN_ROWS)
        row_off = kk[None, :] - r[:, None] + WIN_ROWS - 1
        row_sel = (row_off[:, :, None] == np.arange(2 * WIN_ROWS - 1)) & row_ok[:, :, None]
        t = jnp.einsum('rsi,hiqk->hrqsk', jnp.asarray(row_sel, F32), band, precision=HIGHEST)
        t = t + jnp.asarray(np.where(row_ok, 0.0, NEG), F32)[None, :, None, :, None]
        tabs.append(t.reshape(ATT_HEADS, ATT_R * GRID_W, ATT_KROWS * GRID_W))
    return jnp.stack(tabs).astype(BF16)


def neighbourhood_attention(q, k, v, kc, vc, rpb):
    B, L, _ = q.shape
    Lc = kc.shape[1]
    rows = L // GRID_W
    nblk = rows // ATT_R
    tq = ATT_R * GRID_W
    nkl = ATT_KROWS * GRID_W
    bias = _attn_bias_tables(rpb, rows)
    hpb = ATT_LANES // ATT_HEAD_DIM
    variant = lambda j: jnp.where(j == 0, 0, jnp.where(j == nblk - 1, 2, 1))
    return pl.pallas_call(
        functools.partial(_nattn_kernel, rows=rows),
        grid=(B, D_ATT // ATT_LANES, nblk),
        in_specs=[pl.BlockSpec((None, tq, ATT_LANES), lambda b, g, j: (b, j, g)),
                  pl.BlockSpec((None, L, ATT_LANES), lambda b, g, j: (b, 0, g)),
                  pl.BlockSpec((None, L, ATT_LANES), lambda b, g, j: (b, 0, g)),
                  pl.BlockSpec((None, Lc, ATT_LANES), lambda b, g, j: (b, 0, g)),
                  pl.BlockSpec((None, Lc, ATT_LANES), lambda b, g, j: (b, 0, g)),
                  pl.BlockSpec((None, hpb, tq, nkl), lambda b, g, j: (variant(j), g, 0, 0))],
        out_specs=pl.BlockSpec((None, tq, ATT_LANES), lambda b, g, j: (b, j, g)),
        out_shape=jax.ShapeDtypeStruct((B, L, D_ATT), BF16),
        compiler_params=_cparams(("parallel", "parallel", "arbitrary"), VMEM_LIMIT),
        name="nattn",
    )(q, k, v, kc, vc, bias)


def _cattn_kernel(q_ref, k_ref, v_ref, o_ref):
    _softmax_heads(q_ref[...], [(k_ref[...], v_ref[...])], None, o_ref)


def context_attention(q, k, v):
    B, Lc, _ = q.shape
    spec = pl.BlockSpec((None, Lc, ATT_LANES), lambda b, g: (b, 0, g))
    return pl.pallas_call(
        _cattn_kernel,
        grid=(B, D_ATT // ATT_LANES),
        in_specs=[spec, spec, spec],
        out_specs=spec,
        out_shape=jax.ShapeDtypeStruct((B, Lc, D_ATT), BF16),
        compiler_params=_cparams(("parallel", "parallel")),
        name="cattn",
    )(q, k, v)


FFT_N = 16384
FFT_R = 128
FFT_K1 = FFT_R // 2 + 1
FFT_K1P = 66
FFT_PITCH = 2 * FFT_K1P


def _filter_kernel(w1_ref, wt_ref, b1_ref, w2_ref, b2_ref, w3_ref, b3_ref, fr_ref, wo_ref, o_ref, *, lf, tile):
    i = pl.program_id(0)
    half = FFT_N // 2
    row0 = i * tile
    first_pos = jnp.where(row0 < half, row0, FFT_N - row0 - tile)

    @pl.when(first_pos >= lf)
    def _():
        o_ref[...] = jnp.zeros_like(o_ref)

    @pl.when(first_pos < lf)
    def _():
        rows = tile // FILTER_PACK

        def position(width):
            r = lax.broadcasted_iota(jnp.int32, (rows, FILTER_PACK * width), 0)
            g = lax.broadcasted_iota(jnp.int32, (rows, FILTER_PACK * width), 1) // width
            j = row0 + r + g * rows
            return jnp.where(j < half, j, FFT_N - 1 - j)

        nfeat = 2 * FILTER_BANDS
        feat = lax.broadcasted_iota(jnp.int32, (1, FILTER_PACK * nfeat), 1) % nfeat
        f = 1e-4 + (feat % FILTER_BANDS).astype(F32) * ((FILTER_BANDS - 1 - 1e-4) / (FILTER_BANDS - 1))
        w = (2.0 * math.pi / lf) * position(nfeat).astype(F32)
        z = jnp.cos(f * w + jnp.where(feat < FILTER_BANDS, 0.0, 0.5 * math.pi))
        t_h = position(FILTER_WIDTH).astype(F32) / (lf - 1.0)
        fr = fr_ref[...]
        hdot = lambda a, b: jnp.dot(a, b, precision=HIGHEST, preferred_element_type=F32)
        h = jnp.sin(fr * (hdot(z, w1_ref[...]) + t_h * wt_ref[...] + b1_ref[...]))
        h = jnp.sin(fr * (hdot(h, w2_ref[...]) + b2_ref[...]))
        h = jnp.sin(fr * (hdot(h, w3_ref[...]) + b3_ref[...]))
        y = hdot(h, wo_ref[...])
        nout = HYENA_ORDER * D_HYENA
        pos_o = position(nout)
        ch = lax.broadcasted_iota(jnp.int32, (1, FILTER_PACK * nout), 1) % D_HYENA
        delta = jnp.abs(MIN_DECAY + ch.astype(F32) * ((MAX_DECAY - MIN_DECAY) / (D_HYENA - 1)))
        y = jnp.where(pos_o < lf, y * jnp.exp(-(pos_o.astype(F32) / (lf - 1.0)) * delta), 0.0)
        for g in range(FILTER_PACK):
            o_ref[g * rows:(g + 1) * rows, :] = y[:, g * nout:(g + 1) * nout]


FILTER_PACK = 4


def hyena_filter_taps(lf, w1, b1, w2, b2, w3, b3, freq, w_out, tile=1024):
    P, W = FILTER_PACK, FILTER_WIDTH
    nout = HYENA_ORDER * D_HYENA
    bd = lambda m: jnp.kron(jnp.eye(P, dtype=F32), m.astype(F32))
    wide = lambda a: jnp.tile(a.astype(F32), P).reshape(1, P * W)
    wo = w_out.reshape(W, HYENA_ORDER, 2, D_HYENA).transpose(2, 0, 1, 3).reshape(2, W, nout)
    wo = jnp.stack([bd(wo[0]), bd(wo[1])])
    nt = FFT_N // tile
    const = lambda shape: pl.BlockSpec(shape, lambda i: (0,) * len(shape))
    return pl.pallas_call(
        functools.partial(_filter_kernel, lf=lf, tile=tile),
        grid=(nt,),
        in_specs=[const((P * (FILTER_EMB - 1), P * W)), const((1, P * W)), const((1, P * W)),
                  const((P * W, P * W)), const((1, P * W)), const((P * W, P * W)), const((1, P * W)),
                  const((1, P * W)),
                  pl.BlockSpec((None, P * W, P * nout), lambda i: (i // (nt // 2), 0, 0))],
        out_specs=pl.BlockSpec((tile, nout), lambda i: (i, 0)),
        out_shape=jax.ShapeDtypeStruct((FFT_N, nout), F32),
        compiler_params=_cparams(("parallel",)),
        name="hyena_filter",
    )(bd(w1[1:]), wide(w1[0]), wide(b1), bd(w2), wide(b2), bd(w3), wide(b3), wide(freq), wo)


def _shortconv_kernel(z_ref, w_ref, b_ref, o_ref, zp_ref, *, L):
    C = z_ref.shape[-1]
    zp_ref[0:8, :] = jnp.zeros((8, C), F32)
    zp_ref[L + 8:L + 16, :] = jnp.zeros((8, C), F32)
    zp_ref[8:L + 8, :] = z_ref[...]
    ch = min(L, 512)
    for c in range(L // ch):
        s = c * ch
        o_ref[s:s + ch, :] = (b_ref[...] + zp_ref[s + 7:s + 7 + ch, :] * w_ref[0:1, :]
                              + zp_ref[s + 8:s + 8 + ch, :] * w_ref[1:2, :]
                              + zp_ref[s + 9:s + 9 + ch, :] * w_ref[2:3, :])


def short_conv(z, w, b, cb=128):
    B, L, C = z.shape
    return pl.pallas_call(
        functools.partial(_shortconv_kernel, L=L),
        grid=(B, C // cb),
        in_specs=[pl.BlockSpec((None, L, cb), lambda b_, c: (b_, 0, c)),
                  pl.BlockSpec((3, cb), lambda b_, c: (0, c)),
                  pl.BlockSpec((1, cb), lambda b_, c: (0, c))],
        out_specs=pl.BlockSpec((None, L, cb), lambda b_, c: (b_, 0, c)),
        out_shape=jax.ShapeDtypeStruct((B, L, C), F32),
        scratch_shapes=[pltpu.VMEM((L + 16, cb), F32)],
        compiler_params=_cparams(("parallel", "parallel")),
        name="short_conv",
    )(z, w, b.reshape(1, C))


@functools.lru_cache(maxsize=None)
def _fft_tables(n1_in):
    n2 = np.arange(FFT_R)[:, None, None]
    k1 = np.arange(FFT_K1)[None, :, None]
    n1 = np.arange(n1_in)[None, None, :]
    th = 2.0 * np.pi * ((k1 * (FFT_R * n1 + n2)) % FFT_N) / FFT_N
    ga = np.zeros((FFT_R, 2 * FFT_K1P, n1_in))
    ga[:, :FFT_K1] = np.cos(th)
    ga[:, FFT_K1P:FFT_K1P + FFT_K1] = -np.sin(th)
    wk = np.where((np.arange(FFT_K1) == 0) | (np.arange(FFT_K1) == FFT_R // 2), 1.0, 2.0)[None, None, :]
    thd = th.transpose(0, 2, 1)
    gd = np.zeros((FFT_R, n1_in, 2 * FFT_K1P))
    gd[:, :, :FFT_K1] = wk * np.cos(thd)
    gd[:, :, FFT_K1P:FFT_K1P + FFT_K1] = -wk * np.sin(thd)
    a = np.arange(FFT_R)
    ph = 2.0 * np.pi * ((a[:, None] * a[None, :]) % FFT_R) / FFT_R
    c, s = np.cos(ph), np.sin(ph)
    fb = np.block([[c, s], [-s, c]])
    fc = np.block([[c, -s], [s, c]])
    return ga, gd, fb, fc


FFT_KG = 5
FFT_UNROLL = 8


def _spectrum_slabs(ya_ref, fb, k0):
    cols = []
    for g in range(FFT_KG):
        re = ya_ref[pl.ds(k0 + g, FFT_R, stride=FFT_PITCH), :]
        im = ya_ref[pl.ds(FFT_K1P + k0 + g, FFT_R, stride=FFT_PITCH), :]
        cols.append(jnp.concatenate([re, im], axis=0).astype(BF16))
    return _dot(fb, jnp.concatenate(cols, axis=1))


def _stage_a(x_ref, x_pitch, ga_ref, ya_ref, n1_in):
    def body(n2, carry):
        xs = x_ref[pl.ds(n2, n1_in, stride=x_pitch), :].astype(BF16)
        ya_ref[pl.ds(n2 * FFT_PITCH, FFT_PITCH), :] = _dot(ga_ref[n2], xs)
        return carry
    lax.fori_loop(0, FFT_R, body, 0, unroll=FFT_UNROLL)


def _spectrum_kernel(x_ref, ga_ref, fb_ref, o_ref, ya_ref):
    _stage_a(x_ref, FFT_R, ga_ref, ya_ref, FFT_R)
    fb = fb_ref[...]
    cb = ya_ref.shape[1]

    def body(kg, carry):
        x = _spectrum_slabs(ya_ref, fb, kg * FFT_KG)
        for g in range(FFT_KG):
            o_ref[kg * FFT_KG + g] = x[:, g * cb:(g + 1) * cb]
        return carry
    lax.fori_loop(0, FFT_K1 // FFT_KG, body, 0)


def filter_spectrum(taps, cb=128):
    C = taps.shape[1]
    ga, _, fb, _ = _fft_tables(FFT_R)
    return pl.pallas_call(
        _spectrum_kernel,
        grid=(C // cb,),
        in_specs=[pl.BlockSpec((FFT_N, cb), lambda c: (0, c)),
                  pl.BlockSpec((FFT_R, FFT_PITCH, FFT_R), lambda c: (0, 0, 0)),
                  pl.BlockSpec((2 * FFT_R, 2 * FFT_R), lambda c: (0, 0))],
        out_specs=pl.BlockSpec((FFT_K1, 2 * FFT_R, cb), lambda c: (0, 0, c)),
        out_shape=jax.ShapeDtypeStruct((FFT_K1, 2 * FFT_R, C), F32),
        scratch_shapes=[pltpu.VMEM((FFT_R * FFT_PITCH, cb), F32)],
        compiler_params=_cparams(("parallel",), VMEM_LIMIT),
        name="filter_spectrum",
    )(taps, _bf16_table(ga), _bf16_table(fb))


def _bf16_table(t):
    return jnp.asarray(t, F32).astype(BF16)


def _longconv_kernel(u_ref, gate_ref, bias_ref, spec_ref, ga_ref, gd_ref, fb_ref, fc_ref, o_ref, ya_ref, xp_ref):
    n1_in = FFT_R // 2

    def repitch(n1, carry):
        xp_ref[pl.ds(n1 * FFT_PITCH, FFT_R), :] = u_ref[pl.ds(pl.multiple_of(n1 * FFT_R, FFT_R), FFT_R), :]
        return carry
    lax.fori_loop(0, n1_in, repitch, 0, unroll=FFT_UNROLL)

    _stage_a(xp_ref, FFT_PITCH, ga_ref, ya_ref, n1_in)
    fb = fb_ref[...]
    fc = fc_ref[...]

    cb = ya_ref.shape[1]

    def mid(kg, carry):
        k0 = kg * FFT_KG
        x = _spectrum_slabs(ya_ref, fb, k0)
        s = jnp.concatenate([spec_ref[k0 + g] for g in range(FFT_KG)], axis=1)
        xr, xi = x[:FFT_R], x[FFT_R:]
        sr, si = s[:FFT_R], s[FFT_R:]
        z = jnp.concatenate([xr * sr - xi * si, xr * si + xi * sr], axis=0).astype(BF16)
        v = _dot(fc, z)
        for g in range(FFT_KG):
            ya_ref[pl.ds(k0 + g, FFT_R, stride=FFT_PITCH), :] = v[:FFT_R, g * cb:(g + 1) * cb]
            ya_ref[pl.ds(FFT_K1P + k0 + g, FFT_R, stride=FFT_PITCH), :] = v[FFT_R:, g * cb:(g + 1) * cb]
        return carry
    lax.fori_loop(0, FFT_K1 // FFT_KG, mid, 0)

    def last(n2, carry):
        slab = ya_ref[pl.ds(n2 * FFT_PITCH, FFT_PITCH), :].astype(BF16)
        xp_ref[pl.ds(n2, n1_in, stride=FFT_PITCH), :] = _dot(gd_ref[n2], slab)
        return carry
    lax.fori_loop(0, FFT_R, last, 0, unroll=FFT_UNROLL)

    bias = bias_ref[...]

    def gate(n1, carry):
        rows = pl.ds(pl.multiple_of(n1 * FFT_R, FFT_R), FFT_R)
        y = xp_ref[pl.ds(n1 * FFT_PITCH, FFT_R), :] * (1.0 / FFT_N)
        o_ref[rows, :] = gate_ref[rows, :] * (y + u_ref[rows, :] * bias)
        return carry
    lax.fori_loop(0, n1_in, gate, 0, unroll=FFT_UNROLL)


def gated_long_conv(u, u_col, gate, gate_col, bias, spec, spec_col, C=D_HYENA, cb=128):
    B, L, _ = u.shape
    n1_in = FFT_R // 2
    ga, gd, fb, fc = _fft_tables(n1_in)
    once = pl.Buffered(1)
    seq = lambda col: pl.BlockSpec((None, L, cb), lambda c, b: (b, 0, c + col // cb))
    return pl.pallas_call(
        _longconv_kernel,
        grid=(C // cb, B),
        in_specs=[seq(u_col), seq(gate_col),
                  pl.BlockSpec((1, cb), lambda c, b: (0, c)),
                  pl.BlockSpec((FFT_K1, 2 * FFT_R, cb), lambda c, b: (0, 0, c + spec_col // cb), pipeline_mode=once),
                  pl.BlockSpec((FFT_R, FFT_PITCH, n1_in), lambda c, b: (0, 0, 0), pipeline_mode=once),
                  pl.BlockSpec((FFT_R, n1_in, FFT_PITCH), lambda c, b: (0, 0, 0), pipeline_mode=once),
                  pl.BlockSpec((2 * FFT_R, 2 * FFT_R), lambda c, b: (0, 0)),
                  pl.BlockSpec((2 * FFT_R, 2 * FFT_R), lambda c, b: (0, 0))],
        out_specs=seq(0),
        out_shape=jax.ShapeDtypeStruct((B, L, C), F32),
        scratch_shapes=[pltpu.VMEM((FFT_R * FFT_PITCH, cb), F32), pltpu.VMEM((n1_in * FFT_PITCH, cb), F32)],
        compiler_params=_cparams(("parallel", "arbitrary"), VMEM_LIMIT),
        name="long_conv",
    )(u, gate, bias.reshape(1, C), spec, _bf16_table(ga), _bf16_table(gd), _bf16_table(fb), _bf16_table(fc))


SSM_N = SSM_GROUPS * SSM_STATE
S5_SEQS = 8
S5_CHUNK = 128


def _s5_kernel(uf_ref, ub_ref, bf_ref, bb_ref, lam_ref, cf_ref, cb_ref, yf_ref, yb_ref,
               buf_ref, bub_ref, xsf_ref, xsb_ref, st_ref):
    half = S5_SEQS // 2
    tiles = S5_CHUNK // 2

    @pl.when(pl.program_id(0) == 0)
    def _():
        st_ref[...] = jnp.zeros_like(st_ref)

    slabs = lambda ref: jnp.concatenate([ref[s] for s in range(ref.shape[0])], axis=1).astype(BF16)
    buf_ref[...] = _dot(slabs(uf_ref), bf_ref[...])
    bub_ref[...] = _dot(slabs(ub_ref), bb_ref[...])
    lo = lax.broadcasted_iota(jnp.int32, (S5_SEQS, 1), 0) < half

    def advance(x, lam, b):
        xr, xi = x
        lr, li = lam[:, 0:SSM_N], lam[:, SSM_N:2 * SSM_N]
        return lr * xr - li * xi + b[:, 0:SSM_N], lr * xi + li * xr + b[:, SSM_N:2 * SSM_N]

    def swap(x):
        return pltpu.roll(x[0], half, axis=0), pltpu.roll(x[1], half, axis=0)

    def step(t4, x):
        f_tiles, b_tiles = [], []
        for p in range(2):
            tf = buf_ref[pl.ds(pl.multiple_of((2 * t4 + p) * S5_SEQS, S5_SEQS), S5_SEQS), :]
            tb = bub_ref[pl.ds(pl.multiple_of((tiles - 1 - 2 * t4 - p) * S5_SEQS, S5_SEQS), S5_SEQS), :]
            xe = advance(x, lam_ref[0], jnp.where(lo, tf, tb))
            xo = advance(swap(xe), lam_ref[1], jnp.where(lo, tb, tf))
            x = swap(xo)
            xe = jnp.concatenate(xe, axis=1)
            xo = jnp.concatenate(xo, axis=1)
            f_tiles.append(jnp.where(lo, xe, xo))
            b_tiles.append(jnp.where(lo, xo, xe))
        xsf_ref[pl.ds(pl.multiple_of(t4 * 2 * S5_SEQS, 2 * S5_SEQS), 2 * S5_SEQS), :] = (
            jnp.concatenate(f_tiles, axis=0).astype(BF16))
        xsb_ref[pl.ds(pl.multiple_of((tiles - 2 - 2 * t4) * S5_SEQS, 2 * S5_SEQS), 2 * S5_SEQS), :] = (
            jnp.concatenate(b_tiles[::-1], axis=0).astype(BF16))
        return x

    xr, xi = lax.fori_loop(0, S5_CHUNK // 4, step, (st_ref[:, 0:SSM_N], st_ref[:, SSM_N:2 * SSM_N]))
    st_ref[:, 0:SSM_N] = xr
    st_ref[:, SSM_N:2 * SSM_N] = xi
    yf = _dot(xsf_ref[...], cf_ref[...])
    yb = _dot(xsb_ref[...], cb_ref[...])
    for s in range(yf_ref.shape[0]):
        yf_ref[s] = yf[:, s * 128:(s + 1) * 128]
        yb_ref[s] = yb[:, s * 128:(s + 1) * 128]


def _s5_matrices(lam_re, lam_im, log_step, b_re, b_im, c_re, c_im):
    G, P, Hs = SSM_GROUPS, SSM_STATE, SSM_GROUP_WIDTH
    dt = jnp.exp(log_step)[:, :, None]
    mag = jnp.exp(lam_re * dt)
    ar, ai = mag * jnp.cos(lam_im * dt), mag * jnp.sin(lam_im * dt)
    er, ei = ar - 1.0, ai
    den = lam_re * lam_re + lam_im * lam_im
    cr, ci = (er * lam_re + ei * lam_im) / den, (ei * lam_re - er * lam_im) / den
    bbr = cr[..., None] * b_re - ci[..., None] * b_im
    bbi = cr[..., None] * b_im + ci[..., None] * b_re
    eye = jnp.eye(G, dtype=F32)
    blk_in = lambda m: jnp.einsum('gq,qph->ghqp', eye, m).reshape(G * Hs, G * P)
    bmat = [jnp.concatenate([blk_in(bbr[d]), blk_in(bbi[d])], axis=1).astype(BF16) for d in range(2)]
    blk_out = lambda m: jnp.einsum('gq,qhp->gpqh', eye, m).reshape(G * P, G * Hs)
    cmat = [jnp.concatenate([blk_out(c_re[d]), -blk_out(c_im[d])], axis=0).astype(BF16) for d in range(2)]
    lam = jnp.concatenate([ar.reshape(2, 1, G * P), ai.reshape(2, 1, G * P)], axis=2)
    lam = jnp.broadcast_to(lam, (2, S5_SEQS // 2, 2 * G * P))
    lam8 = jnp.stack([lam.reshape(S5_SEQS, 2 * G * P), lam[::-1].reshape(S5_SEQS, 2 * G * P)])
    return bmat[0], bmat[1], lam8, cmat[0], cmat[1]


def s5_scan(u_x, u_c, lam_re, lam_im, log_step, b_re, b_im, c_re, c_im):
    S = u_x.shape[0]
    half = S5_SEQS // 2
    L, Lc = u_x.shape[1] // half, u_c.shape[1] // half
    assert L % S5_CHUNK == 0 and Lc % S5_CHUNK == 0
    bf, bb, lam8, cf, cb = _s5_matrices(lam_re, lam_im, log_step, b_re, b_im, c_re, c_im)
    un = jnp.concatenate([u_x, u_c], axis=1)
    rows = S5_CHUNK * half
    nx, nc = L // S5_CHUNK, Lc // S5_CHUNK
    n = nx + nc
    fwd = pl.BlockSpec((S, rows, 128), lambda i: (0, jnp.where(i < nc, nx + i, i - nc), 0))
    bwd = pl.BlockSpec((S, rows, 128), lambda i: (0, n - 1 - i, 0))
    const = lambda shape: pl.BlockSpec(shape, lambda i: (0,) * len(shape))
    yf, yb = pl.pallas_call(
        _s5_kernel,
        grid=(n,),
        in_specs=[fwd, bwd, const((D_SSM, 2 * SSM_N)), const((D_SSM, 2 * SSM_N)),
                  const((2, S5_SEQS, 2 * SSM_N)), const((2 * SSM_N, D_SSM)), const((2 * SSM_N, D_SSM))],
        out_specs=[fwd, bwd],
        out_shape=[jax.ShapeDtypeStruct(un.shape, F32)] * 2,
        scratch_shapes=[pltpu.VMEM((rows, 2 * SSM_N), F32), pltpu.VMEM((rows, 2 * SSM_N), F32),
                        pltpu.VMEM((rows, 2 * SSM_N), BF16), pltpu.VMEM((rows, 2 * SSM_N), BF16),
                        pltpu.VMEM((S5_SEQS, 2 * SSM_N), F32)],
        compiler_params=_cparams(("arbitrary",), VMEM_LIMIT),
        name="s5_scan",
    )(un, un, bf, bb, lam8, cf, cb)
    return un, yf, yb


def _gelu_tanh(y):
    return 0.5 * y * (1.0 + jnp.tanh(math.sqrt(2.0 / math.pi) * (y + 0.044715 * (y * y * y))))


def _outproj_kernel(*refs, route):
    (x_ref, mod_ref, att_ref, hy_ref, yf_ref, yb_ref, u_ref, dsk_ref, wg_ref, bg_ref, wo_ref, g2_ref) = refs[:12]
    if route:
        wr_ref, x1_ref, h2_ref, rt_ref = refs[12:]
    else:
        x1_ref, h2_ref = refs[12:]
    B, tb, D = x_ref.shape
    rows = lambda a: a.reshape(B * tb, a.shape[-1])

    def scan_rows(ref):
        return jnp.concatenate(
            [jnp.concatenate([ref[s, pl.ds(b, tb, stride=B), :] for s in range(ref.shape[0])], axis=1)
             for b in range(B)], axis=0)

    y = scan_rows(yf_ref) + scan_rows(yb_ref) + scan_rows(u_ref) * dsk_ref[...]
    y = _gelu_tanh(y)
    ss = y * jax.nn.sigmoid(_dot(y.astype(BF16), wg_ref[...]) + bg_ref[...])
    mix = (_dot(rows(att_ref[...]), wo_ref[0:D_ATT, :])
           + _dot(rows(hy_ref[...]).astype(BF16), wo_ref[D_ATT:D_ATT + D_HYENA, :])
           + _dot(ss.astype(BF16), wo_ref[D_ATT + D_HYENA:, :]))
    h2s = []
    for b in range(B):
        x1 = x_ref[b] + mod_ref[b, 2:3, :] * mix[b * tb:(b + 1) * tb]
        x1_ref[b] = x1
        h2s.append(_rms_mod(x1, g2_ref[...], mod_ref[b, 3:4, :], mod_ref[b, 4:5, :]))
        h2_ref[b] = h2s[-1].astype(h2_ref.dtype)
    if route:
        h2 = jnp.concatenate(h2s, axis=0)
        lane = lax.broadcasted_iota(jnp.int32, (1, 128), 1)
        h_hi = h2.astype(BF16)
        h_lo = (h2 - h_hi.astype(F32)).astype(BF16)
        logits = _dot(h_hi, wr_ref[0]) + _dot(h_lo, wr_ref[0]) + _dot(h_hi, wr_ref[1])
        lg = jnp.where(lane < N_EXPERTS, logits, -jnp.inf)
        v1 = lg.max(axis=-1, keepdims=True)
        i1 = jnp.where(lg == v1, lane, 128).min(axis=-1, keepdims=True)
        lg2 = jnp.where(lane == i1, -jnp.inf, lg)
        v2 = lg2.max(axis=-1, keepdims=True)
        i2 = jnp.where(lg2 == v2, lane, 128).min(axis=-1, keepdims=True)
        e = jnp.exp(v2 - v1)
        w1 = 1.0 / (1.0 + e)
        rt = jnp.where(lane == 0, i1.astype(F32),
                       jnp.where(lane == 1, i2.astype(F32),
                                 jnp.where(lane == 2, w1, jnp.where(lane == 3, e * w1, 0.0))))
        for b in range(B):
            rt_ref[b] = rt[b * tb:(b + 1) * tb]


def out_proj(x, mod, att, hy, yf, yb, u, scan_row0, d_skip, w_glu, b_glu, w_out_bf, g2, router_w=None):
    B, T, D = x.shape
    tb = TOK_TB
    route = router_w is not None
    S = u.shape[0]
    tok = lambda n: pl.BlockSpec((B, tb, n), lambda i: (0, i, 0))
    scan = pl.BlockSpec((S, tb * B, 128), lambda i: (0, i + scan_row0 // tb, 0))
    const = lambda shape: pl.BlockSpec(shape, lambda i: (0,) * len(shape))
    in_specs = [tok(D), const((B, 6, D)), tok(D_ATT), tok(D_HYENA),
                scan, scan, scan, const((1, D_SSM)), const((D_SSM, D_SSM)), const((1, D_SSM)),
                const((D, D)), const((1, D))]
    args = [x, mod, att, hy, yf, yb, u, d_skip.reshape(1, D_SSM), w_glu.astype(BF16), b_glu.reshape(1, D_SSM),
            w_out_bf, g2.reshape(1, D)]
    out_specs = [tok(D), tok(D)]
    out_shape = [jax.ShapeDtypeStruct((B, T, D), F32), jax.ShapeDtypeStruct((B, T, D), BF16)]
    if route:
        in_specs.append(const((2, D, 128)))
        wr = jnp.pad(router_w.astype(F32), ((0, 0), (0, 128 - N_EXPERTS)))
        wr_hi = wr.astype(BF16)
        args.append(jnp.stack([wr_hi, (wr - wr_hi.astype(F32)).astype(BF16)]))
        out_specs.append(tok(128))
        out_shape.append(jax.ShapeDtypeStruct((B, T, 128), F32))
    return pl.pallas_call(
        functools.partial(_outproj_kernel, route=route),
        grid=(T // tb,),
        in_specs=in_specs, out_specs=out_specs, out_shape=out_shape,
        compiler_params=_cparams(("parallel",), VMEM_LIMIT),
        name="out_proj",
    )(*args)


FF_SPLIT = 1280


def _swiglu_tile(h, w1_ref, w3_ref, w2_ref):
    acc = None
    for lo, hi in ((0, FF_SPLIT), (FF_SPLIT, D_FF)):
        a = _dot(h, w1_ref[:, lo:hi])
        g = (a * jax.nn.sigmoid(a) * _dot(h, w3_ref[:, lo:hi])).astype(BF16)
        part = _dot(g, w2_ref[lo:hi, :])
        acc = part if acc is None else acc + part
    return acc


def _ffn_kernel(h_ref, x_ref, mod_ref, w1_ref, w3_ref, w2_ref, o_ref):
    o_ref[...] = x_ref[...] + mod_ref[5:6, :] * _swiglu_tile(h_ref[...], w1_ref, w3_ref, w2_ref)


def ffn_dense(h2, x1, mod, w1, w3, w2, tm=256):
    B, T, D = x1.shape
    tm = min(tm, T)
    tok = pl.BlockSpec((None, tm, D), lambda b, i: (b, i, 0))
    once = pl.Buffered(1)
    return pl.pallas_call(
        _ffn_kernel,
        grid=(B, T // tm),
        in_specs=[tok, tok, pl.BlockSpec((None, 6, D), lambda b, i: (b, 0, 0)),
                  pl.BlockSpec((D, D_FF), lambda b, i: (0, 0), pipeline_mode=once),
                  pl.BlockSpec((D, D_FF), lambda b, i: (0, 0), pipeline_mode=once),
                  pl.BlockSpec((D_FF, D), lambda b, i: (0, 0), pipeline_mode=once)],
        out_specs=tok,
        out_shape=jax.ShapeDtypeStruct((B, T, D), F32),
        compiler_params=_cparams(("parallel", "parallel"), VMEM_LIMIT),
        name="ffn_dense",
    )(h2, x1, mod, w1, w3, w2)


MOE_TM = 512


def _moe_kernel(te_ref, tv_ref, h_ref, w1_ref, w3_ref, w2_ref, o_ref):
    valid = tv_ref[pl.program_id(0)] > 0

    @pl.when(valid)
    def _():
        o_ref[...] = _swiglu_tile(h_ref[...], w1_ref, w3_ref, w2_ref).astype(o_ref.dtype)

    @pl.when(jnp.logical_not(valid))
    def _():
        o_ref[...] = jnp.zeros_like(o_ref)


def moe_experts(hs, tile_expert, tile_valid, w1, w3, w2):
    P, D = hs.shape
    nt = P // MOE_TM
    tok = pl.BlockSpec((MOE_TM, D), lambda i, te, tv: (i, 0))
    return pl.pallas_call(
        _moe_kernel,
        grid_spec=pltpu.PrefetchScalarGridSpec(
            num_scalar_prefetch=2,
            grid=(nt,),
            in_specs=[tok,
                      pl.BlockSpec((None, D, D_FF), lambda i, te, tv: (te[i], 0, 0)),
                      pl.BlockSpec((None, D, D_FF), lambda i, te, tv: (te[i], 0, 0)),
                      pl.BlockSpec((None, D_FF, D), lambda i, te, tv: (te[i], 0, 0))],
            out_specs=tok),
        out_shape=jax.ShapeDtypeStruct((P, D), BF16),
        compiler_params=_cparams(("arbitrary",), VMEM_LIMIT),
        name="moe_experts",
    )(tile_expert, tile_valid, hs, w1, w3, w2)


def _combine_kernel(x_ref, mod_ref, ya_ref, yb_ref, rt_ref, o_ref):
    wa = rt_ref[:, 2:3]
    wb = rt_ref[:, 3:4]
    o_ref[...] = x_ref[...] + mod_ref[5:6, :] * (wa * ya_ref[...] + wb * yb_ref[...])


def moe_combine(x1, mod, ya, yb, rt, tm=512):
    B, T, D = x1.shape
    tok = lambda n: pl.BlockSpec((None, tm, n), lambda b, i: (b, i, 0))
    return pl.pallas_call(
        _combine_kernel,
        grid=(B, T // tm),
        in_specs=[tok(D), pl.BlockSpec((None, 6, D), lambda b, i: (b, 0, 0)), tok(D), tok(D), tok(128)],
        out_specs=tok(D),
        out_shape=jax.ShapeDtypeStruct((B, T, D), F32),
        compiler_params=_cparams(("parallel", "parallel")),
        name="moe_combine",
    )(x1, mod, ya, yb, rt)


def moe_layer(h2, x1, mod, rt, w1, w3, w2):
    B, T, D = x1.shape
    n = B * T
    e_flat = jnp.concatenate([rt[..., 0].reshape(n), rt[..., 1].reshape(n)]).astype(jnp.int32)
    order = jnp.argsort(e_flat, stable=True).astype(jnp.int32)
    rank = jnp.argsort(order).astype(jnp.int32)
    onehot = jax.nn.one_hot(e_flat, N_EXPERTS, dtype=jnp.int32)
    counts = jnp.sum(onehot, axis=0)
    padded = ((counts + MOE_TM - 1) // MOE_TM) * MOE_TM
    pend = jnp.cumsum(padded)
    shift = (pend - padded) - (jnp.cumsum(counts) - counts)
    P = 2 * n + N_EXPERTS * MOE_TM
    nt = P // MOE_TM
    tile_row = jnp.arange(nt, dtype=jnp.int32) * MOE_TM
    tile_expert = jnp.sum((tile_row[:, None] >= pend[None, :]).astype(jnp.int32), axis=1)
    tile_valid = (tile_expert < N_EXPERTS).astype(jnp.int32)
    tile_expert = jnp.minimum(tile_expert, N_EXPERTS - 1)
    slot = rank + jnp.sum(onehot * shift[None, :], axis=1)
    tile_shift = jnp.sum(jax.nn.one_hot(tile_expert, N_EXPERTS, dtype=jnp.int32) * shift[None, :], axis=1)
    row_rank = jnp.arange(P, dtype=jnp.int32) - jnp.repeat(tile_shift, MOE_TM)
    src = order[jnp.clip(row_rank, 0, 2 * n - 1)] % n
    take = lambda a, idx: a.at[idx].get(mode='promise_in_bounds')
    hs = take(h2.reshape(n, D), src)
    ys = moe_experts(hs, tile_expert, tile_valid, w1, w3, w2)
    ya = take(ys, slot[:n]).reshape(B, T, D)
    yb = take(ys, slot[n:]).reshape(B, T, D)
    return moe_combine(x1, mod, ya, yb, rt)


def _hyena(hz, conv_w, conv_b, spec, bias):
    zc = short_conv(hz, conv_w, conv_b)
    B, L, C = zc.shape
    lmax = FFT_N // 2
    if L < lmax:
        slot = lmax // B
        assert slot >= 2 * L
        zc = jnp.pad(zc, ((0, 0), (0, slot - L), (0, 0))).reshape(1, lmax, C)
    y = gated_long_conv(zc, 0, zc, D_HYENA, bias[0], spec, 0)
    y = gated_long_conv(y, 0, zc, 2 * D_HYENA, bias[1], spec, D_HYENA)
    if L < lmax:
        y = y.reshape(B, lmax // B, D_HYENA)[:, :L]
    return y


def _split_route(outs, mod):
    x1, h2 = outs[0], outs[1]
    return h2, x1, mod, (outs[2] if len(outs) > 2 else None)


def kernel(x, c, ctx, c_ctx, w_ada, b_ada, g_norm1, g_norm2, w_in, w_out, g_q, g_k, rpb, hy_conv_w, hy_conv_b, filt_w1, filt_b1, filt_w2, filt_b2, filt_w3, filt_b3, filt_freq, filt_w_out, hy_bias, lam_re, lam_im, log_step, b_re, b_im, c_re, c_im, d_skip, w_glu, b_glu, ffn_w1, ffn_w3, ffn_w2, router_w, moe_w1, moe_w3, moe_w2):
    B, L, D = x.shape
    Lc = ctx.shape[1]
    c8 = jnp.zeros((8, D), F32).at[:B].set(c).at[B].set(c_ctx)
    mod_all = adaln_mod(c8, w_ada, b_ada)
    for layer in range(DEPTH):
        last = layer == DEPTH - 1
        m = mod_all[layer].reshape(8, 6, D)
        mod_x = m[:B]
        mod_c = jnp.broadcast_to(m[B], (B, 6, D))
        w_l = w_in[layer].astype(BF16)
        wo_l = w_out[layer].astype(BF16)
        filt = (filt_w1[layer], filt_b1[layer], filt_w2[layer], filt_b2[layer],
                filt_w3[layer], filt_b3[layer], filt_freq[layer], filt_w_out[layer])
        ssm = (lam_re[layer], lam_im[layer], log_step[layer], b_re[layer], b_im[layer], c_re[layer], c_im[layer])

        q_c, k_c, v_c, hz_c, u_c = in_proj(ctx, mod_c, g_norm1[layer], w_l, g_q[layer], g_k[layer])
        q, k, v, hz, u = in_proj(x, mod_x, g_norm1[layer], w_l, g_q[layer], g_k[layer])
        att = neighbourhood_attention(q, k, v, k_c, v_c, rpb[layer])
        spec = filter_spectrum(hyena_filter_taps(L, *filt))
        hy = _hyena(hz, hy_conv_w[layer], hy_conv_b[layer], spec, hy_bias[layer])
        un, yf, yb = s5_scan(u, u_c, *ssm)

        tail = (d_skip[layer], w_glu[layer], b_glu[layer], wo_l, g_norm2[layer])
        i = layer // 2
        if layer % 2 == 0:
            ffw = (ffn_w1[i].astype(BF16), ffn_w3[i].astype(BF16), ffn_w2[i].astype(BF16))
            mixer = lambda h2, x1, mod, rt: ffn_dense(h2, x1, mod, *ffw)
            rw = None
        else:
            mow = (moe_w1[i].astype(BF16), moe_w3[i].astype(BF16), moe_w2[i].astype(BF16))
            mixer = lambda h2, x1, mod, rt: moe_layer(h2, x1, mod, rt, *mow)
            rw = router_w[i]
        x = mixer(*_split_route(out_proj(x, mod_x, att, hy, yf, yb, un, 0, *tail, router_w=rw), mod_x))

        if not last:
            att_c = context_attention(q_c, k_c, v_c)
            spec_c = filter_spectrum(hyena_filter_taps(Lc, *filt))
            hy_c = _hyena(hz_c, hy_conv_w[layer], hy_conv_b[layer], spec_c, hy_bias[layer])
            ctx = mixer(*_split_route(out_proj(ctx, mod_c, att_c, hy_c, yf, yb, un, L, *tail, router_w=rw), mod_c))
    return x
```

```python
import functools
import math

import numpy as np
import jax
import jax.numpy as jnp
from jax import lax
from jax.experimental import pallas as pl
from jax.experimental.pallas import tpu as pltpu

F32 = jnp.float32
BF16 = jnp.bfloat16
HIGHEST = lax.Precision.HIGHEST

D_MODEL = 1024
DEPTH = 2
GRID_W = 64
ATT_HEAD_DIM = 64
ATT_HEADS = 8
D_ATT = 512
WIN_ROWS = 8
WIN_COLS = 16
D_HYENA = 256
HYENA_ORDER = 2
FILTER_EMB = 33
FILTER_BANDS = 16
FILTER_WIDTH = 64
MIN_DECAY = math.log(1e-2) / 1.5
MAX_DECAY = math.log(1e-2) / 0.3
D_SSM = 256
SSM_GROUP_WIDTH = 16
SSM_GROUPS = 16
SSM_STATE = 64
COL_K = D_ATT
COL_V = 2 * D_ATT
COL_HY = 3 * D_ATT
COL_SSM = COL_HY + 3 * D_HYENA
D_IN = COL_SSM + D_SSM
D_FF = 2816
N_EXPERTS = 8
EPS = 1e-6
NEG = -1e30

VMEM_LIMIT = 56 * 1024 * 1024


def _cparams(sem, vmem=None):
    return pltpu.CompilerParams(dimension_semantics=sem, vmem_limit_bytes=vmem)


def _dot(a, b):
    return jnp.dot(a, b, preferred_element_type=F32)


def _dot_nt(a, b):
    return lax.dot_general(a, b, (((1,), (1,)), ((), ())), preferred_element_type=F32)


def _mod_kernel(c_ref, w_ref, b_ref, o_ref):
    c = c_ref[...]
    s = c * jax.nn.sigmoid(c)
    o_ref[...] = jnp.dot(s, w_ref[...], precision=HIGHEST, preferred_element_type=F32) + b_ref[...]


def adaln_mod(c8, w_ada, b_ada, tn=512):
    depth, d, n = w_ada.shape
    return pl.pallas_call(
        _mod_kernel,
        grid=(depth, n // tn),
        in_specs=[pl.BlockSpec((8, d), lambda l, j: (0, 0)),
                  pl.BlockSpec((None, d, tn), lambda l, j: (l, 0, j)),
                  pl.BlockSpec((None, 1, tn), lambda l, j: (l, 0, j))],
        out_specs=pl.BlockSpec((None, 8, tn), lambda l, j: (l, 0, j)),
        out_shape=jax.ShapeDtypeStruct((depth, 8, n), F32),
        compiler_params=_cparams(("parallel", "parallel")),
        name="adaln_mod",
    )(c8, w_ada, b_ada.reshape(depth, 1, n))


def _rms_mod(x, g, shift, scale):
    ms = jnp.mean(x * x, axis=-1, keepdims=True)
    h = x * lax.rsqrt(ms + EPS) * g
    return h * (1.0 + scale) + shift


def _head_norm(z, a, g):
    zz = (z * z).astype(BF16)
    m = jnp.concatenate([_dot(zz[:, c:c + ATT_LANES], a) for c in range(0, z.shape[1], ATT_LANES)], axis=1)
    return z * lax.rsqrt(m + EPS) * g


TOK_TB = 128


def _inproj_kernel(x_ref, mod_ref, g_ref, w_ref, a_ref, gq_ref, gk_ref,
                   q_ref, k_ref, v_ref, hz_ref, u_ref):
    B, tb, _ = x_ref.shape
    h = jnp.concatenate([_rms_mod(x_ref[b], g_ref[...], mod_ref[b, 0:1, :], mod_ref[b, 1:2, :]).astype(BF16)
                         for b in range(B)], axis=0)
    a = a_ref[...]
    q = (_head_norm(_dot(h, w_ref[:, 0:COL_K]), a, gq_ref[...]) * (ATT_HEAD_DIM ** -0.5)).astype(BF16)
    k = _head_norm(_dot(h, w_ref[:, COL_K:COL_V]), a, gk_ref[...]).astype(BF16)
    v = _dot(h, w_ref[:, COL_V:COL_HY]).astype(BF16)
    hz = _dot(h, w_ref[:, COL_HY:COL_SSM])
    u = _dot(h, w_ref[:, COL_SSM:D_IN])
    for b in range(B):
        rows = slice(b * tb, (b + 1) * tb)
        q_ref[b] = q[rows]
        k_ref[b] = k[rows]
        v_ref[b] = v[rows]
        hz_ref[b] = hz[rows]
        for s in range(D_SSM // 128):
            u_ref[s, pl.ds(b, tb, stride=B), :] = u[rows, s * 128:(s + 1) * 128]


def in_proj(x, mod, g1, w_in_bf, gq, gk):
    B, T, D = x.shape
    tb = TOK_TB
    head_avg = jnp.asarray(np.kron(np.eye(ATT_LANES // ATT_HEAD_DIM),
                                   np.full((ATT_HEAD_DIM, ATT_HEAD_DIM), 1.0 / ATT_HEAD_DIM)), BF16)
    tok = lambda n: pl.BlockSpec((B, tb, n), lambda i: (0, i, 0))
    const = lambda shape: pl.BlockSpec(shape, lambda i: (0,) * len(shape))
    return pl.pallas_call(
        _inproj_kernel,
        grid=(T // tb,),
        in_specs=[tok(D), const((B, 6, D)), const((1, D)),
                  const((D, D_IN)), const((ATT_LANES, ATT_LANES)), const((1, D_ATT)), const((1, D_ATT))],
        out_specs=[tok(D_ATT), tok(D_ATT), tok(D_ATT), tok(3 * D_HYENA),
                   pl.BlockSpec((D_SSM // 128, tb * B, 128), lambda i: (0, i, 0))],
        out_shape=[jax.ShapeDtypeStruct((B, T, D_ATT), BF16)] * 3
                  + [jax.ShapeDtypeStruct((B, T, 3 * D_HYENA), F32),
                     jax.ShapeDtypeStruct((D_SSM // 128, T * B, 128), F32)],
        compiler_params=_cparams(("parallel",), VMEM_LIMIT),
        name="in_proj",
    )(x, mod, g1.reshape(1, D), w_in_bf, head_avg,
      jnp.tile(gq, ATT_HEADS).reshape(1, D_ATT), jnp.tile(gk, ATT_HEADS).reshape(1, D_ATT))


ATT_R = 4
ATT_KROWS = ATT_R + WIN_ROWS - 1
ATT_LANES = 256


def _softmax_heads(q, parts, bias_ref, o_ref):
    nh = ATT_LANES // ATT_HEAD_DIM
    lane = lax.broadcasted_iota(jnp.int32, (1, ATT_LANES), 1)
    head = lambda h: (lane >= ATT_HEAD_DIM * h) & (lane < ATT_HEAD_DIM * (h + 1))
    acc = jnp.zeros((q.shape[0], ATT_LANES), F32)
    for h in range(nh):
        hm = head(h)
        qh = jnp.where(hm, q, jnp.zeros_like(q))
        ss = [_dot_nt(qh, kk).astype(BF16) for kk, _ in parts]
        if bias_ref is not None:
            ss[0] = ss[0] + bias_ref[h]
        m = ss[0].max(axis=-1, keepdims=True)
        for s in ss[1:]:
            m = jnp.maximum(m, s.max(axis=-1, keepdims=True))
        om = head((h + 1) % nh)
        o = None
        for s, (_, vv) in zip(ss, parts):
            part = _dot(jnp.exp(s - m), jnp.where(om, jnp.ones_like(vv), vv))
            o = part if o is None else o + part
        l = pltpu.roll(o, ATT_LANES - ATT_HEAD_DIM, axis=1)
        acc = jnp.where(hm, o / l, acc)
    o_ref[...] = acc.astype(o_ref.dtype)


def _nattn_kernel(q_ref, k_ref, v_ref, kc_ref, vc_ref, bias_ref, o_ref, *, rows):
    j = pl.program_id(2)
    ks = jnp.clip(j * ATT_R - WIN_ROWS // 2, 0, rows - ATT_KROWS)
    start = pl.multiple_of(ks * GRID_W, GRID_W)
    kl = k_ref[pl.ds(start, ATT_KROWS * GRID_W), :]
    vl = v_ref[pl.ds(start, ATT_KROWS * GRID_W), :]
    _softmax_heads(q_ref[...], [(kl, vl), (kc_ref[...], vc_ref[...])], bias_ref, o_ref)


def _attn_bias_tables(rpb, rows):
    nblk = rows // ATT_R
    qc = np.arange(GRID_W)
    c0 = np.clip(qc - WIN_COLS // 2, 0, GRID_W - WIN_COLS)
    kc = np.arange(GRID_W)
    col_ok = (kc[None, :] >= c0[:, None]) & (kc[None, :] < c0[:, None] + WIN_COLS)
    col_off = kc[None, :] - qc[:, None] + WIN_COLS - 1
    col_sel = (col_off[:, :, None] == np.arange(2 * WIN_COLS - 1)) & col_ok[:, :, None]
    band = jnp.einsum('hij,qkj->hqik', rpb, jnp.asarray(col_sel, F32), precision=HIGHEST)
    band = jnp.where(jnp.asarray(col_ok)[None, :, None, :], band, NEG)
    band = band.reshape(ATT_HEADS, GRID_W, (2 * WIN_ROWS - 1) * GRID_W).astype(BF16)
    neg = lambda nrows: jnp.full((ATT_HEADS, GRID_W, nrows * GRID_W), NEG, BF16)
    tabs = []
    for blk in (0, 1, nblk - 1):
        ks = int(np.clip(blk * ATT_R - WIN_ROWS // 2, 0, rows - ATT_KROWS))
        qrows = []
        for r in range(blk * ATT_R, (blk + 1) * ATT_R):
            r0 = int(np.clip(r - WIN_ROWS // 2, 0, rows - WIN_ROWS))
            first = r0 - r + WIN_ROWS - 1
            win = band[:, :, first * GRID_W:(first + WIN_ROWS) * GRID_W]
            qrows.append(jnp.concatenate([neg(r0 - ks), win, neg(ATT_KROWS - WIN_ROWS - (r0 - ks))], axis=-1))
        tabs.append(jnp.stack(qrows, axis=1).reshape(ATT_HEADS, ATT_R * GRID_W, ATT_KROWS * GRID_W))
    return jnp.stack(tabs)


def neighbourhood_attention(q, k, v, kc, vc, rpb):
    B, L, _ = q.shape
    Lc = kc.shape[1]
    rows = L // GRID_W
    nblk = rows // ATT_R
    tq = ATT_R * GRID_W
    nkl = ATT_KROWS * GRID_W
    bias = _attn_bias_tables(rpb, rows)
    hpb = ATT_LANES // ATT_HEAD_DIM
    variant = lambda j: jnp.where(j == 0, 0, jnp.where(j == nblk - 1, 2, 1))
    return pl.pallas_call(
        functools.partial(_nattn_kernel, rows=rows),
        grid=(B, D_ATT // ATT_LANES, nblk),
        in_specs=[pl.BlockSpec((None, tq, ATT_LANES), lambda b, g, j: (b, j, g)),
                  pl.BlockSpec((None, L, ATT_LANES), lambda b, g, j: (b, 0, g)),
                  pl.BlockSpec((None, L, ATT_LANES), lambda b, g, j: (b, 0, g)),
                  pl.BlockSpec((None, Lc, ATT_LANES), lambda b, g, j: (b, 0, g)),
                  pl.BlockSpec((None, Lc, ATT_LANES), lambda b, g, j: (b, 0, g)),
                  pl.BlockSpec((None, hpb, tq, nkl), lambda b, g, j: (variant(j), g, 0, 0))],
        out_specs=pl.BlockSpec((None, tq, ATT_LANES), lambda b, g, j: (b, j, g)),
        out_shape=jax.ShapeDtypeStruct((B, L, D_ATT), BF16),
        compiler_params=_cparams(("parallel", "parallel", "arbitrary"), VMEM_LIMIT),
        name="nattn",
    )(q, k, v, kc, vc, bias)


def _cattn_kernel(q_ref, k_ref, v_ref, o_ref):
    _softmax_heads(q_ref[...], [(k_ref[...], v_ref[...])], None, o_ref)


def context_attention(q, k, v):
    B, Lc, _ = q.shape
    spec = pl.BlockSpec((None, Lc, ATT_LANES), lambda b, g: (b, 0, g))
    return pl.pallas_call(
        _cattn_kernel,
        grid=(B, D_ATT // ATT_LANES),
        in_specs=[spec, spec, spec],
        out_specs=spec,
        out_shape=jax.ShapeDtypeStruct((B, Lc, D_ATT), BF16),
        compiler_params=_cparams(("parallel", "parallel")),
        name="cattn",
    )(q, k, v)


FFT_N = 16384
FFT_R = 128
FFT_K1 = FFT_R // 2 + 1
FFT_K1P = 66
FFT_PITCH = 2 * FFT_K1P


def _filter_kernel(w1_ref, wt_ref, b1_ref, w2_ref, b2_ref, w3_ref, b3_ref, fr_ref, wo_ref, o_ref, *, lf, tile):
    i = pl.program_id(0)
    half = FFT_N // 2
    row0 = i * tile
    first_pos = jnp.where(row0 < half, row0, FFT_N - row0 - tile)

    @pl.when(first_pos >= lf)
    def _():
        o_ref[...] = jnp.zeros_like(o_ref)

    @pl.when(first_pos < lf)
    def _():
        rows = tile // FILTER_PACK

        def position(width):
            r = lax.broadcasted_iota(jnp.int32, (rows, FILTER_PACK * width), 0)
            g = lax.broadcasted_iota(jnp.int32, (rows, FILTER_PACK * width), 1) // width
            j = row0 + r + g * rows
            return jnp.where(j < half, j, FFT_N - 1 - j)

        nfeat = 2 * FILTER_BANDS
        feat = lax.broadcasted_iota(jnp.int32, (1, FILTER_PACK * nfeat), 1) % nfeat
        f = 1e-4 + (feat % FILTER_BANDS).astype(F32) * ((FILTER_BANDS - 1 - 1e-4) / (FILTER_BANDS - 1))
        w = (2.0 * math.pi / lf) * position(nfeat).astype(F32)
        z = jnp.cos(f * w + jnp.where(feat < FILTER_BANDS, 0.0, 0.5 * math.pi))
        t_h = position(FILTER_WIDTH).astype(F32) / (lf - 1.0)
        fr = fr_ref[...]
        hdot = lambda a, b: jnp.dot(a, b, precision=HIGHEST, preferred_element_type=F32)
        h = jnp.sin(fr * (hdot(z, w1_ref[...]) + t_h * wt_ref[...] + b1_ref[...]))
        h = jnp.sin(fr * (hdot(h, w2_ref[...]) + b2_ref[...]))
        h = jnp.sin(fr * (hdot(h, w3_ref[...]) + b3_ref[...]))
        y = hdot(h, wo_ref[...])
        nout = HYENA_ORDER * D_HYENA
        pos_o = position(nout)
        ch = lax.broadcasted_iota(jnp.int32, (1, FILTER_PACK * nout), 1) % D_HYENA
        delta = jnp.abs(MIN_DECAY + ch.astype(F32) * ((MAX_DECAY - MIN_DECAY) / (D_HYENA - 1)))
        y = jnp.where(pos_o < lf, y * jnp.exp(-(pos_o.astype(F32) / (lf - 1.0)) * delta), 0.0)
        for g in range(FILTER_PACK):
            o_ref[g * rows:(g + 1) * rows, :] = y[:, g * nout:(g + 1) * nout]


FILTER_PACK = 4


def hyena_filter_taps(lf, w1, b1, w2, b2, w3, b3, freq, w_out, tile=1024):
    P, W = FILTER_PACK, FILTER_WIDTH
    nout = HYENA_ORDER * D_HYENA
    bd = lambda m: jnp.kron(jnp.eye(P, dtype=F32), m.astype(F32))
    wide = lambda a: jnp.tile(a.astype(F32), P).reshape(1, P * W)
    wo = w_out.reshape(W, HYENA_ORDER, 2, D_HYENA).transpose(2, 0, 1, 3).reshape(2, W, nout)
    wo = jnp.stack([bd(wo[0]), bd(wo[1])])
    nt = FFT_N // tile
    const = lambda shape: pl.BlockSpec(shape, lambda i: (0,) * len(shape))
    return pl.pallas_call(
        functools.partial(_filter_kernel, lf=lf, tile=tile),
        grid=(nt,),
        in_specs=[const((P * (FILTER_EMB - 1), P * W)), const((1, P * W)), const((1, P * W)),
                  const((P * W, P * W)), const((1, P * W)), const((P * W, P * W)), const((1, P * W)),
                  const((1, P * W)),
                  pl.BlockSpec((None, P * W, P * nout), lambda i: (i // (nt // 2), 0, 0))],
        out_specs=pl.BlockSpec((tile, nout), lambda i: (i, 0)),
        out_shape=jax.ShapeDtypeStruct((FFT_N, nout), F32),
        compiler_params=_cparams(("parallel",)),
        name="hyena_filter",
    )(bd(w1[1:]), wide(w1[0]), wide(b1), bd(w2), wide(b2), bd(w3), wide(b3), wide(freq), wo)


def _shortconv_kernel(z_ref, w_ref, b_ref, o_ref, zp_ref, *, L):
    C = z_ref.shape[-1]
    zp_ref[0:8, :] = jnp.zeros((8, C), F32)
    zp_ref[L + 8:L + 16, :] = jnp.zeros((8, C), F32)
    zp_ref[8:L + 8, :] = z_ref[...]
    ch = min(L, 512)
    for c in range(L // ch):
        s = c * ch
        o_ref[s:s + ch, :] = (b_ref[...] + zp_ref[s + 7:s + 7 + ch, :] * w_ref[0:1, :]
                              + zp_ref[s + 8:s + 8 + ch, :] * w_ref[1:2, :]
                              + zp_ref[s + 9:s + 9 + ch, :] * w_ref[2:3, :])


def short_conv(z, w, b, cb=128):
    B, L, C = z.shape
    return pl.pallas_call(
        functools.partial(_shortconv_kernel, L=L),
        grid=(B, C // cb),
        in_specs=[pl.BlockSpec((None, L, cb), lambda b_, c: (b_, 0, c)),
                  pl.BlockSpec((3, cb), lambda b_, c: (0, c)),
                  pl.BlockSpec((1, cb), lambda b_, c: (0, c))],
        out_specs=pl.BlockSpec((None, L, cb), lambda b_, c: (b_, 0, c)),
        out_shape=jax.ShapeDtypeStruct((B, L, C), F32),
        scratch_shapes=[pltpu.VMEM((L + 16, cb), F32)],
        compiler_params=_cparams(("parallel", "parallel")),
        name="short_conv",
    )(z, w, b.reshape(1, C))


@functools.lru_cache(maxsize=None)
def _fft_tables(n1_in):
    n2 = np.arange(FFT_R)[:, None, None]
    k1 = np.arange(FFT_K1)[None, :, None]
    n1 = np.arange(n1_in)[None, None, :]
    th = 2.0 * np.pi * ((k1 * (FFT_R * n1 + n2)) % FFT_N) / FFT_N
    ga = np.zeros((FFT_R, 2 * FFT_K1P, n1_in))
    ga[:, :FFT_K1] = np.cos(th)
    ga[:, FFT_K1P:FFT_K1P + FFT_K1] = -np.sin(th)
    wk = np.where((np.arange(FFT_K1) == 0) | (np.arange(FFT_K1) == FFT_R // 2), 1.0, 2.0)[None, None, :]
    thd = th.transpose(0, 2, 1)
    gd = np.zeros((FFT_R, n1_in, 2 * FFT_K1P))
    gd[:, :, :FFT_K1] = wk * np.cos(thd)
    gd[:, :, FFT_K1P:FFT_K1P + FFT_K1] = -wk * np.sin(thd)
    a = np.arange(FFT_R)
    ph = 2.0 * np.pi * ((a[:, None] * a[None, :]) % FFT_R) / FFT_R
    c, s = np.cos(ph), np.sin(ph)
    fb = np.block([[c, s], [-s, c]])
    fc = np.block([[c, -s], [s, c]])
    return ga, gd, fb, fc


FFT_KG = 5
FFT_UNROLL = 8


def _spectrum_slabs(ya_ref, fb, k0):
    cols = []
    for g in range(FFT_KG):
        re = ya_ref[pl.ds(k0 + g, FFT_R, stride=FFT_PITCH), :]
        im = ya_ref[pl.ds(FFT_K1P + k0 + g, FFT_R, stride=FFT_PITCH), :]
        cols.append(jnp.concatenate([re, im], axis=0).astype(BF16))
    return _dot(fb, jnp.concatenate(cols, axis=1))


def _stage_a(x_ref, x_pitch, ga_ref, ya_ref, n1_in):
    def body(n2, carry):
        xs = x_ref[pl.ds(n2, n1_in, stride=x_pitch), :].astype(BF16)
        ya_ref[pl.ds(n2 * FFT_PITCH, FFT_PITCH), :] = _dot(ga_ref[n2], xs)
        return carry
    lax.fori_loop(0, FFT_R, body, 0, unroll=FFT_UNROLL)


def _spectrum_kernel(x_ref, ga_ref, fb_ref, o_ref, ya_ref):
    _stage_a(x_ref, FFT_R, ga_ref, ya_ref, FFT_R)
    fb = fb_ref[...]
    cb = ya_ref.shape[1]

    def body(kg, carry):
        x = _spectrum_slabs(ya_ref, fb, kg * FFT_KG)
        for g in range(FFT_KG):
            o_ref[kg * FFT_KG + g] = x[:, g * cb:(g + 1) * cb]
        return carry
    lax.fori_loop(0, FFT_K1 // FFT_KG, body, 0)


def filter_spectrum(taps, cb=128):
    C = taps.shape[1]
    ga, _, fb, _ = _fft_tables(FFT_R)
    return pl.pallas_call(
        _spectrum_kernel,
        grid=(C // cb,),
        in_specs=[pl.BlockSpec((FFT_N, cb), lambda c: (0, c)),
                  pl.BlockSpec((FFT_R, FFT_PITCH, FFT_R), lambda c: (0, 0, 0)),
                  pl.BlockSpec((2 * FFT_R, 2 * FFT_R), lambda c: (0, 0))],
        out_specs=pl.BlockSpec((FFT_K1, 2 * FFT_R, cb), lambda c: (0, 0, c)),
        out_shape=jax.ShapeDtypeStruct((FFT_K1, 2 * FFT_R, C), F32),
        scratch_shapes=[pltpu.VMEM((FFT_R * FFT_PITCH, cb), F32)],
        compiler_params=_cparams(("parallel",), VMEM_LIMIT),
        name="filter_spectrum",
    )(taps, _bf16_table(ga), _bf16_table(fb))


def _bf16_table(t):
    return jnp.asarray(t, F32).astype(BF16)


def _longconv_kernel(u_ref, gate_ref, bias_ref, spec_ref, ga_ref, gd_ref, fb_ref, fc_ref, o_ref, ya_ref, xp_ref):
    n1_in = FFT_R // 2

    def repitch(n1, carry):
        xp_ref[pl.ds(n1 * FFT_PITCH, FFT_R), :] = u_ref[pl.ds(pl.multiple_of(n1 * FFT_R, FFT_R), FFT_R), :]
        return carry
    lax.fori_loop(0, n1_in, repitch, 0, unroll=FFT_UNROLL)

    _stage_a(xp_ref, FFT_PITCH, ga_ref, ya_ref, n1_in)
    fb = fb_ref[...]
    fc = fc_ref[...]

    cb = ya_ref.shape[1]

    def mid(kg, carry):
        k0 = kg * FFT_KG
        x = _spectrum_slabs(ya_ref, fb, k0)
        s = jnp.concatenate([spec_ref[k0 + g] for g in range(FFT_KG)], axis=1)
        xr, xi = x[:FFT_R], x[FFT_R:]
        sr, si = s[:FFT_R], s[FFT_R:]
        z = jnp.concatenate([xr * sr - xi * si, xr * si + xi * sr], axis=0).astype(BF16)
        v = _dot(fc, z)
        for g in range(FFT_KG):
            ya_ref[pl.ds(k0 + g, FFT_R, stride=FFT_PITCH), :] = v[:FFT_R, g * cb:(g + 1) * cb]
            ya_ref[pl.ds(FFT_K1P + k0 + g, FFT_R, stride=FFT_PITCH), :] = v[FFT_R:, g * cb:(g + 1) * cb]
        return carry
    lax.fori_loop(0, FFT_K1 // FFT_KG, mid, 0)

    def last(n2, carry):
        slab = ya_ref[pl.ds(n2 * FFT_PITCH, FFT_PITCH), :].astype(BF16)
        xp_ref[pl.ds(n2, n1_in, stride=FFT_PITCH), :] = _dot(gd_ref[n2], slab)
        return carry
    lax.fori_loop(0, FFT_R, last, 0, unroll=FFT_UNROLL)

    bias = bias_ref[...]

    def gate(n1, carry):
        rows = pl.ds(pl.multiple_of(n1 * FFT_R, FFT_R), FFT_R)
        y = xp_ref[pl.ds(n1 * FFT_PITCH, FFT_R), :] * (1.0 / FFT_N)
        o_ref[rows, :] = gate_ref[rows, :] * (y + u_ref[rows, :] * bias)
        return carry
    lax.fori_loop(0, n1_in, gate, 0, unroll=FFT_UNROLL)


def gated_long_conv(u, u_col, gate, gate_col, bias, spec, spec_col, C=D_HYENA, cb=128):
    B, L, _ = u.shape
    n1_in = FFT_R // 2
    ga, gd, fb, fc = _fft_tables(n1_in)
    once = pl.Buffered(1)
    seq = lambda col: pl.BlockSpec((None, L, cb), lambda c, b: (b, 0, c + col // cb))
    return pl.pallas_call(
        _longconv_kernel,
        grid=(C // cb, B),
        in_specs=[seq(u_col), seq(gate_col),
                  pl.BlockSpec((1, cb), lambda c, b: (0, c)),
                  pl.BlockSpec((FFT_K1, 2 * FFT_R, cb), lambda c, b: (0, 0, c + spec_col // cb), pipeline_mode=once),
                  pl.BlockSpec((FFT_R, FFT_PITCH, n1_in), lambda c, b: (0, 0, 0), pipeline_mode=once),
                  pl.BlockSpec((FFT_R, n1_in, FFT_PITCH), lambda c, b: (0, 0, 0), pipeline_mode=once),
                  pl.BlockSpec((2 * FFT_R, 2 * FFT_R), lambda c, b: (0, 0)),
                  pl.BlockSpec((2 * FFT_R, 2 * FFT_R), lambda c, b: (0, 0))],
        out_specs=seq(0),
        out_shape=jax.ShapeDtypeStruct((B, L, C), F32),
        scratch_shapes=[pltpu.VMEM((FFT_R * FFT_PITCH, cb), F32), pltpu.VMEM((n1_in * FFT_PITCH, cb), F32)],
        compiler_params=_cparams(("parallel", "arbitrary"), VMEM_LIMIT),
        name="long_conv",
    )(u, gate, bias.reshape(1, C), spec, _bf16_table(ga), _bf16_table(gd), _bf16_table(fb), _bf16_table(fc))


SSM_N = SSM_GROUPS * SSM_STATE
S5_SEQS = 8
S5_CHUNK = 128


def _s5_kernel(uf_ref, ub_ref, bf_ref, bb_ref, lam_ref, cf_ref, cb_ref, yf_ref, yb_ref,
               buf_ref, bub_ref, xsf_ref, xsb_ref, st_ref):
    half = S5_SEQS // 2
    tiles = S5_CHUNK // 2

    @pl.when(pl.program_id(0) == 0)
    def _():
        st_ref[...] = jnp.zeros_like(st_ref)

    slabs = lambda ref: jnp.concatenate([ref[s] for s in range(ref.shape[0])], axis=1).astype(BF16)
    buf_ref[...] = _dot(slabs(uf_ref), bf_ref[...])
    bub_ref[...] = _dot(slabs(ub_ref), bb_ref[...])
    lo = lax.broadcasted_iota(jnp.int32, (S5_SEQS, 1), 0) < half

    def advance(x, lam, b):
        xr, xi = x
        lr, li = lam[:, 0:SSM_N], lam[:, SSM_N:2 * SSM_N]
        return lr * xr - li * xi + b[:, 0:SSM_N], lr * xi + li * xr + b[:, SSM_N:2 * SSM_N]

    def swap(x):
        return pltpu.roll(x[0], half, axis=0), pltpu.roll(x[1], half, axis=0)

    def step(t4, x):
        f_tiles, b_tiles = [], []
        for p in range(2):
            tf = buf_ref[pl.ds(pl.multiple_of((2 * t4 + p) * S5_SEQS, S5_SEQS), S5_SEQS), :]
            tb = bub_ref[pl.ds(pl.multiple_of((tiles - 1 - 2 * t4 - p) * S5_SEQS, S5_SEQS), S5_SEQS), :]
            xe = advance(x, lam_ref[0], jnp.where(lo, tf, tb))
            xo = advance(swap(xe), lam_ref[1], jnp.where(lo, tb, tf))
            x = swap(xo)
            xe = jnp.concatenate(xe, axis=1)
            xo = jnp.concatenate(xo, axis=1)
            f_tiles.append(jnp.where(lo, xe, xo))
            b_tiles.append(jnp.where(lo, xo, xe))
        xsf_ref[pl.ds(pl.multiple_of(t4 * 2 * S5_SEQS, 2 * S5_SEQS), 2 * S5_SEQS), :] = (
            jnp.concatenate(f_tiles, axis=0).astype(BF16))
        xsb_ref[pl.ds(pl.multiple_of((tiles - 2 - 2 * t4) * S5_SEQS, 2 * S5_SEQS), 2 * S5_SEQS), :] = (
            jnp.concatenate(b_tiles[::-1], axis=0).astype(BF16))
        return x

    xr, xi = lax.fori_loop(0, S5_CHUNK // 4, step, (st_ref[:, 0:SSM_N], st_ref[:, SSM_N:2 * SSM_N]))
    st_ref[:, 0:SSM_N] = xr
    st_ref[:, SSM_N:2 * SSM_N] = xi
    yf = _dot(xsf_ref[...], cf_ref[...])
    yb = _dot(xsb_ref[...], cb_ref[...])
    for s in range(yf_ref.shape[0]):
        yf_ref[s] = yf[:, s * 128:(s + 1) * 128]
        yb_ref[s] = yb[:, s * 128:(s + 1) * 128]


def _s5_matrices(lam_re, lam_im, log_step, b_re, b_im, c_re, c_im):
    G, P, Hs = SSM_GROUPS, SSM_STATE, SSM_GROUP_WIDTH
    dt = jnp.exp(log_step)[:, :, None]
    mag = jnp.exp(lam_re * dt)
    ar, ai = mag * jnp.cos(lam_im * dt), mag * jnp.sin(lam_im * dt)
    er, ei = ar - 1.0, ai
    den = lam_re * lam_re + lam_im * lam_im
    cr, ci = (er * lam_re + ei * lam_im) / den, (ei * lam_re - er * lam_im) / den
    bbr = cr[..., None] * b_re - ci[..., None] * b_im
    bbi = cr[..., None] * b_im + ci[..., None] * b_re
    eye = jnp.eye(G, dtype=F32)
    blk_in = lambda m: jnp.einsum('gq,qph->ghqp', eye, m).reshape(G * Hs, G * P)
    bmat = [jnp.concatenate([blk_in(bbr[d]), blk_in(bbi[d])], axis=1).astype(BF16) for d in range(2)]
    blk_out = lambda m: jnp.einsum('gq,qhp->gpqh', eye, m).reshape(G * P, G * Hs)
    cmat = [jnp.concatenate([blk_out(c_re[d]), -blk_out(c_im[d])], axis=0).astype(BF16) for d in range(2)]
    lam = jnp.concatenate([ar.reshape(2, 1, G * P), ai.reshape(2, 1, G * P)], axis=2)
    lam = jnp.broadcast_to(lam, (2, S5_SEQS // 2, 2 * G * P))
    lam8 = jnp.stack([lam.reshape(S5_SEQS, 2 * G * P), lam[::-1].reshape(S5_SEQS, 2 * G * P)])
    return bmat[0], bmat[1], lam8, cmat[0], cmat[1]


def s5_scan(u_x, u_c, lam_re, lam_im, log_step, b_re, b_im, c_re, c_im):
    S = u_x.shape[0]
    half = S5_SEQS // 2
    L, Lc = u_x.shape[1] // half, u_c.shape[1] // half
    assert L % S5_CHUNK == 0 and Lc % S5_CHUNK == 0
    bf, bb, lam8, cf, cb = _s5_matrices(lam_re, lam_im, log_step, b_re, b_im, c_re, c_im)
    un = jnp.concatenate([u_x, u_c], axis=1)
    rows = S5_CHUNK * half
    nx, nc = L // S5_CHUNK, Lc // S5_CHUNK
    n = nx + nc
    fwd = pl.BlockSpec((S, rows, 128), lambda i: (0, jnp.where(i < nc, nx + i, i - nc), 0))
    bwd = pl.BlockSpec((S, rows, 128), lambda i: (0, n - 1 - i, 0))
    const = lambda shape: pl.BlockSpec(shape, lambda i: (0,) * len(shape))
    yf, yb = pl.pallas_call(
        _s5_kernel,
        grid=(n,),
        in_specs=[fwd, bwd, const((D_SSM, 2 * SSM_N)), const((D_SSM, 2 * SSM_N)),
                  const((2, S5_SEQS, 2 * SSM_N)), const((2 * SSM_N, D_SSM)), const((2 * SSM_N, D_SSM))],
        out_specs=[fwd, bwd],
        out_shape=[jax.ShapeDtypeStruct(un.shape, F32)] * 2,
        scratch_shapes=[pltpu.VMEM((rows, 2 * SSM_N), F32), pltpu.VMEM((rows, 2 * SSM_N), F32),
                        pltpu.VMEM((rows, 2 * SSM_N), BF16), pltpu.VMEM((rows, 2 * SSM_N), BF16),
                        pltpu.VMEM((S5_SEQS, 2 * SSM_N), F32)],
        compiler_params=_cparams(("arbitrary",), VMEM_LIMIT),
        name="s5_scan",
    )(un, un, bf, bb, lam8, cf, cb)
    return un, yf, yb


def _gelu_tanh(y):
    return 0.5 * y * (1.0 + jnp.tanh(math.sqrt(2.0 / math.pi) * (y + 0.044715 * (y * y * y))))


def _outproj_kernel(*refs, route):
    (x_ref, mod_ref, att_ref, hy_ref, yf_ref, yb_ref, u_ref, dsk_ref, wg_ref, bg_ref, wo_ref, g2_ref) = refs[:12]
    if route:
        wr_ref, x1_ref, h2_ref, rt_ref = refs[12:]
    else:
        x1_ref, h2_ref = refs[12:]
    B, tb, D = x_ref.shape
    rows = lambda a: a.reshape(B * tb, a.shape[-1])

    def scan_rows(ref):
        return jnp.concatenate(
            [jnp.concatenate([ref[s, pl.ds(b, tb, stride=B), :] for s in range(ref.shape[0])], axis=1)
             for b in range(B)], axis=0)

    y = scan_rows(yf_ref) + scan_rows(yb_ref) + scan_rows(u_ref) * dsk_ref[...]
    y = _gelu_tanh(y)
    ss = y * jax.nn.sigmoid(_dot(y.astype(BF16), wg_ref[...]) + bg_ref[...])
    mix = (_dot(rows(att_ref[...]), wo_ref[0:D_ATT, :])
           + _dot(rows(hy_ref[...]).astype(BF16), wo_ref[D_ATT:D_ATT + D_HYENA, :])
           + _dot(ss.astype(BF16), wo_ref[D_ATT + D_HYENA:, :]))
    h2s = []
    for b in range(B):
        x1 = x_ref[b] + mod_ref[b, 2:3, :] * mix[b * tb:(b + 1) * tb]
        x1_ref[b] = x1
        h2s.append(_rms_mod(x1, g2_ref[...], mod_ref[b, 3:4, :], mod_ref[b, 4:5, :]))
        h2_ref[b] = h2s[-1].astype(h2_ref.dtype)
    if route:
        h2 = jnp.concatenate(h2s, axis=0)
        lane = lax.broadcasted_iota(jnp.int32, (1, 128), 1)
        h_hi = h2.astype(BF16)
        h_lo = (h2 - h_hi.astype(F32)).astype(BF16)
        logits = _dot(h_hi, wr_ref[0]) + _dot(h_lo, wr_ref[0]) + _dot(h_hi, wr_ref[1])
        lg = jnp.where(lane < N_EXPERTS, logits, -jnp.inf)
        v1 = lg.max(axis=-1, keepdims=True)
        i1 = jnp.where(lg == v1, lane, 128).min(axis=-1, keepdims=True)
        lg2 = jnp.where(lane == i1, -jnp.inf, lg)
        v2 = lg2.max(axis=-1, keepdims=True)
        i2 = jnp.where(lg2 == v2, lane, 128).min(axis=-1, keepdims=True)
        e = jnp.exp(v2 - v1)
        w1 = 1.0 / (1.0 + e)
        rt = jnp.where(lane == 0, i1.astype(F32),
                       jnp.where(lane == 1, i2.astype(F32),
                                 jnp.where(lane == 2, w1, jnp.where(lane == 3, e * w1, 0.0))))
        for b in range(B):
            rt_ref[b] = rt[b * tb:(b + 1) * tb]


def out_proj(x, mod, att, hy, yf, yb, u, scan_row0, d_skip, w_glu, b_glu, w_out_bf, g2, router_w=None):
    B, T, D = x.shape
    tb = TOK_TB
    route = router_w is not None
    S = u.shape[0]
    tok = lambda n: pl.BlockSpec((B, tb, n), lambda i: (0, i, 0))
    scan = pl.BlockSpec((S, tb * B, 128), lambda i: (0, i + scan_row0 // tb, 0))
    const = lambda shape: pl.BlockSpec(shape, lambda i: (0,) * len(shape))
    in_specs = [tok(D), const((B, 6, D)), tok(D_ATT), tok(D_HYENA),
                scan, scan, scan, const((1, D_SSM)), const((D_SSM, D_SSM)), const((1, D_SSM)),
                const((D, D)), const((1, D))]
    args = [x, mod, att, hy, yf, yb, u, d_skip.reshape(1, D_SSM), w_glu.astype(BF16), b_glu.reshape(1, D_SSM),
            w_out_bf, g2.reshape(1, D)]
    out_specs = [tok(D), tok(D)]
    out_shape = [jax.ShapeDtypeStruct((B, T, D), F32), jax.ShapeDtypeStruct((B, T, D), BF16)]
    if route:
        in_specs.append(const((2, D, 128)))
        wr = jnp.pad(router_w.astype(F32), ((0, 0), (0, 128 - N_EXPERTS)))
        wr_hi = wr.astype(BF16)
        args.append(jnp.stack([wr_hi, (wr - wr_hi.astype(F32)).astype(BF16)]))
        out_specs.append(tok(128))
        out_shape.append(jax.ShapeDtypeStruct((B, T, 128), F32))
    return pl.pallas_call(
        functools.partial(_outproj_kernel, route=route),
        grid=(T // tb,),
        in_specs=in_specs, out_specs=out_specs, out_shape=out_shape,
        compiler_params=_cparams(("parallel",), VMEM_LIMIT),
        name="out_proj",
    )(*args)


FF_CHUNK = 768
FF_VMEM_LIMIT = 60 * 1024 * 1024


def _swiglu_tile(h, w1_ref, w3_ref, w2_ref):
    acc = None
    for lo in range(0, D_FF, FF_CHUNK):
        hi = min(lo + FF_CHUNK, D_FF)
        a = _dot(h, w1_ref[:, lo:hi].astype(BF16))
        g = (a * jax.nn.sigmoid(a) * _dot(h, w3_ref[:, lo:hi].astype(BF16))).astype(BF16)
        part = _dot(g, w2_ref[lo:hi, :].astype(BF16))
        acc = part if acc is None else acc + part
    return acc


def _ffn_kernel(h_ref, x_ref, mod_ref, w1_ref, w3_ref, w2_ref, o_ref):
    o_ref[...] = x_ref[...] + mod_ref[5:6, :] * _swiglu_tile(h_ref[...], w1_ref, w3_ref, w2_ref)


def ffn_dense(h2, x1, mod, w1, w3, w2, tm=512):
    B, T, D = x1.shape
    tm = min(tm, T)
    tok = pl.BlockSpec((None, tm, D), lambda b, i: (b, i, 0))
    once = pl.Buffered(1)
    return pl.pallas_call(
        _ffn_kernel,
        grid=(B, T // tm),
        in_specs=[tok, tok, pl.BlockSpec((None, 6, D), lambda b, i: (b, 0, 0)),
                  pl.BlockSpec((D, D_FF), lambda b, i: (0, 0), pipeline_mode=once),
                  pl.BlockSpec((D, D_FF), lambda b, i: (0, 0), pipeline_mode=once),
                  pl.BlockSpec((D_FF, D), lambda b, i: (0, 0), pipeline_mode=once)],
        out_specs=tok,
        out_shape=jax.ShapeDtypeStruct((B, T, D), F32),
        compiler_params=_cparams(("parallel", "parallel"), FF_VMEM_LIMIT),
        name="ffn_dense",
    )(h2, x1, mod, w1, w3, w2)


MOE_TM = 512


def _moe_kernel(te_ref, tv_ref, h_ref, w1_ref, w3_ref, w2_ref, o_ref):
    valid = tv_ref[pl.program_id(0)] > 0

    @pl.when(valid)
    def _():
        o_ref[...] = _swiglu_tile(h_ref[...], w1_ref, w3_ref, w2_ref).astype(o_ref.dtype)

    @pl.when(jnp.logical_not(valid))
    def _():
        o_ref[...] = jnp.zeros_like(o_ref)


def moe_experts(hs, tile_expert, tile_valid, w1, w3, w2):
    P, D = hs.shape
    nt = P // MOE_TM
    tok = pl.BlockSpec((MOE_TM, D), lambda i, te, tv: (i, 0))
    once = pl.Buffered(1)
    return pl.pallas_call(
        _moe_kernel,
        grid_spec=pltpu.PrefetchScalarGridSpec(
            num_scalar_prefetch=2,
            grid=(nt,),
            in_specs=[tok,
                      pl.BlockSpec((None, D, D_FF), lambda i, te, tv: (te[i], 0, 0), pipeline_mode=once),
                      pl.BlockSpec((None, D, D_FF), lambda i, te, tv: (te[i], 0, 0), pipeline_mode=once),
                      pl.BlockSpec((None, D_FF, D), lambda i, te, tv: (te[i], 0, 0), pipeline_mode=once)],
            out_specs=tok),
        out_shape=jax.ShapeDtypeStruct((P, D), BF16),
        compiler_params=_cparams(("arbitrary",), FF_VMEM_LIMIT),
        name="moe_experts",
    )(tile_expert, tile_valid, hs, w1, w3, w2)


def _combine_kernel(x_ref, mod_ref, ya_ref, yb_ref, rt_ref, o_ref):
    wa = rt_ref[:, 2:3]
    wb = rt_ref[:, 3:4]
    o_ref[...] = x_ref[...] + mod_ref[5:6, :] * (wa * ya_ref[...] + wb * yb_ref[...])


def moe_combine(x1, mod, yab, rt, tm=512):
    B, T, D = x1.shape
    tok = lambda n: pl.BlockSpec((None, tm, n), lambda b, i: (b, i, 0))
    choice = lambda c: pl.BlockSpec((None, None, tm, D), lambda b, i: (c, b, i, 0))
    return pl.pallas_call(
        _combine_kernel,
        grid=(B, T // tm),
        in_specs=[tok(D), pl.BlockSpec((None, 6, D), lambda b, i: (b, 0, 0)), choice(0), choice(1), tok(128)],
        out_specs=tok(D),
        out_shape=jax.ShapeDtypeStruct((B, T, D), F32),
        compiler_params=_cparams(("parallel", "parallel")),
        name="moe_combine",
    )(x1, mod, yab, yab, rt)


def moe_layer(h2, x1, mod, rt, w1, w3, w2):
    B, T, D = x1.shape
    n = B * T
    e_flat = jnp.concatenate([rt[..., 0].reshape(n), rt[..., 1].reshape(n)]).astype(jnp.int32)
    order = jnp.argsort(e_flat, stable=True).astype(jnp.int32)
    rank = jnp.argsort(order).astype(jnp.int32)
    onehot = jax.nn.one_hot(e_flat, N_EXPERTS, dtype=jnp.int32)
    counts = jnp.sum(onehot, axis=0)
    padded = ((counts + MOE_TM - 1) // MOE_TM) * MOE_TM
    pend = jnp.cumsum(padded)
    shift = (pend - padded) - (jnp.cumsum(counts) - counts)
    P = 2 * n + N_EXPERTS * MOE_TM
    nt = P // MOE_TM
    tile_row = jnp.arange(nt, dtype=jnp.int32) * MOE_TM
    tile_expert = jnp.sum((tile_row[:, None] >= pend[None, :]).astype(jnp.int32), axis=1)
    tile_valid = (tile_expert < N_EXPERTS).astype(jnp.int32)
    tile_expert = jnp.minimum(tile_expert, N_EXPERTS - 1)
    slot = rank + jnp.sum(onehot * shift[None, :], axis=1)
    tile_shift = jnp.sum(jax.nn.one_hot(tile_expert, N_EXPERTS, dtype=jnp.int32) * shift[None, :], axis=1)
    row_rank = jnp.arange(P, dtype=jnp.int32) - jnp.repeat(tile_shift, MOE_TM)
    src = order[jnp.clip(row_rank, 0, 2 * n - 1)] % n
    take = lambda a, idx: a.at[idx].get(mode='promise_in_bounds')
    hs = take(h2.reshape(n, D), src)
    ys = moe_experts(hs, tile_expert, tile_valid, w1, w3, w2)
    return moe_combine(x1, mod, take(ys, slot).reshape(2, B, T, D), rt)


def _hyena(hz, conv_w, conv_b, spec, bias):
    zc = short_conv(hz, conv_w, conv_b)
    B, L, C = zc.shape
    lmax = FFT_N // 2
    if L < lmax:
        slot = lmax // B
        assert slot >= 2 * L
        zc = jnp.pad(zc, ((0, 0), (0, slot - L), (0, 0))).reshape(1, lmax, C)
    y = gated_long_conv(zc, 0, zc, D_HYENA, bias[0], spec, 0)
    y = gated_long_conv(y, 0, zc, 2 * D_HYENA, bias[1], spec, D_HYENA)
    if L < lmax:
        y = y.reshape(B, lmax // B, D_HYENA)[:, :L]
    return y


def _split_route(outs, mod):
    x1, h2 = outs[0], outs[1]
    return h2, x1, mod, (outs[2] if len(outs) > 2 else None)


def kernel(x, c, ctx, c_ctx, w_ada, b_ada, g_norm1, g_norm2, w_in, w_out, g_q, g_k, rpb, hy_conv_w, hy_conv_b, filt_w1, filt_b1, filt_w2, filt_b2, filt_w3, filt_b3, filt_freq, filt_w_out, hy_bias, lam_re, lam_im, log_step, b_re, b_im, c_re, c_im, d_skip, w_glu, b_glu, ffn_w1, ffn_w3, ffn_w2, router_w, moe_w1, moe_w3, moe_w2):
    B, L, D = x.shape
    Lc = ctx.shape[1]
    c8 = jnp.zeros((8, D), F32).at[:B].set(c).at[B].set(c_ctx)
    mod_all = adaln_mod(c8, w_ada, b_ada)
    for layer in range(DEPTH):
        last = layer == DEPTH - 1
        m = mod_all[layer].reshape(8, 6, D)
        mod_x = m[:B]
        mod_c = jnp.broadcast_to(m[B], (B, 6, D))
        w_l = w_in[layer].astype(BF16)
        wo_l = w_out[layer].astype(BF16)
        filt = (filt_w1[layer], filt_b1[layer], filt_w2[layer], filt_b2[layer],
                filt_w3[layer], filt_b3[layer], filt_freq[layer], filt_w_out[layer])
        ssm = (lam_re[layer], lam_im[layer], log_step[layer], b_re[layer], b_im[layer], c_re[layer], c_im[layer])

        q_c, k_c, v_c, hz_c, u_c = in_proj(ctx, mod_c, g_norm1[layer], w_l, g_q[layer], g_k[layer])
        q, k, v, hz, u = in_proj(x, mod_x, g_norm1[layer], w_l, g_q[layer], g_k[layer])
        att = neighbourhood_attention(q, k, v, k_c, v_c, rpb[layer])
        spec = filter_spectrum(hyena_filter_taps(L, *filt))
        hy = _hyena(hz, hy_conv_w[layer], hy_conv_b[layer], spec, hy_bias[layer])
        un, yf, yb = s5_scan(u, u_c, *ssm)

        tail = (d_skip[layer], w_glu[layer], b_glu[layer], wo_l, g_norm2[layer])
        i = layer // 2
        if layer % 2 == 0:
            ffw = (ffn_w1[i], ffn_w3[i], ffn_w2[i])
            mixer = lambda h2, x1, mod, rt: ffn_dense(h2, x1, mod, *ffw)
            rw = None
        else:
            mow = (moe_w1[i], moe_w3[i], moe_w2[i])
            mixer = lambda h2, x1, mod, rt: moe_layer(h2, x1, mod, rt, *mow)
            rw = router_w[i]
        x = mixer(*_split_route(out_proj(x, mod_x, att, hy, yf, yb, un, 0, *tail, router_w=rw), mod_x))

        if not last:
            att_c = context_attention(q_c, k_c, v_c)
            spec_c = filter_spectrum(hyena_filter_taps(Lc, *filt))
            hy_c = _hyena(hz_c, hy_conv_w[layer], hy_conv_b[layer], spec_c, hy_bias[layer])
            ctx = mixer(*_split_route(out_proj(ctx, mod_c, att_c, hy_c, yf, yb, un, L, *tail, router_w=rw), mod_c))
    return x
```

```python
import functools
import math

import numpy as np
import jax
import jax.numpy as jnp
from jax import lax
from jax.experimental import pallas as pl
from jax.experimental.pallas import tpu as pltpu

F32 = jnp.float32
BF16 = jnp.bfloat16
HIGHEST = lax.Precision.HIGHEST

D_MODEL = 1024
DEPTH = 2
GRID_W = 64
ATT_HEAD_DIM = 64
ATT_HEADS = 8
D_ATT = 512
WIN_ROWS = 8
WIN_COLS = 16
D_HYENA = 256
HYENA_ORDER = 2
FILTER_EMB = 33
FILTER_BANDS = 16
FILTER_WIDTH = 64
MIN_DECAY = math.log(1e-2) / 1.5
MAX_DECAY = math.log(1e-2) / 0.3
D_SSM = 256
SSM_GROUP_WIDTH = 16
SSM_GROUPS = 16
SSM_STATE = 64
COL_K = D_ATT
COL_V = 2 * D_ATT
COL_HY = 3 * D_ATT
COL_SSM = COL_HY + 3 * D_HYENA
D_IN = COL_SSM + D_SSM
D_FF = 2816
N_EXPERTS = 8
EPS = 1e-6
NEG = -1e30

VMEM_LIMIT = 56 * 1024 * 1024


def _cparams(sem, vmem=None):
    return pltpu.CompilerParams(dimension_semantics=sem, vmem_limit_bytes=vmem)


def _dot(a, b):
    return jnp.dot(a, b, preferred_element_type=F32)


def _dot_nt(a, b):
    return lax.dot_general(a, b, (((1,), (1,)), ((), ())), preferred_element_type=F32)


def _mod_kernel(c_ref, w_ref, b_ref, o_ref):
    c = c_ref[...]
    s = c * jax.nn.sigmoid(c)
    o_ref[...] = jnp.dot(s, w_ref[...], precision=HIGHEST, preferred_element_type=F32) + b_ref[...]


def adaln_mod(c8, w_ada, b_ada, tn=512):
    depth, d, n = w_ada.shape
    return pl.pallas_call(
        _mod_kernel,
        grid=(depth, n // tn),
        in_specs=[pl.BlockSpec((8, d), lambda l, j: (0, 0)),
                  pl.BlockSpec((None, d, tn), lambda l, j: (l, 0, j)),
                  pl.BlockSpec((None, 1, tn), lambda l, j: (l, 0, j))],
        out_specs=pl.BlockSpec((None, 8, tn), lambda l, j: (l, 0, j)),
        out_shape=jax.ShapeDtypeStruct((depth, 8, n), F32),
        compiler_params=_cparams(("parallel", "parallel")),
        name="adaln_mod",
    )(c8, w_ada, b_ada.reshape(depth, 1, n))


def _rms_mod(x, g, shift, scale):
    ms = jnp.mean(x * x, axis=-1, keepdims=True)
    h = x * lax.rsqrt(ms + EPS) * g
    return h * (1.0 + scale) + shift


def _head_norm(z, a, g):
    zz = (z * z).astype(BF16)
    m = jnp.concatenate([_dot(zz[:, c:c + ATT_LANES], a) for c in range(0, z.shape[1], ATT_LANES)], axis=1)
    return z * lax.rsqrt(m + EPS) * g


TOK_TB = 128


def _inproj_kernel(x_ref, mod_ref, g_ref, w_ref, a_ref, gq_ref, gk_ref,
                   q_ref, k_ref, v_ref, hz_ref, u_ref):
    B, tb, _ = x_ref.shape
    h = jnp.concatenate([_rms_mod(x_ref[b], g_ref[...], mod_ref[b, 0:1, :], mod_ref[b, 1:2, :]).astype(BF16)
                         for b in range(B)], axis=0)
    a = a_ref[...]
    q = (_head_norm(_dot(h, w_ref[:, 0:COL_K]), a, gq_ref[...]) * (ATT_HEAD_DIM ** -0.5)).astype(BF16)
    k = _head_norm(_dot(h, w_ref[:, COL_K:COL_V]), a, gk_ref[...]).astype(BF16)
    v = _dot(h, w_ref[:, COL_V:COL_HY]).astype(BF16)
    hz = _dot(h, w_ref[:, COL_HY:COL_SSM])
    u = _dot(h, w_ref[:, COL_SSM:D_IN])
    for b in range(B):
        rows = slice(b * tb, (b + 1) * tb)
        q_ref[b] = q[rows]
        k_ref[b] = k[rows]
        v_ref[b] = v[rows]
        hz_ref[b] = hz[rows]
        for s in range(D_SSM // 128):
            u_ref[s, pl.ds(b, tb, stride=B), :] = u[rows, s * 128:(s + 1) * 128]


def in_proj(x, mod, g1, w_in_bf, gq, gk):
    B, T, D = x.shape
    tb = TOK_TB
    head_avg = jnp.asarray(np.kron(np.eye(ATT_LANES // ATT_HEAD_DIM),
                                   np.full((ATT_HEAD_DIM, ATT_HEAD_DIM), 1.0 / ATT_HEAD_DIM)), BF16)
    tok = lambda n: pl.BlockSpec((B, tb, n), lambda i: (0, i, 0))
    const = lambda shape: pl.BlockSpec(shape, lambda i: (0,) * len(shape))
    return pl.pallas_call(
        _inproj_kernel,
        grid=(T // tb,),
        in_specs=[tok(D), const((B, 6, D)), const((1, D)),
                  const((D, D_IN)), const((ATT_LANES, ATT_LANES)), const((1, D_ATT)), const((1, D_ATT))],
        out_specs=[tok(D_ATT), tok(D_ATT), tok(D_ATT), tok(3 * D_HYENA),
                   pl.BlockSpec((D_SSM // 128, tb * B, 128), lambda i: (0, i, 0))],
        out_shape=[jax.ShapeDtypeStruct((B, T, D_ATT), BF16)] * 3
                  + [jax.ShapeDtypeStruct((B, T, 3 * D_HYENA), F32),
                     jax.ShapeDtypeStruct((D_SSM // 128, T * B, 128), F32)],
        compiler_params=_cparams(("parallel",), VMEM_LIMIT),
        name="in_proj",
    )(x, mod, g1.reshape(1, D), w_in_bf, head_avg,
      jnp.tile(gq, ATT_HEADS).reshape(1, D_ATT), jnp.tile(gk, ATT_HEADS).reshape(1, D_ATT))


ATT_R = 4
ATT_KROWS = ATT_R + WIN_ROWS - 1
ATT_LANES = 256


def _softmax_heads(q, parts, bias_ref, o_ref):
    nh = ATT_LANES // ATT_HEAD_DIM
    lane = lax.broadcasted_iota(jnp.int32, (1, ATT_LANES), 1)
    head = lambda h: (lane >= ATT_HEAD_DIM * h) & (lane < ATT_HEAD_DIM * (h + 1))
    acc = jnp.zeros((q.shape[0], ATT_LANES), F32)
    for h in range(nh):
        hm = head(h)
        qh = jnp.where(hm, q, jnp.zeros_like(q))
        ss = [_dot_nt(qh, kk).astype(BF16) for kk, _ in parts]
        if bias_ref is not None:
            ss[0] = ss[0] + bias_ref[h]
        m = ss[0].max(axis=-1, keepdims=True)
        for s in ss[1:]:
            m = jnp.maximum(m, s.max(axis=-1, keepdims=True))
        om = head((h + 1) % nh)
        o = None
        for s, (_, vv) in zip(ss, parts):
            part = _dot(jnp.exp(s - m), jnp.where(om, jnp.ones_like(vv), vv))
            o = part if o is None else o + part
        l = pltpu.roll(o, ATT_LANES - ATT_HEAD_DIM, axis=1)
        acc = jnp.where(hm, o / l, acc)
    o_ref[...] = acc.astype(o_ref.dtype)


def _nattn_kernel(q_ref, k_ref, v_ref, kc_ref, vc_ref, bias_ref, o_ref, *, rows):
    j = pl.program_id(2)
    ks = jnp.clip(j * ATT_R - WIN_ROWS // 2, 0, rows - ATT_KROWS)
    start = pl.multiple_of(ks * GRID_W, GRID_W)
    kl = k_ref[pl.ds(start, ATT_KROWS * GRID_W), :]
    vl = v_ref[pl.ds(start, ATT_KROWS * GRID_W), :]
    _softmax_heads(q_ref[...], [(kl, vl), (kc_ref[...], vc_ref[...])], bias_ref, o_ref)


def _attn_bias_tables(rpb, rows):
    nblk = rows // ATT_R
    qc = np.arange(GRID_W)
    c0 = np.clip(qc - WIN_COLS // 2, 0, GRID_W - WIN_COLS)
    kc = np.arange(GRID_W)
    col_ok = (kc[None, :] >= c0[:, None]) & (kc[None, :] < c0[:, None] + WIN_COLS)
    col_off = kc[None, :] - qc[:, None] + WIN_COLS - 1
    col_sel = (col_off[:, :, None] == np.arange(2 * WIN_COLS - 1)) & col_ok[:, :, None]
    band = jnp.einsum('hij,qkj->hqik', rpb, jnp.asarray(col_sel, F32), precision=HIGHEST)
    band = jnp.where(jnp.asarray(col_ok)[None, :, None, :], band, NEG)
    band = band.reshape(ATT_HEADS, GRID_W, (2 * WIN_ROWS - 1) * GRID_W).astype(BF16)
    neg = lambda nrows: jnp.full((ATT_HEADS, GRID_W, nrows * GRID_W), NEG, BF16)
    tabs = []
    for blk in (0, 1, nblk - 1):
        ks = int(np.clip(blk * ATT_R - WIN_ROWS // 2, 0, rows - ATT_KROWS))
        qrows = []
        for r in range(blk * ATT_R, (blk + 1) * ATT_R):
            r0 = int(np.clip(r - WIN_ROWS // 2, 0, rows - WIN_ROWS))
            first = r0 - r + WIN_ROWS - 1
            win = band[:, :, first * GRID_W:(first + WIN_ROWS) * GRID_W]
            qrows.append(jnp.concatenate([neg(r0 - ks), win, neg(ATT_KROWS - WIN_ROWS - (r0 - ks))], axis=-1))
        tabs.append(jnp.stack(qrows, axis=1).reshape(ATT_HEADS, ATT_R * GRID_W, ATT_KROWS * GRID_W))
    return jnp.stack(tabs)


def neighbourhood_attention(q, k, v, kc, vc, rpb):
    B, L, _ = q.shape
    Lc = kc.shape[1]
    rows = L // GRID_W
    nblk = rows // ATT_R
    tq = ATT_R * GRID_W
    nkl = ATT_KROWS * GRID_W
    bias = _attn_bias_tables(rpb, rows)
    hpb = ATT_LANES // ATT_HEAD_DIM
    variant = lambda j: jnp.where(j == 0, 0, jnp.where(j == nblk - 1, 2, 1))
    return pl.pallas_call(
        functools.partial(_nattn_kernel, rows=rows),
        grid=(B, D_ATT // ATT_LANES, nblk),
        in_specs=[pl.BlockSpec((None, tq, ATT_LANES), lambda b, g, j: (b, j, g)),
                  pl.BlockSpec((None, L, ATT_LANES), lambda b, g, j: (b, 0, g)),
                  pl.BlockSpec((None, L, ATT_LANES), lambda b, g, j: (b, 0, g)),
                  pl.BlockSpec((None, Lc, ATT_LANES), lambda b, g, j: (b, 0, g)),
                  pl.BlockSpec((None, Lc, ATT_LANES), lambda b, g, j: (b, 0, g)),
                  pl.BlockSpec((None, hpb, tq, nkl), lambda b, g, j: (variant(j), g, 0, 0))],
        out_specs=pl.BlockSpec((None, tq, ATT_LANES), lambda b, g, j: (b, j, g)),
        out_shape=jax.ShapeDtypeStruct((B, L, D_ATT), BF16),
        compiler_params=_cparams(("parallel", "parallel", "arbitrary"), VMEM_LIMIT),
        name="nattn",
    )(q, k, v, kc, vc, bias)


def _cattn_kernel(q_ref, k_ref, v_ref, o_ref):
    _softmax_heads(q_ref[...], [(k_ref[...], v_ref[...])], None, o_ref)


def context_attention(q, k, v):
    B, Lc, _ = q.shape
    spec = pl.BlockSpec((None, Lc, ATT_LANES), lambda b, g: (b, 0, g))
    return pl.pallas_call(
        _cattn_kernel,
        grid=(B, D_ATT // ATT_LANES),
        in_specs=[spec, spec, spec],
        out_specs=spec,
        out_shape=jax.ShapeDtypeStruct((B, Lc, D_ATT), BF16),
        compiler_params=_cparams(("parallel", "parallel")),
        name="cattn",
    )(q, k, v)


FFT_N = 16384
FFT_R = 128
FFT_K1 = FFT_R // 2 + 1
FFT_K1P = 66
FFT_PITCH = 2 * FFT_K1P


def _filter_kernel(w1_ref, wt_ref, b1_ref, w2_ref, b2_ref, w3_ref, b3_ref, fr_ref, wo_ref, o_ref, *, lf, tile):
    i = pl.program_id(0)
    half = FFT_N // 2
    row0 = i * tile
    first_pos = jnp.where(row0 < half, row0, FFT_N - row0 - tile)

    @pl.when(first_pos >= lf)
    def _():
        o_ref[...] = jnp.zeros_like(o_ref)

    @pl.when(first_pos < lf)
    def _():
        rows = tile // FILTER_PACK

        def position(width):
            r = lax.broadcasted_iota(jnp.int32, (rows, FILTER_PACK * width), 0)
            g = lax.broadcasted_iota(jnp.int32, (rows, FILTER_PACK * width), 1) // width
            j = row0 + r + g * rows
            return jnp.where(j < half, j, FFT_N - 1 - j)

        nfeat = 2 * FILTER_BANDS
        feat = lax.broadcasted_iota(jnp.int32, (1, FILTER_PACK * nfeat), 1) % nfeat
        f = 1e-4 + (feat % FILTER_BANDS).astype(F32) * ((FILTER_BANDS - 1 - 1e-4) / (FILTER_BANDS - 1))
        w = (2.0 * math.pi / lf) * position(nfeat).astype(F32)
        z = jnp.cos(f * w + jnp.where(feat < FILTER_BANDS, 0.0, 0.5 * math.pi))
        t_h = position(FILTER_WIDTH).astype(F32) / (lf - 1.0)
        fr = fr_ref[...]
        hdot = lambda a, b: jnp.dot(a, b, precision=HIGHEST, preferred_element_type=F32)
        h = jnp.sin(fr * (hdot(z, w1_ref[...]) + t_h * wt_ref[...] + b1_ref[...]))
        h = jnp.sin(fr * (hdot(h, w2_ref[...]) + b2_ref[...]))
        h = jnp.sin(fr * (hdot(h, w3_ref[...]) + b3_ref[...]))
        y = hdot(h, wo_ref[...])
        nout = HYENA_ORDER * D_HYENA
        pos_o = position(nout)
        ch = lax.broadcasted_iota(jnp.int32, (1, FILTER_PACK * nout), 1) % D_HYENA
        delta = jnp.abs(MIN_DECAY + ch.astype(F32) * ((MAX_DECAY - MIN_DECAY) / (D_HYENA - 1)))
        y = jnp.where(pos_o < lf, y * jnp.exp(-(pos_o.astype(F32) / (lf - 1.0)) * delta), 0.0)
        for g in range(FILTER_PACK):
            o_ref[g * rows:(g + 1) * rows, :] = y[:, g * nout:(g + 1) * nout]


FILTER_PACK = 4


def hyena_filter_taps(lf, w1, b1, w2, b2, w3, b3, freq, w_out, tile=1024):
    P, W = FILTER_PACK, FILTER_WIDTH
    nout = HYENA_ORDER * D_HYENA
    bd = lambda m: jnp.kron(jnp.eye(P, dtype=F32), m.astype(F32))
    wide = lambda a: jnp.tile(a.astype(F32), P).reshape(1, P * W)
    wo = w_out.reshape(W, HYENA_ORDER, 2, D_HYENA).transpose(2, 0, 1, 3).reshape(2, W, nout)
    wo = jnp.stack([bd(wo[0]), bd(wo[1])])
    nt = FFT_N // tile
    const = lambda shape: pl.BlockSpec(shape, lambda i: (0,) * len(shape))
    return pl.pallas_call(
        functools.partial(_filter_kernel, lf=lf, tile=tile),
        grid=(nt,),
        in_specs=[const((P * (FILTER_EMB - 1), P * W)), const((1, P * W)), const((1, P * W)),
                  const((P * W, P * W)), const((1, P * W)), const((P * W, P * W)), const((1, P * W)),
                  const((1, P * W)),
                  pl.BlockSpec((None, P * W, P * nout), lambda i: (i // (nt // 2), 0, 0))],
        out_specs=pl.BlockSpec((tile, nout), lambda i: (i, 0)),
        out_shape=jax.ShapeDtypeStruct((FFT_N, nout), F32),
        compiler_params=_cparams(("parallel",)),
        name="hyena_filter",
    )(bd(w1[1:]), wide(w1[0]), wide(b1), bd(w2), wide(b2), bd(w3), wide(b3), wide(freq), wo)


def _shortconv_kernel(z_ref, w_ref, b_ref, o_ref, zp_ref, *, L):
    C = z_ref.shape[-1]
    zp_ref[0:8, :] = jnp.zeros((8, C), F32)
    zp_ref[L + 8:L + 16, :] = jnp.zeros((8, C), F32)
    zp_ref[8:L + 8, :] = z_ref[...]
    ch = min(L, 512)
    for c in range(L // ch):
        s = c * ch
        o_ref[s:s + ch, :] = (b_ref[...] + zp_ref[s + 7:s + 7 + ch, :] * w_ref[0:1, :]
                              + zp_ref[s + 8:s + 8 + ch, :] * w_ref[1:2, :]
                              + zp_ref[s + 9:s + 9 + ch, :] * w_ref[2:3, :])


def short_conv(z, w, b, cb=128):
    B, L, C = z.shape
    return pl.pallas_call(
        functools.partial(_shortconv_kernel, L=L),
        grid=(B, C // cb),
        in_specs=[pl.BlockSpec((None, L, cb), lambda b_, c: (b_, 0, c)),
                  pl.BlockSpec((3, cb), lambda b_, c: (0, c)),
                  pl.BlockSpec((1, cb), lambda b_, c: (0, c))],
        out_specs=pl.BlockSpec((None, L, cb), lambda b_, c: (b_, 0, c)),
        out_shape=jax.ShapeDtypeStruct((B, L, C), F32),
        scratch_shapes=[pltpu.VMEM((L + 16, cb), F32)],
        compiler_params=_cparams(("parallel", "parallel")),
        name="short_conv",
    )(z, w, b.reshape(1, C))


@functools.lru_cache(maxsize=None)
def _fft_tables(n1_in):
    n2 = np.arange(FFT_R)[:, None, None]
    k1 = np.arange(FFT_K1)[None, :, None]
    n1 = np.arange(n1_in)[None, None, :]
    th = 2.0 * np.pi * ((k1 * (FFT_R * n1 + n2)) % FFT_N) / FFT_N
    ga = np.zeros((FFT_R, 2 * FFT_K1P, n1_in))
    ga[:, :FFT_K1] = np.cos(th)
    ga[:, FFT_K1P:FFT_K1P + FFT_K1] = -np.sin(th)
    wk = np.where((np.arange(FFT_K1) == 0) | (np.arange(FFT_K1) == FFT_R // 2), 1.0, 2.0)[None, None, :]
    thd = th.transpose(0, 2, 1)
    gd = np.zeros((FFT_R, n1_in, 2 * FFT_K1P))
    gd[:, :, :FFT_K1] = wk * np.cos(thd)
    gd[:, :, FFT_K1P:FFT_K1P + FFT_K1] = -wk * np.sin(thd)
    a = np.arange(FFT_R)
    ph = 2.0 * np.pi * ((a[:, None] * a[None, :]) % FFT_R) / FFT_R
    c, s = np.cos(ph), np.sin(ph)
    fb = np.block([[c, s], [-s, c]])
    fc = np.block([[c, -s], [s, c]])
    return ga, gd, fb, fc


FFT_KG = 5
FFT_UNROLL = 8


def _spectrum_slabs(ya_ref, fb, k0):
    cols = []
    for g in range(FFT_KG):
        re = ya_ref[pl.ds(k0 + g, FFT_R, stride=FFT_PITCH), :]
        im = ya_ref[pl.ds(FFT_K1P + k0 + g, FFT_R, stride=FFT_PITCH), :]
        cols.append(jnp.concatenate([re, im], axis=0).astype(BF16))
    return _dot(fb, jnp.concatenate(cols, axis=1))


def _stage_a(x_ref, x_pitch, ga_ref, ya_ref, n1_in):
    def body(n2, carry):
        xs = x_ref[pl.ds(n2, n1_in, stride=x_pitch), :].astype(BF16)
        ya_ref[pl.ds(n2 * FFT_PITCH, FFT_PITCH), :] = _dot(ga_ref[n2], xs)
        return carry
    lax.fori_loop(0, FFT_R, body, 0, unroll=FFT_UNROLL)


def _spectrum_kernel(x_ref, ga_ref, fb_ref, o_ref, ya_ref):
    _stage_a(x_ref, FFT_R, ga_ref, ya_ref, FFT_R)
    fb = fb_ref[...]
    cb = ya_ref.shape[1]

    def body(kg, carry):
        x = _spectrum_slabs(ya_ref, fb, kg * FFT_KG)
        for g in range(FFT_KG):
            o_ref[kg * FFT_KG + g] = x[:, g * cb:(g + 1) * cb]
        return carry
    lax.fori_loop(0, FFT_K1 // FFT_KG, body, 0)


def filter_spectrum(taps, cb=128):
    C = taps.shape[1]
    ga, _, fb, _ = _fft_tables(FFT_R)
    return pl.pallas_call(
        _spectrum_kernel,
        grid=(C // cb,),
        in_specs=[pl.BlockSpec((FFT_N, cb), lambda c: (0, c)),
                  pl.BlockSpec((FFT_R, FFT_PITCH, FFT_R), lambda c: (0, 0, 0)),
                  pl.BlockSpec((2 * FFT_R, 2 * FFT_R), lambda c: (0, 0))],
        out_specs=pl.BlockSpec((FFT_K1, 2 * FFT_R, cb), lambda c: (0, 0, c)),
        out_shape=jax.ShapeDtypeStruct((FFT_K1, 2 * FFT_R, C), F32),
        scratch_shapes=[pltpu.VMEM((FFT_R * FFT_PITCH, cb), F32)],
        compiler_params=_cparams(("parallel",), VMEM_LIMIT),
        name="filter_spectrum",
    )(taps, _bf16_table(ga), _bf16_table(fb))


def _bf16_table(t):
    return jnp.asarray(t, F32).astype(BF16)


def _longconv_kernel(u_ref, gate_ref, bias_ref, spec_ref, ga_ref, gd_ref, fb_ref, fc_ref, o_ref, ya_ref, xp_ref):
    n1_in = FFT_R // 2

    def repitch(n1, carry):
        xp_ref[pl.ds(n1 * FFT_PITCH, FFT_R), :] = u_ref[pl.ds(pl.multiple_of(n1 * FFT_R, FFT_R), FFT_R), :]
        return carry
    lax.fori_loop(0, n1_in, repitch, 0, unroll=FFT_UNROLL)

    _stage_a(xp_ref, FFT_PITCH, ga_ref, ya_ref, n1_in)
    fb = fb_ref[...]
    fc = fc_ref[...]

    cb = ya_ref.shape[1]

    def mid(kg, carry):
        k0 = kg * FFT_KG
        x = _spectrum_slabs(ya_ref, fb, k0)
        s = jnp.concatenate([spec_ref[k0 + g] for g in range(FFT_KG)], axis=1)
        xr, xi = x[:FFT_R], x[FFT_R:]
        sr, si = s[:FFT_R], s[FFT_R:]
        z = jnp.concatenate([xr * sr - xi * si, xr * si + xi * sr], axis=0).astype(BF16)
        v = _dot(fc, z)
        for g in range(FFT_KG):
            ya_ref[pl.ds(k0 + g, FFT_R, stride=FFT_PITCH), :] = v[:FFT_R, g * cb:(g + 1) * cb]
            ya_ref[pl.ds(FFT_K1P + k0 + g, FFT_R, stride=FFT_PITCH), :] = v[FFT_R:, g * cb:(g + 1) * cb]
        return carry
    lax.fori_loop(0, FFT_K1 // FFT_KG, mid, 0)

    def last(n2, carry):
        slab = ya_ref[pl.ds(n2 * FFT_PITCH, FFT_PITCH), :].astype(BF16)
        xp_ref[pl.ds(n2, n1_in, stride=FFT_PITCH), :] = _dot(gd_ref[n2], slab)
        return carry
    lax.fori_loop(0, FFT_R, last, 0, unroll=FFT_UNROLL)

    bias = bias_ref[...]

    def gate(n1, carry):
        rows = pl.ds(pl.multiple_of(n1 * FFT_R, FFT_R), FFT_R)
        y = xp_ref[pl.ds(n1 * FFT_PITCH, FFT_R), :] * (1.0 / FFT_N)
        o_ref[rows, :] = gate_ref[rows, :] * (y + u_ref[rows, :] * bias)
        return carry
    lax.fori_loop(0, n1_in, gate, 0, unroll=FFT_UNROLL)


def gated_long_conv(u, u_col, gate, gate_col, bias, spec, spec_col, C=D_HYENA, cb=128):
    B, L, _ = u.shape
    n1_in = FFT_R // 2
    ga, gd, fb, fc = _fft_tables(n1_in)
    once = pl.Buffered(1)
    seq = lambda col: pl.BlockSpec((None, L, cb), lambda c, b: (b, 0, c + col // cb))
    return pl.pallas_call(
        _longconv_kernel,
        grid=(C // cb, B),
        in_specs=[seq(u_col), seq(gate_col),
                  pl.BlockSpec((1, cb), lambda c, b: (0, c)),
                  pl.BlockSpec((FFT_K1, 2 * FFT_R, cb), lambda c, b: (0, 0, c + spec_col // cb), pipeline_mode=once),
                  pl.BlockSpec((FFT_R, FFT_PITCH, n1_in), lambda c, b: (0, 0, 0), pipeline_mode=once),
                  pl.BlockSpec((FFT_R, n1_in, FFT_PITCH), lambda c, b: (0, 0, 0), pipeline_mode=once),
                  pl.BlockSpec((2 * FFT_R, 2 * FFT_R), lambda c, b: (0, 0)),
                  pl.BlockSpec((2 * FFT_R, 2 * FFT_R), lambda c, b: (0, 0))],
        out_specs=seq(0),
        out_shape=jax.ShapeDtypeStruct((B, L, C), F32),
        scratch_shapes=[pltpu.VMEM((FFT_R * FFT_PITCH, cb), F32), pltpu.VMEM((n1_in * FFT_PITCH, cb), F32)],
        compiler_params=_cparams(("parallel", "arbitrary"), VMEM_LIMIT),
        name="long_conv",
    )(u, gate, bias.reshape(1, C), spec, _bf16_table(ga), _bf16_table(gd), _bf16_table(fb), _bf16_table(fc))


SSM_N = SSM_GROUPS * SSM_STATE
S5_SEQS = 8
S5_CHUNK = 128


def _s5_kernel(uf_ref, ub_ref, bf_ref, bb_ref, lam_ref, cf_ref, cb_ref, yf_ref, yb_ref,
               buf_ref, bub_ref, xsf_ref, xsb_ref, st_ref):
    half = S5_SEQS // 2
    tiles = S5_CHUNK // 2

    @pl.when(pl.program_id(0) == 0)
    def _():
        st_ref[...] = jnp.zeros_like(st_ref)

    slabs = lambda ref: jnp.concatenate([ref[s] for s in range(ref.shape[0])], axis=1).astype(BF16)
    buf_ref[...] = _dot(slabs(uf_ref), bf_ref[...])
    bub_ref[...] = _dot(slabs(ub_ref), bb_ref[...])
    lo = lax.broadcasted_iota(jnp.int32, (S5_SEQS, 1), 0) < half

    def advance(x, lam, b):
        xr, xi = x
        lr, li = lam[:, 0:SSM_N], lam[:, SSM_N:2 * SSM_N]
        return lr * xr - li * xi + b[:, 0:SSM_N], lr * xi + li * xr + b[:, SSM_N:2 * SSM_N]

    def swap(x):
        return pltpu.roll(x[0], half, axis=0), pltpu.roll(x[1], half, axis=0)

    def step(t4, x):
        f_tiles, b_tiles = [], []
        for p in range(2):
            tf = buf_ref[pl.ds(pl.multiple_of((2 * t4 + p) * S5_SEQS, S5_SEQS), S5_SEQS), :]
            tb = bub_ref[pl.ds(pl.multiple_of((tiles - 1 - 2 * t4 - p) * S5_SEQS, S5_SEQS), S5_SEQS), :]
            xe = advance(x, lam_ref[0], jnp.where(lo, tf, tb))
            xo = advance(swap(xe), lam_ref[1], jnp.where(lo, tb, tf))
            x = swap(xo)
            xe = jnp.concatenate(xe, axis=1)
            xo = jnp.concatenate(xo, axis=1)
            f_tiles.append(jnp.where(lo, xe, xo))
            b_tiles.append(jnp.where(lo, xo, xe))
        xsf_ref[pl.ds(pl.multiple_of(t4 * 2 * S5_SEQS, 2 * S5_SEQS), 2 * S5_SEQS), :] = (
            jnp.concatenate(f_tiles, axis=0).astype(BF16))
        xsb_ref[pl.ds(pl.multiple_of((tiles - 2 - 2 * t4) * S5_SEQS, 2 * S5_SEQS), 2 * S5_SEQS), :] = (
            jnp.concatenate(b_tiles[::-1], axis=0).astype(BF16))
        return x

    xr, xi = lax.fori_loop(0, S5_CHUNK // 4, step, (st_ref[:, 0:SSM_N], st_ref[:, SSM_N:2 * SSM_N]))
    st_ref[:, 0:SSM_N] = xr
    st_ref[:, SSM_N:2 * SSM_N] = xi
    yf = _dot(xsf_ref[...], cf_ref[...])
    yb = _dot(xsb_ref[...], cb_ref[...])
    for s in range(yf_ref.shape[0]):
        yf_ref[s] = yf[:, s * 128:(s + 1) * 128]
        yb_ref[s] = yb[:, s * 128:(s + 1) * 128]


def _s5_matrices(lam_re, lam_im, log_step, b_re, b_im, c_re, c_im):
    G, P, Hs = SSM_GROUPS, SSM_STATE, SSM_GROUP_WIDTH
    dt = jnp.exp(log_step)[:, :, None]
    mag = jnp.exp(lam_re * dt)
    ar, ai = mag * jnp.cos(lam_im * dt), mag * jnp.sin(lam_im * dt)
    er, ei = ar - 1.0, ai
    den = lam_re * lam_re + lam_im * lam_im
    cr, ci = (er * lam_re + ei * lam_im) / den, (ei * lam_re - er * lam_im) / den
    bbr = cr[..., None] * b_re - ci[..., None] * b_im
    bbi = cr[..., None] * b_im + ci[..., None] * b_re
    eye = jnp.eye(G, dtype=F32)
    blk_in = lambda m: jnp.einsum('gq,qph->ghqp', eye, m).reshape(G * Hs, G * P)
    bmat = [jnp.concatenate([blk_in(bbr[d]), blk_in(bbi[d])], axis=1).astype(BF16) for d in range(2)]
    blk_out = lambda m: jnp.einsum('gq,qhp->gpqh', eye, m).reshape(G * P, G * Hs)
    cmat = [jnp.concatenate([blk_out(c_re[d]), -blk_out(c_im[d])], axis=0).astype(BF16) for d in range(2)]
    lam = jnp.concatenate([ar.reshape(2, 1, G * P), ai.reshape(2, 1, G * P)], axis=2)
    lam = jnp.broadcast_to(lam, (2, S5_SEQS // 2, 2 * G * P))
    lam8 = jnp.stack([lam.reshape(S5_SEQS, 2 * G * P), lam[::-1].reshape(S5_SEQS, 2 * G * P)])
    return bmat[0], bmat[1], lam8, cmat[0], cmat[1]


def s5_scan(u_x, u_c, lam_re, lam_im, log_step, b_re, b_im, c_re, c_im):
    S = u_x.shape[0]
    half = S5_SEQS // 2
    L, Lc = u_x.shape[1] // half, u_c.shape[1] // half
    assert L % S5_CHUNK == 0 and Lc % S5_CHUNK == 0
    bf, bb, lam8, cf, cb = _s5_matrices(lam_re, lam_im, log_step, b_re, b_im, c_re, c_im)
    un = jnp.concatenate([u_x, u_c], axis=1)
    rows = S5_CHUNK * half
    nx, nc = L // S5_CHUNK, Lc // S5_CHUNK
    n = nx + nc
    fwd = pl.BlockSpec((S, rows, 128), lambda i: (0, jnp.where(i < nc, nx + i, i - nc), 0))
    bwd = pl.BlockSpec((S, rows, 128), lambda i: (0, n - 1 - i, 0))
    const = lambda shape: pl.BlockSpec(shape, lambda i: (0,) * len(shape))
    yf, yb = pl.pallas_call(
        _s5_kernel,
        grid=(n,),
        in_specs=[fwd, bwd, const((D_SSM, 2 * SSM_N)), const((D_SSM, 2 * SSM_N)),
                  const((2, S5_SEQS, 2 * SSM_N)), const((2 * SSM_N, D_SSM)), const((2 * SSM_N, D_SSM))],
        out_specs=[fwd, bwd],
        out_shape=[jax.ShapeDtypeStruct(un.shape, F32)] * 2,
        scratch_shapes=[pltpu.VMEM((rows, 2 * SSM_N), F32), pltpu.VMEM((rows, 2 * SSM_N), F32),
                        pltpu.VMEM((rows, 2 * SSM_N), BF16), pltpu.VMEM((rows, 2 * SSM_N), BF16),
                        pltpu.VMEM((S5_SEQS, 2 * SSM_N), F32)],
        compiler_params=_cparams(("arbitrary",), VMEM_LIMIT),
        name="s5_scan",
    )(un, un, bf, bb, lam8, cf, cb)
    return un, yf, yb


def _gelu_tanh(y):
    return 0.5 * y * (1.0 + jnp.tanh(math.sqrt(2.0 / math.pi) * (y + 0.044715 * (y * y * y))))


def _outproj_kernel(*refs, route):
    (x_ref, mod_ref, att_ref, hy_ref, yf_ref, yb_ref, u_ref, dsk_ref, wg_ref, bg_ref, wo_ref, g2_ref) = refs[:12]
    if route:
        wr_ref, x1_ref, h2_ref, rt_ref = refs[12:]
    else:
        x1_ref, h2_ref = refs[12:]
    B, tb, D = x_ref.shape
    rows = lambda a: a.reshape(B * tb, a.shape[-1])

    def scan_rows(ref):
        return jnp.concatenate(
            [jnp.concatenate([ref[s, pl.ds(b, tb, stride=B), :] for s in range(ref.shape[0])], axis=1)
             for b in range(B)], axis=0)

    y = scan_rows(yf_ref) + scan_rows(yb_ref) + scan_rows(u_ref) * dsk_ref[...]
    y = _gelu_tanh(y)
    ss = y * jax.nn.sigmoid(_dot(y.astype(BF16), wg_ref[...]) + bg_ref[...])
    mix = (_dot(rows(att_ref[...]), wo_ref[0:D_ATT, :])
           + _dot(rows(hy_ref[...]).astype(BF16), wo_ref[D_ATT:D_ATT + D_HYENA, :])
           + _dot(ss.astype(BF16), wo_ref[D_ATT + D_HYENA:, :]))
    h2s = []
    for b in range(B):
        x1 = x_ref[b] + mod_ref[b, 2:3, :] * mix[b * tb:(b + 1) * tb]
        x1_ref[b] = x1
        h2s.append(_rms_mod(x1, g2_ref[...], mod_ref[b, 3:4, :], mod_ref[b, 4:5, :]))
        h2_ref[b] = h2s[-1].astype(h2_ref.dtype)
    if route:
        h2 = jnp.concatenate(h2s, axis=0)
        lane = lax.broadcasted_iota(jnp.int32, (1, 128), 1)
        h_hi = h2.astype(BF16)
        h_lo = (h2 - h_hi.astype(F32)).astype(BF16)
        logits = _dot(h_hi, wr_ref[0]) + _dot(h_lo, wr_ref[0]) + _dot(h_hi, wr_ref[1])
        lg = jnp.where(lane < N_EXPERTS, logits, -jnp.inf)
        v1 = lg.max(axis=-1, keepdims=True)
        i1 = jnp.where(lg == v1, lane, 128).min(axis=-1, keepdims=True)
        lg2 = jnp.where(lane == i1, -jnp.inf, lg)
        v2 = lg2.max(axis=-1, keepdims=True)
        i2 = jnp.where(lg2 == v2, lane, 128).min(axis=-1, keepdims=True)
        e = jnp.exp(v2 - v1)
        w1 = 1.0 / (1.0 + e)
        rt = jnp.where(lane == 0, i1.astype(F32),
                       jnp.where(lane == 1, i2.astype(F32),
                                 jnp.where(lane == 2, w1, jnp.where(lane == 3, e * w1, 0.0))))
        for b in range(B):
            rt_ref[b] = rt[b * tb:(b + 1) * tb]


def out_proj(x, mod, att, hy, yf, yb, u, scan_row0, d_skip, w_glu, b_glu, w_out_bf, g2, router_w=None):
    B, T, D = x.shape
    tb = TOK_TB
    route = router_w is not None
    S = u.shape[0]
    tok = lambda n: pl.BlockSpec((B, tb, n), lambda i: (0, i, 0))
    scan = pl.BlockSpec((S, tb * B, 128), lambda i: (0, i + scan_row0 // tb, 0))
    const = lambda shape: pl.BlockSpec(shape, lambda i: (0,) * len(shape))
    in_specs = [tok(D), const((B, 6, D)), tok(D_ATT), tok(D_HYENA),
                scan, scan, scan, const((1, D_SSM)), const((D_SSM, D_SSM)), const((1, D_SSM)),
                const((D, D)), const((1, D))]
    args = [x, mod, att, hy, yf, yb, u, d_skip.reshape(1, D_SSM), w_glu.astype(BF16), b_glu.reshape(1, D_SSM),
            w_out_bf, g2.reshape(1, D)]
    out_specs = [tok(D), tok(D)]
    out_shape = [jax.ShapeDtypeStruct((B, T, D), F32), jax.ShapeDtypeStruct((B, T, D), BF16)]
    if route:
        in_specs.append(const((2, D, 128)))
        wr = jnp.pad(router_w.astype(F32), ((0, 0), (0, 128 - N_EXPERTS)))
        wr_hi = wr.astype(BF16)
        args.append(jnp.stack([wr_hi, (wr - wr_hi.astype(F32)).astype(BF16)]))
        out_specs.append(tok(128))
        out_shape.append(jax.ShapeDtypeStruct((B, T, 128), F32))
    return pl.pallas_call(
        functools.partial(_outproj_kernel, route=route),
        grid=(T // tb,),
        in_specs=in_specs, out_specs=out_specs, out_shape=out_shape,
        compiler_params=_cparams(("parallel",), VMEM_LIMIT),
        name="out_proj",
    )(*args)


FF_CHUNK = 768
FF_VMEM_LIMIT = 60 * 1024 * 1024


def _swiglu_tile(h, w1_ref, w3_ref, w2_ref):
    acc = None
    for lo in range(0, D_FF, FF_CHUNK):
        hi = min(lo + FF_CHUNK, D_FF)
        a = _dot(h, w1_ref[:, lo:hi].astype(BF16))
        g = (a * jax.nn.sigmoid(a) * _dot(h, w3_ref[:, lo:hi].astype(BF16))).astype(BF16)
        part = _dot(g, w2_ref[lo:hi, :].astype(BF16))
        acc = part if acc is None else acc + part
    return acc


def _ffn_kernel(h_ref, x_ref, mod_ref, w1_ref, w3_ref, w2_ref, o_ref):
    o_ref[...] = x_ref[...] + mod_ref[5:6, :] * _swiglu_tile(h_ref[...], w1_ref, w3_ref, w2_ref)


def ffn_dense(h2, x1, mod, w1, w3, w2, tm=512):
    B, T, D = x1.shape
    tm = min(tm, T)
    tok = pl.BlockSpec((None, tm, D), lambda b, i: (b, i, 0))
    once = pl.Buffered(1)
    return pl.pallas_call(
        _ffn_kernel,
        grid=(B, T // tm),
        in_specs=[tok, tok, pl.BlockSpec((None, 6, D), lambda b, i: (b, 0, 0)),
                  pl.BlockSpec((D, D_FF), lambda b, i: (0, 0), pipeline_mode=once),
                  pl.BlockSpec((D, D_FF), lambda b, i: (0, 0), pipeline_mode=once),
                  pl.BlockSpec((D_FF, D), lambda b, i: (0, 0), pipeline_mode=once)],
        out_specs=tok,
        out_shape=jax.ShapeDtypeStruct((B, T, D), F32),
        compiler_params=_cparams(("parallel", "parallel"), FF_VMEM_LIMIT),
        name="ffn_dense",
    )(h2, x1, mod, w1, w3, w2)


MOE_TM = 512


def _moe_kernel(te_ref, tv_ref, h_ref, w1_ref, w3_ref, w2_ref, o_ref):
    valid = tv_ref[pl.program_id(0)] > 0

    @pl.when(valid)
    def _():
        o_ref[...] = _swiglu_tile(h_ref[...], w1_ref, w3_ref, w2_ref).astype(o_ref.dtype)

    @pl.when(jnp.logical_not(valid))
    def _():
        o_ref[...] = jnp.zeros_like(o_ref)


def moe_experts(hs, tile_expert, tile_valid, w1, w3, w2):
    P, D = hs.shape
    nt = P // MOE_TM
    tok = pl.BlockSpec((MOE_TM, D), lambda i, te, tv: (i, 0))
    return pl.pallas_call(
        _moe_kernel,
        grid_spec=pltpu.PrefetchScalarGridSpec(
            num_scalar_prefetch=2,
            grid=(nt,),
            in_specs=[tok,
                      pl.BlockSpec((None, D, D_FF), lambda i, te, tv: (te[i], 0, 0)),
                      pl.BlockSpec((None, D, D_FF), lambda i, te, tv: (te[i], 0, 0)),
                      pl.BlockSpec((None, D_FF, D), lambda i, te, tv: (te[i], 0, 0))],
            out_specs=tok),
        out_shape=jax.ShapeDtypeStruct((P, D), BF16),
        compiler_params=_cparams(("arbitrary",), FF_VMEM_LIMIT),
        name="moe_experts",
    )(tile_expert, tile_valid, hs, w1, w3, w2)


def _combine_kernel(x_ref, mod_ref, ya_ref, yb_ref, rt_ref, *rest):
    o_ref = rest[-1]
    wa = rt_ref[:, 2:3]
    wb = rt_ref[:, 3:4]
    o_ref[...] = x_ref[...] + mod_ref[5:6, :] * (wa * ya_ref[...] + wb * yb_ref[...])


def moe_combine(x1, mod, yab, rt, b0, out_prev, tm=512):
    B, T, D = x1.shape
    gb = yab.shape[1]
    tok = lambda n: pl.BlockSpec((None, tm, n), lambda b, i: (b + b0, i, 0))
    choice = lambda c: pl.BlockSpec((None, None, tm, D), lambda b, i: (c, b, i, 0))
    in_specs = [tok(D), pl.BlockSpec((None, 6, D), lambda b, i: (b + b0, 0, 0)), choice(0), choice(1), tok(128)]
    args = [x1, mod, yab, yab, rt]
    aliases = {}
    if out_prev is not None:
        in_specs.append(pl.BlockSpec(memory_space=pl.ANY))
        args.append(out_prev)
        aliases = {len(args) - 1: 0}
    return pl.pallas_call(
        _combine_kernel,
        grid=(gb, T // tm),
        in_specs=in_specs,
        out_specs=tok(D),
        out_shape=jax.ShapeDtypeStruct((B, T, D), F32),
        input_output_aliases=aliases,
        compiler_params=_cparams(("parallel", "parallel")),
        name="moe_combine",
    )(*args)


MOE_GROUPS = 2


def moe_layer(h2, x1, mod, rt, w1, w3, w2):
    B = x1.shape[0]
    gb = B // MOE_GROUPS
    weights = (w1.astype(BF16), w3.astype(BF16), w2.astype(BF16))
    out = None
    for b0 in range(0, B, gb):
        yab = _moe_group(h2, rt, b0, gb, *weights)
        out = moe_combine(x1, mod, yab, rt, b0, out)
    return out


def _moe_group(h2, rt, b0, gb, w1, w3, w2):
    _, T, D = h2.shape
    n = gb * T
    rt = rt[b0:b0 + gb]
    e_flat = jnp.concatenate([rt[..., 0].reshape(n), rt[..., 1].reshape(n)]).astype(jnp.int32)
    order = jnp.argsort(e_flat, stable=True).astype(jnp.int32)
    rank = jnp.argsort(order).astype(jnp.int32)
    onehot = jax.nn.one_hot(e_flat, N_EXPERTS, dtype=jnp.int32)
    counts = jnp.sum(onehot, axis=0)
    padded = ((counts + MOE_TM - 1) // MOE_TM) * MOE_TM
    pend = jnp.cumsum(padded)
    shift = (pend - padded) - (jnp.cumsum(counts) - counts)
    P = 2 * n + N_EXPERTS * MOE_TM
    nt = P // MOE_TM
    tile_row = jnp.arange(nt, dtype=jnp.int32) * MOE_TM
    tile_expert = jnp.sum((tile_row[:, None] >= pend[None, :]).astype(jnp.int32), axis=1)
    tile_valid = (tile_expert < N_EXPERTS).astype(jnp.int32)
    tile_expert = jnp.minimum(tile_expert, N_EXPERTS - 1)
    slot = rank + jnp.sum(onehot * shift[None, :], axis=1)
    tile_shift = jnp.sum(jax.nn.one_hot(tile_expert, N_EXPERTS, dtype=jnp.int32) * shift[None, :], axis=1)
    row_rank = jnp.arange(P, dtype=jnp.int32) - jnp.repeat(tile_shift, MOE_TM)
    src = order[jnp.clip(row_rank, 0, 2 * n - 1)] % n
    take = lambda a, idx: a.at[idx].get(mode='promise_in_bounds')
    hs = take(h2.reshape(-1, D), src + b0 * T)
    ys = moe_experts(hs, tile_expert, tile_valid, w1, w3, w2)
    return take(ys, slot).reshape(2, gb, T, D)


def _hyena(hz, conv_w, conv_b, spec, bias):
    zc = short_conv(hz, conv_w, conv_b)
    B, L, C = zc.shape
    lmax = FFT_N // 2
    if L < lmax:
        slot = lmax // B
        assert slot >= 2 * L
        zc = jnp.pad(zc, ((0, 0), (0, slot - L), (0, 0))).reshape(1, lmax, C)
    y = gated_long_conv(zc, 0, zc, D_HYENA, bias[0], spec, 0)
    y = gated_long_conv(y, 0, zc, 2 * D_HYENA, bias[1], spec, D_HYENA)
    if L < lmax:
        y = y.reshape(B, lmax // B, D_HYENA)[:, :L]
    return y


def _split_route(outs, mod):
    x1, h2 = outs[0], outs[1]
    return h2, x1, mod, (outs[2] if len(outs) > 2 else None)


def kernel(x, c, ctx, c_ctx, w_ada, b_ada, g_norm1, g_norm2, w_in, w_out, g_q, g_k, rpb, hy_conv_w, hy_conv_b, filt_w1, filt_b1, filt_w2, filt_b2, filt_w3, filt_b3, filt_freq, filt_w_out, hy_bias, lam_re, lam_im, log_step, b_re, b_im, c_re, c_im, d_skip, w_glu, b_glu, ffn_w1, ffn_w3, ffn_w2, router_w, moe_w1, moe_w3, moe_w2):
    B, L, D = x.shape
    Lc = ctx.shape[1]
    c8 = jnp.zeros((8, D), F32).at[:B].set(c).at[B].set(c_ctx)
    mod_all = adaln_mod(c8, w_ada, b_ada)
    for layer in range(DEPTH):
        last = layer == DEPTH - 1
        m = mod_all[layer].reshape(8, 6, D)
        mod_x = m[:B]
        mod_c = jnp.broadcast_to(m[B], (B, 6, D))
        w_l = w_in[layer].astype(BF16)
        wo_l = w_out[layer].astype(BF16)
        filt = (filt_w1[layer], filt_b1[layer], filt_w2[layer], filt_b2[layer],
                filt_w3[layer], filt_b3[layer], filt_freq[layer], filt_w_out[layer])
        ssm = (lam_re[layer], lam_im[layer], log_step[layer], b_re[layer], b_im[layer], c_re[layer], c_im[layer])

        q_c, k_c, v_c, hz_c, u_c = in_proj(ctx, mod_c, g_norm1[layer], w_l, g_q[layer], g_k[layer])
        q, k, v, hz, u = in_proj(x, mod_x, g_norm1[layer], w_l, g_q[layer], g_k[layer])
        att = neighbourhood_attention(q, k, v, k_c, v_c, rpb[layer])
        spec = filter_spectrum(hyena_filter_taps(L, *filt))
        hy = _hyena(hz, hy_conv_w[layer], hy_conv_b[layer], spec, hy_bias[layer])
        un, yf, yb = s5_scan(u, u_c, *ssm)

        tail = (d_skip[layer], w_glu[layer], b_glu[layer], wo_l, g_norm2[layer])
        i = layer // 2
        if layer % 2 == 0:
            ffw = (ffn_w1[i], ffn_w3[i], ffn_w2[i])
            mixer = lambda h2, x1, mod, rt: ffn_dense(h2, x1, mod, *ffw)
            rw = None
        else:
            mow = (moe_w1[i], moe_w3[i], moe_w2[i])
            mixer = lambda h2, x1, mod, rt: moe_layer(h2, x1, mod, rt, *mow)
            rw = router_w[i]
        x = mixer(*_split_route(out_proj(x, mod_x, att, hy, yf, yb, un, 0, *tail, router_w=rw), mod_x))

        if not last:
            att_c = context_attention(q_c, k_c, v_c)
            spec_c = filter_spectrum(hyena_filter_taps(Lc, *filt))
            hy_c = _hyena(hz_c, hy_conv_w[layer], hy_conv_b[layer], spec_c, hy_bias[layer])
            ctx = mixer(*_split_route(out_proj(ctx, mod_c, att_c, hy_c, yf, yb, un, L, *tail, router_w=rw), mod_c))
    return x
```

```python
import functools
import math

import numpy as np
import jax
import jax.numpy as jnp
from jax import lax
from jax.experimental import pallas as pl
from jax.experimental.pallas import tpu as pltpu

F32 = jnp.float32
BF16 = jnp.bfloat16
HIGHEST = lax.Precision.HIGHEST

D_MODEL = 1024
DEPTH = 2
GRID_W = 64
ATT_HEAD_DIM = 64
ATT_HEADS = 8
D_ATT = 512
WIN_ROWS = 8
WIN_COLS = 16
D_HYENA = 256
HYENA_ORDER = 2
FILTER_EMB = 33
FILTER_BANDS = 16
FILTER_WIDTH = 64
MIN_DECAY = math.log(1e-2) / 1.5
MAX_DECAY = math.log(1e-2) / 0.3
D_SSM = 256
SSM_GROUP_WIDTH = 16
SSM_GROUPS = 16
SSM_STATE = 64
COL_K = D_ATT
COL_V = 2 * D_ATT
COL_HY = 3 * D_ATT
COL_SSM = COL_HY + 3 * D_HYENA
D_IN = COL_SSM + D_SSM
D_FF = 2816
N_EXPERTS = 8
EPS = 1e-6
NEG = -1e30

VMEM_LIMIT = 56 * 1024 * 1024


def _cparams(sem, vmem=None):
    return pltpu.CompilerParams(dimension_semantics=sem, vmem_limit_bytes=vmem)


def _dot(a, b):
    return jnp.dot(a, b, preferred_element_type=F32)


def _dot_nt(a, b):
    return lax.dot_general(a, b, (((1,), (1,)), ((), ())), preferred_element_type=F32)


def _mod_kernel(c_ref, w_ref, b_ref, o_ref):
    c = c_ref[...]
    s = c * jax.nn.sigmoid(c)
    o_ref[...] = jnp.dot(s, w_ref[...], precision=HIGHEST, preferred_element_type=F32) + b_ref[...]


def adaln_mod(c8, w_ada, b_ada, tn=512):
    depth, d, n = w_ada.shape
    return pl.pallas_call(
        _mod_kernel,
        grid=(depth, n // tn),
        in_specs=[pl.BlockSpec((8, d), lambda l, j: (0, 0)),
                  pl.BlockSpec((None, d, tn), lambda l, j: (l, 0, j)),
                  pl.BlockSpec((None, 1, tn), lambda l, j: (l, 0, j))],
        out_specs=pl.BlockSpec((None, 8, tn), lambda l, j: (l, 0, j)),
        out_shape=jax.ShapeDtypeStruct((depth, 8, n), F32),
        compiler_params=_cparams(("parallel", "parallel")),
        name="adaln_mod",
    )(c8, w_ada, b_ada.reshape(depth, 1, n))


def _rms_mod(x, g, shift, scale):
    ms = jnp.mean(x * x, axis=-1, keepdims=True)
    h = x * lax.rsqrt(ms + EPS) * g
    return h * (1.0 + scale) + shift


def _head_norm(z, a, g):
    zz = (z * z).astype(BF16)
    m = jnp.concatenate([_dot(zz[:, c:c + ATT_LANES], a) for c in range(0, z.shape[1], ATT_LANES)], axis=1)
    return z * lax.rsqrt(m + EPS) * g


TOK_TB = 128


def _inproj_kernel(x_ref, mod_ref, g_ref, w_ref, a_ref, gq_ref, gk_ref,
                   q_ref, k_ref, v_ref, hz_ref, u_ref):
    B, tb, _ = x_ref.shape
    h = jnp.concatenate([_rms_mod(x_ref[b], g_ref[...], mod_ref[b, 0:1, :], mod_ref[b, 1:2, :]).astype(BF16)
                         for b in range(B)], axis=0)
    a = a_ref[...]
    q = (_head_norm(_dot(h, w_ref[:, 0:COL_K]), a, gq_ref[...]) * (ATT_HEAD_DIM ** -0.5)).astype(BF16)
    k = _head_norm(_dot(h, w_ref[:, COL_K:COL_V]), a, gk_ref[...]).astype(BF16)
    v = _dot(h, w_ref[:, COL_V:COL_HY]).astype(BF16)
    hz = _dot(h, w_ref[:, COL_HY:COL_SSM])
    u = _dot(h, w_ref[:, COL_SSM:D_IN])
    for b in range(B):
        rows = slice(b * tb, (b + 1) * tb)
        q_ref[b] = q[rows]
        k_ref[b] = k[rows]
        v_ref[b] = v[rows]
        hz_ref[b] = hz[rows]
        for s in range(D_SSM // 128):
            u_ref[s, pl.ds(b, tb, stride=B), :] = u[rows, s * 128:(s + 1) * 128]


def in_proj(x, mod, g1, w_in_bf, gq, gk):
    B, T, D = x.shape
    tb = TOK_TB
    head_avg = jnp.asarray(np.kron(np.eye(ATT_LANES // ATT_HEAD_DIM),
                                   np.full((ATT_HEAD_DIM, ATT_HEAD_DIM), 1.0 / ATT_HEAD_DIM)), BF16)
    tok = lambda n: pl.BlockSpec((B, tb, n), lambda i: (0, i, 0))
    const = lambda shape: pl.BlockSpec(shape, lambda i: (0,) * len(shape))
    return pl.pallas_call(
        _inproj_kernel,
        grid=(T // tb,),
        in_specs=[tok(D), const((B, 6, D)), const((1, D)),
                  const((D, D_IN)), const((ATT_LANES, ATT_LANES)), const((1, D_ATT)), const((1, D_ATT))],
        out_specs=[tok(D_ATT), tok(D_ATT), tok(D_ATT), tok(3 * D_HYENA),
                   pl.BlockSpec((D_SSM // 128, tb * B, 128), lambda i: (0, i, 0))],
        out_shape=[jax.ShapeDtypeStruct((B, T, D_ATT), BF16)] * 3
                  + [jax.ShapeDtypeStruct((B, T, 3 * D_HYENA), F32),
                     jax.ShapeDtypeStruct((D_SSM // 128, T * B, 128), F32)],
        compiler_params=_cparams(("parallel",), VMEM_LIMIT),
        name="in_proj",
    )(x, mod, g1.reshape(1, D), w_in_bf, head_avg,
      jnp.tile(gq, ATT_HEADS).reshape(1, D_ATT), jnp.tile(gk, ATT_HEADS).reshape(1, D_ATT))


ATT_R = 4
ATT_KROWS = ATT_R + WIN_ROWS - 1
ATT_LANES = 256


def _softmax_heads(q, parts, bias_ref, o_ref):
    nh = ATT_LANES // ATT_HEAD_DIM
    lane = lax.broadcasted_iota(jnp.int32, (1, ATT_LANES), 1)
    head = lambda h: (lane >= ATT_HEAD_DIM * h) & (lane < ATT_HEAD_DIM * (h + 1))
    acc = jnp.zeros((q.shape[0], ATT_LANES), F32)
    for h in range(nh):
        hm = head(h)
        qh = jnp.where(hm, q, jnp.zeros_like(q))
        ss = [_dot_nt(qh, kk).astype(BF16) for kk, _ in parts]
        if bias_ref is not None:
            ss[0] = ss[0] + bias_ref[h]
        m = ss[0].max(axis=-1, keepdims=True)
        for s in ss[1:]:
            m = jnp.maximum(m, s.max(axis=-1, keepdims=True))
        om = head((h + 1) % nh)
        o = None
        for s, (_, vv) in zip(ss, parts):
            part = _dot(jnp.exp(s - m), jnp.where(om, jnp.ones_like(vv), vv))
            o = part if o is None else o + part
        l = pltpu.roll(o, ATT_LANES - ATT_HEAD_DIM, axis=1)
        acc = jnp.where(hm, o / l, acc)
    o_ref[...] = acc.astype(o_ref.dtype)


def _nattn_kernel(q_ref, k_ref, v_ref, kc_ref, vc_ref, bias_ref, o_ref, *, rows):
    j = pl.program_id(2)
    ks = jnp.clip(j * ATT_R - WIN_ROWS // 2, 0, rows - ATT_KROWS)
    start = pl.multiple_of(ks * GRID_W, GRID_W)
    kl = k_ref[pl.ds(start, ATT_KROWS * GRID_W), :]
    vl = v_ref[pl.ds(start, ATT_KROWS * GRID_W), :]
    _softmax_heads(q_ref[...], [(kl, vl), (kc_ref[...], vc_ref[...])], bias_ref, o_ref)


def _attn_bias_tables(rpb, rows):
    nblk = rows // ATT_R
    qc = np.arange(GRID_W)
    c0 = np.clip(qc - WIN_COLS // 2, 0, GRID_W - WIN_COLS)
    kc = np.arange(GRID_W)
    col_ok = (kc[None, :] >= c0[:, None]) & (kc[None, :] < c0[:, None] + WIN_COLS)
    col_off = kc[None, :] - qc[:, None] + WIN_COLS - 1
    col_sel = (col_off[:, :, None] == np.arange(2 * WIN_COLS - 1)) & col_ok[:, :, None]
    band = jnp.einsum('hij,qkj->hqik', rpb, jnp.asarray(col_sel, F32), precision=HIGHEST)
    band = jnp.where(jnp.asarray(col_ok)[None, :, None, :], band, NEG)
    band = band.reshape(ATT_HEADS, GRID_W, (2 * WIN_ROWS - 1) * GRID_W).astype(BF16)
    neg = lambda nrows: jnp.full((ATT_HEADS, GRID_W, nrows * GRID_W), NEG, BF16)
    tabs = []
    for blk in (0, 1, nblk - 1):
        ks = int(np.clip(blk * ATT_R - WIN_ROWS // 2, 0, rows - ATT_KROWS))
        qrows = []
        for r in range(blk * ATT_R, (blk + 1) * ATT_R):
            r0 = int(np.clip(r - WIN_ROWS // 2, 0, rows - WIN_ROWS))
            first = r0 - r + WIN_ROWS - 1
            win = band[:, :, first * GRID_W:(first + WIN_ROWS) * GRID_W]
            qrows.append(jnp.concatenate([neg(r0 - ks), win, neg(ATT_KROWS - WIN_ROWS - (r0 - ks))], axis=-1))
        tabs.append(jnp.stack(qrows, axis=1).reshape(ATT_HEADS, ATT_R * GRID_W, ATT_KROWS * GRID_W))
    return jnp.stack(tabs)


def neighbourhood_attention(q, k, v, kc, vc, rpb):
    B, L, _ = q.shape
    Lc = kc.shape[1]
    rows = L // GRID_W
    nblk = rows // ATT_R
    tq = ATT_R * GRID_W
    nkl = ATT_KROWS * GRID_W
    bias = _attn_bias_tables(rpb, rows)
    hpb = ATT_LANES // ATT_HEAD_DIM
    variant = lambda j: jnp.where(j == 0, 0, jnp.where(j == nblk - 1, 2, 1))
    return pl.pallas_call(
        functools.partial(_nattn_kernel, rows=rows),
        grid=(B, D_ATT // ATT_LANES, nblk),
        in_specs=[pl.BlockSpec((None, tq, ATT_LANES), lambda b, g, j: (b, j, g)),
                  pl.BlockSpec((None, L, ATT_LANES), lambda b, g, j: (b, 0, g)),
                  pl.BlockSpec((None, L, ATT_LANES), lambda b, g, j: (b, 0, g)),
                  pl.BlockSpec((None, Lc, ATT_LANES), lambda b, g, j: (b, 0, g)),
                  pl.BlockSpec((None, Lc, ATT_LANES), lambda b, g, j: (b, 0, g)),
                  pl.BlockSpec((None, hpb, tq, nkl), lambda b, g, j: (variant(j), g, 0, 0))],
        out_specs=pl.BlockSpec((None, tq, ATT_LANES), lambda b, g, j: (b, j, g)),
        out_shape=jax.ShapeDtypeStruct((B, L, D_ATT), BF16),
        compiler_params=_cparams(("parallel", "parallel", "arbitrary"), VMEM_LIMIT),
        name="nattn",
    )(q, k, v, kc, vc, bias)


def _cattn_kernel(q_ref, k_ref, v_ref, o_ref):
    _softmax_heads(q_ref[...], [(k_ref[...], v_ref[...])], None, o_ref)


def context_attention(q, k, v):
    B, Lc, _ = q.shape
    spec = pl.BlockSpec((None, Lc, ATT_LANES), lambda b, g: (b, 0, g))
    return pl.pallas_call(
        _cattn_kernel,
        grid=(B, D_ATT // ATT_LANES),
        in_specs=[spec, spec, spec],
        out_specs=spec,
        out_shape=jax.ShapeDtypeStruct((B, Lc, D_ATT), BF16),
        compiler_params=_cparams(("parallel", "parallel")),
        name="cattn",
    )(q, k, v)


FFT_N = 16384
FFT_R = 128
FFT_K1 = FFT_R // 2 + 1
FFT_K1P = 66
FFT_PITCH = 2 * FFT_K1P


def _filter_kernel(w1_ref, wt_ref, b1_ref, w2_ref, b2_ref, w3_ref, b3_ref, fr_ref, wo_ref, o_ref, *, lf, tile):
    i = pl.program_id(0)
    half = FFT_N // 2
    row0 = i * tile
    first_pos = jnp.where(row0 < half, row0, FFT_N - row0 - tile)

    @pl.when(first_pos >= lf)
    def _():
        o_ref[...] = jnp.zeros_like(o_ref)

    @pl.when(first_pos < lf)
    def _():
        rows = tile // FILTER_PACK

        def position(width):
            r = lax.broadcasted_iota(jnp.int32, (rows, FILTER_PACK * width), 0)
            g = lax.broadcasted_iota(jnp.int32, (rows, FILTER_PACK * width), 1) // width
            j = row0 + r + g * rows
            return jnp.where(j < half, j, FFT_N - 1 - j)

        nfeat = 2 * FILTER_BANDS
        feat = lax.broadcasted_iota(jnp.int32, (1, FILTER_PACK * nfeat), 1) % nfeat
        f = 1e-4 + (feat % FILTER_BANDS).astype(F32) * ((FILTER_BANDS - 1 - 1e-4) / (FILTER_BANDS - 1))
        w = (2.0 * math.pi / lf) * position(nfeat).astype(F32)
        z = jnp.cos(f * w + jnp.where(feat < FILTER_BANDS, 0.0, 0.5 * math.pi))
        t_h = position(FILTER_WIDTH).astype(F32) / (lf - 1.0)
        fr = fr_ref[...]
        hdot = lambda a, b: jnp.dot(a, b, precision=HIGHEST, preferred_element_type=F32)
        h = jnp.sin(fr * (hdot(z, w1_ref[...]) + t_h * wt_ref[...] + b1_ref[...]))
        h = jnp.sin(fr * (hdot(h, w2_ref[...]) + b2_ref[...]))
        h = jnp.sin(fr * (hdot(h, w3_ref[...]) + b3_ref[...]))
        y = hdot(h, wo_ref[...])
        nout = HYENA_ORDER * D_HYENA
        pos_o = position(nout)
        ch = lax.broadcasted_iota(jnp.int32, (1, FILTER_PACK * nout), 1) % D_HYENA
        delta = jnp.abs(MIN_DECAY + ch.astype(F32) * ((MAX_DECAY - MIN_DECAY) / (D_HYENA - 1)))
        y = jnp.where(pos_o < lf, y * jnp.exp(-(pos_o.astype(F32) / (lf - 1.0)) * delta), 0.0)
        for g in range(FILTER_PACK):
            o_ref[g * rows:(g + 1) * rows, :] = y[:, g * nout:(g + 1) * nout]


FILTER_PACK = 4


def hyena_filter_taps(lf, w1, b1, w2, b2, w3, b3, freq, w_out, tile=1024):
    P, W = FILTER_PACK, FILTER_WIDTH
    nout = HYENA_ORDER * D_HYENA
    bd = lambda m: jnp.kron(jnp.eye(P, dtype=F32), m.astype(F32))
    wide = lambda a: jnp.tile(a.astype(F32), P).reshape(1, P * W)
    wo = w_out.reshape(W, HYENA_ORDER, 2, D_HYENA).transpose(2, 0, 1, 3).reshape(2, W, nout)
    wo = jnp.stack([bd(wo[0]), bd(wo[1])])
    nt = FFT_N // tile
    const = lambda shape: pl.BlockSpec(shape, lambda i: (0,) * len(shape))
    return pl.pallas_call(
        functools.partial(_filter_kernel, lf=lf, tile=tile),
        grid=(nt,),
        in_specs=[const((P * (FILTER_EMB - 1), P * W)), const((1, P * W)), const((1, P * W)),
                  const((P * W, P * W)), const((1, P * W)), const((P * W, P * W)), const((1, P * W)),
                  const((1, P * W)),
                  pl.BlockSpec((None, P * W, P * nout), lambda i: (i // (nt // 2), 0, 0))],
        out_specs=pl.BlockSpec((tile, nout), lambda i: (i, 0)),
        out_shape=jax.ShapeDtypeStruct((FFT_N, nout), F32),
        compiler_params=_cparams(("parallel",)),
        name="hyena_filter",
    )(bd(w1[1:]), wide(w1[0]), wide(b1), bd(w2), wide(b2), bd(w3), wide(b3), wide(freq), wo)


def _shortconv_kernel(z_ref, w_ref, b_ref, o_ref, zp_ref, *, L):
    C = z_ref.shape[-1]
    zp_ref[0:8, :] = jnp.zeros((8, C), F32)
    zp_ref[L + 8:L + 16, :] = jnp.zeros((8, C), F32)
    zp_ref[8:L + 8, :] = z_ref[...]
    ch = min(L, 512)
    for c in range(L // ch):
        s = c * ch
        o_ref[s:s + ch, :] = (b_ref[...] + zp_ref[s + 7:s + 7 + ch, :] * w_ref[0:1, :]
                              + zp_ref[s + 8:s + 8 + ch, :] * w_ref[1:2, :]
                              + zp_ref[s + 9:s + 9 + ch, :] * w_ref[2:3, :])


def short_conv(z, w, b, cb=128):
    B, L, C = z.shape
    return pl.pallas_call(
        functools.partial(_shortconv_kernel, L=L),
        grid=(B, C // cb),
        in_specs=[pl.BlockSpec((None, L, cb), lambda b_, c: (b_, 0, c)),
                  pl.BlockSpec((3, cb), lambda b_, c: (0, c)),
                  pl.BlockSpec((1, cb), lambda b_, c: (0, c))],
        out_specs=pl.BlockSpec((None, L, cb), lambda b_, c: (b_, 0, c)),
        out_shape=jax.ShapeDtypeStruct((B, L, C), F32),
        scratch_shapes=[pltpu.VMEM((L + 16, cb), F32)],
        compiler_params=_cparams(("parallel", "parallel")),
        name="short_conv",
    )(z, w, b.reshape(1, C))


@functools.lru_cache(maxsize=None)
def _fft_tables(n1_in):
    n2 = np.arange(FFT_R)[:, None, None]
    k1 = np.arange(FFT_K1)[None, :, None]
    n1 = np.arange(n1_in)[None, None, :]
    th = 2.0 * np.pi * ((k1 * (FFT_R * n1 + n2)) % FFT_N) / FFT_N
    ga = np.zeros((FFT_R, 2 * FFT_K1P, n1_in))
    ga[:, :FFT_K1] = np.cos(th)
    ga[:, FFT_K1P:FFT_K1P + FFT_K1] = -np.sin(th)
    wk = np.where((np.arange(FFT_K1) == 0) | (np.arange(FFT_K1) == FFT_R // 2), 1.0, 2.0)[None, None, :]
    thd = th.transpose(0, 2, 1)
    gd = np.zeros((FFT_R, n1_in, 2 * FFT_K1P))
    gd[:, :, :FFT_K1] = wk * np.cos(thd)
    gd[:, :, FFT_K1P:FFT_K1P + FFT_K1] = -wk * np.sin(thd)
    a = np.arange(FFT_R)
    ph = 2.0 * np.pi * ((a[:, None] * a[None, :]) % FFT_R) / FFT_R
    c, s = np.cos(ph), np.sin(ph)
    fb = np.block([[c, s], [-s, c]])
    fc = np.block([[c, -s], [s, c]])
    return ga, gd, fb, fc


FFT_KG = 13
FFT_UNROLL = 16


def _spectrum_slabs(ya_ref, fb, k0):
    cols = []
    for g in range(FFT_KG):
        re = ya_ref[pl.ds(k0 + g, FFT_R, stride=FFT_PITCH), :]
        im = ya_ref[pl.ds(FFT_K1P + k0 + g, FFT_R, stride=FFT_PITCH), :]
        cols.append(jnp.concatenate([re, im], axis=0).astype(BF16))
    return _dot(fb, jnp.concatenate(cols, axis=1))


def _stage_a(x_ref, x_pitch, ga_ref, ya_ref, n1_in):
    def body(n2, carry):
        xs = x_ref[pl.ds(n2, n1_in, stride=x_pitch), :].astype(BF16)
        ya_ref[pl.ds(n2 * FFT_PITCH, FFT_PITCH), :] = _dot(ga_ref[n2], xs)
        return carry
    lax.fori_loop(0, FFT_R, body, 0, unroll=FFT_UNROLL)


def _spectrum_kernel(x_ref, ga_ref, fb_ref, o_ref, ya_ref):
    _stage_a(x_ref, FFT_R, ga_ref, ya_ref, FFT_R)
    fb = fb_ref[...]
    cb = ya_ref.shape[1]

    def body(kg, carry):
        x = _spectrum_slabs(ya_ref, fb, kg * FFT_KG)
        for g in range(FFT_KG):
            o_ref[kg * FFT_KG + g] = x[:, g * cb:(g + 1) * cb]
        return carry
    lax.fori_loop(0, FFT_K1 // FFT_KG, body, 0)


def filter_spectrum(taps, cb=128):
    C = taps.shape[1]
    ga, _, fb, _ = _fft_tables(FFT_R)
    return pl.pallas_call(
        _spectrum_kernel,
        grid=(C // cb,),
        in_specs=[pl.BlockSpec((FFT_N, cb), lambda c: (0, c)),
                  pl.BlockSpec((FFT_R, FFT_PITCH, FFT_R), lambda c: (0, 0, 0)),
                  pl.BlockSpec((2 * FFT_R, 2 * FFT_R), lambda c: (0, 0))],
        out_specs=pl.BlockSpec((FFT_K1, 2 * FFT_R, cb), lambda c: (0, 0, c)),
        out_shape=jax.ShapeDtypeStruct((FFT_K1, 2 * FFT_R, C), F32),
        scratch_shapes=[pltpu.VMEM((FFT_R * FFT_PITCH, cb), F32)],
        compiler_params=_cparams(("parallel",), VMEM_LIMIT),
        name="filter_spectrum",
    )(taps, _bf16_table(ga), _bf16_table(fb))


def _bf16_table(t):
    return jnp.asarray(t, F32).astype(BF16)


def _longconv_kernel(u_ref, gate_ref, bias_ref, spec_ref, ga_ref, gd_ref, fb_ref, fc_ref, o_ref, ya_ref, xp_ref):
    n1_in = FFT_R // 2

    def repitch(n1, carry):
        xp_ref[pl.ds(n1 * FFT_PITCH, FFT_R), :] = u_ref[pl.ds(pl.multiple_of(n1 * FFT_R, FFT_R), FFT_R), :]
        return carry
    lax.fori_loop(0, n1_in, repitch, 0, unroll=FFT_UNROLL)

    _stage_a(xp_ref, FFT_PITCH, ga_ref, ya_ref, n1_in)
    fb = fb_ref[...]
    fc = fc_ref[...]

    cb = ya_ref.shape[1]

    def mid(kg, carry):
        k0 = kg * FFT_KG
        x = _spectrum_slabs(ya_ref, fb, k0)
        s = jnp.concatenate([spec_ref[k0 + g] for g in range(FFT_KG)], axis=1)
        xr, xi = x[:FFT_R], x[FFT_R:]
        sr, si = s[:FFT_R], s[FFT_R:]
        z = jnp.concatenate([xr * sr - xi * si, xr * si + xi * sr], axis=0).astype(BF16)
        v = _dot(fc, z)
        for g in range(FFT_KG):
            ya_ref[pl.ds(k0 + g, FFT_R, stride=FFT_PITCH), :] = v[:FFT_R, g * cb:(g + 1) * cb]
            ya_ref[pl.ds(FFT_K1P + k0 + g, FFT_R, stride=FFT_PITCH), :] = v[FFT_R:, g * cb:(g + 1) * cb]
        return carry
    lax.fori_loop(0, FFT_K1 // FFT_KG, mid, 0)

    def last(n2, carry):
        slab = ya_ref[pl.ds(n2 * FFT_PITCH, FFT_PITCH), :].astype(BF16)
        xp_ref[pl.ds(n2, n1_in, stride=FFT_PITCH), :] = _dot(gd_ref[n2], slab)
        return carry
    lax.fori_loop(0, FFT_R, last, 0, unroll=FFT_UNROLL)

    bias = bias_ref[...]

    def gate(n1, carry):
        rows = pl.ds(pl.multiple_of(n1 * FFT_R, FFT_R), FFT_R)
        y = xp_ref[pl.ds(n1 * FFT_PITCH, FFT_R), :] * (1.0 / FFT_N)
        o_ref[rows, :] = gate_ref[rows, :] * (y + u_ref[rows, :] * bias)
        return carry
    lax.fori_loop(0, n1_in, gate, 0, unroll=FFT_UNROLL)


def gated_long_conv(u, u_col, gate, gate_col, bias, spec, spec_col, C=D_HYENA, cb=128):
    B, L, _ = u.shape
    n1_in = FFT_R // 2
    ga, gd, fb, fc = _fft_tables(n1_in)
    once = pl.Buffered(1)
    seq = lambda col: pl.BlockSpec((None, L, cb), lambda c, b: (b, 0, c + col // cb))
    return pl.pallas_call(
        _longconv_kernel,
        grid=(C // cb, B),
        in_specs=[seq(u_col), seq(gate_col),
                  pl.BlockSpec((1, cb), lambda c, b: (0, c)),
                  pl.BlockSpec((FFT_K1, 2 * FFT_R, cb), lambda c, b: (0, 0, c + spec_col // cb), pipeline_mode=once),
                  pl.BlockSpec((FFT_R, FFT_PITCH, n1_in), lambda c, b: (0, 0, 0), pipeline_mode=once),
                  pl.BlockSpec((FFT_R, n1_in, FFT_PITCH), lambda c, b: (0, 0, 0), pipeline_mode=once),
                  pl.BlockSpec((2 * FFT_R, 2 * FFT_R), lambda c, b: (0, 0)),
                  pl.BlockSpec((2 * FFT_R, 2 * FFT_R), lambda c, b: (0, 0))],
        out_specs=seq(0),
        out_shape=jax.ShapeDtypeStruct((B, L, C), F32),
        scratch_shapes=[pltpu.VMEM((FFT_R * FFT_PITCH, cb), F32), pltpu.VMEM((n1_in * FFT_PITCH, cb), F32)],
        compiler_params=_cparams(("parallel", "arbitrary"), VMEM_LIMIT),
        name="long_conv",
    )(u, gate, bias.reshape(1, C), spec, _bf16_table(ga), _bf16_table(gd), _bf16_table(fb), _bf16_table(fc))


SSM_N = SSM_GROUPS * SSM_STATE
S5_SEQS = 8
S5_CHUNK = 128


def _s5_kernel(uf_ref, ub_ref, bf_ref, bb_ref, lam_ref, cf_ref, cb_ref, yf_ref, yb_ref,
               buf_ref, bub_ref, xsf_ref, xsb_ref, st_ref):
    half = S5_SEQS // 2
    tiles = S5_CHUNK // 2

    @pl.when(pl.program_id(0) == 0)
    def _():
        st_ref[...] = jnp.zeros_like(st_ref)

    slabs = lambda ref: jnp.concatenate([ref[s] for s in range(ref.shape[0])], axis=1).astype(BF16)
    buf_ref[...] = _dot(slabs(uf_ref), bf_ref[...])
    bub_ref[...] = _dot(slabs(ub_ref), bb_ref[...])
    lo = lax.broadcasted_iota(jnp.int32, (S5_SEQS, 1), 0) < half

    def advance(x, lam, b):
        xr, xi = x
        lr, li = lam[:, 0:SSM_N], lam[:, SSM_N:2 * SSM_N]
        return lr * xr - li * xi + b[:, 0:SSM_N], lr * xi + li * xr + b[:, SSM_N:2 * SSM_N]

    def swap(x):
        return pltpu.roll(x[0], half, axis=0), pltpu.roll(x[1], half, axis=0)

    def step(t4, x):
        f_tiles, b_tiles = [], []
        for p in range(2):
            tf = buf_ref[pl.ds(pl.multiple_of((2 * t4 + p) * S5_SEQS, S5_SEQS), S5_SEQS), :]
            tb = bub_ref[pl.ds(pl.multiple_of((tiles - 1 - 2 * t4 - p) * S5_SEQS, S5_SEQS), S5_SEQS), :]
            xe = advance(x, lam_ref[0], jnp.where(lo, tf, tb))
            xo = advance(swap(xe), lam_ref[1], jnp.where(lo, tb, tf))
            x = swap(xo)
            xe = jnp.concatenate(xe, axis=1)
            xo = jnp.concatenate(xo, axis=1)
            f_tiles.append(jnp.where(lo, xe, xo))
            b_tiles.append(jnp.where(lo, xo, xe))
        xsf_ref[pl.ds(pl.multiple_of(t4 * 2 * S5_SEQS, 2 * S5_SEQS), 2 * S5_SEQS), :] = (
            jnp.concatenate(f_tiles, axis=0).astype(BF16))
        xsb_ref[pl.ds(pl.multiple_of((tiles - 2 - 2 * t4) * S5_SEQS, 2 * S5_SEQS), 2 * S5_SEQS), :] = (
            jnp.concatenate(b_tiles[::-1], axis=0).astype(BF16))
        return x

    xr, xi = lax.fori_loop(0, S5_CHUNK // 4, step, (st_ref[:, 0:SSM_N], st_ref[:, SSM_N:2 * SSM_N]))
    st_ref[:, 0:SSM_N] = xr
    st_ref[:, SSM_N:2 * SSM_N] = xi
    yf = _dot(xsf_ref[...], cf_ref[...])
    yb = _dot(xsb_ref[...], cb_ref[...])
    for s in range(yf_ref.shape[0]):
        yf_ref[s] = yf[:, s * 128:(s + 1) * 128]
        yb_ref[s] = yb[:, s * 128:(s + 1) * 128]


def _s5_matrices(lam_re, lam_im, log_step, b_re, b_im, c_re, c_im):
    G, P, Hs = SSM_GROUPS, SSM_STATE, SSM_GROUP_WIDTH
    dt = jnp.exp(log_step)[:, :, None]
    mag = jnp.exp(lam_re * dt)
    ar, ai = mag * jnp.cos(lam_im * dt), mag * jnp.sin(lam_im * dt)
    er, ei = ar - 1.0, ai
    den = lam_re * lam_re + lam_im * lam_im
    cr, ci = (er * lam_re + ei * lam_im) / den, (ei * lam_re - er * lam_im) / den
    bbr = cr[..., None] * b_re - ci[..., None] * b_im
    bbi = cr[..., None] * b_im + ci[..., None] * b_re
    eye = jnp.eye(G, dtype=F32)
    blk_in = lambda m: jnp.einsum('gq,qph->ghqp', eye, m).reshape(G * Hs, G * P)
    bmat = [jnp.concatenate([blk_in(bbr[d]), blk_in(bbi[d])], axis=1).astype(BF16) for d in range(2)]
    blk_out = lambda m: jnp.einsum('gq,qhp->gpqh', eye, m).reshape(G * P, G * Hs)
    cmat = [jnp.concatenate([blk_out(c_re[d]), -blk_out(c_im[d])], axis=0).astype(BF16) for d in range(2)]
    lam = jnp.concatenate([ar.reshape(2, 1, G * P), ai.reshape(2, 1, G * P)], axis=2)
    lam = jnp.broadcast_to(lam, (2, S5_SEQS // 2, 2 * G * P))
    lam8 = jnp.stack([lam.reshape(S5_SEQS, 2 * G * P), lam[::-1].reshape(S5_SEQS, 2 * G * P)])
    return bmat[0], bmat[1], lam8, cmat[0], cmat[1]


def s5_scan(u_x, u_c, lam_re, lam_im, log_step, b_re, b_im, c_re, c_im):
    S = u_x.shape[0]
    half = S5_SEQS // 2
    L, Lc = u_x.shape[1] // half, u_c.shape[1] // half
    assert L % S5_CHUNK == 0 and Lc % S5_CHUNK == 0
    bf, bb, lam8, cf, cb = _s5_matrices(lam_re, lam_im, log_step, b_re, b_im, c_re, c_im)
    un = jnp.concatenate([u_x, u_c], axis=1)
    rows = S5_CHUNK * half
    nx, nc = L // S5_CHUNK, Lc // S5_CHUNK
    n = nx + nc
    fwd = pl.BlockSpec((S, rows, 128), lambda i: (0, jnp.where(i < nc, nx + i, i - nc), 0))
    bwd = pl.BlockSpec((S, rows, 128), lambda i: (0, n - 1 - i, 0))
    const = lambda shape: pl.BlockSpec(shape, lambda i: (0,) * len(shape))
    yf, yb = pl.pallas_call(
        _s5_kernel,
        grid=(n,),
        in_specs=[fwd, bwd, const((D_SSM, 2 * SSM_N)), const((D_SSM, 2 * SSM_N)),
                  const((2, S5_SEQS, 2 * SSM_N)), const((2 * SSM_N, D_SSM)), const((2 * SSM_N, D_SSM))],
        out_specs=[fwd, bwd],
        out_shape=[jax.ShapeDtypeStruct(un.shape, F32)] * 2,
        scratch_shapes=[pltpu.VMEM((rows, 2 * SSM_N), F32), pltpu.VMEM((rows, 2 * SSM_N), F32),
                        pltpu.VMEM((rows, 2 * SSM_N), BF16), pltpu.VMEM((rows, 2 * SSM_N), BF16),
                        pltpu.VMEM((S5_SEQS, 2 * SSM_N), F32)],
        compiler_params=_cparams(("arbitrary",), VMEM_LIMIT),
        name="s5_scan",
    )(un, un, bf, bb, lam8, cf, cb)
    return un, yf, yb


def _gelu_tanh(y):
    return 0.5 * y * (1.0 + jnp.tanh(math.sqrt(2.0 / math.pi) * (y + 0.044715 * (y * y * y))))


def _outproj_kernel(*refs, route):
    (x_ref, mod_ref, att_ref, hy_ref, yf_ref, yb_ref, u_ref, dsk_ref, wg_ref, bg_ref, wo_ref, g2_ref) = refs[:12]
    if route:
        wr_ref, x1_ref, h2_ref, rt_ref = refs[12:]
    else:
        x1_ref, h2_ref = refs[12:]
    B, tb, D = x_ref.shape
    rows = lambda a: a.reshape(B * tb, a.shape[-1])

    def scan_rows(ref):
        return jnp.concatenate(
            [jnp.concatenate([ref[s, pl.ds(b, tb, stride=B), :] for s in range(ref.shape[0])], axis=1)
             for b in range(B)], axis=0)

    y = scan_rows(yf_ref) + scan_rows(yb_ref) + scan_rows(u_ref) * dsk_ref[...]
    y = _gelu_tanh(y)
    ss = y * jax.nn.sigmoid(_dot(y.astype(BF16), wg_ref[...]) + bg_ref[...])
    mix = (_dot(rows(att_ref[...]), wo_ref[0:D_ATT, :])
           + _dot(rows(hy_ref[...]).astype(BF16), wo_ref[D_ATT:D_ATT + D_HYENA, :])
           + _dot(ss.astype(BF16), wo_ref[D_ATT + D_HYENA:, :]))
    h2s = []
    for b in range(B):
        x1 = x_ref[b] + mod_ref[b, 2:3, :] * mix[b * tb:(b + 1) * tb]
        x1_ref[b] = x1
        h2s.append(_rms_mod(x1, g2_ref[...], mod_ref[b, 3:4, :], mod_ref[b, 4:5, :]))
        h2_ref[b] = h2s[-1].astype(h2_ref.dtype)
    if route:
        h2 = jnp.concatenate(h2s, axis=0)
        h_hi = h2.astype(BF16)
        h_lo = (h2 - h_hi.astype(F32)).astype(BF16)
        logits = _dot_nt(wr_ref[0], h_hi) + _dot_nt(wr_ref[0], h_lo) + _dot_nt(wr_ref[1], h_hi)
        row = lax.broadcasted_iota(jnp.int32, (ROUTE_ROWS, 1), 0)
        lg = jnp.where(row < N_EXPERTS, logits, -jnp.inf)
        v1 = lg.max(axis=0, keepdims=True)
        i1 = jnp.where(lg == v1, row, ROUTE_ROWS).min(axis=0, keepdims=True)
        lg2 = jnp.where(row == i1, -jnp.inf, lg)
        v2 = lg2.max(axis=0, keepdims=True)
        i2 = jnp.where(lg2 == v2, row, ROUTE_ROWS).min(axis=0, keepdims=True)
        e = jnp.exp(v2 - v1)
        w1 = 1.0 / (1.0 + e)
        out_row = lax.broadcasted_iota(jnp.int32, (8, 1), 0)
        rt_ref[...] = jnp.where(out_row == 0, i1.astype(F32),
                                jnp.where(out_row == 1, i2.astype(F32),
                                          jnp.where(out_row == 2, w1, jnp.where(out_row == 3, e * w1, 0.0))))


def out_proj(x, mod, att, hy, yf, yb, u, scan_row0, d_skip, w_glu, b_glu, w_out_bf, g2, router_w=None):
    B, T, D = x.shape
    tb = TOK_TB
    route = router_w is not None
    S = u.shape[0]
    tok = lambda n: pl.BlockSpec((B, tb, n), lambda i: (0, i, 0))
    scan = pl.BlockSpec((S, tb * B, 128), lambda i: (0, i + scan_row0 // tb, 0))
    const = lambda shape: pl.BlockSpec(shape, lambda i: (0,) * len(shape))
    in_specs = [tok(D), const((B, 6, D)), tok(D_ATT), tok(D_HYENA),
                scan, scan, scan, const((1, D_SSM)), const((D_SSM, D_SSM)), const((1, D_SSM)),
                const((D, D)), const((1, D))]
    args = [x, mod, att, hy, yf, yb, u, d_skip.reshape(1, D_SSM), w_glu.astype(BF16), b_glu.reshape(1, D_SSM),
            w_out_bf, g2.reshape(1, D)]
    out_specs = [tok(D), tok(D)]
    out_shape = [jax.ShapeDtypeStruct((B, T, D), F32), jax.ShapeDtypeStruct((B, T, D), BF16)]
    if route:
        in_specs.append(const((2, ROUTE_ROWS, D)))
        wr = jnp.pad(router_w.astype(F32).T, ((0, ROUTE_ROWS - N_EXPERTS), (0, 0)))
        wr_hi = wr.astype(BF16)
        args.append(jnp.stack([wr_hi, (wr - wr_hi.astype(F32)).astype(BF16)]))
        out_specs.append(pl.BlockSpec((None, 8, B * tb), lambda i: (i, 0, 0)))
        out_shape.append(jax.ShapeDtypeStruct((T // tb, 8, B * tb), F32))
    outs = pl.pallas_call(
        functools.partial(_outproj_kernel, route=route),
        grid=(T // tb,),
        in_specs=in_specs, out_specs=out_specs, out_shape=out_shape,
        compiler_params=_cparams(("parallel",), VMEM_LIMIT),
        name="out_proj",
    )(*args)
    outs = list(outs)
    if route:
        outs[2] = outs[2].reshape(T // tb, 8, B, tb).transpose(2, 0, 3, 1).reshape(B, T, 8)
    return outs


ROUTE_ROWS = 16


FF_CHUNK = 768
FF_VMEM_LIMIT = 60 * 1024 * 1024


def _swiglu_tile(h, w1_ref, w3_ref, w2_ref):
    acc = None
    for lo in range(0, D_FF, FF_CHUNK):
        hi = min(lo + FF_CHUNK, D_FF)
        a = _dot(h, w1_ref[:, lo:hi].astype(BF16))
        g = (a * jax.nn.sigmoid(a) * _dot(h, w3_ref[:, lo:hi].astype(BF16))).astype(BF16)
        part = _dot(g, w2_ref[lo:hi, :].astype(BF16))
        acc = part if acc is None else acc + part
    return acc


def _ffn_kernel(h_ref, x_ref, mod_ref, w1_ref, w3_ref, w2_ref, o_ref):
    o_ref[...] = x_ref[...] + mod_ref[5:6, :] * _swiglu_tile(h_ref[...], w1_ref, w3_ref, w2_ref)


def ffn_dense(h2, x1, mod, w1, w3, w2, tm=512):
    B, T, D = x1.shape
    tm = min(tm, T)
    tok = pl.BlockSpec((None, tm, D), lambda b, i: (b, i, 0))
    once = pl.Buffered(1)
    return pl.pallas_call(
        _ffn_kernel,
        grid=(B, T // tm),
        in_specs=[tok, tok, pl.BlockSpec((None, 6, D), lambda b, i: (b, 0, 0)),
                  pl.BlockSpec((D, D_FF), lambda b, i: (0, 0), pipeline_mode=once),
                  pl.BlockSpec((D, D_FF), lambda b, i: (0, 0), pipeline_mode=once),
                  pl.BlockSpec((D_FF, D), lambda b, i: (0, 0), pipeline_mode=once)],
        out_specs=tok,
        out_shape=jax.ShapeDtypeStruct((B, T, D), F32),
        compiler_params=_cparams(("parallel", "parallel"), FF_VMEM_LIMIT),
        name="ffn_dense",
    )(h2, x1, mod, w1, w3, w2)


MOE_TM = 512


def _moe_kernel(te_ref, tv_ref, h_ref, w1_ref, w3_ref, w2_ref, o_ref):
    valid = tv_ref[pl.program_id(0)] > 0

    @pl.when(valid)
    def _():
        o_ref[...] = _swiglu_tile(h_ref[...], w1_ref, w3_ref, w2_ref).astype(o_ref.dtype)

    @pl.when(jnp.logical_not(valid))
    def _():
        o_ref[...] = jnp.zeros_like(o_ref)


def moe_experts(hs, tile_expert, tile_valid, w1, w3, w2):
    P, D = hs.shape
    nt = P // MOE_TM
    tok = pl.BlockSpec((MOE_TM, D), lambda i, te, tv: (i, 0))
    return pl.pallas_call(
        _moe_kernel,
        grid_spec=pltpu.PrefetchScalarGridSpec(
            num_scalar_prefetch=2,
            grid=(nt,),
            in_specs=[tok,
                      pl.BlockSpec((None, D, D_FF), lambda i, te, tv: (te[i], 0, 0)),
                      pl.BlockSpec((None, D, D_FF), lambda i, te, tv: (te[i], 0, 0)),
                      pl.BlockSpec((None, D_FF, D), lambda i, te, tv: (te[i], 0, 0))],
            out_specs=tok),
        out_shape=jax.ShapeDtypeStruct((P, D), BF16),
        compiler_params=_cparams(("arbitrary",), FF_VMEM_LIMIT),
        name="moe_experts",
    )(tile_expert, tile_valid, hs, w1, w3, w2)


def _combine_kernel(x_ref, mod_ref, ya_ref, yb_ref, rt_ref, *rest):
    o_ref = rest[-1]
    wa = rt_ref[:, 2:3]
    wb = rt_ref[:, 3:4]
    o_ref[...] = x_ref[...] + mod_ref[5:6, :] * (wa * ya_ref[...] + wb * yb_ref[...])


def moe_combine(x1, mod, yab, rt, b0, out_prev, tm=512):
    B, T, D = x1.shape
    gb = yab.shape[1]
    tok = lambda n: pl.BlockSpec((None, tm, n), lambda b, i: (b + b0, i, 0))
    choice = lambda c: pl.BlockSpec((None, None, tm, D), lambda b, i: (c, b, i, 0))
    in_specs = [tok(D), pl.BlockSpec((None, 6, D), lambda b, i: (b + b0, 0, 0)), choice(0), choice(1),
                tok(rt.shape[-1])]
    args = [x1, mod, yab, yab, rt]
    aliases = {}
    if out_prev is not None:
        in_specs.append(pl.BlockSpec(memory_space=pl.ANY))
        args.append(out_prev)
        aliases = {len(args) - 1: 0}
    return pl.pallas_call(
        _combine_kernel,
        grid=(gb, T // tm),
        in_specs=in_specs,
        out_specs=tok(D),
        out_shape=jax.ShapeDtypeStruct((B, T, D), F32),
        input_output_aliases=aliases,
        compiler_params=_cparams(("parallel", "parallel")),
        name="moe_combine",
    )(*args)


MOE_GROUPS = 2


def moe_layer(h2, x1, mod, rt, w1, w3, w2):
    B = x1.shape[0]
    gb = B // MOE_GROUPS
    weights = (w1.astype(BF16), w3.astype(BF16), w2.astype(BF16))
    out = None
    for b0 in range(0, B, gb):
        yab = _moe_group(h2, rt, b0, gb, *weights)
        out = moe_combine(x1, mod, yab, rt, b0, out)
    return out


def _moe_group(h2, rt, b0, gb, w1, w3, w2):
    _, T, D = h2.shape
    n = gb * T
    rt = rt[b0:b0 + gb]
    e_flat = jnp.concatenate([rt[..., 0].reshape(n), rt[..., 1].reshape(n)]).astype(jnp.int32)
    order = jnp.argsort(e_flat, stable=True).astype(jnp.int32)
    rank = jnp.argsort(order).astype(jnp.int32)
    onehot = jax.nn.one_hot(e_flat, N_EXPERTS, dtype=jnp.int32)
    counts = jnp.sum(onehot, axis=0)
    padded = ((counts + MOE_TM - 1) // MOE_TM) * MOE_TM
    pend = jnp.cumsum(padded)
    shift = (pend - padded) - (jnp.cumsum(counts) - counts)
    P = 2 * n + N_EXPERTS * MOE_TM
    nt = P // MOE_TM
    tile_row = jnp.arange(nt, dtype=jnp.int32) * MOE_TM
    tile_expert = jnp.sum((tile_row[:, None] >= pend[None, :]).astype(jnp.int32), axis=1)
    tile_valid = (tile_expert < N_EXPERTS).astype(jnp.int32)
    tile_expert = jnp.minimum(tile_expert, N_EXPERTS - 1)
    slot = rank + jnp.sum(onehot * shift[None, :], axis=1)
    tile_shift = jnp.sum(jax.nn.one_hot(tile_expert, N_EXPERTS, dtype=jnp.int32) * shift[None, :], axis=1)
    row_rank = jnp.arange(P, dtype=jnp.int32) - jnp.repeat(tile_shift, MOE_TM)
    src = order[jnp.clip(row_rank, 0, 2 * n - 1)] % n
    take = lambda a, idx: a.at[idx].get(mode='promise_in_bounds')
    hs = take(h2.reshape(-1, D), src + b0 * T)
    ys = moe_experts(hs, tile_expert, tile_valid, w1, w3, w2)
    return take(ys, slot).reshape(2, gb, T, D)


def _hyena(hz, conv_w, conv_b, spec, bias):
    zc = short_conv(hz, conv_w, conv_b)
    B, L, C = zc.shape
    lmax = FFT_N // 2
    if L < lmax:
        slot = lmax // B
        assert slot >= 2 * L
        zc = jnp.pad(zc, ((0, 0), (0, slot - L), (0, 0))).reshape(1, lmax, C)
    y = gated_long_conv(zc, 0, zc, D_HYENA, bias[0], spec, 0)
    y = gated_long_conv(y, 0, zc, 2 * D_HYENA, bias[1], spec, D_HYENA)
    if L < lmax:
        y = y.reshape(B, lmax // B, D_HYENA)[:, :L]
    return y


def _split_route(outs, mod):
    x1, h2 = outs[0], outs[1]
    return h2, x1, mod, (outs[2] if len(outs) > 2 else None)


def kernel(x, c, ctx, c_ctx, w_ada, b_ada, g_norm1, g_norm2, w_in, w_out, g_q, g_k, rpb, hy_conv_w, hy_conv_b, filt_w1, filt_b1, filt_w2, filt_b2, filt_w3, filt_b3, filt_freq, filt_w_out, hy_bias, lam_re, lam_im, log_step, b_re, b_im, c_re, c_im, d_skip, w_glu, b_glu, ffn_w1, ffn_w3, ffn_w2, router_w, moe_w1, moe_w3, moe_w2):
    B, L, D = x.shape
    Lc = ctx.shape[1]
    c8 = jnp.zeros((8, D), F32).at[:B].set(c).at[B].set(c_ctx)
    mod_all = adaln_mod(c8, w_ada, b_ada)
    for layer in range(DEPTH):
        last = layer == DEPTH - 1
        m = mod_all[layer].reshape(8, 6, D)
        mod_x = m[:B]
        mod_c = jnp.broadcast_to(m[B], (B, 6, D))
        w_l = w_in[layer].astype(BF16)
        wo_l = w_out[layer].astype(BF16)
        filt = (filt_w1[layer], filt_b1[layer], filt_w2[layer], filt_b2[layer],
                filt_w3[layer], filt_b3[layer], filt_freq[layer], filt_w_out[layer])
        ssm = (lam_re[layer], lam_im[layer], log_step[layer], b_re[layer], b_im[layer], c_re[layer], c_im[layer])

        q_c, k_c, v_c, hz_c, u_c = in_proj(ctx, mod_c, g_norm1[layer], w_l, g_q[layer], g_k[layer])
        q, k, v, hz, u = in_proj(x, mod_x, g_norm1[layer], w_l, g_q[layer], g_k[layer])
        att = neighbourhood_attention(q, k, v, k_c, v_c, rpb[layer])
        spec = filter_spectrum(hyena_filter_taps(L, *filt))
        hy = _hyena(hz, hy_conv_w[layer], hy_conv_b[layer], spec, hy_bias[layer])
        un, yf, yb = s5_scan(u, u_c, *ssm)

        tail = (d_skip[layer], w_glu[layer], b_glu[layer], wo_l, g_norm2[layer])
        i = layer // 2
        if layer % 2 == 0:
            ffw = (ffn_w1[i], ffn_w3[i], ffn_w2[i])
            mixer = lambda h2, x1, mod, rt: ffn_dense(h2, x1, mod, *ffw)
            rw = None
        else:
            mow = (moe_w1[i], moe_w3[i], moe_w2[i])
            mixer = lambda h2, x1, mod, rt: moe_layer(h2, x1, mod, rt, *mow)
            rw = router_w[i]
        x = mixer(*_split_route(out_proj(x, mod_x, att, hy, yf, yb, un, 0, *tail, router_w=rw), mod_x))

        if not last:
            att_c = context_attention(q_c, k_c, v_c)
            spec_c = filter_spectrum(hyena_filter_taps(Lc, *filt))
            hy_c = _hyena(hz_c, hy_conv_w[layer], hy_conv_b[layer], spec_c, hy_bias[layer])
            ctx = mixer(*_split_route(out_proj(ctx, mod_c, att_c, hy_c, yf, yb, un, L, *tail, router_w=rw), mod_c))
    return x
```

```python
import functools
import math

import numpy as np
import jax
import jax.numpy as jnp
from jax import lax
from jax.experimental import pallas as pl
from jax.experimental.pallas import tpu as pltpu

F32 = jnp.float32
BF16 = jnp.bfloat16
HIGHEST = lax.Precision.HIGHEST

D_MODEL = 1024
DEPTH = 2
GRID_W = 64
ATT_HEAD_DIM = 64
ATT_HEADS = 8
D_ATT = 512
WIN_ROWS = 8
WIN_COLS = 16
D_HYENA = 256
HYENA_ORDER = 2
FILTER_EMB = 33
FILTER_BANDS = 16
FILTER_WIDTH = 64
MIN_DECAY = math.log(1e-2) / 1.5
MAX_DECAY = math.log(1e-2) / 0.3
D_SSM = 256
SSM_GROUP_WIDTH = 16
SSM_GROUPS = 16
SSM_STATE = 64
COL_K = D_ATT
COL_V = 2 * D_ATT
COL_HY = 3 * D_ATT
COL_SSM = COL_HY + 3 * D_HYENA
D_IN = COL_SSM + D_SSM
D_FF = 2816
N_EXPERTS = 8
EPS = 1e-6
NEG = -1e30

VMEM_LIMIT = 56 * 1024 * 1024


def _cparams(sem, vmem=None):
    return pltpu.CompilerParams(dimension_semantics=sem, vmem_limit_bytes=vmem)


def _dot(a, b):
    return jnp.dot(a, b, preferred_element_type=F32)


def _dot_nt(a, b):
    return lax.dot_general(a, b, (((1,), (1,)), ((), ())), preferred_element_type=F32)


def _mod_kernel(c_ref, w_ref, b_ref, o_ref):
    c = c_ref[...]
    s = c * jax.nn.sigmoid(c)
    o_ref[...] = jnp.dot(s, w_ref[...], precision=HIGHEST, preferred_element_type=F32) + b_ref[...]


def adaln_mod(c8, w_ada, b_ada, tn=512):
    depth, d, n = w_ada.shape
    return pl.pallas_call(
        _mod_kernel,
        grid=(depth, n // tn),
        in_specs=[pl.BlockSpec((8, d), lambda l, j: (0, 0)),
                  pl.BlockSpec((None, d, tn), lambda l, j: (l, 0, j)),
                  pl.BlockSpec((None, 1, tn), lambda l, j: (l, 0, j))],
        out_specs=pl.BlockSpec((None, 8, tn), lambda l, j: (l, 0, j)),
        out_shape=jax.ShapeDtypeStruct((depth, 8, n), F32),
        compiler_params=_cparams(("parallel", "parallel")),
        name="adaln_mod",
    )(c8, w_ada, b_ada.reshape(depth, 1, n))


def _rms_mod(x, g, shift, scale):
    ms = jnp.mean(x * x, axis=-1, keepdims=True)
    h = x * lax.rsqrt(ms + EPS) * g
    return h * (1.0 + scale) + shift


def _head_norm(z, a, g):
    zz = (z * z).astype(BF16)
    m = jnp.concatenate([_dot(zz[:, c:c + ATT_LANES], a) for c in range(0, z.shape[1], ATT_LANES)], axis=1)
    return z * lax.rsqrt(m + EPS) * g


TOK_TB = 128


def _inproj_kernel(x_ref, mod_ref, g_ref, w_ref, a_ref, gq_ref, gk_ref,
                   q_ref, k_ref, v_ref, hz_ref, u_ref):
    B, tb, _ = x_ref.shape
    h = jnp.concatenate([_rms_mod(x_ref[b], g_ref[...], mod_ref[b, 0:1, :], mod_ref[b, 1:2, :]).astype(BF16)
                         for b in range(B)], axis=0)
    a = a_ref[...]
    q = (_head_norm(_dot(h, w_ref[:, 0:COL_K]), a, gq_ref[...]) * (ATT_HEAD_DIM ** -0.5)).astype(BF16)
    k = _head_norm(_dot(h, w_ref[:, COL_K:COL_V]), a, gk_ref[...]).astype(BF16)
    v = _dot(h, w_ref[:, COL_V:COL_HY]).astype(BF16)
    hz = _dot(h, w_ref[:, COL_HY:COL_SSM])
    u = _dot(h, w_ref[:, COL_SSM:D_IN])
    for b in range(B):
        rows = slice(b * tb, (b + 1) * tb)
        q_ref[b] = q[rows]
        k_ref[b] = k[rows]
        v_ref[b] = v[rows]
        hz_ref[b] = hz[rows]
        for s in range(D_SSM // 128):
            u_ref[s, pl.ds(b, tb, stride=B), :] = u[rows, s * 128:(s + 1) * 128]


def in_proj(x, mod, g1, w_in_bf, gq, gk):
    B, T, D = x.shape
    tb = TOK_TB
    head_avg = jnp.asarray(np.kron(np.eye(ATT_LANES // ATT_HEAD_DIM),
                                   np.full((ATT_HEAD_DIM, ATT_HEAD_DIM), 1.0 / ATT_HEAD_DIM)), BF16)
    tok = lambda n: pl.BlockSpec((B, tb, n), lambda i: (0, i, 0))
    const = lambda shape: pl.BlockSpec(shape, lambda i: (0,) * len(shape))
    return pl.pallas_call(
        _inproj_kernel,
        grid=(T // tb,),
        in_specs=[tok(D), const((B, 6, D)), const((1, D)),
                  const((D, D_IN)), const((ATT_LANES, ATT_LANES)), const((1, D_ATT)), const((1, D_ATT))],
        out_specs=[tok(D_ATT), tok(D_ATT), tok(D_ATT), tok(3 * D_HYENA),
                   pl.BlockSpec((D_SSM // 128, tb * B, 128), lambda i: (0, i, 0))],
        out_shape=[jax.ShapeDtypeStruct((B, T, D_ATT), BF16)] * 3
                  + [jax.ShapeDtypeStruct((B, T, 3 * D_HYENA), F32),
                     jax.ShapeDtypeStruct((D_SSM // 128, T * B, 128), F32)],
        compiler_params=_cparams(("parallel",), VMEM_LIMIT),
        name="in_proj",
    )(x, mod, g1.reshape(1, D), w_in_bf, head_avg,
      jnp.tile(gq, ATT_HEADS).reshape(1, D_ATT), jnp.tile(gk, ATT_HEADS).reshape(1, D_ATT))


ATT_R = 4
ATT_KROWS = ATT_R + WIN_ROWS - 1
ATT_LANES = 256


def _softmax_heads(q, parts, bias_ref, o_ref):
    nh = ATT_LANES // ATT_HEAD_DIM
    lane = lax.broadcasted_iota(jnp.int32, (1, ATT_LANES), 1)
    head = lambda h: (lane >= ATT_HEAD_DIM * h) & (lane < ATT_HEAD_DIM * (h + 1))
    acc = jnp.zeros((q.shape[0], ATT_LANES), F32)
    for h in range(nh):
        hm = head(h)
        qh = jnp.where(hm, q, jnp.zeros_like(q))
        ss = [_dot_nt(qh, kk).astype(BF16) for kk, _ in parts]
        if bias_ref is not None:
            ss[0] = ss[0] + bias_ref[h]
        m = ss[0].max(axis=-1, keepdims=True)
        for s in ss[1:]:
            m = jnp.maximum(m, s.max(axis=-1, keepdims=True))
        om = head((h + 1) % nh)
        o = None
        for s, (_, vv) in zip(ss, parts):
            part = _dot(jnp.exp(s - m), jnp.where(om, jnp.ones_like(vv), vv))
            o = part if o is None else o + part
        l = pltpu.roll(o, ATT_LANES - ATT_HEAD_DIM, axis=1)
        acc = jnp.where(hm, o / l, acc)
    o_ref[...] = acc.astype(o_ref.dtype)


ATT_SUB = 4


def _nattn_kernel(q_ref, k_ref, v_ref, kc_ref, vc_ref, *rest, rows):
    bias_refs, o_ref = rest[:ATT_SUB], rest[ATT_SUB]
    tq = ATT_R * GRID_W
    ctx = (kc_ref[...], vc_ref[...])
    for sub in range(ATT_SUB):
        blk = pl.program_id(2) * ATT_SUB + sub
        ks = jnp.clip(blk * ATT_R - WIN_ROWS // 2, 0, rows - ATT_KROWS)
        start = pl.multiple_of(ks * GRID_W, GRID_W)
        kl = k_ref[pl.ds(start, ATT_KROWS * GRID_W), :]
        vl = v_ref[pl.ds(start, ATT_KROWS * GRID_W), :]
        qrows = slice(sub * tq, (sub + 1) * tq)
        _softmax_heads(q_ref[qrows, :], [(kl, vl), ctx], bias_refs[sub], o_ref.at[qrows, :])


def _attn_bias_tables(rpb, rows):
    nblk = rows // ATT_R
    qc = np.arange(GRID_W)
    c0 = np.clip(qc - WIN_COLS // 2, 0, GRID_W - WIN_COLS)
    kc = np.arange(GRID_W)
    col_ok = (kc[None, :] >= c0[:, None]) & (kc[None, :] < c0[:, None] + WIN_COLS)
    col_off = kc[None, :] - qc[:, None] + WIN_COLS - 1
    col_sel = (col_off[:, :, None] == np.arange(2 * WIN_COLS - 1)) & col_ok[:, :, None]
    band = jnp.einsum('hij,qkj->hqik', rpb, jnp.asarray(col_sel, F32), precision=HIGHEST)
    band = jnp.where(jnp.asarray(col_ok)[None, :, None, :], band, NEG)
    band = band.reshape(ATT_HEADS, GRID_W, (2 * WIN_ROWS - 1) * GRID_W).astype(BF16)
    neg = lambda nrows: jnp.full((ATT_HEADS, GRID_W, nrows * GRID_W), NEG, BF16)
    tabs = []
    for blk in (0, 1, nblk - 1):
        ks = int(np.clip(blk * ATT_R - WIN_ROWS // 2, 0, rows - ATT_KROWS))
        qrows = []
        for r in range(blk * ATT_R, (blk + 1) * ATT_R):
            r0 = int(np.clip(r - WIN_ROWS // 2, 0, rows - WIN_ROWS))
            first = r0 - r + WIN_ROWS - 1
            win = band[:, :, first * GRID_W:(first + WIN_ROWS) * GRID_W]
            qrows.append(jnp.concatenate([neg(r0 - ks), win, neg(ATT_KROWS - WIN_ROWS - (r0 - ks))], axis=-1))
        tabs.append(jnp.stack(qrows, axis=1).reshape(ATT_HEADS, ATT_R * GRID_W, ATT_KROWS * GRID_W))
    return jnp.stack(tabs)


def neighbourhood_attention(q, k, v, kc, vc, rpb):
    B, L, _ = q.shape
    Lc = kc.shape[1]
    rows = L // GRID_W
    nblk = rows // ATT_R
    tq = ATT_R * GRID_W
    nkl = ATT_KROWS * GRID_W
    bias = _attn_bias_tables(rpb, rows)
    hpb = ATT_LANES // ATT_HEAD_DIM
    variant = lambda blk: jnp.where(blk == 0, 0, jnp.where(blk == nblk - 1, 2, 1))
    bias_spec = lambda sub: pl.BlockSpec((None, hpb, tq, nkl),
                                         lambda b, g, j: (variant(j * ATT_SUB + sub), g, 0, 0))
    return pl.pallas_call(
        functools.partial(_nattn_kernel, rows=rows),
        grid=(B, D_ATT // ATT_LANES, nblk // ATT_SUB),
        in_specs=[pl.BlockSpec((None, ATT_SUB * tq, ATT_LANES), lambda b, g, j: (b, j, g)),
                  pl.BlockSpec((None, L, ATT_LANES), lambda b, g, j: (b, 0, g)),
                  pl.BlockSpec((None, L, ATT_LANES), lambda b, g, j: (b, 0, g)),
                  pl.BlockSpec((None, Lc, ATT_LANES), lambda b, g, j: (b, 0, g)),
                  pl.BlockSpec((None, Lc, ATT_LANES), lambda b, g, j: (b, 0, g))]
                 + [bias_spec(sub) for sub in range(ATT_SUB)],
        out_specs=pl.BlockSpec((None, ATT_SUB * tq, ATT_LANES), lambda b, g, j: (b, j, g)),
        out_shape=jax.ShapeDtypeStruct((B, L, D_ATT), BF16),
        compiler_params=_cparams(("parallel", "parallel", "arbitrary"), VMEM_LIMIT),
        name="nattn",
    )(q, k, v, kc, vc, *([bias] * ATT_SUB))


def _cattn_kernel(q_ref, k_ref, v_ref, o_ref):
    _softmax_heads(q_ref[...], [(k_ref[...], v_ref[...])], None, o_ref)


def context_attention(q, k, v):
    B, Lc, _ = q.shape
    spec = pl.BlockSpec((None, Lc, ATT_LANES), lambda b, g: (b, 0, g))
    return pl.pallas_call(
        _cattn_kernel,
        grid=(B, D_ATT // ATT_LANES),
        in_specs=[spec, spec, spec],
        out_specs=spec,
        out_shape=jax.ShapeDtypeStruct((B, Lc, D_ATT), BF16),
        compiler_params=_cparams(("parallel", "parallel")),
        name="cattn",
    )(q, k, v)


FFT_N = 16384
FFT_R = 128
FFT_K1 = FFT_R // 2 + 1
FFT_K1P = 66
FFT_PITCH = 2 * FFT_K1P


def _filter_kernel(w1_ref, wt_ref, b1_ref, w2_ref, b2_ref, w3_ref, b3_ref, fr_ref, wo_ref, o_ref, *, lf, tile):
    i = pl.program_id(0)
    half = FFT_N // 2
    row0 = i * tile
    first_pos = jnp.where(row0 < half, row0, FFT_N - row0 - tile)

    @pl.when(first_pos >= lf)
    def _():
        o_ref[...] = jnp.zeros_like(o_ref)

    @pl.when(first_pos < lf)
    def _():
        rows = tile // FILTER_PACK

        def position(width):
            r = lax.broadcasted_iota(jnp.int32, (rows, FILTER_PACK * width), 0)
            g = lax.broadcasted_iota(jnp.int32, (rows, FILTER_PACK * width), 1) // width
            j = row0 + r + g * rows
            return jnp.where(j < half, j, FFT_N - 1 - j)

        nfeat = 2 * FILTER_BANDS
        feat = lax.broadcasted_iota(jnp.int32, (1, FILTER_PACK * nfeat), 1) % nfeat
        f = 1e-4 + (feat % FILTER_BANDS).astype(F32) * ((FILTER_BANDS - 1 - 1e-4) / (FILTER_BANDS - 1))
        w = (2.0 * math.pi / lf) * position(nfeat).astype(F32)
        z = jnp.cos(f * w + jnp.where(feat < FILTER_BANDS, 0.0, 0.5 * math.pi))
        t_h = position(FILTER_WIDTH).astype(F32) / (lf - 1.0)
        fr = fr_ref[...]
        hdot = lambda a, b: jnp.dot(a, b, precision=HIGHEST, preferred_element_type=F32)
        h = jnp.sin(fr * (hdot(z, w1_ref[...]) + t_h * wt_ref[...] + b1_ref[...]))
        h = jnp.sin(fr * (hdot(h, w2_ref[...]) + b2_ref[...]))
        h = jnp.sin(fr * (hdot(h, w3_ref[...]) + b3_ref[...]))
        y = hdot(h, wo_ref[...])
        nout = HYENA_ORDER * D_HYENA
        pos_o = position(nout)
        ch = lax.broadcasted_iota(jnp.int32, (1, FILTER_PACK * nout), 1) % D_HYENA
        delta = jnp.abs(MIN_DECAY + ch.astype(F32) * ((MAX_DECAY - MIN_DECAY) / (D_HYENA - 1)))
        y = jnp.where(pos_o < lf, y * jnp.exp(-(pos_o.astype(F32) / (lf - 1.0)) * delta), 0.0)
        for g in range(FILTER_PACK):
            o_ref[g * rows:(g + 1) * rows, :] = y[:, g * nout:(g + 1) * nout]


FILTER_PACK = 4


def hyena_filter_taps(lf, w1, b1, w2, b2, w3, b3, freq, w_out, tile=1024):
    P, W = FILTER_PACK, FILTER_WIDTH
    nout = HYENA_ORDER * D_HYENA
    bd = lambda m: jnp.kron(jnp.eye(P, dtype=F32), m.astype(F32))
    wide = lambda a: jnp.tile(a.astype(F32), P).reshape(1, P * W)
    wo = w_out.reshape(W, HYENA_ORDER, 2, D_HYENA).transpose(2, 0, 1, 3).reshape(2, W, nout)
    wo = jnp.stack([bd(wo[0]), bd(wo[1])])
    nt = FFT_N // tile
    const = lambda shape: pl.BlockSpec(shape, lambda i: (0,) * len(shape))
    return pl.pallas_call(
        functools.partial(_filter_kernel, lf=lf, tile=tile),
        grid=(nt,),
        in_specs=[const((P * (FILTER_EMB - 1), P * W)), const((1, P * W)), const((1, P * W)),
                  const((P * W, P * W)), const((1, P * W)), const((P * W, P * W)), const((1, P * W)),
                  const((1, P * W)),
                  pl.BlockSpec((None, P * W, P * nout), lambda i: (i // (nt // 2), 0, 0))],
        out_specs=pl.BlockSpec((tile, nout), lambda i: (i, 0)),
        out_shape=jax.ShapeDtypeStruct((FFT_N, nout), F32),
        compiler_params=_cparams(("parallel",)),
        name="hyena_filter",
    )(bd(w1[1:]), wide(w1[0]), wide(b1), bd(w2), wide(b2), bd(w3), wide(b3), wide(freq), wo)


def _shortconv_kernel(z_ref, w_ref, b_ref, o_ref, zp_ref, *, L):
    C = z_ref.shape[-1]
    zp_ref[0:8, :] = jnp.zeros((8, C), F32)
    zp_ref[L + 8:L + 16, :] = jnp.zeros((8, C), F32)
    zp_ref[8:L + 8, :] = z_ref[...]
    ch = min(L, 512)
    for c in range(L // ch):
        s = c * ch
        o_ref[s:s + ch, :] = (b_ref[...] + zp_ref[s + 7:s + 7 + ch, :] * w_ref[0:1, :]
                              + zp_ref[s + 8:s + 8 + ch, :] * w_ref[1:2, :]
                              + zp_ref[s + 9:s + 9 + ch, :] * w_ref[2:3, :])


def short_conv(z, w, b, cb=128):
    B, L, C = z.shape
    return pl.pallas_call(
        functools.partial(_shortconv_kernel, L=L),
        grid=(B, C // cb),
        in_specs=[pl.BlockSpec((None, L, cb), lambda b_, c: (b_, 0, c)),
                  pl.BlockSpec((3, cb), lambda b_, c: (0, c)),
                  pl.BlockSpec((1, cb), lambda b_, c: (0, c))],
        out_specs=pl.BlockSpec((None, L, cb), lambda b_, c: (b_, 0, c)),
        out_shape=jax.ShapeDtypeStruct((B, L, C), F32),
        scratch_shapes=[pltpu.VMEM((L + 16, cb), F32)],
        compiler_params=_cparams(("parallel", "parallel")),
        name="short_conv",
    )(z, w, b.reshape(1, C))


@functools.lru_cache(maxsize=None)
def _fft_tables(n1_in):
    n2 = np.arange(FFT_R)[:, None, None]
    k1 = np.arange(FFT_K1)[None, :, None]
    n1 = np.arange(n1_in)[None, None, :]
    th = 2.0 * np.pi * ((k1 * (FFT_R * n1 + n2)) % FFT_N) / FFT_N
    ga = np.zeros((FFT_R, 2 * FFT_K1P, n1_in))
    ga[:, :FFT_K1] = np.cos(th)
    ga[:, FFT_K1P:FFT_K1P + FFT_K1] = -np.sin(th)
    wk = np.where((np.arange(FFT_K1) == 0) | (np.arange(FFT_K1) == FFT_R // 2), 1.0, 2.0)[None, None, :]
    thd = th.transpose(0, 2, 1)
    gd = np.zeros((FFT_R, n1_in, 2 * FFT_K1P))
    gd[:, :, :FFT_K1] = wk * np.cos(thd)
    gd[:, :, FFT_K1P:FFT_K1P + FFT_K1] = -wk * np.sin(thd)
    a = np.arange(FFT_R)
    ph = 2.0 * np.pi * ((a[:, None] * a[None, :]) % FFT_R) / FFT_R
    c, s = np.cos(ph), np.sin(ph)
    fb = np.block([[c, s], [-s, c]])
    fc = np.block([[c, -s], [s, c]])
    return ga, gd, fb, fc


FFT_KG = 13
FFT_UNROLL = 16


def _spectrum_slabs(ya_ref, fb, k0):
    cols = []
    for g in range(FFT_KG):
        re = ya_ref[pl.ds(k0 + g, FFT_R, stride=FFT_PITCH), :]
        im = ya_ref[pl.ds(FFT_K1P + k0 + g, FFT_R, stride=FFT_PITCH), :]
        cols.append(jnp.concatenate([re, im], axis=0).astype(BF16))
    return _dot(fb, jnp.concatenate(cols, axis=1))


def _stage_a(x_ref, x_pitch, ga_ref, ya_ref, n1_in):
    def body(n2, carry):
        xs = x_ref[pl.ds(n2, n1_in, stride=x_pitch), :].astype(BF16)
        ya_ref[pl.ds(n2 * FFT_PITCH, FFT_PITCH), :] = _dot(ga_ref[n2], xs)
        return carry
    lax.fori_loop(0, FFT_R, body, 0, unroll=FFT_UNROLL)


def _spectrum_kernel(x_ref, ga_ref, fb_ref, o_ref, ya_ref):
    _stage_a(x_ref, FFT_R, ga_ref, ya_ref, FFT_R)
    fb = fb_ref[...]
    cb = ya_ref.shape[1]

    def body(kg, carry):
        x = _spectrum_slabs(ya_ref, fb, kg * FFT_KG)
        for g in range(FFT_KG):
            o_ref[kg * FFT_KG + g] = x[:, g * cb:(g + 1) * cb]
        return carry
    lax.fori_loop(0, FFT_K1 // FFT_KG, body, 0)


def filter_spectrum(taps, cb=128):
    C = taps.shape[1]
    ga, _, fb, _ = _fft_tables(FFT_R)
    return pl.pallas_call(
        _spectrum_kernel,
        grid=(C // cb,),
        in_specs=[pl.BlockSpec((FFT_N, cb), lambda c: (0, c)),
                  pl.BlockSpec((FFT_R, FFT_PITCH, FFT_R), lambda c: (0, 0, 0)),
                  pl.BlockSpec((2 * FFT_R, 2 * FFT_R), lambda c: (0, 0))],
        out_specs=pl.BlockSpec((FFT_K1, 2 * FFT_R, cb), lambda c: (0, 0, c)),
        out_shape=jax.ShapeDtypeStruct((FFT_K1, 2 * FFT_R, C), F32),
        scratch_shapes=[pltpu.VMEM((FFT_R * FFT_PITCH, cb), F32)],
        compiler_params=_cparams(("parallel",), VMEM_LIMIT),
        name="filter_spectrum",
    )(taps, _bf16_table(ga), _bf16_table(fb))


def _bf16_table(t):
    return jnp.asarray(t, F32).astype(BF16)


def _longconv_kernel(u_ref, gate_ref, bias_ref, spec_ref, ga_ref, gd_ref, fb_ref, fc_ref, o_ref, ya_ref, xp_ref):
    n1_in = FFT_R // 2

    def repitch(n1, carry):
        xp_ref[pl.ds(n1 * FFT_PITCH, FFT_R), :] = u_ref[pl.ds(pl.multiple_of(n1 * FFT_R, FFT_R), FFT_R), :]
        return carry
    lax.fori_loop(0, n1_in, repitch, 0, unroll=FFT_UNROLL)

    _stage_a(xp_ref, FFT_PITCH, ga_ref, ya_ref, n1_in)
    fb = fb_ref[...]
    fc = fc_ref[...]

    cb = ya_ref.shape[1]

    def mid(kg, carry):
        k0 = kg * FFT_KG
        x = _spectrum_slabs(ya_ref, fb, k0)
        s = jnp.concatenate([spec_ref[k0 + g] for g in range(FFT_KG)], axis=1)
        xr, xi = x[:FFT_R], x[FFT_R:]
        sr, si = s[:FFT_R], s[FFT_R:]
        z = jnp.concatenate([xr * sr - xi * si, xr * si + xi * sr], axis=0).astype(BF16)
        v = _dot(fc, z)
        for g in range(FFT_KG):
            ya_ref[pl.ds(k0 + g, FFT_R, stride=FFT_PITCH), :] = v[:FFT_R, g * cb:(g + 1) * cb]
            ya_ref[pl.ds(FFT_K1P + k0 + g, FFT_R, stride=FFT_PITCH), :] = v[FFT_R:, g * cb:(g + 1) * cb]
        return carry
    lax.fori_loop(0, FFT_K1 // FFT_KG, mid, 0)

    def last(n2, carry):
        slab = ya_ref[pl.ds(n2 * FFT_PITCH, FFT_PITCH), :].astype(BF16)
        xp_ref[pl.ds(n2, n1_in, stride=FFT_PITCH), :] = _dot(gd_ref[n2], slab)
        return carry
    lax.fori_loop(0, FFT_R, last, 0, unroll=FFT_UNROLL)

    bias = bias_ref[...]

    def gate(n1, carry):
        rows = pl.ds(pl.multiple_of(n1 * FFT_R, FFT_R), FFT_R)
        y = xp_ref[pl.ds(n1 * FFT_PITCH, FFT_R), :] * (1.0 / FFT_N)
        o_ref[rows, :] = gate_ref[rows, :] * (y + u_ref[rows, :] * bias)
        return carry
    lax.fori_loop(0, n1_in, gate, 0, unroll=FFT_UNROLL)


def gated_long_conv(u, u_col, gate, gate_col, bias, spec, spec_col, C=D_HYENA, cb=128):
    B, L, _ = u.shape
    n1_in = FFT_R // 2
    ga, gd, fb, fc = _fft_tables(n1_in)
    once = pl.Buffered(1)
    seq = lambda col: pl.BlockSpec((None, L, cb), lambda c, b: (b, 0, c + col // cb))
    return pl.pallas_call(
        _longconv_kernel,
        grid=(C // cb, B),
        in_specs=[seq(u_col), seq(gate_col),
                  pl.BlockSpec((1, cb), lambda c, b: (0, c)),
                  pl.BlockSpec((FFT_K1, 2 * FFT_R, cb), lambda c, b: (0, 0, c + spec_col // cb), pipeline_mode=once),
                  pl.BlockSpec((FFT_R, FFT_PITCH, n1_in), lambda c, b: (0, 0, 0), pipeline_mode=once),
                  pl.BlockSpec((FFT_R, n1_in, FFT_PITCH), lambda c, b: (0, 0, 0), pipeline_mode=once),
                  pl.BlockSpec((2 * FFT_R, 2 * FFT_R), lambda c, b: (0, 0)),
                  pl.BlockSpec((2 * FFT_R, 2 * FFT_R), lambda c, b: (0, 0))],
        out_specs=seq(0),
        out_shape=jax.ShapeDtypeStruct((B, L, C), F32),
        scratch_shapes=[pltpu.VMEM((FFT_R * FFT_PITCH, cb), F32), pltpu.VMEM((n1_in * FFT_PITCH, cb), F32)],
        compiler_params=_cparams(("parallel", "arbitrary"), VMEM_LIMIT),
        name="long_conv",
    )(u, gate, bias.reshape(1, C), spec, _bf16_table(ga), _bf16_table(gd), _bf16_table(fb), _bf16_table(fc))


SSM_N = SSM_GROUPS * SSM_STATE
S5_SEQS = 8
S5_CHUNK = 256


def _s5_kernel(uf_ref, ub_ref, bf_ref, bb_ref, lam_ref, cf_ref, cb_ref, yf_ref, yb_ref,
               buf_ref, bub_ref, xsf_ref, xsb_ref, st_ref):
    half = S5_SEQS // 2
    tiles = S5_CHUNK // 2

    @pl.when(pl.program_id(0) == 0)
    def _():
        st_ref[...] = jnp.zeros_like(st_ref)

    slabs = lambda ref: jnp.concatenate([ref[s] for s in range(ref.shape[0])], axis=1).astype(BF16)
    buf_ref[...] = _dot(slabs(uf_ref), bf_ref[...])
    bub_ref[...] = _dot(slabs(ub_ref), bb_ref[...])
    lo = lax.broadcasted_iota(jnp.int32, (S5_SEQS, 1), 0) < half

    def advance(x, lam, b):
        xr, xi = x
        lr, li = lam[:, 0:SSM_N], lam[:, SSM_N:2 * SSM_N]
        return lr * xr - li * xi + b[:, 0:SSM_N], lr * xi + li * xr + b[:, SSM_N:2 * SSM_N]

    def swap(x):
        return pltpu.roll(x[0], half, axis=0), pltpu.roll(x[1], half, axis=0)

    def step(t4, x):
        f_tiles, b_tiles = [], []
        for p in range(2):
            tf = buf_ref[pl.ds(pl.multiple_of((2 * t4 + p) * S5_SEQS, S5_SEQS), S5_SEQS), :]
            tb = bub_ref[pl.ds(pl.multiple_of((tiles - 1 - 2 * t4 - p) * S5_SEQS, S5_SEQS), S5_SEQS), :]
            xe = advance(x, lam_ref[0], jnp.where(lo, tf, tb))
            xo = advance(swap(xe), lam_ref[1], jnp.where(lo, tb, tf))
            x = swap(xo)
            xe = jnp.concatenate(xe, axis=1)
            xo = jnp.concatenate(xo, axis=1)
            f_tiles.append(jnp.where(lo, xe, xo))
            b_tiles.append(jnp.where(lo, xo, xe))
        xsf_ref[pl.ds(pl.multiple_of(t4 * 2 * S5_SEQS, 2 * S5_SEQS), 2 * S5_SEQS), :] = (
            jnp.concatenate(f_tiles, axis=0).astype(BF16))
        xsb_ref[pl.ds(pl.multiple_of((tiles - 2 - 2 * t4) * S5_SEQS, 2 * S5_SEQS), 2 * S5_SEQS), :] = (
            jnp.concatenate(b_tiles[::-1], axis=0).astype(BF16))
        return x

    xr, xi = lax.fori_loop(0, S5_CHUNK // 4, step, (st_ref[:, 0:SSM_N], st_ref[:, SSM_N:2 * SSM_N]))
    st_ref[:, 0:SSM_N] = xr
    st_ref[:, SSM_N:2 * SSM_N] = xi
    yf = _dot(xsf_ref[...], cf_ref[...])
    yb = _dot(xsb_ref[...], cb_ref[...])
    for s in range(yf_ref.shape[0]):
        yf_ref[s] = yf[:, s * 128:(s + 1) * 128]
        yb_ref[s] = yb[:, s * 128:(s + 1) * 128]


def _s5_matrices(lam_re, lam_im, log_step, b_re, b_im, c_re, c_im):
    G, P, Hs = SSM_GROUPS, SSM_STATE, SSM_GROUP_WIDTH
    dt = jnp.exp(log_step)[:, :, None]
    mag = jnp.exp(lam_re * dt)
    ar, ai = mag * jnp.cos(lam_im * dt), mag * jnp.sin(lam_im * dt)
    er, ei = ar - 1.0, ai
    den = lam_re * lam_re + lam_im * lam_im
    cr, ci = (er * lam_re + ei * lam_im) / den, (ei * lam_re - er * lam_im) / den
    bbr = cr[..., None] * b_re - ci[..., None] * b_im
    bbi = cr[..., None] * b_im + ci[..., None] * b_re
    eye = jnp.eye(G, dtype=F32)
    blk_in = lambda m: jnp.einsum('gq,qph->ghqp', eye, m).reshape(G * Hs, G * P)
    bmat = [jnp.concatenate([blk_in(bbr[d]), blk_in(bbi[d])], axis=1).astype(BF16) for d in range(2)]
    blk_out = lambda m: jnp.einsum('gq,qhp->gpqh', eye, m).reshape(G * P, G * Hs)
    cmat = [jnp.concatenate([blk_out(c_re[d]), -blk_out(c_im[d])], axis=0).astype(BF16) for d in range(2)]
    lam = jnp.concatenate([ar.reshape(2, 1, G * P), ai.reshape(2, 1, G * P)], axis=2)
    lam = jnp.broadcast_to(lam, (2, S5_SEQS // 2, 2 * G * P))
    lam8 = jnp.stack([lam.reshape(S5_SEQS, 2 * G * P), lam[::-1].reshape(S5_SEQS, 2 * G * P)])
    return bmat[0], bmat[1], lam8, cmat[0], cmat[1]


def s5_scan(u_x, u_c, lam_re, lam_im, log_step, b_re, b_im, c_re, c_im):
    S = u_x.shape[0]
    half = S5_SEQS // 2
    L, Lc = u_x.shape[1] // half, u_c.shape[1] // half
    assert L % S5_CHUNK == 0 and Lc % S5_CHUNK == 0
    bf, bb, lam8, cf, cb = _s5_matrices(lam_re, lam_im, log_step, b_re, b_im, c_re, c_im)
    un = jnp.concatenate([u_x, u_c], axis=1)
    rows = S5_CHUNK * half
    nx, nc = L // S5_CHUNK, Lc // S5_CHUNK
    n = nx + nc
    fwd = pl.BlockSpec((S, rows, 128), lambda i: (0, jnp.where(i < nc, nx + i, i - nc), 0))
    bwd = pl.BlockSpec((S, rows, 128), lambda i: (0, n - 1 - i, 0))
    const = lambda shape: pl.BlockSpec(shape, lambda i: (0,) * len(shape))
    yf, yb = pl.pallas_call(
        _s5_kernel,
        grid=(n,),
        in_specs=[fwd, bwd, const((D_SSM, 2 * SSM_N)), const((D_SSM, 2 * SSM_N)),
                  const((2, S5_SEQS, 2 * SSM_N)), const((2 * SSM_N, D_SSM)), const((2 * SSM_N, D_SSM))],
        out_specs=[fwd, bwd],
        out_shape=[jax.ShapeDtypeStruct(un.shape, F32)] * 2,
        scratch_shapes=[pltpu.VMEM((rows, 2 * SSM_N), F32), pltpu.VMEM((rows, 2 * SSM_N), F32),
                        pltpu.VMEM((rows, 2 * SSM_N), BF16), pltpu.VMEM((rows, 2 * SSM_N), BF16),
                        pltpu.VMEM((S5_SEQS, 2 * SSM_N), F32)],
        compiler_params=_cparams(("arbitrary",), VMEM_LIMIT),
        name="s5_scan",
    )(un, un, bf, bb, lam8, cf, cb)
    return un, yf, yb


def _gelu_tanh(y):
    return 0.5 * y * (1.0 + jnp.tanh(math.sqrt(2.0 / math.pi) * (y + 0.044715 * (y * y * y))))


def _outproj_kernel(*refs, route):
    (x_ref, mod_ref, att_ref, hy_ref, yf_ref, yb_ref, u_ref, dsk_ref, wg_ref, bg_ref, wo_ref, g2_ref) = refs[:12]
    if route:
        wr_ref, x1_ref, h2_ref, rt_ref = refs[12:]
    else:
        x1_ref, h2_ref = refs[12:]
    B, tb, D = x_ref.shape
    rows = lambda a: a.reshape(B * tb, a.shape[-1])

    def scan_rows(ref):
        return jnp.concatenate(
            [jnp.concatenate([ref[s, pl.ds(b, tb, stride=B), :] for s in range(ref.shape[0])], axis=1)
             for b in range(B)], axis=0)

    y = scan_rows(yf_ref) + scan_rows(yb_ref) + scan_rows(u_ref) * dsk_ref[...]
    y = _gelu_tanh(y)
    ss = y * jax.nn.sigmoid(_dot(y.astype(BF16), wg_ref[...]) + bg_ref[...])
    mix = (_dot(rows(att_ref[...]), wo_ref[0:D_ATT, :])
           + _dot(rows(hy_ref[...]).astype(BF16), wo_ref[D_ATT:D_ATT + D_HYENA, :])
           + _dot(ss.astype(BF16), wo_ref[D_ATT + D_HYENA:, :]))
    h2s = []
    for b in range(B):
        x1 = x_ref[b] + mod_ref[b, 2:3, :] * mix[b * tb:(b + 1) * tb]
        x1_ref[b] = x1
        h2s.append(_rms_mod(x1, g2_ref[...], mod_ref[b, 3:4, :], mod_ref[b, 4:5, :]))
        h2_ref[b] = h2s[-1].astype(h2_ref.dtype)
    if route:
        h2 = jnp.concatenate(h2s, axis=0)
        h_hi = h2.astype(BF16)
        h_lo = (h2 - h_hi.astype(F32)).astype(BF16)
        logits = _dot_nt(wr_ref[0], h_hi) + _dot_nt(wr_ref[0], h_lo) + _dot_nt(wr_ref[1], h_hi)
        row = lax.broadcasted_iota(jnp.int32, (ROUTE_ROWS, 1), 0)
        lg = jnp.where(row < N_EXPERTS, logits, -jnp.inf)
        v1 = lg.max(axis=0, keepdims=True)
        i1 = jnp.where(lg == v1, row, ROUTE_ROWS).min(axis=0, keepdims=True)
        lg2 = jnp.where(row == i1, -jnp.inf, lg)
        v2 = lg2.max(axis=0, keepdims=True)
        i2 = jnp.where(lg2 == v2, row, ROUTE_ROWS).min(axis=0, keepdims=True)
        e = jnp.exp(v2 - v1)
        w1 = 1.0 / (1.0 + e)
        out_row = lax.broadcasted_iota(jnp.int32, (8, 1), 0)
        rt_ref[...] = jnp.where(out_row == 0, i1.astype(F32),
                                jnp.where(out_row == 1, i2.astype(F32),
                                          jnp.where(out_row == 2, w1, jnp.where(out_row == 3, e * w1, 0.0))))


def out_proj(x, mod, att, hy, yf, yb, u, scan_row0, d_skip, w_glu, b_glu, w_out_bf, g2, router_w=None):
    B, T, D = x.shape
    tb = TOK_TB
    route = router_w is not None
    S = u.shape[0]
    tok = lambda n: pl.BlockSpec((B, tb, n), lambda i: (0, i, 0))
    scan = pl.BlockSpec((S, tb * B, 128), lambda i: (0, i + scan_row0 // tb, 0))
    const = lambda shape: pl.BlockSpec(shape, lambda i: (0,) * len(shape))
    in_specs = [tok(D), const((B, 6, D)), tok(D_ATT), tok(D_HYENA),
                scan, scan, scan, const((1, D_SSM)), const((D_SSM, D_SSM)), const((1, D_SSM)),
                const((D, D)), const((1, D))]
    args = [x, mod, att, hy, yf, yb, u, d_skip.reshape(1, D_SSM), w_glu.astype(BF16), b_glu.reshape(1, D_SSM),
            w_out_bf, g2.reshape(1, D)]
    out_specs = [tok(D), tok(D)]
    out_shape = [jax.ShapeDtypeStruct((B, T, D), F32), jax.ShapeDtypeStruct((B, T, D), BF16)]
    if route:
        in_specs.append(const((2, ROUTE_ROWS, D)))
        wr = jnp.pad(router_w.astype(F32).T, ((0, ROUTE_ROWS - N_EXPERTS), (0, 0)))
        wr_hi = wr.astype(BF16)
        args.append(jnp.stack([wr_hi, (wr - wr_hi.astype(F32)).astype(BF16)]))
        out_specs.append(pl.BlockSpec((None, 8, B * tb), lambda i: (i, 0, 0)))
        out_shape.append(jax.ShapeDtypeStruct((T // tb, 8, B * tb), F32))
    outs = pl.pallas_call(
        functools.partial(_outproj_kernel, route=route),
        grid=(T // tb,),
        in_specs=in_specs, out_specs=out_specs, out_shape=out_shape,
        compiler_params=_cparams(("parallel",), VMEM_LIMIT),
        name="out_proj",
    )(*args)
    outs = list(outs)
    if route:
        outs[2] = outs[2].reshape(T // tb, 8, B, tb).transpose(2, 0, 3, 1).reshape(B, T, 8)
    return outs


ROUTE_ROWS = 16


FF_CHUNK = 768
FF_VMEM_LIMIT = 60 * 1024 * 1024


def _swiglu_tile(h, w1_ref, w3_ref, w2_ref):
    acc = None
    for lo in range(0, D_FF, FF_CHUNK):
        hi = min(lo + FF_CHUNK, D_FF)
        a = _dot(h, w1_ref[:, lo:hi].astype(BF16))
        g = (a * jax.nn.sigmoid(a) * _dot(h, w3_ref[:, lo:hi].astype(BF16))).astype(BF16)
        part = _dot(g, w2_ref[lo:hi, :].astype(BF16))
        acc = part if acc is None else acc + part
    return acc


def _ffn_kernel(h_ref, x_ref, mod_ref, w1_ref, w3_ref, w2_ref, o_ref):
    o_ref[...] = x_ref[...] + mod_ref[5:6, :] * _swiglu_tile(h_ref[...], w1_ref, w3_ref, w2_ref)


def ffn_dense(h2, x1, mod, w1, w3, w2, tm=512):
    B, T, D = x1.shape
    tm = min(tm, T)
    tok = pl.BlockSpec((None, tm, D), lambda b, i: (b, i, 0))
    once = pl.Buffered(1)
    return pl.pallas_call(
        _ffn_kernel,
        grid=(B, T // tm),
        in_specs=[tok, tok, pl.BlockSpec((None, 6, D), lambda b, i: (b, 0, 0)),
                  pl.BlockSpec((D, D_FF), lambda b, i: (0, 0), pipeline_mode=once),
                  pl.BlockSpec((D, D_FF), lambda b, i: (0, 0), pipeline_mode=once),
                  pl.BlockSpec((D_FF, D), lambda b, i: (0, 0), pipeline_mode=once)],
        out_specs=tok,
        out_shape=jax.ShapeDtypeStruct((B, T, D), F32),
        compiler_params=_cparams(("parallel", "parallel"), FF_VMEM_LIMIT),
        name="ffn_dense",
    )(h2, x1, mod, w1, w3, w2)


MOE_TM = 512


def _moe_kernel(te_ref, tv_ref, h_ref, w1_ref, w3_ref, w2_ref, o_ref):
    valid = tv_ref[pl.program_id(0)] > 0

    @pl.when(valid)
    def _():
        o_ref[...] = _swiglu_tile(h_ref[...], w1_ref, w3_ref, w2_ref).astype(o_ref.dtype)

    @pl.when(jnp.logical_not(valid))
    def _():
        o_ref[...] = jnp.zeros_like(o_ref)


def moe_experts(hs, tile_expert, tile_valid, w1, w3, w2):
    P, D = hs.shape
    nt = P // MOE_TM
    tok = pl.BlockSpec((MOE_TM, D), lambda i, te, tv: (i, 0))
    return pl.pallas_call(
        _moe_kernel,
        grid_spec=pltpu.PrefetchScalarGridSpec(
            num_scalar_prefetch=2,
            grid=(nt,),
            in_specs=[tok,
                      pl.BlockSpec((None, D, D_FF), lambda i, te, tv: (te[i], 0, 0)),
                      pl.BlockSpec((None, D, D_FF), lambda i, te, tv: (te[i], 0, 0)),
                      pl.BlockSpec((None, D_FF, D), lambda i, te, tv: (te[i], 0, 0))],
            out_specs=tok),
        out_shape=jax.ShapeDtypeStruct((P, D), BF16),
        compiler_params=_cparams(("arbitrary",), FF_VMEM_LIMIT),
        name="moe_experts",
    )(tile_expert, tile_valid, hs, w1, w3, w2)


def _combine_kernel(x_ref, mod_ref, ya_ref, yb_ref, rt_ref, *rest):
    o_ref = rest[-1]
    wa = rt_ref[:, 2:3]
    wb = rt_ref[:, 3:4]
    o_ref[...] = x_ref[...] + mod_ref[5:6, :] * (wa * ya_ref[...] + wb * yb_ref[...])


def moe_combine(x1, mod, yab, rt, b0, out_prev, after, tm=512):
    B, T, D = x1.shape
    gb = yab.shape[1]
    tok = lambda n: pl.BlockSpec((None, tm, n), lambda b, i: (b + b0, i, 0))
    choice = lambda c: pl.BlockSpec((None, None, tm, D), lambda b, i: (c, b, i, 0))
    in_specs = [tok(D), pl.BlockSpec((None, 6, D), lambda b, i: (b + b0, 0, 0)), choice(0), choice(1),
                tok(rt.shape[-1])]
    args = [x1, mod, yab, yab, rt]
    aliases = {}
    if out_prev is not None:
        in_specs.append(pl.BlockSpec(memory_space=pl.ANY))
        args.append(out_prev)
        aliases = {len(args) - 1: 0}
    if after is not None:
        in_specs.append(pl.BlockSpec(memory_space=pl.ANY))
        args.append(after)
    return pl.pallas_call(
        _combine_kernel,
        grid=(gb, T // tm),
        in_specs=in_specs,
        out_specs=tok(D),
        out_shape=jax.ShapeDtypeStruct((B, T, D), F32),
        input_output_aliases=aliases,
        compiler_params=_cparams(("parallel", "parallel")),
        name="moe_combine",
    )(*args)


MOE_GROUPS = 2


def moe_layer(h2, x1, mod, rt, w1, w3, w2):
    B = x1.shape[0]
    gb = B // MOE_GROUPS
    weights = (w1.astype(BF16), w3.astype(BF16), w2.astype(BF16))
    groups = [_moe_group(h2, rt, b0, gb, *weights) for b0 in range(0, B, gb)]
    out = None
    for g, (ys, slot) in enumerate(groups):
        yab = ys.at[slot].get(mode='promise_in_bounds').reshape(2, gb, *x1.shape[1:])
        after = groups[g + 1][0] if g + 1 < len(groups) else None
        out = moe_combine(x1, mod, yab, rt, g * gb, out, after)
    return out


def _moe_group(h2, rt, b0, gb, w1, w3, w2):
    _, T, D = h2.shape
    n = gb * T
    rt = rt[b0:b0 + gb]
    e_flat = jnp.concatenate([rt[..., 0].reshape(n), rt[..., 1].reshape(n)]).astype(jnp.int32)
    order = jnp.argsort(e_flat, stable=True).astype(jnp.int32)
    rank = jnp.argsort(order).astype(jnp.int32)
    onehot = jax.nn.one_hot(e_flat, N_EXPERTS, dtype=jnp.int32)
    counts = jnp.sum(onehot, axis=0)
    padded = ((counts + MOE_TM - 1) // MOE_TM) * MOE_TM
    pend = jnp.cumsum(padded)
    shift = (pend - padded) - (jnp.cumsum(counts) - counts)
    P = 2 * n + N_EXPERTS * MOE_TM
    nt = P // MOE_TM
    tile_row = jnp.arange(nt, dtype=jnp.int32) * MOE_TM
    tile_expert = jnp.sum((tile_row[:, None] >= pend[None, :]).astype(jnp.int32), axis=1)
    tile_valid = (tile_expert < N_EXPERTS).astype(jnp.int32)
    tile_expert = jnp.minimum(tile_expert, N_EXPERTS - 1)
    slot = rank + jnp.sum(onehot * shift[None, :], axis=1)
    tile_shift = jnp.sum(jax.nn.one_hot(tile_expert, N_EXPERTS, dtype=jnp.int32) * shift[None, :], axis=1)
    row_rank = jnp.arange(P, dtype=jnp.int32) - jnp.repeat(tile_shift, MOE_TM)
    src = order[jnp.clip(row_rank, 0, 2 * n - 1)] % n
    take = lambda a, idx: a.at[idx].get(mode='promise_in_bounds')
    hs = take(h2.reshape(-1, D), src + b0 * T)
    ys = moe_experts(hs, tile_expert, tile_valid, w1, w3, w2)
    return ys, slot


def _hyena(hz, conv_w, conv_b, spec, bias):
    zc = short_conv(hz, conv_w, conv_b)
    B, L, C = zc.shape
    lmax = FFT_N // 2
    if L < lmax:
        slot = lmax // B
        assert slot >= 2 * L
        zc = jnp.pad(zc, ((0, 0), (0, slot - L), (0, 0))).reshape(1, lmax, C)
    y = gated_long_conv(zc, 0, zc, D_HYENA, bias[0], spec, 0)
    y = gated_long_conv(y, 0, zc, 2 * D_HYENA, bias[1], spec, D_HYENA)
    if L < lmax:
        y = y.reshape(B, lmax // B, D_HYENA)[:, :L]
    return y


def _split_route(outs, mod):
    x1, h2 = outs[0], outs[1]
    return h2, x1, mod, (outs[2] if len(outs) > 2 else None)


def kernel(x, c, ctx, c_ctx, w_ada, b_ada, g_norm1, g_norm2, w_in, w_out, g_q, g_k, rpb, hy_conv_w, hy_conv_b, filt_w1, filt_b1, filt_w2, filt_b2, filt_w3, filt_b3, filt_freq, filt_w_out, hy_bias, lam_re, lam_im, log_step, b_re, b_im, c_re, c_im, d_skip, w_glu, b_glu, ffn_w1, ffn_w3, ffn_w2, router_w, moe_w1, moe_w3, moe_w2):
    B, L, D = x.shape
    Lc = ctx.shape[1]
    c8 = jnp.zeros((8, D), F32).at[:B].set(c).at[B].set(c_ctx)
    mod_all = adaln_mod(c8, w_ada, b_ada)
    for layer in range(DEPTH):
        last = layer == DEPTH - 1
        m = mod_all[layer].reshape(8, 6, D)
        mod_x = m[:B]
        mod_c = jnp.broadcast_to(m[B], (B, 6, D))
        w_l = w_in[layer].astype(BF16)
        wo_l = w_out[layer].astype(BF16)
        filt = (filt_w1[layer], filt_b1[layer], filt_w2[layer], filt_b2[layer],
                filt_w3[layer], filt_b3[layer], filt_freq[layer], filt_w_out[layer])
        ssm = (lam_re[layer], lam_im[layer], log_step[layer], b_re[layer], b_im[layer], c_re[layer], c_im[layer])

        q_c, k_c, v_c, hz_c, u_c = in_proj(ctx, mod_c, g_norm1[layer], w_l, g_q[layer], g_k[layer])
        q, k, v, hz, u = in_proj(x, mod_x, g_norm1[layer], w_l, g_q[layer], g_k[layer])
        att = neighbourhood_attention(q, k, v, k_c, v_c, rpb[layer])
        spec = filter_spectrum(hyena_filter_taps(L, *filt))
        hy = _hyena(hz, hy_conv_w[layer], hy_conv_b[layer], spec, hy_bias[layer])
        un, yf, yb = s5_scan(u, u_c, *ssm)

        tail = (d_skip[layer], w_glu[layer], b_glu[layer], wo_l, g_norm2[layer])
        i = layer // 2
        if layer % 2 == 0:
            ffw = (ffn_w1[i], ffn_w3[i], ffn_w2[i])
            mixer = lambda h2, x1, mod, rt: ffn_dense(h2, x1, mod, *ffw)
            rw = None
        else:
            mow = (moe_w1[i], moe_w3[i], moe_w2[i])
            mixer = lambda h2, x1, mod, rt: moe_layer(h2, x1, mod, rt, *mow)
            rw = router_w[i]
        x = mixer(*_split_route(out_proj(x, mod_x, att, hy, yf, yb, un, 0, *tail, router_w=rw), mod_x))

        if not last:
            att_c = context_attention(q_c, k_c, v_c)
            spec_c = filter_spectrum(hyena_filter_taps(Lc, *filt))
            hy_c = _hyena(hz_c, hy_conv_w[layer], hy_conv_b[layer], spec_c, hy_bias[layer])
            ctx = mixer(*_split_route(out_proj(ctx, mod_c, att_c, hy_c, yf, yb, un, L, *tail, router_w=rw), mod_c))
    return x
```

```python
import functools
import math

import numpy as np
import jax
import jax.numpy as jnp
from jax import lax
from jax.experimental import pallas as pl
from jax.experimental.pallas import tpu as pltpu

F32 = jnp.float32
BF16 = jnp.bfloat16
HIGHEST = lax.Precision.HIGHEST

D_MODEL = 1024
DEPTH = 2
GRID_W = 64
ATT_HEAD_DIM = 64
ATT_HEADS = 8
D_ATT = 512
WIN_ROWS = 8
WIN_COLS = 16
D_HYENA = 256
HYENA_ORDER = 2
FILTER_EMB = 33
FILTER_BANDS = 16
FILTER_WIDTH = 64
MIN_DECAY = math.log(1e-2) / 1.5
MAX_DECAY = math.log(1e-2) / 0.3
D_SSM = 256
SSM_GROUP_WIDTH = 16
SSM_GROUPS = 16
SSM_STATE = 64
COL_K = D_ATT
COL_V = 2 * D_ATT
COL_HY = 3 * D_ATT
COL_SSM = COL_HY + 3 * D_HYENA
D_IN = COL_SSM + D_SSM
D_FF = 2816
N_EXPERTS = 8
EPS = 1e-6
NEG = -1e30

VMEM_LIMIT = 56 * 1024 * 1024


def _cparams(sem, vmem=None):
    return pltpu.CompilerParams(dimension_semantics=sem, vmem_limit_bytes=vmem)


def _dot(a, b):
    return jnp.dot(a, b, preferred_element_type=F32)


def _dot_nt(a, b):
    return lax.dot_general(a, b, (((1,), (1,)), ((), ())), preferred_element_type=F32)


def _mod_kernel(c_ref, w_ref, b_ref, o_ref):
    c = c_ref[...]
    s = c * jax.nn.sigmoid(c)
    o_ref[...] = jnp.dot(s, w_ref[...], precision=HIGHEST, preferred_element_type=F32) + b_ref[...]


def adaln_mod(c8, w_ada, b_ada, tn=512):
    depth, d, n = w_ada.shape
    return pl.pallas_call(
        _mod_kernel,
        grid=(depth, n // tn),
        in_specs=[pl.BlockSpec((8, d), lambda l, j: (0, 0)),
                  pl.BlockSpec((None, d, tn), lambda l, j: (l, 0, j)),
                  pl.BlockSpec((None, 1, tn), lambda l, j: (l, 0, j))],
        out_specs=pl.BlockSpec((None, 8, tn), lambda l, j: (l, 0, j)),
        out_shape=jax.ShapeDtypeStruct((depth, 8, n), F32),
        compiler_params=_cparams(("parallel", "parallel")),
        name="adaln_mod",
    )(c8, w_ada, b_ada.reshape(depth, 1, n))


def _rms_mod(x, g, shift, scale):
    ms = jnp.mean(x * x, axis=-1, keepdims=True)
    h = x * lax.rsqrt(ms + EPS) * g
    return h * (1.0 + scale) + shift


def _head_norm(z, a, g):
    zz = (z * z).astype(BF16)
    m = jnp.concatenate([_dot(zz[:, c:c + ATT_LANES], a) for c in range(0, z.shape[1], ATT_LANES)], axis=1)
    return z * lax.rsqrt(m + EPS) * g


TOK_TB = 128


def _inproj_kernel(x_ref, mod_ref, g_ref, w_ref, a_ref, gq_ref, gk_ref,
                   q_ref, k_ref, v_ref, hz_ref, u_ref):
    B, tb, _ = x_ref.shape
    h = jnp.concatenate([_rms_mod(x_ref[b], g_ref[...], mod_ref[b, 0:1, :], mod_ref[b, 1:2, :]).astype(BF16)
                         for b in range(B)], axis=0)
    a = a_ref[...]
    q = (_head_norm(_dot(h, w_ref[:, 0:COL_K]), a, gq_ref[...]) * (ATT_HEAD_DIM ** -0.5)).astype(BF16)
    k = _head_norm(_dot(h, w_ref[:, COL_K:COL_V]), a, gk_ref[...]).astype(BF16)
    v = _dot(h, w_ref[:, COL_V:COL_HY]).astype(BF16)
    hz = _dot(h, w_ref[:, COL_HY:COL_SSM])
    u = _dot(h, w_ref[:, COL_SSM:D_IN])
    for b in range(B):
        rows = slice(b * tb, (b + 1) * tb)
        q_ref[b] = q[rows]
        k_ref[b] = k[rows]
        v_ref[b] = v[rows]
        hz_ref[b] = hz[rows].astype(hz_ref.dtype)
        for s in range(D_SSM // 128):
            u_ref[s, pl.ds(b, tb, stride=B), :] = u[rows, s * 128:(s + 1) * 128]


def in_proj(x, mod, g1, w_in_bf, gq, gk):
    B, T, D = x.shape
    tb = TOK_TB
    head_avg = jnp.asarray(np.kron(np.eye(ATT_LANES // ATT_HEAD_DIM),
                                   np.full((ATT_HEAD_DIM, ATT_HEAD_DIM), 1.0 / ATT_HEAD_DIM)), BF16)
    tok = lambda n: pl.BlockSpec((B, tb, n), lambda i: (0, i, 0))
    const = lambda shape: pl.BlockSpec(shape, lambda i: (0,) * len(shape))
    return pl.pallas_call(
        _inproj_kernel,
        grid=(T // tb,),
        in_specs=[tok(D), const((B, 6, D)), const((1, D)),
                  const((D, D_IN)), const((ATT_LANES, ATT_LANES)), const((1, D_ATT)), const((1, D_ATT))],
        out_specs=[tok(D_ATT), tok(D_ATT), tok(D_ATT), tok(3 * D_HYENA),
                   pl.BlockSpec((D_SSM // 128, tb * B, 128), lambda i: (0, i, 0))],
        out_shape=[jax.ShapeDtypeStruct((B, T, D_ATT), BF16)] * 3
                  + [jax.ShapeDtypeStruct((B, T, 3 * D_HYENA), BF16),
                     jax.ShapeDtypeStruct((D_SSM // 128, T * B, 128), F32)],
        compiler_params=_cparams(("parallel",), VMEM_LIMIT),
        name="in_proj",
    )(x, mod, g1.reshape(1, D), w_in_bf, head_avg,
      jnp.tile(gq, ATT_HEADS).reshape(1, D_ATT), jnp.tile(gk, ATT_HEADS).reshape(1, D_ATT))


ATT_R = 4
ATT_KROWS = ATT_R + WIN_ROWS - 1
ATT_LANES = 256


def _softmax_heads(q, parts, bias_ref, o_ref):
    nh = ATT_LANES // ATT_HEAD_DIM
    lane = lax.broadcasted_iota(jnp.int32, (1, ATT_LANES), 1)
    head = lambda h: (lane >= ATT_HEAD_DIM * h) & (lane < ATT_HEAD_DIM * (h + 1))
    acc = jnp.zeros((q.shape[0], ATT_LANES), F32)
    for h in range(nh):
        hm = head(h)
        qh = jnp.where(hm, q, jnp.zeros_like(q))
        ss = [_dot_nt(qh, kk).astype(BF16) for kk, _ in parts]
        if bias_ref is not None:
            ss[0] = ss[0] + bias_ref[h]
        m = ss[0].max(axis=-1, keepdims=True)
        for s in ss[1:]:
            m = jnp.maximum(m, s.max(axis=-1, keepdims=True))
        om = head((h + 1) % nh)
        o = None
        for s, (_, vv) in zip(ss, parts):
            part = _dot(jnp.exp(s - m), jnp.where(om, jnp.ones_like(vv), vv))
            o = part if o is None else o + part
        l = pltpu.roll(o, ATT_LANES - ATT_HEAD_DIM, axis=1)
        acc = jnp.where(hm, o / l, acc)
    o_ref[...] = acc.astype(o_ref.dtype)


ATT_SUB = 8


def _nattn_kernel(q_ref, k_ref, v_ref, kc_ref, vc_ref, *rest, rows):
    bias_refs, o_ref = rest[:ATT_SUB], rest[ATT_SUB]
    tq = ATT_R * GRID_W
    ctx = (kc_ref[...], vc_ref[...])
    for sub in range(ATT_SUB):
        blk = pl.program_id(2) * ATT_SUB + sub
        ks = jnp.clip(blk * ATT_R - WIN_ROWS // 2, 0, rows - ATT_KROWS)
        start = pl.multiple_of(ks * GRID_W, GRID_W)
        kl = k_ref[pl.ds(start, ATT_KROWS * GRID_W), :]
        vl = v_ref[pl.ds(start, ATT_KROWS * GRID_W), :]
        qrows = slice(sub * tq, (sub + 1) * tq)
        _softmax_heads(q_ref[qrows, :], [(kl, vl), ctx], bias_refs[sub], o_ref.at[qrows, :])


def _attn_bias_tables(rpb, rows):
    nblk = rows // ATT_R
    qc = np.arange(GRID_W)
    c0 = np.clip(qc - WIN_COLS // 2, 0, GRID_W - WIN_COLS)
    kc = np.arange(GRID_W)
    col_ok = (kc[None, :] >= c0[:, None]) & (kc[None, :] < c0[:, None] + WIN_COLS)
    col_off = kc[None, :] - qc[:, None] + WIN_COLS - 1
    col_sel = (col_off[:, :, None] == np.arange(2 * WIN_COLS - 1)) & col_ok[:, :, None]
    band = jnp.einsum('hij,qkj->hqik', rpb, jnp.asarray(col_sel, F32), precision=HIGHEST)
    band = jnp.where(jnp.asarray(col_ok)[None, :, None, :], band, NEG)
    band = band.reshape(ATT_HEADS, GRID_W, (2 * WIN_ROWS - 1) * GRID_W).astype(BF16)
    neg = lambda nrows: jnp.full((ATT_HEADS, GRID_W, nrows * GRID_W), NEG, BF16)
    tabs = []
    for blk in (0, 1, nblk - 1):
        ks = int(np.clip(blk * ATT_R - WIN_ROWS // 2, 0, rows - ATT_KROWS))
        qrows = []
        for r in range(blk * ATT_R, (blk + 1) * ATT_R):
            r0 = int(np.clip(r - WIN_ROWS // 2, 0, rows - WIN_ROWS))
            first = r0 - r + WIN_ROWS - 1
            win = band[:, :, first * GRID_W:(first + WIN_ROWS) * GRID_W]
            qrows.append(jnp.concatenate([neg(r0 - ks), win, neg(ATT_KROWS - WIN_ROWS - (r0 - ks))], axis=-1))
        tabs.append(jnp.stack(qrows, axis=1).reshape(ATT_HEADS, ATT_R * GRID_W, ATT_KROWS * GRID_W))
    return jnp.stack(tabs)


def neighbourhood_attention(q, k, v, kc, vc, rpb):
    B, L, _ = q.shape
    Lc = kc.shape[1]
    rows = L // GRID_W
    nblk = rows // ATT_R
    tq = ATT_R * GRID_W
    nkl = ATT_KROWS * GRID_W
    bias = _attn_bias_tables(rpb, rows)
    hpb = ATT_LANES // ATT_HEAD_DIM
    variant = lambda blk: jnp.where(blk == 0, 0, jnp.where(blk == nblk - 1, 2, 1))
    bias_spec = lambda sub: pl.BlockSpec((None, hpb, tq, nkl),
                                         lambda b, g, j: (variant(j * ATT_SUB + sub), g, 0, 0))
    return pl.pallas_call(
        functools.partial(_nattn_kernel, rows=rows),
        grid=(B, D_ATT // ATT_LANES, nblk // ATT_SUB),
        in_specs=[pl.BlockSpec((None, ATT_SUB * tq, ATT_LANES), lambda b, g, j: (b, j, g)),
                  pl.BlockSpec((None, L, ATT_LANES), lambda b, g, j: (b, 0, g)),
                  pl.BlockSpec((None, L, ATT_LANES), lambda b, g, j: (b, 0, g)),
                  pl.BlockSpec((None, Lc, ATT_LANES), lambda b, g, j: (b, 0, g)),
                  pl.BlockSpec((None, Lc, ATT_LANES), lambda b, g, j: (b, 0, g))]
                 + [bias_spec(sub) for sub in range(ATT_SUB)],
        out_specs=pl.BlockSpec((None, ATT_SUB * tq, ATT_LANES), lambda b, g, j: (b, j, g)),
        out_shape=jax.ShapeDtypeStruct((B, L, D_ATT), BF16),
        compiler_params=_cparams(("parallel", "parallel", "arbitrary"), VMEM_LIMIT),
        name="nattn",
    )(q, k, v, kc, vc, *([bias] * ATT_SUB))


def _cattn_kernel(q_ref, k_ref, v_ref, o_ref):
    _softmax_heads(q_ref[...], [(k_ref[...], v_ref[...])], None, o_ref)


def context_attention(q, k, v):
    B, Lc, _ = q.shape
    spec = pl.BlockSpec((None, Lc, ATT_LANES), lambda b, g: (b, 0, g))
    return pl.pallas_call(
        _cattn_kernel,
        grid=(B, D_ATT // ATT_LANES),
        in_specs=[spec, spec, spec],
        out_specs=spec,
        out_shape=jax.ShapeDtypeStruct((B, Lc, D_ATT), BF16),
        compiler_params=_cparams(("parallel", "parallel")),
        name="cattn",
    )(q, k, v)


FFT_N = 16384
FFT_R = 128
FFT_K1 = FFT_R // 2 + 1
FFT_K1P = 66
FFT_PITCH = 2 * FFT_K1P


def _filter_kernel(w1_ref, wt_ref, b1_ref, w2_ref, b2_ref, w3_ref, b3_ref, fr_ref, wo_ref, o_ref, *, lf, tile):
    i = pl.program_id(0)
    half = FFT_N // 2
    row0 = i * tile
    first_pos = jnp.where(row0 < half, row0, FFT_N - row0 - tile)

    @pl.when(first_pos >= lf)
    def _():
        o_ref[...] = jnp.zeros_like(o_ref)

    @pl.when(first_pos < lf)
    def _():
        rows = tile // FILTER_PACK

        def position(width):
            r = lax.broadcasted_iota(jnp.int32, (rows, FILTER_PACK * width), 0)
            g = lax.broadcasted_iota(jnp.int32, (rows, FILTER_PACK * width), 1) // width
            j = row0 + r + g * rows
            return jnp.where(j < half, j, FFT_N - 1 - j)

        nfeat = 2 * FILTER_BANDS
        feat = lax.broadcasted_iota(jnp.int32, (1, FILTER_PACK * nfeat), 1) % nfeat
        f = 1e-4 + (feat % FILTER_BANDS).astype(F32) * ((FILTER_BANDS - 1 - 1e-4) / (FILTER_BANDS - 1))
        w = (2.0 * math.pi / lf) * position(nfeat).astype(F32)
        z = jnp.cos(f * w + jnp.where(feat < FILTER_BANDS, 0.0, 0.5 * math.pi))
        t_h = position(FILTER_WIDTH).astype(F32) / (lf - 1.0)
        fr = fr_ref[...]
        hdot = lambda a, b: jnp.dot(a, b, precision=HIGHEST, preferred_element_type=F32)
        h = jnp.sin(fr * (hdot(z, w1_ref[...]) + t_h * wt_ref[...] + b1_ref[...]))
        h = jnp.sin(fr * (hdot(h, w2_ref[...]) + b2_ref[...]))
        h = jnp.sin(fr * (hdot(h, w3_ref[...]) + b3_ref[...]))
        y = hdot(h, wo_ref[...])
        nout = HYENA_ORDER * D_HYENA
        pos_o = position(nout)
        ch = lax.broadcasted_iota(jnp.int32, (1, FILTER_PACK * nout), 1) % D_HYENA
        delta = jnp.abs(MIN_DECAY + ch.astype(F32) * ((MAX_DECAY - MIN_DECAY) / (D_HYENA - 1)))
        y = jnp.where(pos_o < lf, y * jnp.exp(-(pos_o.astype(F32) / (lf - 1.0)) * delta), 0.0)
        for g in range(FILTER_PACK):
            o_ref[g * rows:(g + 1) * rows, :] = y[:, g * nout:(g + 1) * nout]


FILTER_PACK = 4


def hyena_filter_taps(lf, w1, b1, w2, b2, w3, b3, freq, w_out, tile=1024):
    P, W = FILTER_PACK, FILTER_WIDTH
    nout = HYENA_ORDER * D_HYENA
    bd = lambda m: jnp.kron(jnp.eye(P, dtype=F32), m.astype(F32))
    wide = lambda a: jnp.tile(a.astype(F32), P).reshape(1, P * W)
    wo = w_out.reshape(W, HYENA_ORDER, 2, D_HYENA).transpose(2, 0, 1, 3).reshape(2, W, nout)
    wo = jnp.stack([bd(wo[0]), bd(wo[1])])
    nt = FFT_N // tile
    const = lambda shape: pl.BlockSpec(shape, lambda i: (0,) * len(shape))
    return pl.pallas_call(
        functools.partial(_filter_kernel, lf=lf, tile=tile),
        grid=(nt,),
        in_specs=[const((P * (FILTER_EMB - 1), P * W)), const((1, P * W)), const((1, P * W)),
                  const((P * W, P * W)), const((1, P * W)), const((P * W, P * W)), const((1, P * W)),
                  const((1, P * W)),
                  pl.BlockSpec((None, P * W, P * nout), lambda i: (i // (nt // 2), 0, 0))],
        out_specs=pl.BlockSpec((tile, nout), lambda i: (i, 0)),
        out_shape=jax.ShapeDtypeStruct((FFT_N, nout), F32),
        compiler_params=_cparams(("parallel",)),
        name="hyena_filter",
    )(bd(w1[1:]), wide(w1[0]), wide(b1), bd(w2), wide(b2), bd(w3), wide(b3), wide(freq), wo)


def _shortconv_kernel(z_ref, w_ref, b_ref, o_ref, zp_ref, *, L):
    C = z_ref.shape[-1]
    zp_ref[0:8, :] = jnp.zeros((8, C), F32)
    zp_ref[L + 8:L + 16, :] = jnp.zeros((8, C), F32)
    zp_ref[8:L + 8, :] = z_ref[...].astype(F32)
    ch = min(L, 512)
    for c in range(L // ch):
        s = c * ch
        o_ref[s:s + ch, :] = (b_ref[...] + zp_ref[s + 7:s + 7 + ch, :] * w_ref[0:1, :]
                              + zp_ref[s + 8:s + 8 + ch, :] * w_ref[1:2, :]
                              + zp_ref[s + 9:s + 9 + ch, :] * w_ref[2:3, :]).astype(o_ref.dtype)


def short_conv(z, w, b, cb=128):
    B, L, C = z.shape
    return pl.pallas_call(
        functools.partial(_shortconv_kernel, L=L),
        grid=(B, C // cb),
        in_specs=[pl.BlockSpec((None, L, cb), lambda b_, c: (b_, 0, c)),
                  pl.BlockSpec((3, cb), lambda b_, c: (0, c)),
                  pl.BlockSpec((1, cb), lambda b_, c: (0, c))],
        out_specs=pl.BlockSpec((None, L, cb), lambda b_, c: (b_, 0, c)),
        out_shape=jax.ShapeDtypeStruct((B, L, C), BF16),
        scratch_shapes=[pltpu.VMEM((L + 16, cb), F32)],
        compiler_params=_cparams(("parallel", "parallel")),
        name="short_conv",
    )(z, w, b.reshape(1, C))


@functools.lru_cache(maxsize=None)
def _fft_tables(n1_in):
    n2 = np.arange(FFT_R)[:, None, None]
    k1 = np.arange(FFT_K1)[None, :, None]
    n1 = np.arange(n1_in)[None, None, :]
    th = 2.0 * np.pi * ((k1 * (FFT_R * n1 + n2)) % FFT_N) / FFT_N
    ga = np.zeros((FFT_R, 2 * FFT_K1P, n1_in))
    ga[:, :FFT_K1] = np.cos(th)
    ga[:, FFT_K1P:FFT_K1P + FFT_K1] = -np.sin(th)
    wk = np.where((np.arange(FFT_K1) == 0) | (np.arange(FFT_K1) == FFT_R // 2), 1.0, 2.0)[None, None, :]
    thd = th.transpose(0, 2, 1)
    gd = np.zeros((FFT_R, n1_in, 2 * FFT_K1P))
    gd[:, :, :FFT_K1] = wk * np.cos(thd)
    gd[:, :, FFT_K1P:FFT_K1P + FFT_K1] = -wk * np.sin(thd)
    a = np.arange(FFT_R)
    ph = 2.0 * np.pi * ((a[:, None] * a[None, :]) % FFT_R) / FFT_R
    c, s = np.cos(ph), np.sin(ph)
    fb = np.block([[c, s], [-s, c]])
    fc = np.block([[c, -s], [s, c]])
    return ga, gd, fb, fc


FFT_KG = 13
FFT_UNROLL = 16


def _spectrum_slabs(ya_ref, fb, k0):
    cols = []
    for g in range(FFT_KG):
        re = ya_ref[pl.ds(k0 + g, FFT_R, stride=FFT_PITCH), :]
        im = ya_ref[pl.ds(FFT_K1P + k0 + g, FFT_R, stride=FFT_PITCH), :]
        cols.append(jnp.concatenate([re, im], axis=0).astype(BF16))
    return _dot(fb, jnp.concatenate(cols, axis=1))


def _stage_a(x_ref, x_pitch, ga_ref, ya_ref, n1_in):
    def body(n2, carry):
        xs = x_ref[pl.ds(n2, n1_in, stride=x_pitch), :].astype(BF16)
        ya_ref[pl.ds(n2 * FFT_PITCH, FFT_PITCH), :] = _dot(ga_ref[n2], xs)
        return carry
    lax.fori_loop(0, FFT_R, body, 0, unroll=FFT_UNROLL)


def _spectrum_kernel(x_ref, ga_ref, fb_ref, o_ref, ya_ref):
    _stage_a(x_ref, FFT_R, ga_ref, ya_ref, FFT_R)
    fb = fb_ref[...]
    cb = ya_ref.shape[1]

    def body(kg, carry):
        x = _spectrum_slabs(ya_ref, fb, kg * FFT_KG)
        for g in range(FFT_KG):
            o_ref[kg * FFT_KG + g] = x[:, g * cb:(g + 1) * cb]
        return carry
    lax.fori_loop(0, FFT_K1 // FFT_KG, body, 0)


def filter_spectrum(taps, cb=128):
    C = taps.shape[1]
    ga, _, fb, _ = _fft_tables(FFT_R)
    return pl.pallas_call(
        _spectrum_kernel,
        grid=(C // cb,),
        in_specs=[pl.BlockSpec((FFT_N, cb), lambda c: (0, c)),
                  pl.BlockSpec((FFT_R, FFT_PITCH, FFT_R), lambda c: (0, 0, 0)),
                  pl.BlockSpec((2 * FFT_R, 2 * FFT_R), lambda c: (0, 0))],
        out_specs=pl.BlockSpec((FFT_K1, 2 * FFT_R, cb), lambda c: (0, 0, c)),
        out_shape=jax.ShapeDtypeStruct((FFT_K1, 2 * FFT_R, C), F32),
        scratch_shapes=[pltpu.VMEM((FFT_R * FFT_PITCH, cb), F32)],
        compiler_params=_cparams(("parallel",), VMEM_LIMIT),
        name="filter_spectrum",
    )(taps, _bf16_table(ga), _bf16_table(fb))


def _bf16_table(t):
    return jnp.asarray(t, F32).astype(BF16)


def _longconv_kernel(u_ref, gate_ref, bias_ref, spec_ref, ga_ref, gd_ref, fb_ref, fc_ref, o_ref, ya_ref, xp_ref):
    n1_in = FFT_R // 2

    def repitch(n1, carry):
        xp_ref[pl.ds(n1 * FFT_PITCH, FFT_R), :] = (
            u_ref[pl.ds(pl.multiple_of(n1 * FFT_R, FFT_R), FFT_R), :].astype(F32))
        return carry
    lax.fori_loop(0, n1_in, repitch, 0, unroll=FFT_UNROLL)

    _stage_a(xp_ref, FFT_PITCH, ga_ref, ya_ref, n1_in)
    fb = fb_ref[...]
    fc = fc_ref[...]

    cb = ya_ref.shape[1]

    def mid(kg, carry):
        k0 = kg * FFT_KG
        x = _spectrum_slabs(ya_ref, fb, k0)
        s = jnp.concatenate([spec_ref[k0 + g] for g in range(FFT_KG)], axis=1)
        xr, xi = x[:FFT_R], x[FFT_R:]
        sr, si = s[:FFT_R], s[FFT_R:]
        z = jnp.concatenate([xr * sr - xi * si, xr * si + xi * sr], axis=0).astype(BF16)
        v = _dot(fc, z)
        for g in range(FFT_KG):
            ya_ref[pl.ds(k0 + g, FFT_R, stride=FFT_PITCH), :] = v[:FFT_R, g * cb:(g + 1) * cb]
            ya_ref[pl.ds(FFT_K1P + k0 + g, FFT_R, stride=FFT_PITCH), :] = v[FFT_R:, g * cb:(g + 1) * cb]
        return carry
    lax.fori_loop(0, FFT_K1 // FFT_KG, mid, 0)

    def last(n2, carry):
        slab = ya_ref[pl.ds(n2 * FFT_PITCH, FFT_PITCH), :].astype(BF16)
        xp_ref[pl.ds(n2, n1_in, stride=FFT_PITCH), :] = _dot(gd_ref[n2], slab)
        return carry
    lax.fori_loop(0, FFT_R, last, 0, unroll=FFT_UNROLL)

    bias = bias_ref[...]

    def gate(n1, carry):
        rows = pl.ds(pl.multiple_of(n1 * FFT_R, FFT_R), FFT_R)
        y = xp_ref[pl.ds(n1 * FFT_PITCH, FFT_R), :] * (1.0 / FFT_N)
        o_ref[rows, :] = (gate_ref[rows, :].astype(F32) * (y + u_ref[rows, :].astype(F32) * bias)).astype(o_ref.dtype)
        return carry
    lax.fori_loop(0, n1_in, gate, 0, unroll=FFT_UNROLL)


def gated_long_conv(u, u_col, gate, gate_col, bias, spec, spec_col, C=D_HYENA, cb=128):
    B, L, _ = u.shape
    n1_in = FFT_R // 2
    ga, gd, fb, fc = _fft_tables(n1_in)
    once = pl.Buffered(1)
    seq = lambda col: pl.BlockSpec((None, L, cb), lambda c, b: (b, 0, c + col // cb))
    return pl.pallas_call(
        _longconv_kernel,
        grid=(C // cb, B),
        in_specs=[seq(u_col), seq(gate_col),
                  pl.BlockSpec((1, cb), lambda c, b: (0, c)),
                  pl.BlockSpec((FFT_K1, 2 * FFT_R, cb), lambda c, b: (0, 0, c + spec_col // cb), pipeline_mode=once),
                  pl.BlockSpec((FFT_R, FFT_PITCH, n1_in), lambda c, b: (0, 0, 0), pipeline_mode=once),
                  pl.BlockSpec((FFT_R, n1_in, FFT_PITCH), lambda c, b: (0, 0, 0), pipeline_mode=once),
                  pl.BlockSpec((2 * FFT_R, 2 * FFT_R), lambda c, b: (0, 0)),
                  pl.BlockSpec((2 * FFT_R, 2 * FFT_R), lambda c, b: (0, 0))],
        out_specs=seq(0),
        out_shape=jax.ShapeDtypeStruct((B, L, C), BF16),
        scratch_shapes=[pltpu.VMEM((FFT_R * FFT_PITCH, cb), F32), pltpu.VMEM((n1_in * FFT_PITCH, cb), F32)],
        compiler_params=_cparams(("parallel", "arbitrary"), VMEM_LIMIT),
        name="long_conv",
    )(u, gate, bias.reshape(1, C), spec, _bf16_table(ga), _bf16_table(gd), _bf16_table(fb), _bf16_table(fc))


SSM_N = SSM_GROUPS * SSM_STATE
S5_SEQS = 8
S5_CHUNK = 256


def _s5_kernel(uf_ref, ub_ref, bf_ref, bb_ref, lam_ref, cf_ref, cb_ref, yf_ref, yb_ref,
               buf_ref, bub_ref, xsf_ref, xsb_ref, st_ref):
    half = S5_SEQS // 2
    tiles = S5_CHUNK // 2

    @pl.when(pl.program_id(0) == 0)
    def _():
        st_ref[...] = jnp.zeros_like(st_ref)

    slabs = lambda ref: jnp.concatenate([ref[s] for s in range(ref.shape[0])], axis=1).astype(BF16)
    buf_ref[...] = _dot(slabs(uf_ref), bf_ref[...])
    bub_ref[...] = _dot(slabs(ub_ref), bb_ref[...])
    lo = lax.broadcasted_iota(jnp.int32, (S5_SEQS, 1), 0) < half

    def advance(x, lam, b):
        xr, xi = x
        lr, li = lam[:, 0:SSM_N], lam[:, SSM_N:2 * SSM_N]
        return lr * xr - li * xi + b[:, 0:SSM_N], lr * xi + li * xr + b[:, SSM_N:2 * SSM_N]

    def swap(x):
        return pltpu.roll(x[0], half, axis=0), pltpu.roll(x[1], half, axis=0)

    def step(t4, x):
        f_tiles, b_tiles = [], []
        for p in range(2):
            tf = buf_ref[pl.ds(pl.multiple_of((2 * t4 + p) * S5_SEQS, S5_SEQS), S5_SEQS), :]
            tb = bub_ref[pl.ds(pl.multiple_of((tiles - 1 - 2 * t4 - p) * S5_SEQS, S5_SEQS), S5_SEQS), :]
            xe = advance(x, lam_ref[0], jnp.where(lo, tf, tb))
            xo = advance(swap(xe), lam_ref[1], jnp.where(lo, tb, tf))
            x = swap(xo)
            xe = jnp.concatenate(xe, axis=1)
            xo = jnp.concatenate(xo, axis=1)
            f_tiles.append(jnp.where(lo, xe, xo))
            b_tiles.append(jnp.where(lo, xo, xe))
        xsf_ref[pl.ds(pl.multiple_of(t4 * 2 * S5_SEQS, 2 * S5_SEQS), 2 * S5_SEQS), :] = (
            jnp.concatenate(f_tiles, axis=0).astype(BF16))
        xsb_ref[pl.ds(pl.multiple_of((tiles - 2 - 2 * t4) * S5_SEQS, 2 * S5_SEQS), 2 * S5_SEQS), :] = (
            jnp.concatenate(b_tiles[::-1], axis=0).astype(BF16))
        return x

    xr, xi = lax.fori_loop(0, S5_CHUNK // 4, step, (st_ref[:, 0:SSM_N], st_ref[:, SSM_N:2 * SSM_N]))
    st_ref[:, 0:SSM_N] = xr
    st_ref[:, SSM_N:2 * SSM_N] = xi
    yf = _dot(xsf_ref[...], cf_ref[...])
    yb = _dot(xsb_ref[...], cb_ref[...])
    for s in range(yf_ref.shape[0]):
        yf_ref[s] = yf[:, s * 128:(s + 1) * 128]
        yb_ref[s] = yb[:, s * 128:(s + 1) * 128]


def _s5_matrices(lam_re, lam_im, log_step, b_re, b_im, c_re, c_im):
    G, P, Hs = SSM_GROUPS, SSM_STATE, SSM_GROUP_WIDTH
    dt = jnp.exp(log_step)[:, :, None]
    mag = jnp.exp(lam_re * dt)
    ar, ai = mag * jnp.cos(lam_im * dt), mag * jnp.sin(lam_im * dt)
    er, ei = ar - 1.0, ai
    den = lam_re * lam_re + lam_im * lam_im
    cr, ci = (er * lam_re + ei * lam_im) / den, (ei * lam_re - er * lam_im) / den
    bbr = cr[..., None] * b_re - ci[..., None] * b_im
    bbi = cr[..., None] * b_im + ci[..., None] * b_re
    eye = jnp.eye(G, dtype=F32)
    blk_in = lambda m: jnp.einsum('gq,qph->ghqp', eye, m).reshape(G * Hs, G * P)
    bmat = [jnp.concatenate([blk_in(bbr[d]), blk_in(bbi[d])], axis=1).astype(BF16) for d in range(2)]
    blk_out = lambda m: jnp.einsum('gq,qhp->gpqh', eye, m).reshape(G * P, G * Hs)
    cmat = [jnp.concatenate([blk_out(c_re[d]), -blk_out(c_im[d])], axis=0).astype(BF16) for d in range(2)]
    lam = jnp.concatenate([ar.reshape(2, 1, G * P), ai.reshape(2, 1, G * P)], axis=2)
    lam = jnp.broadcast_to(lam, (2, S5_SEQS // 2, 2 * G * P))
    lam8 = jnp.stack([lam.reshape(S5_SEQS, 2 * G * P), lam[::-1].reshape(S5_SEQS, 2 * G * P)])
    return bmat[0], bmat[1], lam8, cmat[0], cmat[1]


def s5_scan(u_x, u_c, lam_re, lam_im, log_step, b_re, b_im, c_re, c_im):
    S = u_x.shape[0]
    half = S5_SEQS // 2
    L, Lc = u_x.shape[1] // half, u_c.shape[1] // half
    assert L % S5_CHUNK == 0 and Lc % S5_CHUNK == 0
    bf, bb, lam8, cf, cb = _s5_matrices(lam_re, lam_im, log_step, b_re, b_im, c_re, c_im)
    un = jnp.concatenate([u_x, u_c], axis=1)
    rows = S5_CHUNK * half
    nx, nc = L // S5_CHUNK, Lc // S5_CHUNK
    n = nx + nc
    fwd = pl.BlockSpec((S, rows, 128), lambda i: (0, jnp.where(i < nc, nx + i, i - nc), 0))
    bwd = pl.BlockSpec((S, rows, 128), lambda i: (0, n - 1 - i, 0))
    const = lambda shape: pl.BlockSpec(shape, lambda i: (0,) * len(shape))
    yf, yb = pl.pallas_call(
        _s5_kernel,
        grid=(n,),
        in_specs=[fwd, bwd, const((D_SSM, 2 * SSM_N)), const((D_SSM, 2 * SSM_N)),
                  const((2, S5_SEQS, 2 * SSM_N)), const((2 * SSM_N, D_SSM)), const((2 * SSM_N, D_SSM))],
        out_specs=[fwd, bwd],
        out_shape=[jax.ShapeDtypeStruct(un.shape, F32)] * 2,
        scratch_shapes=[pltpu.VMEM((rows, 2 * SSM_N), F32), pltpu.VMEM((rows, 2 * SSM_N), F32),
                        pltpu.VMEM((rows, 2 * SSM_N), BF16), pltpu.VMEM((rows, 2 * SSM_N), BF16),
                        pltpu.VMEM((S5_SEQS, 2 * SSM_N), F32)],
        compiler_params=_cparams(("arbitrary",), VMEM_LIMIT),
        name="s5_scan",
    )(un, un, bf, bb, lam8, cf, cb)
    return un, yf, yb


def _gelu_tanh(y):
    return 0.5 * y * (1.0 + jnp.tanh(math.sqrt(2.0 / math.pi) * (y + 0.044715 * (y * y * y))))


def _outproj_kernel(*refs, route):
    (x_ref, mod_ref, att_ref, hy_ref, yf_ref, yb_ref, u_ref, dsk_ref, wg_ref, bg_ref, wo_ref, g2_ref) = refs[:12]
    if route:
        wr_ref, x1_ref, h2_ref, rt_ref = refs[12:]
    else:
        x1_ref, h2_ref = refs[12:]
    B, tb, D = x_ref.shape
    rows = lambda a: a.reshape(B * tb, a.shape[-1])

    def scan_rows(ref):
        return jnp.concatenate(
            [jnp.concatenate([ref[s, pl.ds(b, tb, stride=B), :] for s in range(ref.shape[0])], axis=1)
             for b in range(B)], axis=0)

    y = scan_rows(yf_ref) + scan_rows(yb_ref) + scan_rows(u_ref) * dsk_ref[...]
    y = _gelu_tanh(y)
    ss = y * jax.nn.sigmoid(_dot(y.astype(BF16), wg_ref[...]) + bg_ref[...])
    mix = (_dot(rows(att_ref[...]), wo_ref[0:D_ATT, :])
           + _dot(rows(hy_ref[...]).astype(BF16), wo_ref[D_ATT:D_ATT + D_HYENA, :])
           + _dot(ss.astype(BF16), wo_ref[D_ATT + D_HYENA:, :]))
    h2s = []
    for b in range(B):
        x1 = x_ref[b] + mod_ref[b, 2:3, :] * mix[b * tb:(b + 1) * tb]
        x1_ref[b] = x1
        h2s.append(_rms_mod(x1, g2_ref[...], mod_ref[b, 3:4, :], mod_ref[b, 4:5, :]))
        h2_ref[b] = h2s[-1].astype(h2_ref.dtype)
    if route:
        h2 = jnp.concatenate(h2s, axis=0)
        h_hi = h2.astype(BF16)
        h_lo = (h2 - h_hi.astype(F32)).astype(BF16)
        logits = _dot_nt(wr_ref[0], h_hi) + _dot_nt(wr_ref[0], h_lo) + _dot_nt(wr_ref[1], h_hi)
        row = lax.broadcasted_iota(jnp.int32, (ROUTE_ROWS, 1), 0)
        lg = jnp.where(row < N_EXPERTS, logits, -jnp.inf)
        v1 = lg.max(axis=0, keepdims=True)
        i1 = jnp.where(lg == v1, row, ROUTE_ROWS).min(axis=0, keepdims=True)
        lg2 = jnp.where(row == i1, -jnp.inf, lg)
        v2 = lg2.max(axis=0, keepdims=True)
        i2 = jnp.where(lg2 == v2, row, ROUTE_ROWS).min(axis=0, keepdims=True)
        e = jnp.exp(v2 - v1)
        w1 = 1.0 / (1.0 + e)
        out_row = lax.broadcasted_iota(jnp.int32, (8, 1), 0)
        rt_ref[...] = jnp.where(out_row == 0, i1.astype(F32),
                                jnp.where(out_row == 1, i2.astype(F32),
                                          jnp.where(out_row == 2, w1, jnp.where(out_row == 3, e * w1, 0.0))))


def out_proj(x, mod, att, hy, yf, yb, u, scan_row0, d_skip, w_glu, b_glu, w_out_bf, g2, router_w=None):
    B, T, D = x.shape
    tb = TOK_TB
    route = router_w is not None
    S = u.shape[0]
    tok = lambda n: pl.BlockSpec((B, tb, n), lambda i: (0, i, 0))
    scan = pl.BlockSpec((S, tb * B, 128), lambda i: (0, i + scan_row0 // tb, 0))
    const = lambda shape: pl.BlockSpec(shape, lambda i: (0,) * len(shape))
    in_specs = [tok(D), const((B, 6, D)), tok(D_ATT), tok(D_HYENA),
                scan, scan, scan, const((1, D_SSM)), const((D_SSM, D_SSM)), const((1, D_SSM)),
                const((D, D)), const((1, D))]
    args = [x, mod, att, hy, yf, yb, u, d_skip.reshape(1, D_SSM), w_glu.astype(BF16), b_glu.reshape(1, D_SSM),
            w_out_bf, g2.reshape(1, D)]
    out_specs = [tok(D), tok(D)]
    out_shape = [jax.ShapeDtypeStruct((B, T, D), F32), jax.ShapeDtypeStruct((B, T, D), BF16)]
    if route:
        in_specs.append(const((2, ROUTE_ROWS, D)))
        wr = jnp.pad(router_w.astype(F32).T, ((0, ROUTE_ROWS - N_EXPERTS), (0, 0)))
        wr_hi = wr.astype(BF16)
        args.append(jnp.stack([wr_hi, (wr - wr_hi.astype(F32)).astype(BF16)]))
        out_specs.append(pl.BlockSpec((None, 8, B * tb), lambda i: (i, 0, 0)))
        out_shape.append(jax.ShapeDtypeStruct((T // tb, 8, B * tb), F32))
    outs = pl.pallas_call(
        functools.partial(_outproj_kernel, route=route),
        grid=(T // tb,),
        in_specs=in_specs, out_specs=out_specs, out_shape=out_shape,
        compiler_params=_cparams(("parallel",), VMEM_LIMIT),
        name="out_proj",
    )(*args)
    outs = list(outs)
    if route:
        outs[2] = outs[2].reshape(T // tb, 8, B, tb).transpose(2, 0, 3, 1).reshape(B, T, 8)
    return outs


ROUTE_ROWS = 16


FF_CHUNK = 768
FF_VMEM_LIMIT = 60 * 1024 * 1024


def _swiglu_tile(h, w1_ref, w3_ref, w2_ref):
    acc = None
    for lo in range(0, D_FF, FF_CHUNK):
        hi = min(lo + FF_CHUNK, D_FF)
        a = _dot(h, w1_ref[:, lo:hi].astype(BF16))
        g = (a * jax.nn.sigmoid(a) * _dot(h, w3_ref[:, lo:hi].astype(BF16))).astype(BF16)
        part = _dot(g, w2_ref[lo:hi, :].astype(BF16))
        acc = part if acc is None else acc + part
    return acc


def _ffn_kernel(h_ref, x_ref, mod_ref, w1_ref, w3_ref, w2_ref, o_ref):
    o_ref[...] = x_ref[...] + mod_ref[5:6, :] * _swiglu_tile(h_ref[...], w1_ref, w3_ref, w2_ref)


def ffn_dense(h2, x1, mod, w1, w3, w2, tm=512):
    B, T, D = x1.shape
    tm = min(tm, T)
    tok = pl.BlockSpec((None, tm, D), lambda b, i: (b, i, 0))
    once = pl.Buffered(1)
    return pl.pallas_call(
        _ffn_kernel,
        grid=(B, T // tm),
        in_specs=[tok, tok, pl.BlockSpec((None, 6, D), lambda b, i: (b, 0, 0)),
                  pl.BlockSpec((D, D_FF), lambda b, i: (0, 0), pipeline_mode=once),
                  pl.BlockSpec((D, D_FF), lambda b, i: (0, 0), pipeline_mode=once),
                  pl.BlockSpec((D_FF, D), lambda b, i: (0, 0), pipeline_mode=once)],
        out_specs=tok,
        out_shape=jax.ShapeDtypeStruct((B, T, D), F32),
        compiler_params=_cparams(("parallel", "parallel"), FF_VMEM_LIMIT),
        name="ffn_dense",
    )(h2, x1, mod, w1, w3, w2)


MOE_TM = 512


def _moe_kernel(te_ref, tv_ref, h_ref, w1_ref, w3_ref, w2_ref, o_ref):
    valid = tv_ref[pl.program_id(0)] > 0

    @pl.when(valid)
    def _():
        o_ref[...] = _swiglu_tile(h_ref[...], w1_ref, w3_ref, w2_ref).astype(o_ref.dtype)

    @pl.when(jnp.logical_not(valid))
    def _():
        o_ref[...] = jnp.zeros_like(o_ref)


def moe_experts(hs, tile_expert, tile_valid, w1, w3, w2):
    P, D = hs.shape
    nt = P // MOE_TM
    tok = pl.BlockSpec((MOE_TM, D), lambda i, te, tv: (i, 0))
    return pl.pallas_call(
        _moe_kernel,
        grid_spec=pltpu.PrefetchScalarGridSpec(
            num_scalar_prefetch=2,
            grid=(nt,),
            in_specs=[tok,
                      pl.BlockSpec((None, D, D_FF), lambda i, te, tv: (te[i], 0, 0)),
                      pl.BlockSpec((None, D, D_FF), lambda i, te, tv: (te[i], 0, 0)),
                      pl.BlockSpec((None, D_FF, D), lambda i, te, tv: (te[i], 0, 0))],
            out_specs=tok),
        out_shape=jax.ShapeDtypeStruct((P, D), BF16),
        compiler_params=_cparams(("arbitrary",), FF_VMEM_LIMIT),
        name="moe_experts",
    )(tile_expert, tile_valid, hs, w1, w3, w2)


def _combine_kernel(x_ref, mod_ref, ya_ref, yb_ref, rt_ref, *rest):
    o_ref = rest[-1]
    wa = rt_ref[:, 2:3]
    wb = rt_ref[:, 3:4]
    o_ref[...] = x_ref[...] + mod_ref[5:6, :] * (wa * ya_ref[...] + wb * yb_ref[...])


def moe_combine(x1, mod, yab, rt, b0, out_prev, after, tm=512):
    B, T, D = x1.shape
    gb = yab.shape[1]
    tok = lambda n: pl.BlockSpec((None, tm, n), lambda b, i: (b + b0, i, 0))
    choice = lambda c: pl.BlockSpec((None, None, tm, D), lambda b, i: (c, b, i, 0))
    in_specs = [tok(D), pl.BlockSpec((None, 6, D), lambda b, i: (b + b0, 0, 0)), choice(0), choice(1),
                tok(rt.shape[-1])]
    args = [x1, mod, yab, yab, rt]
    aliases = {}
    if out_prev is not None:
        in_specs.append(pl.BlockSpec(memory_space=pl.ANY))
        args.append(out_prev)
        aliases = {len(args) - 1: 0}
    if after is not None:
        in_specs.append(pl.BlockSpec(memory_space=pl.ANY))
        args.append(after)
    return pl.pallas_call(
        _combine_kernel,
        grid=(gb, T // tm),
        in_specs=in_specs,
        out_specs=tok(D),
        out_shape=jax.ShapeDtypeStruct((B, T, D), F32),
        input_output_aliases=aliases,
        compiler_params=_cparams(("parallel", "parallel")),
        name="moe_combine",
    )(*args)


MOE_GROUPS = 2


def moe_layer(h2, x1, mod, rt, w1, w3, w2):
    B = x1.shape[0]
    gb = B // MOE_GROUPS
    weights = (w1.astype(BF16), w3.astype(BF16), w2.astype(BF16))
    groups = [_moe_group(h2, rt, b0, gb, *weights) for b0 in range(0, B, gb)]
    out = None
    for g, (ys, slot) in enumerate(groups):
        yab = ys.at[slot].get(mode='promise_in_bounds').reshape(2, gb, *x1.shape[1:])
        after = groups[g + 1][0] if g + 1 < len(groups) else None
        out = moe_combine(x1, mod, yab, rt, g * gb, out, after)
    return out


def _moe_group(h2, rt, b0, gb, w1, w3, w2):
    _, T, D = h2.shape
    n = gb * T
    rt = rt[b0:b0 + gb]
    e_flat = jnp.concatenate([rt[..., 0].reshape(n), rt[..., 1].reshape(n)]).astype(jnp.int32)
    order = jnp.argsort(e_flat, stable=True).astype(jnp.int32)
    rank = jnp.argsort(order).astype(jnp.int32)
    onehot = jax.nn.one_hot(e_flat, N_EXPERTS, dtype=jnp.int32)
    counts = jnp.sum(onehot, axis=0)
    padded = ((counts + MOE_TM - 1) // MOE_TM) * MOE_TM
    pend = jnp.cumsum(padded)
    shift = (pend - padded) - (jnp.cumsum(counts) - counts)
    P = 2 * n + N_EXPERTS * MOE_TM
    nt = P // MOE_TM
    tile_row = jnp.arange(nt, dtype=jnp.int32) * MOE_TM
    tile_expert = jnp.sum((tile_row[:, None] >= pend[None, :]).astype(jnp.int32), axis=1)
    tile_valid = (tile_expert < N_EXPERTS).astype(jnp.int32)
    tile_expert = jnp.minimum(tile_expert, N_EXPERTS - 1)
    slot = rank + jnp.sum(onehot * shift[None, :], axis=1)
    tile_shift = jnp.sum(jax.nn.one_hot(tile_expert, N_EXPERTS, dtype=jnp.int32) * shift[None, :], axis=1)
    row_rank = jnp.arange(P, dtype=jnp.int32) - jnp.repeat(tile_shift, MOE_TM)
    src = order[jnp.clip(row_rank, 0, 2 * n - 1)] % n
    take = lambda a, idx: a.at[idx].get(mode='promise_in_bounds')
    hs = take(h2.reshape(-1, D), src + b0 * T)
    ys = moe_experts(hs, tile_expert, tile_valid, w1, w3, w2)
    return ys, slot


def _hyena(hz, conv_w, conv_b, spec, bias):
    zc = short_conv(hz, conv_w, conv_b)
    B, L, C = zc.shape
    lmax = FFT_N // 2
    if L < lmax:
        slot = lmax // B
        assert slot >= 2 * L
        zc = jnp.pad(zc, ((0, 0), (0, slot - L), (0, 0))).reshape(1, lmax, C)
    y = gated_long_conv(zc, 0, zc, D_HYENA, bias[0], spec, 0)
    y = gated_long_conv(y, 0, zc, 2 * D_HYENA, bias[1], spec, D_HYENA)
    if L < lmax:
        y = y.reshape(B, lmax // B, D_HYENA)[:, :L]
    return y


def _split_route(outs, mod):
    x1, h2 = outs[0], outs[1]
    return h2, x1, mod, (outs[2] if len(outs) > 2 else None)


def kernel(x, c, ctx, c_ctx, w_ada, b_ada, g_norm1, g_norm2, w_in, w_out, g_q, g_k, rpb, hy_conv_w, hy_conv_b, filt_w1, filt_b1, filt_w2, filt_b2, filt_w3, filt_b3, filt_freq, filt_w_out, hy_bias, lam_re, lam_im, log_step, b_re, b_im, c_re, c_im, d_skip, w_glu, b_glu, ffn_w1, ffn_w3, ffn_w2, router_w, moe_w1, moe_w3, moe_w2):
    B, L, D = x.shape
    Lc = ctx.shape[1]
    c8 = jnp.zeros((8, D), F32).at[:B].set(c).at[B].set(c_ctx)
    mod_all = adaln_mod(c8, w_ada, b_ada)
    for layer in range(DEPTH):
        last = layer == DEPTH - 1
        m = mod_all[layer].reshape(8, 6, D)
        mod_x = m[:B]
        mod_c = jnp.broadcast_to(m[B], (B, 6, D))
        w_l = w_in[layer].astype(BF16)
        wo_l = w_out[layer].astype(BF16)
        filt = (filt_w1[layer], filt_b1[layer], filt_w2[layer], filt_b2[layer],
                filt_w3[layer], filt_b3[layer], filt_freq[layer], filt_w_out[layer])
        ssm = (lam_re[layer], lam_im[layer], log_step[layer], b_re[layer], b_im[layer], c_re[layer], c_im[layer])

        q_c, k_c, v_c, hz_c, u_c = in_proj(ctx, mod_c, g_norm1[layer], w_l, g_q[layer], g_k[layer])
        q, k, v, hz, u = in_proj(x, mod_x, g_norm1[layer], w_l, g_q[layer], g_k[layer])
        att = neighbourhood_attention(q, k, v, k_c, v_c, rpb[layer])
        spec = filter_spectrum(hyena_filter_taps(L, *filt))
        hy = _hyena(hz, hy_conv_w[layer], hy_conv_b[layer], spec, hy_bias[layer])
        un, yf, yb = s5_scan(u, u_c, *ssm)

        tail = (d_skip[layer], w_glu[layer], b_glu[layer], wo_l, g_norm2[layer])
        i = layer // 2
        if layer % 2 == 0:
            ffw = (ffn_w1[i], ffn_w3[i], ffn_w2[i])
            mixer = lambda h2, x1, mod, rt: ffn_dense(h2, x1, mod, *ffw)
            rw = None
        else:
            mow = (moe_w1[i], moe_w3[i], moe_w2[i])
            mixer = lambda h2, x1, mod, rt: moe_layer(h2, x1, mod, rt, *mow)
            rw = router_w[i]
        x = mixer(*_split_route(out_proj(x, mod_x, att, hy, yf, yb, un, 0, *tail, router_w=rw), mod_x))

        if not last:
            att_c = context_attention(q_c, k_c, v_c)
            spec_c = filter_spectrum(hyena_filter_taps(Lc, *filt))
            hy_c = _hyena(hz_c, hy_conv_w[layer], hy_conv_b[layer], spec_c, hy_bias[layer])
            ctx = mixer(*_split_route(out_proj(ctx, mod_c, att_c, hy_c, yf, yb, un, L, *tail, router_w=rw), mod_c))
    return x
```

```python
import functools
import math

import numpy as np
import jax
import jax.numpy as jnp
from jax import lax
from jax.experimental import pallas as pl
from jax.experimental.pallas import tpu as pltpu

F32 = jnp.float32
BF16 = jnp.bfloat16
HIGHEST = lax.Precision.HIGHEST

D_MODEL = 1024
DEPTH = 2
GRID_W = 64
ATT_HEAD_DIM = 64
ATT_HEADS = 8
D_ATT = 512
WIN_ROWS = 8
WIN_COLS = 16
D_HYENA = 256
HYENA_ORDER = 2
FILTER_EMB = 33
FILTER_BANDS = 16
FILTER_WIDTH = 64
MIN_DECAY = math.log(1e-2) / 1.5
MAX_DECAY = math.log(1e-2) / 0.3
D_SSM = 256
SSM_GROUP_WIDTH = 16
SSM_GROUPS = 16
SSM_STATE = 64
COL_K = D_ATT
COL_V = 2 * D_ATT
COL_HY = 3 * D_ATT
COL_SSM = COL_HY + 3 * D_HYENA
D_IN = COL_SSM + D_SSM
D_FF = 2816
N_EXPERTS = 8
EPS = 1e-6
NEG = -1e30

VMEM_LIMIT = 56 * 1024 * 1024


def _cparams(sem, vmem=None):
    return pltpu.CompilerParams(dimension_semantics=sem, vmem_limit_bytes=vmem)


def _dot(a, b):
    return jnp.dot(a, b, preferred_element_type=F32)


def _dot_nt(a, b):
    return lax.dot_general(a, b, (((1,), (1,)), ((), ())), preferred_element_type=F32)


def _mod_kernel(c_ref, w_ref, b_ref, o_ref):
    c = c_ref[...]
    s = c * jax.nn.sigmoid(c)
    o_ref[...] = jnp.dot(s, w_ref[...], precision=HIGHEST, preferred_element_type=F32) + b_ref[...]


def adaln_mod(c8, w_ada, b_ada, tn=512):
    depth, d, n = w_ada.shape
    return pl.pallas_call(
        _mod_kernel,
        grid=(depth, n // tn),
        in_specs=[pl.BlockSpec((8, d), lambda l, j: (0, 0)),
                  pl.BlockSpec((None, d, tn), lambda l, j: (l, 0, j)),
                  pl.BlockSpec((None, 1, tn), lambda l, j: (l, 0, j))],
        out_specs=pl.BlockSpec((None, 8, tn), lambda l, j: (l, 0, j)),
        out_shape=jax.ShapeDtypeStruct((depth, 8, n), F32),
        compiler_params=_cparams(("parallel", "parallel")),
        name="adaln_mod",
    )(c8, w_ada, b_ada.reshape(depth, 1, n))


def _rms_mod(x, g, shift, scale):
    ms = jnp.mean(x * x, axis=-1, keepdims=True)
    h = x * lax.rsqrt(ms + EPS) * g
    return h * (1.0 + scale) + shift


def _head_norm(z, a, g):
    zz = (z * z).astype(BF16)
    m = jnp.concatenate([_dot(zz[:, c:c + ATT_LANES], a) for c in range(0, z.shape[1], ATT_LANES)], axis=1)
    return z * lax.rsqrt(m + EPS) * g


TOK_TB = 128


def _inproj_kernel(x_ref, mod_ref, g_ref, w_ref, a_ref, gq_ref, gk_ref,
                   q_ref, k_ref, v_ref, hz_ref, u_ref):
    B, tb, _ = x_ref.shape
    h = jnp.concatenate([_rms_mod(x_ref[b], g_ref[...], mod_ref[b, 0:1, :], mod_ref[b, 1:2, :]).astype(BF16)
                         for b in range(B)], axis=0)
    a = a_ref[...]
    q = (_head_norm(_dot(h, w_ref[:, 0:COL_K]), a, gq_ref[...]) * (ATT_HEAD_DIM ** -0.5)).astype(BF16)
    k = _head_norm(_dot(h, w_ref[:, COL_K:COL_V]), a, gk_ref[...]).astype(BF16)
    v = _dot(h, w_ref[:, COL_V:COL_HY]).astype(BF16)
    hz = _dot(h, w_ref[:, COL_HY:COL_SSM])
    u = _dot(h, w_ref[:, COL_SSM:D_IN])
    for b in range(B):
        rows = slice(b * tb, (b + 1) * tb)
        q_ref[b] = q[rows]
        k_ref[b] = k[rows]
        v_ref[b] = v[rows]
        hz_ref[b] = hz[rows].astype(hz_ref.dtype)
        for s in range(D_SSM // 128):
            u_ref[s, pl.ds(b, tb, stride=B), :] = u[rows, s * 128:(s + 1) * 128]


def in_proj(x, mod, g1, w_in_bf, gq, gk):
    B, T, D = x.shape
    tb = TOK_TB
    head_avg = jnp.asarray(np.kron(np.eye(ATT_LANES // ATT_HEAD_DIM),
                                   np.full((ATT_HEAD_DIM, ATT_HEAD_DIM), 1.0 / ATT_HEAD_DIM)), BF16)
    tok = lambda n: pl.BlockSpec((B, tb, n), lambda i: (0, i, 0))
    const = lambda shape: pl.BlockSpec(shape, lambda i: (0,) * len(shape))
    return pl.pallas_call(
        _inproj_kernel,
        grid=(T // tb,),
        in_specs=[tok(D), const((B, 6, D)), const((1, D)),
                  const((D, D_IN)), const((ATT_LANES, ATT_LANES)), const((1, D_ATT)), const((1, D_ATT))],
        out_specs=[tok(D_ATT), tok(D_ATT), tok(D_ATT), tok(3 * D_HYENA),
                   pl.BlockSpec((D_SSM // 128, tb * B, 128), lambda i: (0, i, 0))],
        out_shape=[jax.ShapeDtypeStruct((B, T, D_ATT), BF16)] * 3
                  + [jax.ShapeDtypeStruct((B, T, 3 * D_HYENA), BF16),
                     jax.ShapeDtypeStruct((D_SSM // 128, T * B, 128), F32)],
        compiler_params=_cparams(("parallel",), VMEM_LIMIT),
        name="in_proj",
    )(x, mod, g1.reshape(1, D), w_in_bf, head_avg,
      jnp.tile(gq, ATT_HEADS).reshape(1, D_ATT), jnp.tile(gk, ATT_HEADS).reshape(1, D_ATT))


ATT_R = 4
ATT_KROWS = ATT_R + WIN_ROWS - 1
ATT_LANES = 256


def _softmax_heads(q, parts, bias_ref, o_ref):
    nh = ATT_LANES // ATT_HEAD_DIM
    lane = lax.broadcasted_iota(jnp.int32, (1, ATT_LANES), 1)
    head = lambda h: (lane >= ATT_HEAD_DIM * h) & (lane < ATT_HEAD_DIM * (h + 1))
    acc = jnp.zeros((q.shape[0], ATT_LANES), F32)
    for h in range(nh):
        hm = head(h)
        qh = jnp.where(hm, q, jnp.zeros_like(q))
        ss = [_dot_nt(qh, kk).astype(BF16) for kk, _ in parts]
        if bias_ref is not None:
            ss[0] = ss[0] + bias_ref[h]
        m = ss[0].max(axis=-1, keepdims=True)
        for s in ss[1:]:
            m = jnp.maximum(m, s.max(axis=-1, keepdims=True))
        om = head((h + 1) % nh)
        o = None
        for s, (_, vv) in zip(ss, parts):
            part = _dot(jnp.exp(s - m), jnp.where(om, jnp.ones_like(vv), vv))
            o = part if o is None else o + part
        l = pltpu.roll(o, ATT_LANES - ATT_HEAD_DIM, axis=1)
        acc = jnp.where(hm, o / l, acc)
    o_ref[...] = acc.astype(o_ref.dtype)


ATT_SUB = 8


def _nattn_kernel(q_ref, k_ref, v_ref, kc_ref, vc_ref, *rest, rows):
    bias_refs, o_ref = rest[:ATT_SUB], rest[ATT_SUB]
    tq = ATT_R * GRID_W
    ctx = (kc_ref[...], vc_ref[...])
    for sub in range(ATT_SUB):
        blk = pl.program_id(2) * ATT_SUB + sub
        ks = jnp.clip(blk * ATT_R - WIN_ROWS // 2, 0, rows - ATT_KROWS)
        start = pl.multiple_of(ks * GRID_W, GRID_W)
        kl = k_ref[pl.ds(start, ATT_KROWS * GRID_W), :]
        vl = v_ref[pl.ds(start, ATT_KROWS * GRID_W), :]
        qrows = slice(sub * tq, (sub + 1) * tq)
        _softmax_heads(q_ref[qrows, :], [(kl, vl), ctx], bias_refs[sub], o_ref.at[qrows, :])


def _attn_bias_tables(rpb, rows):
    nblk = rows // ATT_R
    qc = np.arange(GRID_W)
    c0 = np.clip(qc - WIN_COLS // 2, 0, GRID_W - WIN_COLS)
    kc = np.arange(GRID_W)
    col_ok = (kc[None, :] >= c0[:, None]) & (kc[None, :] < c0[:, None] + WIN_COLS)
    col_off = kc[None, :] - qc[:, None] + WIN_COLS - 1
    col_sel = (col_off[:, :, None] == np.arange(2 * WIN_COLS - 1)) & col_ok[:, :, None]
    band = jnp.einsum('hij,qkj->hqik', rpb, jnp.asarray(col_sel, F32), precision=HIGHEST)
    band = jnp.where(jnp.asarray(col_ok)[None, :, None, :], band, NEG)
    band = band.reshape(ATT_HEADS, GRID_W, (2 * WIN_ROWS - 1) * GRID_W).astype(BF16)
    neg = lambda nrows: jnp.full((ATT_HEADS, GRID_W, nrows * GRID_W), NEG, BF16)
    tabs = []
    for blk in (0, 1, nblk - 1):
        ks = int(np.clip(blk * ATT_R - WIN_ROWS // 2, 0, rows - ATT_KROWS))
        qrows = []
        for r in range(blk * ATT_R, (blk + 1) * ATT_R):
            r0 = int(np.clip(r - WIN_ROWS // 2, 0, rows - WIN_ROWS))
            first = r0 - r + WIN_ROWS - 1
            win = band[:, :, first * GRID_W:(first + WIN_ROWS) * GRID_W]
            qrows.append(jnp.concatenate([neg(r0 - ks), win, neg(ATT_KROWS - WIN_ROWS - (r0 - ks))], axis=-1))
        tabs.append(jnp.stack(qrows, axis=1).reshape(ATT_HEADS, ATT_R * GRID_W, ATT_KROWS * GRID_W))
    return jnp.stack(tabs)


def neighbourhood_attention(q, k, v, kc, vc, rpb):
    B, L, _ = q.shape
    Lc = kc.shape[1]
    rows = L // GRID_W
    nblk = rows // ATT_R
    tq = ATT_R * GRID_W
    nkl = ATT_KROWS * GRID_W
    bias = _attn_bias_tables(rpb, rows)
    hpb = ATT_LANES // ATT_HEAD_DIM
    variant = lambda blk: jnp.where(blk == 0, 0, jnp.where(blk == nblk - 1, 2, 1))
    bias_spec = lambda sub: pl.BlockSpec((None, hpb, tq, nkl),
                                         lambda b, g, j: (variant(j * ATT_SUB + sub), g, 0, 0))
    return pl.pallas_call(
        functools.partial(_nattn_kernel, rows=rows),
        grid=(B, D_ATT // ATT_LANES, nblk // ATT_SUB),
        in_specs=[pl.BlockSpec((None, ATT_SUB * tq, ATT_LANES), lambda b, g, j: (b, j, g)),
                  pl.BlockSpec((None, L, ATT_LANES), lambda b, g, j: (b, 0, g)),
                  pl.BlockSpec((None, L, ATT_LANES), lambda b, g, j: (b, 0, g)),
                  pl.BlockSpec((None, Lc, ATT_LANES), lambda b, g, j: (b, 0, g)),
                  pl.BlockSpec((None, Lc, ATT_LANES), lambda b, g, j: (b, 0, g))]
                 + [bias_spec(sub) for sub in range(ATT_SUB)],
        out_specs=pl.BlockSpec((None, ATT_SUB * tq, ATT_LANES), lambda b, g, j: (b, j, g)),
        out_shape=jax.ShapeDtypeStruct((B, L, D_ATT), BF16),
        compiler_params=_cparams(("parallel", "parallel", "arbitrary"), VMEM_LIMIT),
        name="nattn",
    )(q, k, v, kc, vc, *([bias] * ATT_SUB))


def _cattn_kernel(q_ref, k_ref, v_ref, o_ref):
    _softmax_heads(q_ref[...], [(k_ref[...], v_ref[...])], None, o_ref)


def context_attention(q, k, v):
    B, Lc, _ = q.shape
    spec = pl.BlockSpec((None, Lc, ATT_LANES), lambda b, g: (b, 0, g))
    return pl.pallas_call(
        _cattn_kernel,
        grid=(B, D_ATT // ATT_LANES),
        in_specs=[spec, spec, spec],
        out_specs=spec,
        out_shape=jax.ShapeDtypeStruct((B, Lc, D_ATT), BF16),
        compiler_params=_cparams(("parallel", "parallel")),
        name="cattn",
    )(q, k, v)


FFT_N = 16384
FFT_R = 128
FFT_K1 = FFT_R // 2 + 1
FFT_K1P = 66
FFT_PITCH = 2 * FFT_K1P


def _filter_kernel(w1_ref, wt_ref, b1_ref, w2_ref, b2_ref, w3_ref, b3_ref, fr_ref, wo_ref, o_ref, *, lf, tile):
    i = pl.program_id(0)
    half = FFT_N // 2
    row0 = i * tile
    first_pos = jnp.where(row0 < half, row0, FFT_N - row0 - tile)

    @pl.when(first_pos >= lf)
    def _():
        o_ref[...] = jnp.zeros_like(o_ref)

    @pl.when(first_pos < lf)
    def _():
        rows = tile // FILTER_PACK

        def position(width):
            r = lax.broadcasted_iota(jnp.int32, (rows, FILTER_PACK * width), 0)
            g = lax.broadcasted_iota(jnp.int32, (rows, FILTER_PACK * width), 1) // width
            j = row0 + r + g * rows
            return jnp.where(j < half, j, FFT_N - 1 - j)

        nfeat = 2 * FILTER_BANDS
        feat = lax.broadcasted_iota(jnp.int32, (1, FILTER_PACK * nfeat), 1) % nfeat
        f = 1e-4 + (feat % FILTER_BANDS).astype(F32) * ((FILTER_BANDS - 1 - 1e-4) / (FILTER_BANDS - 1))
        w = (2.0 * math.pi / lf) * position(nfeat).astype(F32)
        z = jnp.cos(f * w + jnp.where(feat < FILTER_BANDS, 0.0, 0.5 * math.pi))
        t_h = position(FILTER_WIDTH).astype(F32) / (lf - 1.0)
        fr = fr_ref[...]
        hdot = lambda a, b: jnp.dot(a, b, precision=HIGHEST, preferred_element_type=F32)
        h = jnp.sin(fr * (hdot(z, w1_ref[...]) + t_h * wt_ref[...] + b1_ref[...]))
        h = jnp.sin(fr * (hdot(h, w2_ref[...]) + b2_ref[...]))
        h = jnp.sin(fr * (hdot(h, w3_ref[...]) + b3_ref[...]))
        y = hdot(h, wo_ref[...])
        nout = HYENA_ORDER * D_HYENA
        pos_o = position(nout)
        ch = lax.broadcasted_iota(jnp.int32, (1, FILTER_PACK * nout), 1) % D_HYENA
        delta = jnp.abs(MIN_DECAY + ch.astype(F32) * ((MAX_DECAY - MIN_DECAY) / (D_HYENA - 1)))
        y = jnp.where(pos_o < lf, y * jnp.exp(-(pos_o.astype(F32) / (lf - 1.0)) * delta), 0.0)
        for g in range(FILTER_PACK):
            o_ref[g * rows:(g + 1) * rows, :] = y[:, g * nout:(g + 1) * nout]


FILTER_PACK = 4


def hyena_filter_taps(lf, w1, b1, w2, b2, w3, b3, freq, w_out, tile=1024):
    P, W = FILTER_PACK, FILTER_WIDTH
    nout = HYENA_ORDER * D_HYENA
    bd = lambda m: jnp.kron(jnp.eye(P, dtype=F32), m.astype(F32))
    wide = lambda a: jnp.tile(a.astype(F32), P).reshape(1, P * W)
    wo = w_out.reshape(W, HYENA_ORDER, 2, D_HYENA).transpose(2, 0, 1, 3).reshape(2, W, nout)
    wo = jnp.stack([bd(wo[0]), bd(wo[1])])
    nt = FFT_N // tile
    const = lambda shape: pl.BlockSpec(shape, lambda i: (0,) * len(shape))
    return pl.pallas_call(
        functools.partial(_filter_kernel, lf=lf, tile=tile),
        grid=(nt,),
        in_specs=[const((P * (FILTER_EMB - 1), P * W)), const((1, P * W)), const((1, P * W)),
                  const((P * W, P * W)), const((1, P * W)), const((P * W, P * W)), const((1, P * W)),
                  const((1, P * W)),
                  pl.BlockSpec((None, P * W, P * nout), lambda i: (i // (nt // 2), 0, 0))],
        out_specs=pl.BlockSpec((tile, nout), lambda i: (i, 0)),
        out_shape=jax.ShapeDtypeStruct((FFT_N, nout), F32),
        compiler_params=_cparams(("parallel",)),
        name="hyena_filter",
    )(bd(w1[1:]), wide(w1[0]), wide(b1), bd(w2), wide(b2), bd(w3), wide(b3), wide(freq), wo)


def _shortconv_kernel(z_ref, w_ref, b_ref, o_ref, zp_ref, *, L):
    C = z_ref.shape[-1]
    zp_ref[0:8, :] = jnp.zeros((8, C), F32)
    zp_ref[L + 8:L + 16, :] = jnp.zeros((8, C), F32)
    zp_ref[8:L + 8, :] = z_ref[...].astype(F32)
    ch = min(L, 512)
    for c in range(L // ch):
        s = c * ch
        o_ref[s:s + ch, :] = (b_ref[...] + zp_ref[s + 7:s + 7 + ch, :] * w_ref[0:1, :]
                              + zp_ref[s + 8:s + 8 + ch, :] * w_ref[1:2, :]
                              + zp_ref[s + 9:s + 9 + ch, :] * w_ref[2:3, :]).astype(o_ref.dtype)


def short_conv(z, w, b, cb=128):
    B, L, C = z.shape
    return pl.pallas_call(
        functools.partial(_shortconv_kernel, L=L),
        grid=(B, C // cb),
        in_specs=[pl.BlockSpec((None, L, cb), lambda b_, c: (b_, 0, c)),
                  pl.BlockSpec((3, cb), lambda b_, c: (0, c)),
                  pl.BlockSpec((1, cb), lambda b_, c: (0, c))],
        out_specs=pl.BlockSpec((None, L, cb), lambda b_, c: (b_, 0, c)),
        out_shape=jax.ShapeDtypeStruct((B, L, C), BF16),
        scratch_shapes=[pltpu.VMEM((L + 16, cb), F32)],
        compiler_params=_cparams(("parallel", "parallel")),
        name="short_conv",
    )(z, w, b.reshape(1, C))


@functools.lru_cache(maxsize=None)
def _fft_tables(n1_in):
    n2 = np.arange(FFT_R)[:, None, None]
    k1 = np.arange(FFT_K1)[None, :, None]
    n1 = np.arange(n1_in)[None, None, :]
    th = 2.0 * np.pi * ((k1 * (FFT_R * n1 + n2)) % FFT_N) / FFT_N
    ga = np.zeros((FFT_R, 2 * FFT_K1P, n1_in))
    ga[:, :FFT_K1] = np.cos(th)
    ga[:, FFT_K1P:FFT_K1P + FFT_K1] = -np.sin(th)
    wk = np.where((np.arange(FFT_K1) == 0) | (np.arange(FFT_K1) == FFT_R // 2), 1.0, 2.0)[None, None, :]
    thd = th.transpose(0, 2, 1)
    gd = np.zeros((FFT_R, n1_in, 2 * FFT_K1P))
    gd[:, :, :FFT_K1] = wk * np.cos(thd)
    gd[:, :, FFT_K1P:FFT_K1P + FFT_K1] = -wk * np.sin(thd)
    a = np.arange(FFT_R)
    ph = 2.0 * np.pi * ((a[:, None] * a[None, :]) % FFT_R) / FFT_R
    c, s = np.cos(ph), np.sin(ph)
    fb = np.block([[c, s], [-s, c]])
    fc = np.block([[c, -s], [s, c]])
    return ga, gd, fb, fc


FFT_KG = 13
FFT_UNROLL = 16


def _spectrum_slabs(ya_ref, fb, k0):
    cols = []
    for g in range(FFT_KG):
        re = ya_ref[pl.ds(k0 + g, FFT_R, stride=FFT_PITCH), :]
        im = ya_ref[pl.ds(FFT_K1P + k0 + g, FFT_R, stride=FFT_PITCH), :]
        cols.append(jnp.concatenate([re, im], axis=0).astype(BF16))
    return _dot(fb, jnp.concatenate(cols, axis=1))


def _stage_a(x_ref, x_pitch, ga_ref, ya_ref, n1_in):
    def body(n2, carry):
        xs = x_ref[pl.ds(n2, n1_in, stride=x_pitch), :].astype(BF16)
        ya_ref[pl.ds(n2 * FFT_PITCH, FFT_PITCH), :] = _dot(ga_ref[n2], xs)
        return carry
    lax.fori_loop(0, FFT_R, body, 0, unroll=FFT_UNROLL)


def _spectrum_kernel(x_ref, ga_ref, fb_ref, o_ref, ya_ref):
    _stage_a(x_ref, FFT_R, ga_ref, ya_ref, FFT_R)
    fb = fb_ref[...]
    cb = ya_ref.shape[1]

    def body(kg, carry):
        x = _spectrum_slabs(ya_ref, fb, kg * FFT_KG)
        for g in range(FFT_KG):
            o_ref[kg * FFT_KG + g] = x[:, g * cb:(g + 1) * cb]
        return carry
    lax.fori_loop(0, FFT_K1 // FFT_KG, body, 0)


def filter_spectrum(taps, cb=128):
    C = taps.shape[1]
    ga, _, fb, _ = _fft_tables(FFT_R)
    return pl.pallas_call(
        _spectrum_kernel,
        grid=(C // cb,),
        in_specs=[pl.BlockSpec((FFT_N, cb), lambda c: (0, c)),
                  pl.BlockSpec((FFT_R, FFT_PITCH, FFT_R), lambda c: (0, 0, 0)),
                  pl.BlockSpec((2 * FFT_R, 2 * FFT_R), lambda c: (0, 0))],
        out_specs=pl.BlockSpec((FFT_K1, 2 * FFT_R, cb), lambda c: (0, 0, c)),
        out_shape=jax.ShapeDtypeStruct((FFT_K1, 2 * FFT_R, C), F32),
        scratch_shapes=[pltpu.VMEM((FFT_R * FFT_PITCH, cb), F32)],
        compiler_params=_cparams(("parallel",), VMEM_LIMIT),
        name="filter_spectrum",
    )(taps, _bf16_table(ga), _bf16_table(fb))


def _bf16_table(t):
    return jnp.asarray(t, F32).astype(BF16)


def _longconv_kernel(u_ref, gate_ref, bias_ref, spec_ref, ga_ref, gd_ref, fb_ref, fc_ref, o_ref, ya_ref, xp_ref):
    n1_in = FFT_R // 2

    def repitch(n1, carry):
        xp_ref[pl.ds(n1 * FFT_PITCH, FFT_R), :] = (
            u_ref[pl.ds(pl.multiple_of(n1 * FFT_R, FFT_R), FFT_R), :].astype(F32))
        return carry
    lax.fori_loop(0, n1_in, repitch, 0, unroll=FFT_UNROLL)

    _stage_a(xp_ref, FFT_PITCH, ga_ref, ya_ref, n1_in)
    fb = fb_ref[...]
    fc = fc_ref[...]

    cb = ya_ref.shape[1]

    def mid(kg, carry):
        k0 = kg * FFT_KG
        x = _spectrum_slabs(ya_ref, fb, k0)
        s = jnp.concatenate([spec_ref[k0 + g] for g in range(FFT_KG)], axis=1)
        xr, xi = x[:FFT_R], x[FFT_R:]
        sr, si = s[:FFT_R], s[FFT_R:]
        z = jnp.concatenate([xr * sr - xi * si, xr * si + xi * sr], axis=0).astype(BF16)
        v = _dot(fc, z)
        for g in range(FFT_KG):
            ya_ref[pl.ds(k0 + g, FFT_R, stride=FFT_PITCH), :] = v[:FFT_R, g * cb:(g + 1) * cb]
            ya_ref[pl.ds(FFT_K1P + k0 + g, FFT_R, stride=FFT_PITCH), :] = v[FFT_R:, g * cb:(g + 1) * cb]
        return carry
    lax.fori_loop(0, FFT_K1 // FFT_KG, mid, 0)

    def last(n2, carry):
        slab = ya_ref[pl.ds(n2 * FFT_PITCH, FFT_PITCH), :].astype(BF16)
        xp_ref[pl.ds(n2, n1_in, stride=FFT_PITCH), :] = _dot(gd_ref[n2], slab)
        return carry
    lax.fori_loop(0, FFT_R, last, 0, unroll=FFT_UNROLL)

    bias = bias_ref[...]

    def gate(n1, carry):
        rows = pl.ds(pl.multiple_of(n1 * FFT_R, FFT_R), FFT_R)
        y = xp_ref[pl.ds(n1 * FFT_PITCH, FFT_R), :] * (1.0 / FFT_N)
        o_ref[rows, :] = (gate_ref[rows, :].astype(F32) * (y + u_ref[rows, :].astype(F32) * bias)).astype(o_ref.dtype)
        return carry
    lax.fori_loop(0, n1_in, gate, 0, unroll=FFT_UNROLL)


def gated_long_conv(u, u_col, gate, gate_col, bias, spec, spec_col, C=D_HYENA, cb=128):
    B, L, _ = u.shape
    n1_in = FFT_R // 2
    ga, gd, fb, fc = _fft_tables(n1_in)
    once = pl.Buffered(1)
    seq = lambda col: pl.BlockSpec((None, L, cb), lambda c, b: (b, 0, c + col // cb))
    return pl.pallas_call(
        _longconv_kernel,
        grid=(C // cb, B),
        in_specs=[seq(u_col), seq(gate_col),
                  pl.BlockSpec((1, cb), lambda c, b: (0, c)),
                  pl.BlockSpec((FFT_K1, 2 * FFT_R, cb), lambda c, b: (0, 0, c + spec_col // cb), pipeline_mode=once),
                  pl.BlockSpec((FFT_R, FFT_PITCH, n1_in), lambda c, b: (0, 0, 0), pipeline_mode=once),
                  pl.BlockSpec((FFT_R, n1_in, FFT_PITCH), lambda c, b: (0, 0, 0), pipeline_mode=once),
                  pl.BlockSpec((2 * FFT_R, 2 * FFT_R), lambda c, b: (0, 0)),
                  pl.BlockSpec((2 * FFT_R, 2 * FFT_R), lambda c, b: (0, 0))],
        out_specs=seq(0),
        out_shape=jax.ShapeDtypeStruct((B, L, C), BF16),
        scratch_shapes=[pltpu.VMEM((FFT_R * FFT_PITCH, cb), F32), pltpu.VMEM((n1_in * FFT_PITCH, cb), F32)],
        compiler_params=_cparams(("parallel", "arbitrary"), VMEM_LIMIT),
        name="long_conv",
    )(u, gate, bias.reshape(1, C), spec, _bf16_table(ga), _bf16_table(gd), _bf16_table(fb), _bf16_table(fc))


SSM_N = SSM_GROUPS * SSM_STATE
S5_SEQS = 8
S5_CHUNK = 256


def _s5_kernel(uf_ref, ub_ref, bf_ref, bb_ref, lam_ref, cf_ref, cb_ref, yf_ref, yb_ref,
               buf_ref, bub_ref, xsf_ref, xsb_ref, st_ref):
    half = S5_SEQS // 2
    tiles = S5_CHUNK // 2

    @pl.when(pl.program_id(0) == 0)
    def _():
        st_ref[...] = jnp.zeros_like(st_ref)

    slabs = lambda ref: jnp.concatenate([ref[s] for s in range(ref.shape[0])], axis=1).astype(BF16)
    buf_ref[...] = _dot(slabs(uf_ref), bf_ref[...])
    bub_ref[...] = _dot(slabs(ub_ref), bb_ref[...])
    lo = lax.broadcasted_iota(jnp.int32, (S5_SEQS, 1), 0) < half

    def advance(x, lam, b):
        xr, xi = x
        lr, li = lam[:, 0:SSM_N], lam[:, SSM_N:2 * SSM_N]
        return lr * xr - li * xi + b[:, 0:SSM_N], lr * xi + li * xr + b[:, SSM_N:2 * SSM_N]

    def swap(x):
        return pltpu.roll(x[0], half, axis=0), pltpu.roll(x[1], half, axis=0)

    def step(t4, x):
        f_tiles, b_tiles = [], []
        for p in range(2):
            tf = buf_ref[pl.ds(pl.multiple_of((2 * t4 + p) * S5_SEQS, S5_SEQS), S5_SEQS), :]
            tb = bub_ref[pl.ds(pl.multiple_of((tiles - 1 - 2 * t4 - p) * S5_SEQS, S5_SEQS), S5_SEQS), :]
            xe = advance(x, lam_ref[0], jnp.where(lo, tf, tb))
            xo = advance(swap(xe), lam_ref[1], jnp.where(lo, tb, tf))
            x = swap(xo)
            xe = jnp.concatenate(xe, axis=1)
            xo = jnp.concatenate(xo, axis=1)
            f_tiles.append(jnp.where(lo, xe, xo))
            b_tiles.append(jnp.where(lo, xo, xe))
        xsf_ref[pl.ds(pl.multiple_of(t4 * 2 * S5_SEQS, 2 * S5_SEQS), 2 * S5_SEQS), :] = (
            jnp.concatenate(f_tiles, axis=0).astype(BF16))
        xsb_ref[pl.ds(pl.multiple_of((tiles - 2 - 2 * t4) * S5_SEQS, 2 * S5_SEQS), 2 * S5_SEQS), :] = (
            jnp.concatenate(b_tiles[::-1], axis=0).astype(BF16))
        return x

    xr, xi = lax.fori_loop(0, S5_CHUNK // 4, step, (st_ref[:, 0:SSM_N], st_ref[:, SSM_N:2 * SSM_N]))
    st_ref[:, 0:SSM_N] = xr
    st_ref[:, SSM_N:2 * SSM_N] = xi
    yf = _dot(xsf_ref[...], cf_ref[...])
    yb = _dot(xsb_ref[...], cb_ref[...])
    for s in range(yf_ref.shape[0]):
        yf_ref[s] = yf[:, s * 128:(s + 1) * 128]
        yb_ref[s] = yb[:, s * 128:(s + 1) * 128]


def _s5_matrices(lam_re, lam_im, log_step, b_re, b_im, c_re, c_im):
    G, P, Hs = SSM_GROUPS, SSM_STATE, SSM_GROUP_WIDTH
    dt = jnp.exp(log_step)[:, :, None]
    mag = jnp.exp(lam_re * dt)
    ar, ai = mag * jnp.cos(lam_im * dt), mag * jnp.sin(lam_im * dt)
    er, ei = ar - 1.0, ai
    den = lam_re * lam_re + lam_im * lam_im
    cr, ci = (er * lam_re + ei * lam_im) / den, (ei * lam_re - er * lam_im) / den
    bbr = cr[..., None] * b_re - ci[..., None] * b_im
    bbi = cr[..., None] * b_im + ci[..., None] * b_re
    eye = jnp.eye(G, dtype=F32)
    blk_in = lambda m: jnp.einsum('gq,qph->ghqp', eye, m).reshape(G * Hs, G * P)
    bmat = [jnp.concatenate([blk_in(bbr[d]), blk_in(bbi[d])], axis=1).astype(BF16) for d in range(2)]
    blk_out = lambda m: jnp.einsum('gq,qhp->gpqh', eye, m).reshape(G * P, G * Hs)
    cmat = [jnp.concatenate([blk_out(c_re[d]), -blk_out(c_im[d])], axis=0).astype(BF16) for d in range(2)]
    lam = jnp.concatenate([ar.reshape(2, 1, G * P), ai.reshape(2, 1, G * P)], axis=2)
    lam = jnp.broadcast_to(lam, (2, S5_SEQS // 2, 2 * G * P))
    lam8 = jnp.stack([lam.reshape(S5_SEQS, 2 * G * P), lam[::-1].reshape(S5_SEQS, 2 * G * P)])
    return bmat[0], bmat[1], lam8, cmat[0], cmat[1]


def s5_scan(u_x, u_c, lam_re, lam_im, log_step, b_re, b_im, c_re, c_im):
    S = u_x.shape[0]
    half = S5_SEQS // 2
    L, Lc = u_x.shape[1] // half, u_c.shape[1] // half
    assert L % S5_CHUNK == 0 and Lc % S5_CHUNK == 0
    bf, bb, lam8, cf, cb = _s5_matrices(lam_re, lam_im, log_step, b_re, b_im, c_re, c_im)
    un = jnp.concatenate([u_x, u_c], axis=1)
    rows = S5_CHUNK * half
    nx, nc = L // S5_CHUNK, Lc // S5_CHUNK
    n = nx + nc
    fwd = pl.BlockSpec((S, rows, 128), lambda i: (0, jnp.where(i < nc, nx + i, i - nc), 0))
    bwd = pl.BlockSpec((S, rows, 128), lambda i: (0, n - 1 - i, 0))
    const = lambda shape: pl.BlockSpec(shape, lambda i: (0,) * len(shape))
    yf, yb = pl.pallas_call(
        _s5_kernel,
        grid=(n,),
        in_specs=[fwd, bwd, const((D_SSM, 2 * SSM_N)), const((D_SSM, 2 * SSM_N)),
                  const((2, S5_SEQS, 2 * SSM_N)), const((2 * SSM_N, D_SSM)), const((2 * SSM_N, D_SSM))],
        out_specs=[fwd, bwd],
        out_shape=[jax.ShapeDtypeStruct(un.shape, F32)] * 2,
        scratch_shapes=[pltpu.VMEM((rows, 2 * SSM_N), F32), pltpu.VMEM((rows, 2 * SSM_N), F32),
                        pltpu.VMEM((rows, 2 * SSM_N), BF16), pltpu.VMEM((rows, 2 * SSM_N), BF16),
                        pltpu.VMEM((S5_SEQS, 2 * SSM_N), F32)],
        compiler_params=_cparams(("arbitrary",), VMEM_LIMIT),
        name="s5_scan",
    )(un, un, bf, bb, lam8, cf, cb)
    return un, yf, yb


def _gelu_tanh(y):
    return 0.5 * y * (1.0 + jnp.tanh(math.sqrt(2.0 / math.pi) * (y + 0.044715 * (y * y * y))))


def _outproj_kernel(*refs, route):
    (x_ref, mod_ref, att_ref, hy_ref, yf_ref, yb_ref, u_ref, dsk_ref, wg_ref, bg_ref, wo_ref, g2_ref) = refs[:12]
    if route:
        wr_ref, x1_ref, h2_ref, rt_ref = refs[12:]
    else:
        x1_ref, h2_ref = refs[12:]
    B, tb, D = x_ref.shape
    rows = lambda a: a.reshape(B * tb, a.shape[-1])

    def scan_rows(ref):
        return jnp.concatenate(
            [jnp.concatenate([ref[s, pl.ds(b, tb, stride=B), :] for s in range(ref.shape[0])], axis=1)
             for b in range(B)], axis=0)

    y = scan_rows(yf_ref) + scan_rows(yb_ref) + scan_rows(u_ref) * dsk_ref[...]
    y = _gelu_tanh(y)
    ss = y * jax.nn.sigmoid(_dot(y.astype(BF16), wg_ref[...]) + bg_ref[...])
    mix = (_dot(rows(att_ref[...]), wo_ref[0:D_ATT, :])
           + _dot(rows(hy_ref[...]).astype(BF16), wo_ref[D_ATT:D_ATT + D_HYENA, :])
           + _dot(ss.astype(BF16), wo_ref[D_ATT + D_HYENA:, :]))
    h2s = []
    for b in range(B):
        x1 = x_ref[b] + mod_ref[b, 2:3, :] * mix[b * tb:(b + 1) * tb]
        x1_ref[b] = x1
        h2s.append(_rms_mod(x1, g2_ref[...], mod_ref[b, 3:4, :], mod_ref[b, 4:5, :]))
        h2_ref[b] = h2s[-1].astype(h2_ref.dtype)
    if route:
        h2 = jnp.concatenate(h2s, axis=0)
        h_hi = h2.astype(BF16)
        h_lo = (h2 - h_hi.astype(F32)).astype(BF16)
        logits = _dot_nt(wr_ref[0], h_hi) + _dot_nt(wr_ref[0], h_lo) + _dot_nt(wr_ref[1], h_hi)
        row = lax.broadcasted_iota(jnp.int32, (ROUTE_ROWS, 1), 0)
        lg = jnp.where(row < N_EXPERTS, logits, -jnp.inf)
        v1 = lg.max(axis=0, keepdims=True)
        i1 = jnp.where(lg == v1, row, ROUTE_ROWS).min(axis=0, keepdims=True)
        lg2 = jnp.where(row == i1, -jnp.inf, lg)
        v2 = lg2.max(axis=0, keepdims=True)
        i2 = jnp.where(lg2 == v2, row, ROUTE_ROWS).min(axis=0, keepdims=True)
        e = jnp.exp(v2 - v1)
        w1 = 1.0 / (1.0 + e)
        out_row = lax.broadcasted_iota(jnp.int32, (8, 1), 0)
        rt_ref[...] = jnp.where(out_row == 0, i1.astype(F32),
                                jnp.where(out_row == 1, i2.astype(F32),
                                          jnp.where(out_row == 2, w1, jnp.where(out_row == 3, e * w1, 0.0))))


def out_proj(x, mod, att, hy, yf, yb, u, scan_row0, d_skip, w_glu, b_glu, w_out_bf, g2, router_w=None):
    B, T, D = x.shape
    tb = TOK_TB
    route = router_w is not None
    S = u.shape[0]
    tok = lambda n: pl.BlockSpec((B, tb, n), lambda i: (0, i, 0))
    scan = pl.BlockSpec((S, tb * B, 128), lambda i: (0, i + scan_row0 // tb, 0))
    const = lambda shape: pl.BlockSpec(shape, lambda i: (0,) * len(shape))
    in_specs = [tok(D), const((B, 6, D)), tok(D_ATT), tok(D_HYENA),
                scan, scan, scan, const((1, D_SSM)), const((D_SSM, D_SSM)), const((1, D_SSM)),
                const((D, D)), const((1, D))]
    args = [x, mod, att, hy, yf, yb, u, d_skip.reshape(1, D_SSM), w_glu.astype(BF16), b_glu.reshape(1, D_SSM),
            w_out_bf, g2.reshape(1, D)]
    out_specs = [tok(D), tok(D)]
    out_shape = [jax.ShapeDtypeStruct((B, T, D), F32), jax.ShapeDtypeStruct((B, T, D), BF16)]
    if route:
        in_specs.append(const((2, ROUTE_ROWS, D)))
        wr = jnp.pad(router_w.astype(F32).T, ((0, ROUTE_ROWS - N_EXPERTS), (0, 0)))
        wr_hi = wr.astype(BF16)
        args.append(jnp.stack([wr_hi, (wr - wr_hi.astype(F32)).astype(BF16)]))
        out_specs.append(pl.BlockSpec((None, 8, B * tb), lambda i: (i, 0, 0)))
        out_shape.append(jax.ShapeDtypeStruct((T // tb, 8, B * tb), F32))
    outs = pl.pallas_call(
        functools.partial(_outproj_kernel, route=route),
        grid=(T // tb,),
        in_specs=in_specs, out_specs=out_specs, out_shape=out_shape,
        compiler_params=_cparams(("parallel",), VMEM_LIMIT),
        name="out_proj",
    )(*args)
    outs = list(outs)
    if route:
        outs[2] = outs[2].reshape(T // tb, 8, B, tb).transpose(2, 0, 3, 1).reshape(B, T, 8)
    return outs


ROUTE_ROWS = 16


FF_CHUNK = 768
FF_VMEM_LIMIT = 60 * 1024 * 1024


def _swiglu_tile(h, w1_ref, w3_ref, w2_ref):
    acc = None
    for lo in range(0, D_FF, FF_CHUNK):
        hi = min(lo + FF_CHUNK, D_FF)
        a = _dot(h, w1_ref[:, lo:hi].astype(BF16))
        g = (a * jax.nn.sigmoid(a) * _dot(h, w3_ref[:, lo:hi].astype(BF16))).astype(BF16)
        part = _dot(g, w2_ref[lo:hi, :].astype(BF16))
        acc = part if acc is None else acc + part
    return acc


def _ffn_kernel(h_ref, x_ref, mod_ref, w1_ref, w3_ref, w2_ref, o_ref):
    o_ref[...] = x_ref[...] + mod_ref[5:6, :] * _swiglu_tile(h_ref[...], w1_ref, w3_ref, w2_ref)


def ffn_dense(h2, x1, mod, w1, w3, w2, tm=512):
    B, T, D = x1.shape
    tm = min(tm, T)
    tok = pl.BlockSpec((None, tm, D), lambda b, i: (b, i, 0))
    once = pl.Buffered(1)
    return pl.pallas_call(
        _ffn_kernel,
        grid=(B, T // tm),
        in_specs=[tok, tok, pl.BlockSpec((None, 6, D), lambda b, i: (b, 0, 0)),
                  pl.BlockSpec((D, D_FF), lambda b, i: (0, 0), pipeline_mode=once),
                  pl.BlockSpec((D, D_FF), lambda b, i: (0, 0), pipeline_mode=once),
                  pl.BlockSpec((D_FF, D), lambda b, i: (0, 0), pipeline_mode=once)],
        out_specs=tok,
        out_shape=jax.ShapeDtypeStruct((B, T, D), F32),
        compiler_params=_cparams(("parallel", "parallel"), FF_VMEM_LIMIT),
        name="ffn_dense",
    )(h2, x1, mod, w1, w3, w2)


MOE_TM = 512


def _moe_kernel(te_ref, tv_ref, h_ref, w1_ref, w3_ref, w2_ref, o_ref):
    valid = tv_ref[pl.program_id(0)] > 0

    @pl.when(valid)
    def _():
        o_ref[...] = _swiglu_tile(h_ref[...], w1_ref, w3_ref, w2_ref).astype(o_ref.dtype)

    @pl.when(jnp.logical_not(valid))
    def _():
        o_ref[...] = jnp.zeros_like(o_ref)


def moe_experts(hs, tile_expert, tile_valid, w1, w3, w2):
    P, D = hs.shape
    nt = P // MOE_TM
    tok = pl.BlockSpec((MOE_TM, D), lambda i, te, tv: (i, 0))
    return pl.pallas_call(
        _moe_kernel,
        grid_spec=pltpu.PrefetchScalarGridSpec(
            num_scalar_prefetch=2,
            grid=(nt,),
            in_specs=[tok,
                      pl.BlockSpec((None, D, D_FF), lambda i, te, tv: (te[i], 0, 0)),
                      pl.BlockSpec((None, D, D_FF), lambda i, te, tv: (te[i], 0, 0)),
                      pl.BlockSpec((None, D_FF, D), lambda i, te, tv: (te[i], 0, 0))],
            out_specs=tok),
        out_shape=jax.ShapeDtypeStruct((P, D), BF16),
        compiler_params=_cparams(("arbitrary",), FF_VMEM_LIMIT),
        name="moe_experts",
    )(tile_expert, tile_valid, hs, w1, w3, w2)


def _combine_kernel(x_ref, mod_ref, ya_ref, yb_ref, rt_ref, *rest):
    o_ref = rest[-1]
    wa = rt_ref[:, 2:3]
    wb = rt_ref[:, 3:4]
    o_ref[...] = x_ref[...] + mod_ref[5:6, :] * (wa * ya_ref[...] + wb * yb_ref[...])


def moe_combine(x1, mod, yab, rt, b0, out_prev, after, tm=512):
    B, T, D = x1.shape
    gb = yab.shape[1]
    tok = lambda n: pl.BlockSpec((None, tm, n), lambda b, i: (b + b0, i, 0))
    choice = lambda c: pl.BlockSpec((None, None, tm, D), lambda b, i: (c, b, i, 0))
    in_specs = [tok(D), pl.BlockSpec((None, 6, D), lambda b, i: (b + b0, 0, 0)), choice(0), choice(1),
                tok(rt.shape[-1])]
    args = [x1, mod, yab, yab, rt]
    aliases = {}
    if out_prev is not None:
        in_specs.append(pl.BlockSpec(memory_space=pl.ANY))
        args.append(out_prev)
        aliases = {len(args) - 1: 0}
    if after is not None:
        in_specs.append(pl.BlockSpec(memory_space=pl.ANY))
        args.append(after)
    return pl.pallas_call(
        _combine_kernel,
        grid=(gb, T // tm),
        in_specs=in_specs,
        out_specs=tok(D),
        out_shape=jax.ShapeDtypeStruct((B, T, D), F32),
        input_output_aliases=aliases,
        compiler_params=_cparams(("parallel", "parallel")),
        name="moe_combine",
    )(*args)


MOE_GROUPS = 2


def moe_layer(h2, x1, mod, rt, w1, w3, w2):
    B = x1.shape[0]
    gb = B // MOE_GROUPS
    weights = (w1.astype(BF16), w3.astype(BF16), w2.astype(BF16))
    groups = [_moe_group(h2, rt, b0, gb, *weights) for b0 in range(0, B, gb)]
    out = None
    for g, (ys, slot) in enumerate(groups):
        yab = ys.at[slot].get(mode='promise_in_bounds').reshape(2, gb, *x1.shape[1:])
        after = groups[g + 1][0] if g + 1 < len(groups) else None
        out = moe_combine(x1, mod, yab, rt, g * gb, out, after)
    return out


def _moe_group(h2, rt, b0, gb, w1, w3, w2):
    _, T, D = h2.shape
    n = gb * T
    rt = rt[b0:b0 + gb]
    e_flat = jnp.concatenate([rt[..., 0].reshape(n), rt[..., 1].reshape(n)]).astype(jnp.int32)
    order = jnp.argsort(e_flat, stable=True).astype(jnp.int32)
    rank = jnp.argsort(order).astype(jnp.int32)
    onehot = jax.nn.one_hot(e_flat, N_EXPERTS, dtype=jnp.int32)
    counts = jnp.sum(onehot, axis=0)
    padded = ((counts + MOE_TM - 1) // MOE_TM) * MOE_TM
    pend = jnp.cumsum(padded)
    shift = (pend - padded) - (jnp.cumsum(counts) - counts)
    P = 2 * n + N_EXPERTS * MOE_TM
    nt = P // MOE_TM
    tile_row = jnp.arange(nt, dtype=jnp.int32) * MOE_TM
    tile_expert = jnp.sum((tile_row[:, None] >= pend[None, :]).astype(jnp.int32), axis=1)
    tile_valid = (tile_expert < N_EXPERTS).astype(jnp.int32)
    tile_expert = jnp.minimum(tile_expert, N_EXPERTS - 1)
    slot = rank + jnp.sum(onehot * shift[None, :], axis=1)
    tile_shift = jnp.sum(jax.nn.one_hot(tile_expert, N_EXPERTS, dtype=jnp.int32) * shift[None, :], axis=1)
    row_rank = jnp.arange(P, dtype=jnp.int32) - jnp.repeat(tile_shift, MOE_TM)
    src = order[jnp.clip(row_rank, 0, 2 * n - 1)] % n
    take = lambda a, idx: a.at[idx].get(mode='promise_in_bounds')
    hs = take(h2.reshape(-1, D), src + b0 * T)
    ys = moe_experts(hs, tile_expert, tile_valid, w1, w3, w2)
    return ys, slot


def _hyena(hz, conv_w, conv_b, spec, bias):
    zc = short_conv(hz, conv_w, conv_b)
    y = gated_long_conv(zc, 0, zc, D_HYENA, bias[0], spec, 0)
    return gated_long_conv(y, 0, zc, 2 * D_HYENA, bias[1], spec, D_HYENA)


def _short_hyena_kernel(z_ref, taps_ref, bias_ref, fx_ref, ft_ref, gi_ref, o_ref):
    n = ft_ref.shape[1]
    z = z_ref[...]
    u = z[:, 0:D_HYENA].astype(F32)
    for o in range(HYENA_ORDER):
        s = _dot(ft_ref[...], taps_ref[o].astype(BF16))
        x = _dot(fx_ref[...], u.astype(BF16))
        xr, xi, sr, si = x[:n], x[n:], s[:n], s[n:]
        prod = jnp.concatenate([xr * sr - xi * si, xr * si + xi * sr], axis=0).astype(BF16)
        y = _dot(gi_ref[...], prod) * (1.0 / n)
        gate = z[:, (o + 1) * D_HYENA:(o + 2) * D_HYENA].astype(F32)
        u = gate * (y + u * bias_ref[o])
    o_ref[...] = u.astype(o_ref.dtype)


def short_hyena(hz, conv_w, conv_b, taps, bias):
    zc = short_conv(hz, conv_w, conv_b)
    B, L, C = zc.shape
    n = 2 * L
    taps = jnp.concatenate([taps[:L], taps[FFT_N - L:]], axis=0)
    taps = taps.reshape(n, HYENA_ORDER, D_HYENA).transpose(1, 0, 2)
    k = np.arange(n)
    th_x = 2.0 * np.pi * ((k[:, None] * np.arange(L)[None, :]) % n) / n
    th_t = 2.0 * np.pi * ((k[:, None] * k[None, :]) % n) / n
    fx = np.concatenate([np.cos(th_x), -np.sin(th_x)], axis=0)
    ft = np.concatenate([np.cos(th_t), -np.sin(th_t)], axis=0)
    gi = np.concatenate([np.cos(th_x.T), -np.sin(th_x.T)], axis=1)
    const = lambda shape: pl.BlockSpec(shape, lambda b: (0,) * len(shape))
    return pl.pallas_call(
        _short_hyena_kernel,
        grid=(B,),
        in_specs=[pl.BlockSpec((None, L, C), lambda b: (b, 0, 0)), const((HYENA_ORDER, n, D_HYENA)),
                  const((HYENA_ORDER, 1, D_HYENA)), const((2 * n, L)), const((2 * n, n)), const((L, 2 * n))],
        out_specs=pl.BlockSpec((None, L, D_HYENA), lambda b: (b, 0, 0)),
        out_shape=jax.ShapeDtypeStruct((B, L, D_HYENA), BF16),
        compiler_params=_cparams(("parallel",)),
        name="short_hyena",
    )(zc, taps, bias.reshape(HYENA_ORDER, 1, D_HYENA), _bf16_table(fx), _bf16_table(ft), _bf16_table(gi))


def _split_route(outs, mod):
    x1, h2 = outs[0], outs[1]
    return h2, x1, mod, (outs[2] if len(outs) > 2 else None)


def kernel(x, c, ctx, c_ctx, w_ada, b_ada, g_norm1, g_norm2, w_in, w_out, g_q, g_k, rpb, hy_conv_w, hy_conv_b, filt_w1, filt_b1, filt_w2, filt_b2, filt_w3, filt_b3, filt_freq, filt_w_out, hy_bias, lam_re, lam_im, log_step, b_re, b_im, c_re, c_im, d_skip, w_glu, b_glu, ffn_w1, ffn_w3, ffn_w2, router_w, moe_w1, moe_w3, moe_w2):
    B, L, D = x.shape
    Lc = ctx.shape[1]
    c8 = jnp.zeros((8, D), F32).at[:B].set(c).at[B].set(c_ctx)
    mod_all = adaln_mod(c8, w_ada, b_ada)
    for layer in range(DEPTH):
        last = layer == DEPTH - 1
        m = mod_all[layer].reshape(8, 6, D)
        mod_x = m[:B]
        mod_c = jnp.broadcast_to(m[B], (B, 6, D))
        w_l = w_in[layer].astype(BF16)
        wo_l = w_out[layer].astype(BF16)
        filt = (filt_w1[layer], filt_b1[layer], filt_w2[layer], filt_b2[layer],
                filt_w3[layer], filt_b3[layer], filt_freq[layer], filt_w_out[layer])
        ssm = (lam_re[layer], lam_im[layer], log_step[layer], b_re[layer], b_im[layer], c_re[layer], c_im[layer])

        q_c, k_c, v_c, hz_c, u_c = in_proj(ctx, mod_c, g_norm1[layer], w_l, g_q[layer], g_k[layer])
        q, k, v, hz, u = in_proj(x, mod_x, g_norm1[layer], w_l, g_q[layer], g_k[layer])
        att = neighbourhood_attention(q, k, v, k_c, v_c, rpb[layer])
        spec = filter_spectrum(hyena_filter_taps(L, *filt))
        hy = _hyena(hz, hy_conv_w[layer], hy_conv_b[layer], spec, hy_bias[layer])
        un, yf, yb = s5_scan(u, u_c, *ssm)

        tail = (d_skip[layer], w_glu[layer], b_glu[layer], wo_l, g_norm2[layer])
        i = layer // 2
        if layer % 2 == 0:
            ffw = (ffn_w1[i], ffn_w3[i], ffn_w2[i])
            mixer = lambda h2, x1, mod, rt: ffn_dense(h2, x1, mod, *ffw)
            rw = None
        else:
            mow = (moe_w1[i], moe_w3[i], moe_w2[i])
            mixer = lambda h2, x1, mod, rt: moe_layer(h2, x1, mod, rt, *mow)
            rw = router_w[i]
        x = mixer(*_split_route(out_proj(x, mod_x, att, hy, yf, yb, un, 0, *tail, router_w=rw), mod_x))

        if not last:
            att_c = context_attention(q_c, k_c, v_c)
            hy_c = short_hyena(hz_c, hy_conv_w[layer], hy_conv_b[layer], hyena_filter_taps(Lc, *filt),
                               hy_bias[layer])
            ctx = mixer(*_split_route(out_proj(ctx, mod_c, att_c, hy_c, yf, yb, un, L, *tail, router_w=rw), mod_c))
    return x
```

```python
import functools
import math

import numpy as np
import jax
import jax.numpy as jnp
from jax import lax
from jax.experimental import pallas as pl
from jax.experimental.pallas import tpu as pltpu

F32 = jnp.float32
BF16 = jnp.bfloat16
HIGHEST = lax.Precision.HIGHEST

D_MODEL = 1024
DEPTH = 2
GRID_W = 64
ATT_HEAD_DIM = 64
ATT_HEADS = 8
D_ATT = 512
WIN_ROWS = 8
WIN_COLS = 16
D_HYENA = 256
HYENA_ORDER = 2
FILTER_EMB = 33
FILTER_BANDS = 16
FILTER_WIDTH = 64
MIN_DECAY = math.log(1e-2) / 1.5
MAX_DECAY = math.log(1e-2) / 0.3
D_SSM = 256
SSM_GROUP_WIDTH = 16
SSM_GROUPS = 16
SSM_STATE = 64
COL_K = D_ATT
COL_V = 2 * D_ATT
COL_HY = 3 * D_ATT
COL_SSM = COL_HY + 3 * D_HYENA
D_IN = COL_SSM + D_SSM
D_FF = 2816
N_EXPERTS = 8
EPS = 1e-6
NEG = -1e30

VMEM_LIMIT = 56 * 1024 * 1024


def _cparams(sem, vmem=None):
    return pltpu.CompilerParams(dimension_semantics=sem, vmem_limit_bytes=vmem)


def _dot(a, b):
    return jnp.dot(a, b, preferred_element_type=F32)


def _dot_nt(a, b):
    return lax.dot_general(a, b, (((1,), (1,)), ((), ())), preferred_element_type=F32)


def _mod_kernel(c_ref, w_ref, b_ref, o_ref):
    c = c_ref[...]
    s = c * jax.nn.sigmoid(c)
    o_ref[...] = jnp.dot(s, w_ref[...], precision=HIGHEST, preferred_element_type=F32) + b_ref[...]


def adaln_mod(c8, w_ada, b_ada, tn=512):
    depth, d, n = w_ada.shape
    return pl.pallas_call(
        _mod_kernel,
        grid=(depth, n // tn),
        in_specs=[pl.BlockSpec((8, d), lambda l, j: (0, 0)),
                  pl.BlockSpec((None, d, tn), lambda l, j: (l, 0, j)),
                  pl.BlockSpec((None, 1, tn), lambda l, j: (l, 0, j))],
        out_specs=pl.BlockSpec((None, 8, tn), lambda l, j: (l, 0, j)),
        out_shape=jax.ShapeDtypeStruct((depth, 8, n), F32),
        compiler_params=_cparams(("parallel", "parallel")),
        name="adaln_mod",
    )(c8, w_ada, b_ada.reshape(depth, 1, n))


def _rms_mod(x, g, shift, scale):
    ms = jnp.mean(x * x, axis=-1, keepdims=True)
    h = x * lax.rsqrt(ms + EPS) * g
    return h * (1.0 + scale) + shift


def _head_norm(z, a, g):
    zz = (z * z).astype(BF16)
    m = jnp.concatenate([_dot(zz[:, c:c + ATT_LANES], a) for c in range(0, z.shape[1], ATT_LANES)], axis=1)
    return z * lax.rsqrt(m + EPS) * g


TOK_TB = 128


def _inproj_kernel(x_ref, mod_ref, g_ref, w_ref, a_ref, gq_ref, gk_ref,
                   q_ref, k_ref, v_ref, hz_ref, u_ref):
    B, tb, _ = x_ref.shape
    h = jnp.concatenate([_rms_mod(x_ref[b], g_ref[...], mod_ref[b, 0:1, :], mod_ref[b, 1:2, :]).astype(BF16)
                         for b in range(B)], axis=0)
    a = a_ref[...]
    q = (_head_norm(_dot(h, w_ref[:, 0:COL_K]), a, gq_ref[...]) * (ATT_HEAD_DIM ** -0.5)).astype(BF16)
    k = _head_norm(_dot(h, w_ref[:, COL_K:COL_V]), a, gk_ref[...]).astype(BF16)
    v = _dot(h, w_ref[:, COL_V:COL_HY]).astype(BF16)
    hz = _dot(h, w_ref[:, COL_HY:COL_SSM])
    u = _dot(h, w_ref[:, COL_SSM:D_IN])
    for b in range(B):
        rows = slice(b * tb, (b + 1) * tb)
        q_ref[b] = q[rows]
        k_ref[b] = k[rows]
        v_ref[b] = v[rows]
        hz_ref[b] = hz[rows].astype(hz_ref.dtype)
        for s in range(D_SSM // 128):
            u_ref[s, pl.ds(b, tb, stride=B), :] = u[rows, s * 128:(s + 1) * 128]


def in_proj(x, mod, g1, w_in_bf, gq, gk):
    B, T, D = x.shape
    tb = TOK_TB
    head_avg = jnp.asarray(np.kron(np.eye(ATT_LANES // ATT_HEAD_DIM),
                                   np.full((ATT_HEAD_DIM, ATT_HEAD_DIM), 1.0 / ATT_HEAD_DIM)), BF16)
    tok = lambda n: pl.BlockSpec((B, tb, n), lambda i: (0, i, 0))
    const = lambda shape: pl.BlockSpec(shape, lambda i: (0,) * len(shape))
    return pl.pallas_call(
        _inproj_kernel,
        grid=(T // tb,),
        in_specs=[tok(D), const((B, 6, D)), const((1, D)),
                  const((D, D_IN)), const((ATT_LANES, ATT_LANES)), const((1, D_ATT)), const((1, D_ATT))],
        out_specs=[tok(D_ATT), tok(D_ATT), tok(D_ATT), tok(3 * D_HYENA),
                   pl.BlockSpec((D_SSM // 128, tb * B, 128), lambda i: (0, i, 0))],
        out_shape=[jax.ShapeDtypeStruct((B, T, D_ATT), BF16)] * 3
                  + [jax.ShapeDtypeStruct((B, T, 3 * D_HYENA), BF16),
                     jax.ShapeDtypeStruct((D_SSM // 128, T * B, 128), F32)],
        compiler_params=_cparams(("parallel",), VMEM_LIMIT),
        name="in_proj",
    )(x, mod, g1.reshape(1, D), w_in_bf, head_avg,
      jnp.tile(gq, ATT_HEADS).reshape(1, D_ATT), jnp.tile(gk, ATT_HEADS).reshape(1, D_ATT))


ATT_R = 4
ATT_KROWS = ATT_R + WIN_ROWS - 1
ATT_LANES = 256


def _softmax_heads(q, parts, bias_ref, o_ref):
    nh = ATT_LANES // ATT_HEAD_DIM
    lane = lax.broadcasted_iota(jnp.int32, (1, ATT_LANES), 1)
    head = lambda h: (lane >= ATT_HEAD_DIM * h) & (lane < ATT_HEAD_DIM * (h + 1))
    acc = jnp.zeros((q.shape[0], ATT_LANES), F32)
    for h in range(nh):
        hm = head(h)
        qh = jnp.where(hm, q, jnp.zeros_like(q))
        ss = [_dot_nt(qh, kk).astype(BF16) for kk, _ in parts]
        if bias_ref is not None:
            ss[0] = ss[0] + bias_ref[h]
        m = ss[0].max(axis=-1, keepdims=True)
        for s in ss[1:]:
            m = jnp.maximum(m, s.max(axis=-1, keepdims=True))
        om = head((h + 1) % nh)
        o = None
        for s, (_, vv) in zip(ss, parts):
            part = _dot(jnp.exp(s - m), jnp.where(om, jnp.ones_like(vv), vv))
            o = part if o is None else o + part
        l = pltpu.roll(o, ATT_LANES - ATT_HEAD_DIM, axis=1)
        acc = jnp.where(hm, o / l, acc)
    o_ref[...] = acc.astype(o_ref.dtype)


ATT_SUB = 8


def _nattn_kernel(q_ref, k_ref, v_ref, kc_ref, vc_ref, *rest, rows):
    bias_refs, o_ref = rest[:ATT_SUB], rest[ATT_SUB]
    tq = ATT_R * GRID_W
    ctx = (kc_ref[...], vc_ref[...])
    for sub in range(ATT_SUB):
        blk = pl.program_id(2) * ATT_SUB + sub
        ks = jnp.clip(blk * ATT_R - WIN_ROWS // 2, 0, rows - ATT_KROWS)
        start = pl.multiple_of(ks * GRID_W, GRID_W)
        kl = k_ref[pl.ds(start, ATT_KROWS * GRID_W), :]
        vl = v_ref[pl.ds(start, ATT_KROWS * GRID_W), :]
        qrows = slice(sub * tq, (sub + 1) * tq)
        _softmax_heads(q_ref[qrows, :], [(kl, vl), ctx], bias_refs[sub], o_ref.at[qrows, :])


def _attn_bias_tables(rpb, rows):
    nblk = rows // ATT_R
    qc = np.arange(GRID_W)
    c0 = np.clip(qc - WIN_COLS // 2, 0, GRID_W - WIN_COLS)
    kc = np.arange(GRID_W)
    col_ok = (kc[None, :] >= c0[:, None]) & (kc[None, :] < c0[:, None] + WIN_COLS)
    col_off = kc[None, :] - qc[:, None] + WIN_COLS - 1
    col_sel = (col_off[:, :, None] == np.arange(2 * WIN_COLS - 1)) & col_ok[:, :, None]
    band = jnp.einsum('hij,qkj->hqik', rpb, jnp.asarray(col_sel, F32), precision=HIGHEST)
    band = jnp.where(jnp.asarray(col_ok)[None, :, None, :], band, NEG)
    band = band.reshape(ATT_HEADS, GRID_W, (2 * WIN_ROWS - 1) * GRID_W).astype(BF16)
    neg = lambda nrows: jnp.full((ATT_HEADS, GRID_W, nrows * GRID_W), NEG, BF16)
    tabs = []
    for blk in (0, 1, nblk - 1):
        ks = int(np.clip(blk * ATT_R - WIN_ROWS // 2, 0, rows - ATT_KROWS))
        qrows = []
        for r in range(blk * ATT_R, (blk + 1) * ATT_R):
            r0 = int(np.clip(r - WIN_ROWS // 2, 0, rows - WIN_ROWS))
            first = r0 - r + WIN_ROWS - 1
            win = band[:, :, first * GRID_W:(first + WIN_ROWS) * GRID_W]
            qrows.append(jnp.concatenate([neg(r0 - ks), win, neg(ATT_KROWS - WIN_ROWS - (r0 - ks))], axis=-1))
        tabs.append(jnp.stack(qrows, axis=1).reshape(ATT_HEADS, ATT_R * GRID_W, ATT_KROWS * GRID_W))
    return jnp.stack(tabs)


def neighbourhood_attention(q, k, v, kc, vc, rpb):
    B, L, _ = q.shape
    Lc = kc.shape[1]
    rows = L // GRID_W
    nblk = rows // ATT_R
    tq = ATT_R * GRID_W
    nkl = ATT_KROWS * GRID_W
    bias = _attn_bias_tables(rpb, rows)
    hpb = ATT_LANES // ATT_HEAD_DIM
    variant = lambda blk: jnp.where(blk == 0, 0, jnp.where(blk == nblk - 1, 2, 1))
    bias_spec = lambda sub: pl.BlockSpec((None, hpb, tq, nkl),
                                         lambda b, g, j: (variant(j * ATT_SUB + sub), g, 0, 0))
    return pl.pallas_call(
        functools.partial(_nattn_kernel, rows=rows),
        grid=(B, D_ATT // ATT_LANES, nblk // ATT_SUB),
        in_specs=[pl.BlockSpec((None, ATT_SUB * tq, ATT_LANES), lambda b, g, j: (b, j, g)),
                  pl.BlockSpec((None, L, ATT_LANES), lambda b, g, j: (b, 0, g)),
                  pl.BlockSpec((None, L, ATT_LANES), lambda b, g, j: (b, 0, g)),
                  pl.BlockSpec((None, Lc, ATT_LANES), lambda b, g, j: (b, 0, g)),
                  pl.BlockSpec((None, Lc, ATT_LANES), lambda b, g, j: (b, 0, g))]
                 + [bias_spec(sub) for sub in range(ATT_SUB)],
        out_specs=pl.BlockSpec((None, ATT_SUB * tq, ATT_LANES), lambda b, g, j: (b, j, g)),
        out_shape=jax.ShapeDtypeStruct((B, L, D_ATT), BF16),
        compiler_params=_cparams(("parallel", "parallel", "arbitrary"), VMEM_LIMIT),
        name="nattn",
    )(q, k, v, kc, vc, *([bias] * ATT_SUB))


def _cattn_kernel(q_ref, k_ref, v_ref, o_ref):
    _softmax_heads(q_ref[...], [(k_ref[...], v_ref[...])], None, o_ref)


def context_attention(q, k, v):
    B, Lc, _ = q.shape
    spec = pl.BlockSpec((None, Lc, ATT_LANES), lambda b, g: (b, 0, g))
    return pl.pallas_call(
        _cattn_kernel,
        grid=(B, D_ATT // ATT_LANES),
        in_specs=[spec, spec, spec],
        out_specs=spec,
        out_shape=jax.ShapeDtypeStruct((B, Lc, D_ATT), BF16),
        compiler_params=_cparams(("parallel", "parallel")),
        name="cattn",
    )(q, k, v)


FFT_N = 16384
FFT_R = 128
FFT_K1 = FFT_R // 2 + 1
FFT_K1P = 66
FFT_PITCH = 2 * FFT_K1P


def _filter_kernel(w1_ref, wt_ref, b1_ref, w2_ref, b2_ref, w3_ref, b3_ref, fr_ref, wo_ref, o_ref, *, lf, tile):
    i = pl.program_id(0)
    half = FFT_N // 2
    row0 = i * tile
    first_pos = jnp.where(row0 < half, row0, FFT_N - row0 - tile)

    @pl.when(first_pos >= lf)
    def _():
        o_ref[...] = jnp.zeros_like(o_ref)

    @pl.when(first_pos < lf)
    def _():
        rows = tile // FILTER_PACK

        def position(width):
            r = lax.broadcasted_iota(jnp.int32, (rows, FILTER_PACK * width), 0)
            g = lax.broadcasted_iota(jnp.int32, (rows, FILTER_PACK * width), 1) // width
            j = row0 + r + g * rows
            return jnp.where(j < half, j, FFT_N - 1 - j)

        nfeat = 2 * FILTER_BANDS
        feat = lax.broadcasted_iota(jnp.int32, (1, FILTER_PACK * nfeat), 1) % nfeat
        f = 1e-4 + (feat % FILTER_BANDS).astype(F32) * ((FILTER_BANDS - 1 - 1e-4) / (FILTER_BANDS - 1))
        w = (2.0 * math.pi / lf) * position(nfeat).astype(F32)
        z = jnp.cos(f * w + jnp.where(feat < FILTER_BANDS, 0.0, 0.5 * math.pi))
        t_h = position(FILTER_WIDTH).astype(F32) / (lf - 1.0)
        fr = fr_ref[...]
        hdot = lambda a, b: jnp.dot(a, b, precision=HIGHEST, preferred_element_type=F32)
        h = jnp.sin(fr * (hdot(z, w1_ref[...]) + t_h * wt_ref[...] + b1_ref[...]))
        h = jnp.sin(fr * (hdot(h, w2_ref[...]) + b2_ref[...]))
        h = jnp.sin(fr * (hdot(h, w3_ref[...]) + b3_ref[...]))
        y = hdot(h, wo_ref[...])
        nout = HYENA_ORDER * D_HYENA
        pos_o = position(nout)
        ch = lax.broadcasted_iota(jnp.int32, (1, FILTER_PACK * nout), 1) % D_HYENA
        delta = jnp.abs(MIN_DECAY + ch.astype(F32) * ((MAX_DECAY - MIN_DECAY) / (D_HYENA - 1)))
        y = jnp.where(pos_o < lf, y * jnp.exp(-(pos_o.astype(F32) / (lf - 1.0)) * delta), 0.0)
        for g in range(FILTER_PACK):
            o_ref[g * rows:(g + 1) * rows, :] = y[:, g * nout:(g + 1) * nout]


FILTER_PACK = 4


def hyena_filter_taps(lf, w1, b1, w2, b2, w3, b3, freq, w_out, tile=1024):
    P, W = FILTER_PACK, FILTER_WIDTH
    nout = HYENA_ORDER * D_HYENA
    bd = lambda m: jnp.kron(jnp.eye(P, dtype=F32), m.astype(F32))
    wide = lambda a: jnp.tile(a.astype(F32), P).reshape(1, P * W)
    wo = w_out.reshape(W, HYENA_ORDER, 2, D_HYENA).transpose(2, 0, 1, 3).reshape(2, W, nout)
    wo = jnp.stack([bd(wo[0]), bd(wo[1])])
    nt = FFT_N // tile
    const = lambda shape: pl.BlockSpec(shape, lambda i: (0,) * len(shape))
    return pl.pallas_call(
        functools.partial(_filter_kernel, lf=lf, tile=tile),
        grid=(nt,),
        in_specs=[const((P * (FILTER_EMB - 1), P * W)), const((1, P * W)), const((1, P * W)),
                  const((P * W, P * W)), const((1, P * W)), const((P * W, P * W)), const((1, P * W)),
                  const((1, P * W)),
                  pl.BlockSpec((None, P * W, P * nout), lambda i: (i // (nt // 2), 0, 0))],
        out_specs=pl.BlockSpec((tile, nout), lambda i: (i, 0)),
        out_shape=jax.ShapeDtypeStruct((FFT_N, nout), F32),
        compiler_params=_cparams(("parallel",)),
        name="hyena_filter",
    )(bd(w1[1:]), wide(w1[0]), wide(b1), bd(w2), wide(b2), bd(w3), wide(b3), wide(freq), wo)


def _shortconv_kernel(z_ref, w_ref, b_ref, o_ref, zp_ref, *, L):
    C = z_ref.shape[-1]
    zp_ref[0:8, :] = jnp.zeros((8, C), F32)
    zp_ref[L + 8:L + 16, :] = jnp.zeros((8, C), F32)
    zp_ref[8:L + 8, :] = z_ref[...].astype(F32)
    ch = min(L, 512)
    for c in range(L // ch):
        s = c * ch
        o_ref[s:s + ch, :] = (b_ref[...] + zp_ref[s + 7:s + 7 + ch, :] * w_ref[0:1, :]
                              + zp_ref[s + 8:s + 8 + ch, :] * w_ref[1:2, :]
                              + zp_ref[s + 9:s + 9 + ch, :] * w_ref[2:3, :]).astype(o_ref.dtype)


def short_conv(z, w, b, cb=128):
    B, L, C = z.shape
    return pl.pallas_call(
        functools.partial(_shortconv_kernel, L=L),
        grid=(B, C // cb),
        in_specs=[pl.BlockSpec((None, L, cb), lambda b_, c: (b_, 0, c)),
                  pl.BlockSpec((3, cb), lambda b_, c: (0, c)),
                  pl.BlockSpec((1, cb), lambda b_, c: (0, c))],
        out_specs=pl.BlockSpec((None, L, cb), lambda b_, c: (b_, 0, c)),
        out_shape=jax.ShapeDtypeStruct((B, L, C), BF16),
        scratch_shapes=[pltpu.VMEM((L + 16, cb), F32)],
        compiler_params=_cparams(("parallel", "parallel")),
        name="short_conv",
    )(z, w, b.reshape(1, C))


@functools.lru_cache(maxsize=None)
def _fft_tables(n1_in):
    n2 = np.arange(FFT_R)[:, None, None]
    k1 = np.arange(FFT_K1)[None, :, None]
    n1 = np.arange(n1_in)[None, None, :]
    th = 2.0 * np.pi * ((k1 * (FFT_R * n1 + n2)) % FFT_N) / FFT_N
    ga = np.zeros((FFT_R, 2 * FFT_K1P, n1_in))
    ga[:, :FFT_K1] = np.cos(th)
    ga[:, FFT_K1P:FFT_K1P + FFT_K1] = -np.sin(th)
    wk = np.where((np.arange(FFT_K1) == 0) | (np.arange(FFT_K1) == FFT_R // 2), 1.0, 2.0)[None, None, :]
    thd = th.transpose(0, 2, 1)
    gd = np.zeros((FFT_R, n1_in, 2 * FFT_K1P))
    gd[:, :, :FFT_K1] = wk * np.cos(thd)
    gd[:, :, FFT_K1P:FFT_K1P + FFT_K1] = -wk * np.sin(thd)
    a = np.arange(FFT_R)
    ph = 2.0 * np.pi * ((a[:, None] * a[None, :]) % FFT_R) / FFT_R
    c, s = np.cos(ph), np.sin(ph)
    fb = np.block([[c, s], [-s, c]])
    fc = np.block([[c, -s], [s, c]])
    return ga, gd, fb, fc


FFT_KG = 13
FFT_UNROLL = 16


def _spectrum_slabs(ya_ref, fb, k0):
    cols = []
    for g in range(FFT_KG):
        re = ya_ref[pl.ds(k0 + g, FFT_R, stride=FFT_PITCH), :]
        im = ya_ref[pl.ds(FFT_K1P + k0 + g, FFT_R, stride=FFT_PITCH), :]
        cols.append(jnp.concatenate([re, im], axis=0).astype(BF16))
    return _dot(fb, jnp.concatenate(cols, axis=1))


def _stage_a(x_ref, x_pitch, ga_ref, ya_ref, n1_in):
    def body(n2, carry):
        xs = x_ref[pl.ds(n2, n1_in, stride=x_pitch), :].astype(BF16)
        ya_ref[pl.ds(n2 * FFT_PITCH, FFT_PITCH), :] = _dot(ga_ref[n2], xs)
        return carry
    lax.fori_loop(0, FFT_R, body, 0, unroll=FFT_UNROLL)


def _spectrum_kernel(x_ref, ga_ref, fb_ref, o_ref, ya_ref):
    _stage_a(x_ref, FFT_R, ga_ref, ya_ref, FFT_R)
    fb = fb_ref[...]
    cb = ya_ref.shape[1]

    def body(kg, carry):
        x = _spectrum_slabs(ya_ref, fb, kg * FFT_KG)
        for g in range(FFT_KG):
            o_ref[kg * FFT_KG + g] = x[:, g * cb:(g + 1) * cb]
        return carry
    lax.fori_loop(0, FFT_K1 // FFT_KG, body, 0)


def filter_spectrum(taps, cb=128):
    C = taps.shape[1]
    ga, _, fb, _ = _fft_tables(FFT_R)
    return pl.pallas_call(
        _spectrum_kernel,
        grid=(C // cb,),
        in_specs=[pl.BlockSpec((FFT_N, cb), lambda c: (0, c)),
                  pl.BlockSpec((FFT_R, FFT_PITCH, FFT_R), lambda c: (0, 0, 0)),
                  pl.BlockSpec((2 * FFT_R, 2 * FFT_R), lambda c: (0, 0))],
        out_specs=pl.BlockSpec((FFT_K1, 2 * FFT_R, cb), lambda c: (0, 0, c)),
        out_shape=jax.ShapeDtypeStruct((FFT_K1, 2 * FFT_R, C), F32),
        scratch_shapes=[pltpu.VMEM((FFT_R * FFT_PITCH, cb), F32)],
        compiler_params=_cparams(("parallel",), VMEM_LIMIT),
        name="filter_spectrum",
    )(taps, _bf16_table(ga), _bf16_table(fb))


def _bf16_table(t):
    return jnp.asarray(t, F32).astype(BF16)


def _longconv_kernel(u_ref, gate_ref, bias_ref, spec_ref, ga_ref, gd_ref, fb_ref, fc_ref, o_ref, ya_ref, xp_ref):
    n1_in = FFT_R // 2

    def repitch(n1, carry):
        xp_ref[pl.ds(n1 * FFT_PITCH, FFT_R), :] = (
            u_ref[pl.ds(pl.multiple_of(n1 * FFT_R, FFT_R), FFT_R), :].astype(F32))
        return carry
    lax.fori_loop(0, n1_in, repitch, 0, unroll=FFT_UNROLL)

    _stage_a(xp_ref, FFT_PITCH, ga_ref, ya_ref, n1_in)
    fb = fb_ref[...]
    fc = fc_ref[...]

    cb = ya_ref.shape[1]

    def mid(kg, carry):
        k0 = kg * FFT_KG
        x = _spectrum_slabs(ya_ref, fb, k0)
        s = jnp.concatenate([spec_ref[k0 + g] for g in range(FFT_KG)], axis=1)
        xr, xi = x[:FFT_R], x[FFT_R:]
        sr, si = s[:FFT_R], s[FFT_R:]
        z = jnp.concatenate([xr * sr - xi * si, xr * si + xi * sr], axis=0).astype(BF16)
        v = _dot(fc, z)
        for g in range(FFT_KG):
            ya_ref[pl.ds(k0 + g, FFT_R, stride=FFT_PITCH), :] = v[:FFT_R, g * cb:(g + 1) * cb]
            ya_ref[pl.ds(FFT_K1P + k0 + g, FFT_R, stride=FFT_PITCH), :] = v[FFT_R:, g * cb:(g + 1) * cb]
        return carry
    lax.fori_loop(0, FFT_K1 // FFT_KG, mid, 0)

    def last(n2, carry):
        slab = ya_ref[pl.ds(n2 * FFT_PITCH, FFT_PITCH), :].astype(BF16)
        xp_ref[pl.ds(n2, n1_in, stride=FFT_PITCH), :] = _dot(gd_ref[n2], slab)
        return carry
    lax.fori_loop(0, FFT_R, last, 0, unroll=FFT_UNROLL)

    bias = bias_ref[...]

    def gate(n1, carry):
        rows = pl.ds(pl.multiple_of(n1 * FFT_R, FFT_R), FFT_R)
        y = xp_ref[pl.ds(n1 * FFT_PITCH, FFT_R), :] * (1.0 / FFT_N)
        o_ref[rows, :] = (gate_ref[rows, :].astype(F32) * (y + u_ref[rows, :].astype(F32) * bias)).astype(o_ref.dtype)
        return carry
    lax.fori_loop(0, n1_in, gate, 0, unroll=FFT_UNROLL)


def gated_long_conv(u, u_col, gate, gate_col, bias, spec, spec_col, C=D_HYENA, cb=128):
    B, L, _ = u.shape
    n1_in = FFT_R // 2
    ga, gd, fb, fc = _fft_tables(n1_in)
    once = pl.Buffered(1)
    seq = lambda col: pl.BlockSpec((None, L, cb), lambda c, b: (b, 0, c + col // cb))
    return pl.pallas_call(
        _longconv_kernel,
        grid=(C // cb, B),
        in_specs=[seq(u_col), seq(gate_col),
                  pl.BlockSpec((1, cb), lambda c, b: (0, c)),
                  pl.BlockSpec((FFT_K1, 2 * FFT_R, cb), lambda c, b: (0, 0, c + spec_col // cb), pipeline_mode=once),
                  pl.BlockSpec((FFT_R, FFT_PITCH, n1_in), lambda c, b: (0, 0, 0), pipeline_mode=once),
                  pl.BlockSpec((FFT_R, n1_in, FFT_PITCH), lambda c, b: (0, 0, 0), pipeline_mode=once),
                  pl.BlockSpec((2 * FFT_R, 2 * FFT_R), lambda c, b: (0, 0)),
                  pl.BlockSpec((2 * FFT_R, 2 * FFT_R), lambda c, b: (0, 0))],
        out_specs=seq(0),
        out_shape=jax.ShapeDtypeStruct((B, L, C), BF16),
        scratch_shapes=[pltpu.VMEM((FFT_R * FFT_PITCH, cb), F32), pltpu.VMEM((n1_in * FFT_PITCH, cb), F32)],
        compiler_params=_cparams(("parallel", "arbitrary"), VMEM_LIMIT),
        name="long_conv",
    )(u, gate, bias.reshape(1, C), spec, _bf16_table(ga), _bf16_table(gd), _bf16_table(fb), _bf16_table(fc))


SSM_N = SSM_GROUPS * SSM_STATE
S5_SEQS = 8
S5_CHUNK = 256


def _s5_kernel(uf_ref, ub_ref, bf_ref, bb_ref, lam_ref, cf_ref, cb_ref, yf_ref, yb_ref,
               buf_ref, bub_ref, xsf_ref, xsb_ref, st_ref):
    half = S5_SEQS // 2
    tiles = S5_CHUNK // 2

    @pl.when(pl.program_id(0) == 0)
    def _():
        st_ref[...] = jnp.zeros_like(st_ref)

    slabs = lambda ref: jnp.concatenate([ref[s] for s in range(ref.shape[0])], axis=1).astype(BF16)
    buf_ref[...] = _dot(slabs(uf_ref), bf_ref[...])
    bub_ref[...] = _dot(slabs(ub_ref), bb_ref[...])
    lo = lax.broadcasted_iota(jnp.int32, (S5_SEQS, 1), 0) < half

    def advance(x, lam, b):
        xr, xi = x
        lr, li = lam[:, 0:SSM_N], lam[:, SSM_N:2 * SSM_N]
        return lr * xr - li * xi + b[:, 0:SSM_N], lr * xi + li * xr + b[:, SSM_N:2 * SSM_N]

    def swap(x):
        return pltpu.roll(x[0], half, axis=0), pltpu.roll(x[1], half, axis=0)

    def step(t4, x):
        f_tiles, b_tiles = [], []
        for p in range(2):
            tf = buf_ref[pl.ds(pl.multiple_of((2 * t4 + p) * S5_SEQS, S5_SEQS), S5_SEQS), :]
            tb = bub_ref[pl.ds(pl.multiple_of((tiles - 1 - 2 * t4 - p) * S5_SEQS, S5_SEQS), S5_SEQS), :]
            xe = advance(x, lam_ref[0], jnp.where(lo, tf, tb))
            xo = advance(swap(xe), lam_ref[1], jnp.where(lo, tb, tf))
            x = swap(xo)
            xe = jnp.concatenate(xe, axis=1)
            xo = jnp.concatenate(xo, axis=1)
            f_tiles.append(jnp.where(lo, xe, xo))
            b_tiles.append(jnp.where(lo, xo, xe))
        xsf_ref[pl.ds(pl.multiple_of(t4 * 2 * S5_SEQS, 2 * S5_SEQS), 2 * S5_SEQS), :] = (
            jnp.concatenate(f_tiles, axis=0).astype(BF16))
        xsb_ref[pl.ds(pl.multiple_of((tiles - 2 - 2 * t4) * S5_SEQS, 2 * S5_SEQS), 2 * S5_SEQS), :] = (
            jnp.concatenate(b_tiles[::-1], axis=0).astype(BF16))
        return x

    xr, xi = lax.fori_loop(0, S5_CHUNK // 4, step, (st_ref[:, 0:SSM_N], st_ref[:, SSM_N:2 * SSM_N]))
    st_ref[:, 0:SSM_N] = xr
    st_ref[:, SSM_N:2 * SSM_N] = xi
    yf = _dot(xsf_ref[...], cf_ref[...])
    yb = _dot(xsb_ref[...], cb_ref[...])
    for s in range(yf_ref.shape[0]):
        yf_ref[s] = yf[:, s * 128:(s + 1) * 128]
        yb_ref[s] = yb[:, s * 128:(s + 1) * 128]


def _s5_matrices(lam_re, lam_im, log_step, b_re, b_im, c_re, c_im):
    G, P, Hs = SSM_GROUPS, SSM_STATE, SSM_GROUP_WIDTH
    dt = jnp.exp(log_step)[:, :, None]
    mag = jnp.exp(lam_re * dt)
    ar, ai = mag * jnp.cos(lam_im * dt), mag * jnp.sin(lam_im * dt)
    er, ei = ar - 1.0, ai
    den = lam_re * lam_re + lam_im * lam_im
    cr, ci = (er * lam_re + ei * lam_im) / den, (ei * lam_re - er * lam_im) / den
    bbr = cr[..., None] * b_re - ci[..., None] * b_im
    bbi = cr[..., None] * b_im + ci[..., None] * b_re
    eye = jnp.eye(G, dtype=F32)
    blk_in = lambda m: jnp.einsum('gq,qph->ghqp', eye, m).reshape(G * Hs, G * P)
    bmat = [jnp.concatenate([blk_in(bbr[d]), blk_in(bbi[d])], axis=1).astype(BF16) for d in range(2)]
    blk_out = lambda m: jnp.einsum('gq,qhp->gpqh', eye, m).reshape(G * P, G * Hs)
    cmat = [jnp.concatenate([blk_out(c_re[d]), -blk_out(c_im[d])], axis=0).astype(BF16) for d in range(2)]
    lam = jnp.concatenate([ar.reshape(2, 1, G * P), ai.reshape(2, 1, G * P)], axis=2)
    lam = jnp.broadcast_to(lam, (2, S5_SEQS // 2, 2 * G * P))
    lam8 = jnp.stack([lam.reshape(S5_SEQS, 2 * G * P), lam[::-1].reshape(S5_SEQS, 2 * G * P)])
    return bmat[0], bmat[1], lam8, cmat[0], cmat[1]


def s5_scan(u_x, u_c, lam_re, lam_im, log_step, b_re, b_im, c_re, c_im):
    S = u_x.shape[0]
    half = S5_SEQS // 2
    L, Lc = u_x.shape[1] // half, u_c.shape[1] // half
    assert L % S5_CHUNK == 0 and Lc % S5_CHUNK == 0
    bf, bb, lam8, cf, cb = _s5_matrices(lam_re, lam_im, log_step, b_re, b_im, c_re, c_im)
    un = jnp.concatenate([u_x, u_c], axis=1)
    rows = S5_CHUNK * half
    nx, nc = L // S5_CHUNK, Lc // S5_CHUNK
    n = nx + nc
    fwd = pl.BlockSpec((S, rows, 128), lambda i: (0, jnp.where(i < nc, nx + i, i - nc), 0))
    bwd = pl.BlockSpec((S, rows, 128), lambda i: (0, n - 1 - i, 0))
    const = lambda shape: pl.BlockSpec(shape, lambda i: (0,) * len(shape))
    yf, yb = pl.pallas_call(
        _s5_kernel,
        grid=(n,),
        in_specs=[fwd, bwd, const((D_SSM, 2 * SSM_N)), const((D_SSM, 2 * SSM_N)),
                  const((2, S5_SEQS, 2 * SSM_N)), const((2 * SSM_N, D_SSM)), const((2 * SSM_N, D_SSM))],
        out_specs=[fwd, bwd],
        out_shape=[jax.ShapeDtypeStruct(un.shape, F32)] * 2,
        scratch_shapes=[pltpu.VMEM((rows, 2 * SSM_N), F32), pltpu.VMEM((rows, 2 * SSM_N), F32),
                        pltpu.VMEM((rows, 2 * SSM_N), BF16), pltpu.VMEM((rows, 2 * SSM_N), BF16),
                        pltpu.VMEM((S5_SEQS, 2 * SSM_N), F32)],
        compiler_params=_cparams(("arbitrary",), VMEM_LIMIT),
        name="s5_scan",
    )(un, un, bf, bb, lam8, cf, cb)
    return un, yf, yb


def _gelu_tanh(y):
    return 0.5 * y * (1.0 + jnp.tanh(math.sqrt(2.0 / math.pi) * (y + 0.044715 * (y * y * y))))


def _outproj_kernel(*refs, route):
    (x_ref, mod_ref, att_ref, hy_ref, yf_ref, yb_ref, u_ref, dsk_ref, wg_ref, bg_ref, wo_ref, g2_ref) = refs[:12]
    if route:
        wr_ref, x1_ref, h2_ref, rt_ref = refs[12:]
    else:
        x1_ref, h2_ref = refs[12:]
    B, tb, D = x_ref.shape
    rows = lambda a: a.reshape(B * tb, a.shape[-1])

    def scan_rows(ref):
        return jnp.concatenate(
            [jnp.concatenate([ref[s, pl.ds(b, tb, stride=B), :] for s in range(ref.shape[0])], axis=1)
             for b in range(B)], axis=0)

    y = scan_rows(yf_ref) + scan_rows(yb_ref) + scan_rows(u_ref) * dsk_ref[...]
    y = _gelu_tanh(y)
    ss = y * jax.nn.sigmoid(_dot(y.astype(BF16), wg_ref[...]) + bg_ref[...])
    mix = (_dot(rows(att_ref[...]), wo_ref[0:D_ATT, :])
           + _dot(rows(hy_ref[...]).astype(BF16), wo_ref[D_ATT:D_ATT + D_HYENA, :])
           + _dot(ss.astype(BF16), wo_ref[D_ATT + D_HYENA:, :]))
    h2s = []
    for b in range(B):
        x1 = x_ref[b] + mod_ref[b, 2:3, :] * mix[b * tb:(b + 1) * tb]
        x1_ref[b] = x1
        h2s.append(_rms_mod(x1, g2_ref[...], mod_ref[b, 3:4, :], mod_ref[b, 4:5, :]))
        h2_ref[b] = h2s[-1].astype(h2_ref.dtype)
    if route:
        h2 = jnp.concatenate(h2s, axis=0)
        h_hi = h2.astype(BF16)
        h_lo = (h2 - h_hi.astype(F32)).astype(BF16)
        logits = _dot_nt(wr_ref[0], h_hi) + _dot_nt(wr_ref[0], h_lo) + _dot_nt(wr_ref[1], h_hi)
        row = lax.broadcasted_iota(jnp.int32, (ROUTE_ROWS, 1), 0)
        lg = jnp.where(row < N_EXPERTS, logits, -jnp.inf)
        v1 = lg.max(axis=0, keepdims=True)
        i1 = jnp.where(lg == v1, row, ROUTE_ROWS).min(axis=0, keepdims=True)
        lg2 = jnp.where(row == i1, -jnp.inf, lg)
        v2 = lg2.max(axis=0, keepdims=True)
        i2 = jnp.where(lg2 == v2, row, ROUTE_ROWS).min(axis=0, keepdims=True)
        e = jnp.exp(v2 - v1)
        w1 = 1.0 / (1.0 + e)
        out_row = lax.broadcasted_iota(jnp.int32, (8, 1), 0)
        rt_ref[...] = jnp.where(out_row == 0, i1.astype(F32),
                                jnp.where(out_row == 1, i2.astype(F32),
                                          jnp.where(out_row == 2, w1, jnp.where(out_row == 3, e * w1, 0.0))))


def out_proj(x, mod, att, hy, yf, yb, u, scan_row0, d_skip, w_glu, b_glu, w_out_bf, g2, router_w=None):
    B, T, D = x.shape
    tb = TOK_TB
    route = router_w is not None
    S = u.shape[0]
    tok = lambda n: pl.BlockSpec((B, tb, n), lambda i: (0, i, 0))
    scan = pl.BlockSpec((S, tb * B, 128), lambda i: (0, i + scan_row0 // tb, 0))
    const = lambda shape: pl.BlockSpec(shape, lambda i: (0,) * len(shape))
    in_specs = [tok(D), const((B, 6, D)), tok(D_ATT), tok(D_HYENA),
                scan, scan, scan, const((1, D_SSM)), const((D_SSM, D_SSM)), const((1, D_SSM)),
                const((D, D)), const((1, D))]
    args = [x, mod, att, hy, yf, yb, u, d_skip.reshape(1, D_SSM), w_glu.astype(BF16), b_glu.reshape(1, D_SSM),
            w_out_bf, g2.reshape(1, D)]
    out_specs = [tok(D), tok(D)]
    out_shape = [jax.ShapeDtypeStruct((B, T, D), F32), jax.ShapeDtypeStruct((B, T, D), BF16)]
    if route:
        in_specs.append(const((2, ROUTE_ROWS, D)))
        wr = jnp.pad(router_w.astype(F32).T, ((0, ROUTE_ROWS - N_EXPERTS), (0, 0)))
        wr_hi = wr.astype(BF16)
        args.append(jnp.stack([wr_hi, (wr - wr_hi.astype(F32)).astype(BF16)]))
        out_specs.append(pl.BlockSpec((None, 8, B * tb), lambda i: (i, 0, 0)))
        out_shape.append(jax.ShapeDtypeStruct((T // tb, 8, B * tb), F32))
    outs = pl.pallas_call(
        functools.partial(_outproj_kernel, route=route),
        grid=(T // tb,),
        in_specs=in_specs, out_specs=out_specs, out_shape=out_shape,
        compiler_params=_cparams(("parallel",), VMEM_LIMIT),
        name="out_proj",
    )(*args)
    outs = list(outs)
    if route:
        outs[2] = outs[2].reshape(T // tb, 8, B, tb).transpose(2, 0, 3, 1).reshape(B, T, 8)
    return outs


ROUTE_ROWS = 16


FF_CHUNK = 768
FF_VMEM_LIMIT = 60 * 1024 * 1024


def _swiglu_tile(h, w1_ref, w3_ref, w2_ref):
    acc = None
    for lo in range(0, D_FF, FF_CHUNK):
        hi = min(lo + FF_CHUNK, D_FF)
        a = _dot(h, w1_ref[:, lo:hi].astype(BF16))
        g = (a * jax.nn.sigmoid(a) * _dot(h, w3_ref[:, lo:hi].astype(BF16))).astype(BF16)
        part = _dot(g, w2_ref[lo:hi, :].astype(BF16))
        acc = part if acc is None else acc + part
    return acc


def _ffn_kernel(h_ref, x_ref, mod_ref, w1_ref, w3_ref, w2_ref, o_ref):
    o_ref[...] = x_ref[...] + mod_ref[5:6, :] * _swiglu_tile(h_ref[...], w1_ref, w3_ref, w2_ref)


def ffn_dense(h2, x1, mod, w1, w3, w2, tm=512):
    B, T, D = x1.shape
    tm = min(tm, T)
    tok = pl.BlockSpec((None, tm, D), lambda b, i: (b, i, 0))
    once = pl.Buffered(1)
    return pl.pallas_call(
        _ffn_kernel,
        grid=(B, T // tm),
        in_specs=[tok, tok, pl.BlockSpec((None, 6, D), lambda b, i: (b, 0, 0)),
                  pl.BlockSpec((D, D_FF), lambda b, i: (0, 0), pipeline_mode=once),
                  pl.BlockSpec((D, D_FF), lambda b, i: (0, 0), pipeline_mode=once),
                  pl.BlockSpec((D_FF, D), lambda b, i: (0, 0), pipeline_mode=once)],
        out_specs=tok,
        out_shape=jax.ShapeDtypeStruct((B, T, D), F32),
        compiler_params=_cparams(("parallel", "parallel"), FF_VMEM_LIMIT),
        name="ffn_dense",
    )(h2, x1, mod, w1, w3, w2)


MOE_TM = 512


def _moe_kernel(te_ref, tv_ref, h_ref, w1_ref, w3_ref, w2_ref, o_ref):
    valid = tv_ref[pl.program_id(0)] > 0

    @pl.when(valid)
    def _():
        o_ref[...] = _swiglu_tile(h_ref[...], w1_ref, w3_ref, w2_ref).astype(o_ref.dtype)

    @pl.when(jnp.logical_not(valid))
    def _():
        o_ref[...] = jnp.zeros_like(o_ref)


def moe_experts(hs, tile_expert, tile_valid, w1, w3, w2):
    P, D = hs.shape
    nt = P // MOE_TM
    tok = pl.BlockSpec((MOE_TM, D), lambda i, te, tv: (i, 0))
    return pl.pallas_call(
        _moe_kernel,
        grid_spec=pltpu.PrefetchScalarGridSpec(
            num_scalar_prefetch=2,
            grid=(nt,),
            in_specs=[tok,
                      pl.BlockSpec((None, D, D_FF), lambda i, te, tv: (te[i], 0, 0)),
                      pl.BlockSpec((None, D, D_FF), lambda i, te, tv: (te[i], 0, 0)),
                      pl.BlockSpec((None, D_FF, D), lambda i, te, tv: (te[i], 0, 0))],
            out_specs=tok),
        out_shape=jax.ShapeDtypeStruct((P, D), BF16),
        compiler_params=_cparams(("arbitrary",), FF_VMEM_LIMIT),
        name="moe_experts",
    )(tile_expert, tile_valid, hs, w1, w3, w2)


def _combine_kernel(x_ref, mod_ref, ya_ref, yb_ref, rt_ref, *rest):
    o_ref = rest[-1]
    wa = rt_ref[:, 2:3]
    wb = rt_ref[:, 3:4]
    o_ref[...] = x_ref[...] + mod_ref[5:6, :] * (wa * ya_ref[...] + wb * yb_ref[...])


def moe_combine(x1, mod, yab, rt, b0, after, tm=512):
    B, T, D = x1.shape
    gb = yab.shape[1]
    tok = lambda n: pl.BlockSpec((None, tm, n), lambda b, i: (b + b0, i, 0))
    choice = lambda c: pl.BlockSpec((None, None, tm, D), lambda b, i: (c, b, i, 0))
    in_specs = [tok(D), pl.BlockSpec((None, 6, D), lambda b, i: (b + b0, 0, 0)), choice(0), choice(1),
                tok(rt.shape[-1])]
    args = [x1, mod, yab, yab, rt]
    aliases = {0: 0}
    if after is not None:
        in_specs.append(pl.BlockSpec(memory_space=pl.ANY))
        args.append(after)
    return pl.pallas_call(
        _combine_kernel,
        grid=(gb, T // tm),
        in_specs=in_specs,
        out_specs=tok(D),
        out_shape=jax.ShapeDtypeStruct((B, T, D), F32),
        input_output_aliases=aliases,
        compiler_params=_cparams(("parallel", "parallel")),
        name="moe_combine",
    )(*args)


MOE_GROUPS = 2


def moe_layer(h2, x1, mod, rt, w1, w3, w2):
    B = x1.shape[0]
    gb = B // MOE_GROUPS
    weights = (w1.astype(BF16), w3.astype(BF16), w2.astype(BF16))
    groups = [_moe_group(h2, rt, b0, gb, *weights) for b0 in range(0, B, gb)]
    out = x1
    for g, (ys, slot) in enumerate(groups):
        yab = ys.at[slot].get(mode='promise_in_bounds').reshape(2, gb, *x1.shape[1:])
        after = groups[g + 1][0] if g + 1 < len(groups) else None
        out = moe_combine(out, mod, yab, rt, g * gb, after)
    return out


def _moe_group(h2, rt, b0, gb, w1, w3, w2):
    _, T, D = h2.shape
    n = gb * T
    rt = rt[b0:b0 + gb]
    e_flat = jnp.concatenate([rt[..., 0].reshape(n), rt[..., 1].reshape(n)]).astype(jnp.int32)
    order = jnp.argsort(e_flat, stable=True).astype(jnp.int32)
    rank = jnp.argsort(order).astype(jnp.int32)
    onehot = jax.nn.one_hot(e_flat, N_EXPERTS, dtype=jnp.int32)
    counts = jnp.sum(onehot, axis=0)
    padded = ((counts + MOE_TM - 1) // MOE_TM) * MOE_TM
    pend = jnp.cumsum(padded)
    shift = (pend - padded) - (jnp.cumsum(counts) - counts)
    P = 2 * n + N_EXPERTS * MOE_TM
    nt = P // MOE_TM
    tile_row = jnp.arange(nt, dtype=jnp.int32) * MOE_TM
    tile_expert = jnp.sum((tile_row[:, None] >= pend[None, :]).astype(jnp.int32), axis=1)
    tile_valid = (tile_expert < N_EXPERTS).astype(jnp.int32)
    tile_expert = jnp.minimum(tile_expert, N_EXPERTS - 1)
    slot = rank + jnp.sum(onehot * shift[None, :], axis=1)
    tile_shift = jnp.sum(jax.nn.one_hot(tile_expert, N_EXPERTS, dtype=jnp.int32) * shift[None, :], axis=1)
    row_rank = jnp.arange(P, dtype=jnp.int32) - jnp.repeat(tile_shift, MOE_TM)
    src = order[jnp.clip(row_rank, 0, 2 * n - 1)] % n
    take = lambda a, idx: a.at[idx].get(mode='promise_in_bounds')
    hs = take(h2.reshape(-1, D), src + b0 * T)
    ys = moe_experts(hs, tile_expert, tile_valid, w1, w3, w2)
    return ys, slot


def _hyena(hz, conv_w, conv_b, spec, bias):
    zc = short_conv(hz, conv_w, conv_b)
    y = gated_long_conv(zc, 0, zc, D_HYENA, bias[0], spec, 0)
    return gated_long_conv(y, 0, zc, 2 * D_HYENA, bias[1], spec, D_HYENA)


def _short_hyena_kernel(z_ref, taps_ref, bias_ref, fx_ref, ft_ref, gi_ref, o_ref):
    n = ft_ref.shape[1]
    z = z_ref[...]
    u = z[:, 0:D_HYENA].astype(F32)
    for o in range(HYENA_ORDER):
        s = _dot(ft_ref[...], taps_ref[o].astype(BF16))
        x = _dot(fx_ref[...], u.astype(BF16))
        xr, xi, sr, si = x[:n], x[n:], s[:n], s[n:]
        prod = jnp.concatenate([xr * sr - xi * si, xr * si + xi * sr], axis=0).astype(BF16)
        y = _dot(gi_ref[...], prod) * (1.0 / n)
        gate = z[:, (o + 1) * D_HYENA:(o + 2) * D_HYENA].astype(F32)
        u = gate * (y + u * bias_ref[o])
    o_ref[...] = u.astype(o_ref.dtype)


def short_hyena(hz, conv_w, conv_b, taps, bias):
    zc = short_conv(hz, conv_w, conv_b)
    B, L, C = zc.shape
    n = 2 * L
    taps = jnp.concatenate([taps[:L], taps[FFT_N - L:]], axis=0)
    taps = taps.reshape(n, HYENA_ORDER, D_HYENA).transpose(1, 0, 2)
    k = np.arange(n)
    th_x = 2.0 * np.pi * ((k[:, None] * np.arange(L)[None, :]) % n) / n
    th_t = 2.0 * np.pi * ((k[:, None] * k[None, :]) % n) / n
    fx = np.concatenate([np.cos(th_x), -np.sin(th_x)], axis=0)
    ft = np.concatenate([np.cos(th_t), -np.sin(th_t)], axis=0)
    gi = np.concatenate([np.cos(th_x.T), -np.sin(th_x.T)], axis=1)
    const = lambda shape: pl.BlockSpec(shape, lambda b: (0,) * len(shape))
    return pl.pallas_call(
        _short_hyena_kernel,
        grid=(B,),
        in_specs=[pl.BlockSpec((None, L, C), lambda b: (b, 0, 0)), const((HYENA_ORDER, n, D_HYENA)),
                  const((HYENA_ORDER, 1, D_HYENA)), const((2 * n, L)), const((2 * n, n)), const((L, 2 * n))],
        out_specs=pl.BlockSpec((None, L, D_HYENA), lambda b: (b, 0, 0)),
        out_shape=jax.ShapeDtypeStruct((B, L, D_HYENA), BF16),
        compiler_params=_cparams(("parallel",)),
        name="short_hyena",
    )(zc, taps, bias.reshape(HYENA_ORDER, 1, D_HYENA), _bf16_table(fx), _bf16_table(ft), _bf16_table(gi))


def _split_route(outs, mod):
    x1, h2 = outs[0], outs[1]
    return h2, x1, mod, (outs[2] if len(outs) > 2 else None)


def kernel(x, c, ctx, c_ctx, w_ada, b_ada, g_norm1, g_norm2, w_in, w_out, g_q, g_k, rpb, hy_conv_w, hy_conv_b, filt_w1, filt_b1, filt_w2, filt_b2, filt_w3, filt_b3, filt_freq, filt_w_out, hy_bias, lam_re, lam_im, log_step, b_re, b_im, c_re, c_im, d_skip, w_glu, b_glu, ffn_w1, ffn_w3, ffn_w2, router_w, moe_w1, moe_w3, moe_w2):
    B, L, D = x.shape
    Lc = ctx.shape[1]
    c8 = jnp.zeros((8, D), F32).at[:B].set(c).at[B].set(c_ctx)
    mod_all = adaln_mod(c8, w_ada, b_ada)
    for layer in range(DEPTH):
        last = layer == DEPTH - 1
        m = mod_all[layer].reshape(8, 6, D)
        mod_x = m[:B]
        mod_c = jnp.broadcast_to(m[B], (B, 6, D))
        w_l = w_in[layer].astype(BF16)
        wo_l = w_out[layer].astype(BF16)
        filt = (filt_w1[layer], filt_b1[layer], filt_w2[layer], filt_b2[layer],
                filt_w3[layer], filt_b3[layer], filt_freq[layer], filt_w_out[layer])
        ssm = (lam_re[layer], lam_im[layer], log_step[layer], b_re[layer], b_im[layer], c_re[layer], c_im[layer])

        q_c, k_c, v_c, hz_c, u_c = in_proj(ctx, mod_c, g_norm1[layer], w_l, g_q[layer], g_k[layer])
        q, k, v, hz, u = in_proj(x, mod_x, g_norm1[layer], w_l, g_q[layer], g_k[layer])
        att = neighbourhood_attention(q, k, v, k_c, v_c, rpb[layer])
        spec = filter_spectrum(hyena_filter_taps(L, *filt))
        hy = _hyena(hz, hy_conv_w[layer], hy_conv_b[layer], spec, hy_bias[layer])
        un, yf, yb = s5_scan(u, u_c, *ssm)

        tail = (d_skip[layer], w_glu[layer], b_glu[layer], wo_l, g_norm2[layer])
        i = layer // 2
        if layer % 2 == 0:
            ffw = (ffn_w1[i], ffn_w3[i], ffn_w2[i])
            mixer = lambda h2, x1, mod, rt: ffn_dense(h2, x1, mod, *ffw)
            rw = None
        else:
            mow = (moe_w1[i], moe_w3[i], moe_w2[i])
            mixer = lambda h2, x1, mod, rt: moe_layer(h2, x1, mod, rt, *mow)
            rw = router_w[i]
        x = mixer(*_split_route(out_proj(x, mod_x, att, hy, yf, yb, un, 0, *tail, router_w=rw), mod_x))

        if not last:
            att_c = context_attention(q_c, k_c, v_c)
            hy_c = short_hyena(hz_c, hy_conv_w[layer], hy_conv_b[layer], hyena_filter_taps(Lc, *filt),
                               hy_bias[layer])
            ctx = mixer(*_split_route(out_proj(ctx, mod_c, att_c, hy_c, yf, yb, un, L, *tail, router_w=rw), mod_c))
    return x
```

```python
import functools
import math

import numpy as np
import jax
import jax.numpy as jnp
from jax import lax
from jax.experimental import pallas as pl
from jax.experimental.pallas import tpu as pltpu

F32 = jnp.float32
BF16 = jnp.bfloat16
HIGHEST = lax.Precision.HIGHEST

D_MODEL = 1024
DEPTH = 2
GRID_W = 64
ATT_HEAD_DIM = 64
ATT_HEADS = 8
D_ATT = 512
WIN_ROWS = 8
WIN_COLS = 16
D_HYENA = 256
HYENA_ORDER = 2
FILTER_EMB = 33
FILTER_BANDS = 16
FILTER_WIDTH = 64
MIN_DECAY = math.log(1e-2) / 1.5
MAX_DECAY = math.log(1e-2) / 0.3
D_SSM = 256
SSM_GROUP_WIDTH = 16
SSM_GROUPS = 16
SSM_STATE = 64
COL_K = D_ATT
COL_V = 2 * D_ATT
COL_HY = 3 * D_ATT
COL_SSM = COL_HY + 3 * D_HYENA
D_IN = COL_SSM + D_SSM
D_FF = 2816
N_EXPERTS = 8
EPS = 1e-6
NEG = -1e30

VMEM_LIMIT = 56 * 1024 * 1024


def _cparams(sem, vmem=None):
    return pltpu.CompilerParams(dimension_semantics=sem, vmem_limit_bytes=vmem)


def _dot(a, b):
    return jnp.dot(a, b, preferred_element_type=F32)


def _dot_nt(a, b):
    return lax.dot_general(a, b, (((1,), (1,)), ((), ())), preferred_element_type=F32)


def _mod_kernel(c_ref, w_ref, b_ref, o_ref):
    c = c_ref[...]
    s = c * jax.nn.sigmoid(c)
    o_ref[...] = jnp.dot(s, w_ref[...], precision=HIGHEST, preferred_element_type=F32) + b_ref[...]


def adaln_mod(c8, w_ada, b_ada, tn=512):
    depth, d, n = w_ada.shape
    return pl.pallas_call(
        _mod_kernel,
        grid=(depth, n // tn),
        in_specs=[pl.BlockSpec((8, d), lambda l, j: (0, 0)),
                  pl.BlockSpec((None, d, tn), lambda l, j: (l, 0, j)),
                  pl.BlockSpec((None, 1, tn), lambda l, j: (l, 0, j))],
        out_specs=pl.BlockSpec((None, 8, tn), lambda l, j: (l, 0, j)),
        out_shape=jax.ShapeDtypeStruct((depth, 8, n), F32),
        compiler_params=_cparams(("parallel", "parallel")),
        name="adaln_mod",
    )(c8, w_ada, b_ada.reshape(depth, 1, n))


def _rms_mod(x, g, shift, scale):
    ms = jnp.mean(x * x, axis=-1, keepdims=True)
    h = x * lax.rsqrt(ms + EPS) * g
    return h * (1.0 + scale) + shift


def _head_norm(z, a, g):
    zz = (z * z).astype(BF16)
    m = jnp.concatenate([_dot(zz[:, c:c + ATT_LANES], a) for c in range(0, z.shape[1], ATT_LANES)], axis=1)
    return z * lax.rsqrt(m + EPS) * g


TOK_TB = 128


def _inproj_kernel(x_ref, mod_ref, g_ref, w_ref, a_ref, gq_ref, gk_ref,
                   q_ref, k_ref, v_ref, hz_ref, u_ref):
    B, tb, _ = x_ref.shape
    h = jnp.concatenate([_rms_mod(x_ref[b], g_ref[...], mod_ref[b, 0:1, :], mod_ref[b, 1:2, :]).astype(BF16)
                         for b in range(B)], axis=0)
    a = a_ref[...]
    w = lambda lo, hi: w_ref[:, lo:hi].astype(BF16)
    q = (_head_norm(_dot(h, w(0, COL_K)), a, gq_ref[...]) * (ATT_HEAD_DIM ** -0.5)).astype(BF16)
    k = _head_norm(_dot(h, w(COL_K, COL_V)), a, gk_ref[...]).astype(BF16)
    v = _dot(h, w(COL_V, COL_HY)).astype(BF16)
    hz = _dot(h, w(COL_HY, COL_SSM))
    u = _dot(h, w(COL_SSM, D_IN))
    for b in range(B):
        rows = slice(b * tb, (b + 1) * tb)
        q_ref[b] = q[rows]
        k_ref[b] = k[rows]
        v_ref[b] = v[rows]
        hz_ref[b] = hz[rows].astype(hz_ref.dtype)
        for s in range(D_SSM // 128):
            u_ref[s, pl.ds(b, tb, stride=B), :] = u[rows, s * 128:(s + 1) * 128]


def in_proj(x, mod, g1, w_in_bf, gq, gk):
    B, T, D = x.shape
    tb = TOK_TB
    head_avg = jnp.asarray(np.kron(np.eye(ATT_LANES // ATT_HEAD_DIM),
                                   np.full((ATT_HEAD_DIM, ATT_HEAD_DIM), 1.0 / ATT_HEAD_DIM)), BF16)
    tok = lambda n: pl.BlockSpec((B, tb, n), lambda i: (0, i, 0))
    const = lambda shape: pl.BlockSpec(shape, lambda i: (0,) * len(shape))
    return pl.pallas_call(
        _inproj_kernel,
        grid=(T // tb,),
        in_specs=[tok(D), const((B, 6, D)), const((1, D)),
                  pl.BlockSpec((D, D_IN), lambda i: (0, 0), pipeline_mode=pl.Buffered(1)),
                  const((ATT_LANES, ATT_LANES)), const((1, D_ATT)), const((1, D_ATT))],
        out_specs=[tok(D_ATT), tok(D_ATT), tok(D_ATT), tok(3 * D_HYENA),
                   pl.BlockSpec((D_SSM // 128, tb * B, 128), lambda i: (0, i, 0))],
        out_shape=[jax.ShapeDtypeStruct((B, T, D_ATT), BF16)] * 3
                  + [jax.ShapeDtypeStruct((B, T, 3 * D_HYENA), BF16),
                     jax.ShapeDtypeStruct((D_SSM // 128, T * B, 128), F32)],
        compiler_params=_cparams(("parallel",), VMEM_LIMIT),
        name="in_proj",
    )(x, mod, g1.reshape(1, D), w_in_bf, head_avg,
      jnp.tile(gq, ATT_HEADS).reshape(1, D_ATT), jnp.tile(gk, ATT_HEADS).reshape(1, D_ATT))


ATT_R = 4
ATT_KROWS = ATT_R + WIN_ROWS - 1
ATT_LANES = 256


def _softmax_heads(q, parts, bias_ref, o_ref):
    nh = ATT_LANES // ATT_HEAD_DIM
    lane = lax.broadcasted_iota(jnp.int32, (1, ATT_LANES), 1)
    head = lambda h: (lane >= ATT_HEAD_DIM * h) & (lane < ATT_HEAD_DIM * (h + 1))
    acc = jnp.zeros((q.shape[0], ATT_LANES), F32)
    for h in range(nh):
        hm = head(h)
        qh = jnp.where(hm, q, jnp.zeros_like(q))
        ss = [_dot_nt(qh, kk).astype(BF16) for kk, _ in parts]
        if bias_ref is not None:
            ss[0] = ss[0] + bias_ref[h]
        m = ss[0].max(axis=-1, keepdims=True)
        for s in ss[1:]:
            m = jnp.maximum(m, s.max(axis=-1, keepdims=True))
        om = head((h + 1) % nh)
        o = None
        for s, (_, vv) in zip(ss, parts):
            part = _dot(jnp.exp(s - m), jnp.where(om, jnp.ones_like(vv), vv))
            o = part if o is None else o + part
        l = pltpu.roll(o, ATT_LANES - ATT_HEAD_DIM, axis=1)
        acc = jnp.where(hm, o / l, acc)
    o_ref[...] = acc.astype(o_ref.dtype)


ATT_SUB = 8


def _nattn_kernel(q_ref, k_ref, v_ref, kc_ref, vc_ref, *rest, rows):
    bias_refs, o_ref = rest[:ATT_SUB], rest[ATT_SUB]
    tq = ATT_R * GRID_W
    ctx = (kc_ref[...], vc_ref[...])
    for sub in range(ATT_SUB):
        blk = pl.program_id(2) * ATT_SUB + sub
        ks = jnp.clip(blk * ATT_R - WIN_ROWS // 2, 0, rows - ATT_KROWS)
        start = pl.multiple_of(ks * GRID_W, GRID_W)
        kl = k_ref[pl.ds(start, ATT_KROWS * GRID_W), :]
        vl = v_ref[pl.ds(start, ATT_KROWS * GRID_W), :]
        qrows = slice(sub * tq, (sub + 1) * tq)
        _softmax_heads(q_ref[qrows, :], [(kl, vl), ctx], bias_refs[sub], o_ref.at[qrows, :])


def _attn_bias_tables(rpb, rows):
    nblk = rows // ATT_R
    qc = np.arange(GRID_W)
    c0 = np.clip(qc - WIN_COLS // 2, 0, GRID_W - WIN_COLS)
    kc = np.arange(GRID_W)
    col_ok = (kc[None, :] >= c0[:, None]) & (kc[None, :] < c0[:, None] + WIN_COLS)
    col_off = kc[None, :] - qc[:, None] + WIN_COLS - 1
    col_sel = (col_off[:, :, None] == np.arange(2 * WIN_COLS - 1)) & col_ok[:, :, None]
    band = jnp.einsum('hij,qkj->hqik', rpb, jnp.asarray(col_sel, F32), precision=HIGHEST)
    band = jnp.where(jnp.asarray(col_ok)[None, :, None, :], band, NEG)
    band = band.reshape(ATT_HEADS, GRID_W, (2 * WIN_ROWS - 1) * GRID_W).astype(BF16)
    neg = lambda nrows: jnp.full((ATT_HEADS, GRID_W, nrows * GRID_W), NEG, BF16)
    tabs = []
    for blk in (0, 1, nblk - 1):
        ks = int(np.clip(blk * ATT_R - WIN_ROWS // 2, 0, rows - ATT_KROWS))
        qrows = []
        for r in range(blk * ATT_R, (blk + 1) * ATT_R):
            r0 = int(np.clip(r - WIN_ROWS // 2, 0, rows - WIN_ROWS))
            first = r0 - r + WIN_ROWS - 1
            win = band[:, :, first * GRID_W:(first + WIN_ROWS) * GRID_W]
            qrows.append(jnp.concatenate([neg(r0 - ks), win, neg(ATT_KROWS - WIN_ROWS - (r0 - ks))], axis=-1))
        tabs.append(jnp.stack(qrows, axis=1).reshape(ATT_HEADS, ATT_R * GRID_W, ATT_KROWS * GRID_W))
    return jnp.stack(tabs)


def neighbourhood_attention(q, k, v, kc, vc, rpb):
    B, L, _ = q.shape
    Lc = kc.shape[1]
    rows = L // GRID_W
    nblk = rows // ATT_R
    tq = ATT_R * GRID_W
    nkl = ATT_KROWS * GRID_W
    bias = _attn_bias_tables(rpb, rows)
    hpb = ATT_LANES // ATT_HEAD_DIM
    variant = lambda blk: jnp.where(blk == 0, 0, jnp.where(blk == nblk - 1, 2, 1))
    bias_spec = lambda sub: pl.BlockSpec((None, hpb, tq, nkl),
                                         lambda b, g, j: (variant(j * ATT_SUB + sub), g, 0, 0))
    return pl.pallas_call(
        functools.partial(_nattn_kernel, rows=rows),
        grid=(B, D_ATT // ATT_LANES, nblk // ATT_SUB),
        in_specs=[pl.BlockSpec((None, ATT_SUB * tq, ATT_LANES), lambda b, g, j: (b, j, g)),
                  pl.BlockSpec((None, L, ATT_LANES), lambda b, g, j: (b, 0, g)),
                  pl.BlockSpec((None, L, ATT_LANES), lambda b, g, j: (b, 0, g)),
                  pl.BlockSpec((None, Lc, ATT_LANES), lambda b, g, j: (b, 0, g)),
                  pl.BlockSpec((None, Lc, ATT_LANES), lambda b, g, j: (b, 0, g))]
                 + [bias_spec(sub) for sub in range(ATT_SUB)],
        out_specs=pl.BlockSpec((None, ATT_SUB * tq, ATT_LANES), lambda b, g, j: (b, j, g)),
        out_shape=jax.ShapeDtypeStruct((B, L, D_ATT), BF16),
        compiler_params=_cparams(("parallel", "parallel", "arbitrary"), VMEM_LIMIT),
        name="nattn",
    )(q, k, v, kc, vc, *([bias] * ATT_SUB))


def _cattn_kernel(q_ref, k_ref, v_ref, o_ref):
    _softmax_heads(q_ref[...], [(k_ref[...], v_ref[...])], None, o_ref)


def context_attention(q, k, v):
    B, Lc, _ = q.shape
    spec = pl.BlockSpec((None, Lc, ATT_LANES), lambda b, g: (b, 0, g))
    return pl.pallas_call(
        _cattn_kernel,
        grid=(B, D_ATT // ATT_LANES),
        in_specs=[spec, spec, spec],
        out_specs=spec,
        out_shape=jax.ShapeDtypeStruct((B, Lc, D_ATT), BF16),
        compiler_params=_cparams(("parallel", "parallel")),
        name="cattn",
    )(q, k, v)


FFT_N = 16384
FFT_R = 128
FFT_K1 = FFT_R // 2 + 1
FFT_K1P = 66
FFT_PITCH = 2 * FFT_K1P


def _filter_kernel(w1_ref, wt_ref, b1_ref, w2_ref, b2_ref, w3_ref, b3_ref, fr_ref, wo_ref, o_ref, *, lf, tile):
    i = pl.program_id(0)
    half = FFT_N // 2
    row0 = i * tile
    first_pos = jnp.where(row0 < half, row0, FFT_N - row0 - tile)

    @pl.when(first_pos >= lf)
    def _():
        o_ref[...] = jnp.zeros_like(o_ref)

    @pl.when(first_pos < lf)
    def _():
        rows = tile // FILTER_PACK

        def position(width):
            r = lax.broadcasted_iota(jnp.int32, (rows, FILTER_PACK * width), 0)
            g = lax.broadcasted_iota(jnp.int32, (rows, FILTER_PACK * width), 1) // width
            j = row0 + r + g * rows
            return jnp.where(j < half, j, FFT_N - 1 - j)

        nfeat = 2 * FILTER_BANDS
        feat = lax.broadcasted_iota(jnp.int32, (1, FILTER_PACK * nfeat), 1) % nfeat
        f = 1e-4 + (feat % FILTER_BANDS).astype(F32) * ((FILTER_BANDS - 1 - 1e-4) / (FILTER_BANDS - 1))
        w = (2.0 * math.pi / lf) * position(nfeat).astype(F32)
        z = jnp.cos(f * w + jnp.where(feat < FILTER_BANDS, 0.0, 0.5 * math.pi))
        t_h = position(FILTER_WIDTH).astype(F32) / (lf - 1.0)
        fr = fr_ref[...]
        hdot = lambda a, b: jnp.dot(a, b, precision=HIGHEST, preferred_element_type=F32)
        h = jnp.sin(fr * (hdot(z, w1_ref[...]) + t_h * wt_ref[...] + b1_ref[...]))
        h = jnp.sin(fr * (hdot(h, w2_ref[...]) + b2_ref[...]))
        h = jnp.sin(fr * (hdot(h, w3_ref[...]) + b3_ref[...]))
        y = hdot(h, wo_ref[...])
        nout = HYENA_ORDER * D_HYENA
        pos_o = position(nout)
        ch = lax.broadcasted_iota(jnp.int32, (1, FILTER_PACK * nout), 1) % D_HYENA
        delta = jnp.abs(MIN_DECAY + ch.astype(F32) * ((MAX_DECAY - MIN_DECAY) / (D_HYENA - 1)))
        y = jnp.where(pos_o < lf, y * jnp.exp(-(pos_o.astype(F32) / (lf - 1.0)) * delta), 0.0)
        for g in range(FILTER_PACK):
            o_ref[g * rows:(g + 1) * rows, :] = y[:, g * nout:(g + 1) * nout]


FILTER_PACK = 4


def hyena_filter_taps(lf, w1, b1, w2, b2, w3, b3, freq, w_out, tile=1024):
    P, W = FILTER_PACK, FILTER_WIDTH
    nout = HYENA_ORDER * D_HYENA
    bd = lambda m: jnp.kron(jnp.eye(P, dtype=F32), m.astype(F32))
    wide = lambda a: jnp.tile(a.astype(F32), P).reshape(1, P * W)
    wo = w_out.reshape(W, HYENA_ORDER, 2, D_HYENA).transpose(2, 0, 1, 3).reshape(2, W, nout)
    wo = jnp.stack([bd(wo[0]), bd(wo[1])])
    nt = FFT_N // tile
    const = lambda shape: pl.BlockSpec(shape, lambda i: (0,) * len(shape))
    return pl.pallas_call(
        functools.partial(_filter_kernel, lf=lf, tile=tile),
        grid=(nt,),
        in_specs=[const((P * (FILTER_EMB - 1), P * W)), const((1, P * W)), const((1, P * W)),
                  const((P * W, P * W)), const((1, P * W)), const((P * W, P * W)), const((1, P * W)),
                  const((1, P * W)),
                  pl.BlockSpec((None, P * W, P * nout), lambda i: (i // (nt // 2), 0, 0))],
        out_specs=pl.BlockSpec((tile, nout), lambda i: (i, 0)),
        out_shape=jax.ShapeDtypeStruct((FFT_N, nout), F32),
        compiler_params=_cparams(("parallel",)),
        name="hyena_filter",
    )(bd(w1[1:]), wide(w1[0]), wide(b1), bd(w2), wide(b2), bd(w3), wide(b3), wide(freq), wo)


def _shortconv_kernel(z_ref, w_ref, b_ref, o_ref, zp_ref, *, L):
    C = z_ref.shape[-1]
    zp_ref[0:8, :] = jnp.zeros((8, C), F32)
    zp_ref[L + 8:L + 16, :] = jnp.zeros((8, C), F32)
    zp_ref[8:L + 8, :] = z_ref[...].astype(F32)
    ch = min(L, 512)
    for c in range(L // ch):
        s = c * ch
        o_ref[s:s + ch, :] = (b_ref[...] + zp_ref[s + 7:s + 7 + ch, :] * w_ref[0:1, :]
                              + zp_ref[s + 8:s + 8 + ch, :] * w_ref[1:2, :]
                              + zp_ref[s + 9:s + 9 + ch, :] * w_ref[2:3, :]).astype(o_ref.dtype)


def short_conv(z, w, b, cb=128):
    B, L, C = z.shape
    return pl.pallas_call(
        functools.partial(_shortconv_kernel, L=L),
        grid=(B, C // cb),
        in_specs=[pl.BlockSpec((None, L, cb), lambda b_, c: (b_, 0, c)),
                  pl.BlockSpec((3, cb), lambda b_, c: (0, c)),
                  pl.BlockSpec((1, cb), lambda b_, c: (0, c))],
        out_specs=pl.BlockSpec((None, L, cb), lambda b_, c: (b_, 0, c)),
        out_shape=jax.ShapeDtypeStruct((B, L, C), BF16),
        scratch_shapes=[pltpu.VMEM((L + 16, cb), F32)],
        compiler_params=_cparams(("parallel", "parallel")),
        name="short_conv",
    )(z, w, b.reshape(1, C))


@functools.lru_cache(maxsize=None)
def _fft_tables(n1_in):
    n2 = np.arange(FFT_R)[:, None, None]
    k1 = np.arange(FFT_K1)[None, :, None]
    n1 = np.arange(n1_in)[None, None, :]
    th = 2.0 * np.pi * ((k1 * (FFT_R * n1 + n2)) % FFT_N) / FFT_N
    ga = np.zeros((FFT_R, 2 * FFT_K1P, n1_in))
    ga[:, :FFT_K1] = np.cos(th)
    ga[:, FFT_K1P:FFT_K1P + FFT_K1] = -np.sin(th)
    wk = np.where((np.arange(FFT_K1) == 0) | (np.arange(FFT_K1) == FFT_R // 2), 1.0, 2.0)[None, None, :]
    thd = th.transpose(0, 2, 1)
    gd = np.zeros((FFT_R, n1_in, 2 * FFT_K1P))
    gd[:, :, :FFT_K1] = wk * np.cos(thd)
    gd[:, :, FFT_K1P:FFT_K1P + FFT_K1] = -wk * np.sin(thd)
    a = np.arange(FFT_R)
    ph = 2.0 * np.pi * ((a[:, None] * a[None, :]) % FFT_R) / FFT_R
    c, s = np.cos(ph), np.sin(ph)
    fb = np.block([[c, s], [-s, c]])
    fc = np.block([[c, -s], [s, c]])
    return ga, gd, fb, fc


FFT_KG = 13
FFT_UNROLL = 16


def _spectrum_slabs(ya_ref, fb, k0):
    cols = []
    for g in range(FFT_KG):
        re = ya_ref[pl.ds(k0 + g, FFT_R, stride=FFT_PITCH), :]
        im = ya_ref[pl.ds(FFT_K1P + k0 + g, FFT_R, stride=FFT_PITCH), :]
        cols.append(jnp.concatenate([re, im], axis=0).astype(BF16))
    return _dot(fb, jnp.concatenate(cols, axis=1))


def _stage_a(x_ref, x_pitch, ga_ref, ya_ref, n1_in):
    def body(n2, carry):
        xs = x_ref[pl.ds(n2, n1_in, stride=x_pitch), :].astype(BF16)
        ya_ref[pl.ds(n2 * FFT_PITCH, FFT_PITCH), :] = _dot(ga_ref[n2], xs)
        return carry
    lax.fori_loop(0, FFT_R, body, 0, unroll=FFT_UNROLL)


def _spectrum_kernel(x_ref, ga_ref, fb_ref, o_ref, ya_ref):
    _stage_a(x_ref, FFT_R, ga_ref, ya_ref, FFT_R)
    fb = fb_ref[...]
    cb = ya_ref.shape[1]

    def body(kg, carry):
        x = _spectrum_slabs(ya_ref, fb, kg * FFT_KG)
        for g in range(FFT_KG):
            o_ref[kg * FFT_KG + g] = x[:, g * cb:(g + 1) * cb]
        return carry
    lax.fori_loop(0, FFT_K1 // FFT_KG, body, 0)


def filter_spectrum(taps, cb=128):
    C = taps.shape[1]
    ga, _, fb, _ = _fft_tables(FFT_R)
    return pl.pallas_call(
        _spectrum_kernel,
        grid=(C // cb,),
        in_specs=[pl.BlockSpec((FFT_N, cb), lambda c: (0, c)),
                  pl.BlockSpec((FFT_R, FFT_PITCH, FFT_R), lambda c: (0, 0, 0)),
                  pl.BlockSpec((2 * FFT_R, 2 * FFT_R), lambda c: (0, 0))],
        out_specs=pl.BlockSpec((FFT_K1, 2 * FFT_R, cb), lambda c: (0, 0, c)),
        out_shape=jax.ShapeDtypeStruct((FFT_K1, 2 * FFT_R, C), F32),
        scratch_shapes=[pltpu.VMEM((FFT_R * FFT_PITCH, cb), F32)],
        compiler_params=_cparams(("parallel",), VMEM_LIMIT),
        name="filter_spectrum",
    )(taps, _bf16_table(ga), _bf16_table(fb))


def _bf16_table(t):
    return jnp.asarray(t, F32).astype(BF16)


def _longconv_kernel(u_ref, gate_ref, bias_ref, spec_ref, ga_ref, gd_ref, fb_ref, fc_ref, o_ref, ya_ref, xp_ref):
    n1_in = FFT_R // 2

    def repitch(n1, carry):
        xp_ref[pl.ds(n1 * FFT_PITCH, FFT_R), :] = (
            u_ref[pl.ds(pl.multiple_of(n1 * FFT_R, FFT_R), FFT_R), :].astype(F32))
        return carry
    lax.fori_loop(0, n1_in, repitch, 0, unroll=FFT_UNROLL)

    _stage_a(xp_ref, FFT_PITCH, ga_ref, ya_ref, n1_in)
    fb = fb_ref[...]
    fc = fc_ref[...]

    cb = ya_ref.shape[1]

    def mid(kg, carry):
        k0 = kg * FFT_KG
        x = _spectrum_slabs(ya_ref, fb, k0)
        s = jnp.concatenate([spec_ref[k0 + g] for g in range(FFT_KG)], axis=1)
        xr, xi = x[:FFT_R], x[FFT_R:]
        sr, si = s[:FFT_R], s[FFT_R:]
        z = jnp.concatenate([xr * sr - xi * si, xr * si + xi * sr], axis=0).astype(BF16)
        v = _dot(fc, z)
        for g in range(FFT_KG):
            ya_ref[pl.ds(k0 + g, FFT_R, stride=FFT_PITCH), :] = v[:FFT_R, g * cb:(g + 1) * cb]
            ya_ref[pl.ds(FFT_K1P + k0 + g, FFT_R, stride=FFT_PITCH), :] = v[FFT_R:, g * cb:(g + 1) * cb]
        return carry
    lax.fori_loop(0, FFT_K1 // FFT_KG, mid, 0)

    def last(n2, carry):
        slab = ya_ref[pl.ds(n2 * FFT_PITCH, FFT_PITCH), :].astype(BF16)
        xp_ref[pl.ds(n2, n1_in, stride=FFT_PITCH), :] = _dot(gd_ref[n2], slab)
        return carry
    lax.fori_loop(0, FFT_R, last, 0, unroll=FFT_UNROLL)

    bias = bias_ref[...]

    def gate(n1, carry):
        rows = pl.ds(pl.multiple_of(n1 * FFT_R, FFT_R), FFT_R)
        y = xp_ref[pl.ds(n1 * FFT_PITCH, FFT_R), :] * (1.0 / FFT_N)
        o_ref[rows, :] = (gate_ref[rows, :].astype(F32) * (y + u_ref[rows, :].astype(F32) * bias)).astype(o_ref.dtype)
        return carry
    lax.fori_loop(0, n1_in, gate, 0, unroll=FFT_UNROLL)


def gated_long_conv(u, u_col, gate, gate_col, bias, spec, spec_col, C=D_HYENA, cb=128):
    B, L, _ = u.shape
    n1_in = FFT_R // 2
    ga, gd, fb, fc = _fft_tables(n1_in)
    once = pl.Buffered(1)
    seq = lambda col: pl.BlockSpec((None, L, cb), lambda c, b: (b, 0, c + col // cb))
    return pl.pallas_call(
        _longconv_kernel,
        grid=(C // cb, B),
        in_specs=[seq(u_col), seq(gate_col),
                  pl.BlockSpec((1, cb), lambda c, b: (0, c)),
                  pl.BlockSpec((FFT_K1, 2 * FFT_R, cb), lambda c, b: (0, 0, c + spec_col // cb), pipeline_mode=once),
                  pl.BlockSpec((FFT_R, FFT_PITCH, n1_in), lambda c, b: (0, 0, 0), pipeline_mode=once),
                  pl.BlockSpec((FFT_R, n1_in, FFT_PITCH), lambda c, b: (0, 0, 0), pipeline_mode=once),
                  pl.BlockSpec((2 * FFT_R, 2 * FFT_R), lambda c, b: (0, 0)),
                  pl.BlockSpec((2 * FFT_R, 2 * FFT_R), lambda c, b: (0, 0))],
        out_specs=seq(0),
        out_shape=jax.ShapeDtypeStruct((B, L, C), BF16),
        scratch_shapes=[pltpu.VMEM((FFT_R * FFT_PITCH, cb), F32), pltpu.VMEM((n1_in * FFT_PITCH, cb), F32)],
        compiler_params=_cparams(("parallel", "arbitrary"), VMEM_LIMIT),
        name="long_conv",
    )(u, gate, bias.reshape(1, C), spec, _bf16_table(ga), _bf16_table(gd), _bf16_table(fb), _bf16_table(fc))


SSM_N = SSM_GROUPS * SSM_STATE
S5_SEQS = 8
S5_CHUNK = 256


def _s5_kernel(uf_ref, ub_ref, bf_ref, bb_ref, lam_ref, cf_ref, cb_ref, yf_ref, yb_ref,
               buf_ref, bub_ref, xsf_ref, xsb_ref, st_ref):
    half = S5_SEQS // 2
    tiles = S5_CHUNK // 2

    @pl.when(pl.program_id(0) == 0)
    def _():
        st_ref[...] = jnp.zeros_like(st_ref)

    slabs = lambda ref: jnp.concatenate([ref[s] for s in range(ref.shape[0])], axis=1).astype(BF16)
    buf_ref[...] = _dot(slabs(uf_ref), bf_ref[...])
    bub_ref[...] = _dot(slabs(ub_ref), bb_ref[...])
    lo = lax.broadcasted_iota(jnp.int32, (S5_SEQS, 1), 0) < half

    def advance(x, lam, b):
        xr, xi = x
        lr, li = lam[:, 0:SSM_N], lam[:, SSM_N:2 * SSM_N]
        return lr * xr - li * xi + b[:, 0:SSM_N], lr * xi + li * xr + b[:, SSM_N:2 * SSM_N]

    def swap(x):
        return pltpu.roll(x[0], half, axis=0), pltpu.roll(x[1], half, axis=0)

    def step(t4, x):
        f_tiles, b_tiles = [], []
        for p in range(2):
            tf = buf_ref[pl.ds(pl.multiple_of((2 * t4 + p) * S5_SEQS, S5_SEQS), S5_SEQS), :]
            tb = bub_ref[pl.ds(pl.multiple_of((tiles - 1 - 2 * t4 - p) * S5_SEQS, S5_SEQS), S5_SEQS), :]
            xe = advance(x, lam_ref[0], jnp.where(lo, tf, tb))
            xo = advance(swap(xe), lam_ref[1], jnp.where(lo, tb, tf))
            x = swap(xo)
            xe = jnp.concatenate(xe, axis=1)
            xo = jnp.concatenate(xo, axis=1)
            f_tiles.append(jnp.where(lo, xe, xo))
            b_tiles.append(jnp.where(lo, xo, xe))
        xsf_ref[pl.ds(pl.multiple_of(t4 * 2 * S5_SEQS, 2 * S5_SEQS), 2 * S5_SEQS), :] = (
            jnp.concatenate(f_tiles, axis=0).astype(BF16))
        xsb_ref[pl.ds(pl.multiple_of((tiles - 2 - 2 * t4) * S5_SEQS, 2 * S5_SEQS), 2 * S5_SEQS), :] = (
            jnp.concatenate(b_tiles[::-1], axis=0).astype(BF16))
        return x

    xr, xi = lax.fori_loop(0, S5_CHUNK // 4, step, (st_ref[:, 0:SSM_N], st_ref[:, SSM_N:2 * SSM_N]))
    st_ref[:, 0:SSM_N] = xr
    st_ref[:, SSM_N:2 * SSM_N] = xi
    yf = _dot(xsf_ref[...], cf_ref[...])
    yb = _dot(xsb_ref[...], cb_ref[...])
    for s in range(yf_ref.shape[0]):
        yf_ref[s] = yf[:, s * 128:(s + 1) * 128]
        yb_ref[s] = yb[:, s * 128:(s + 1) * 128]


def _s5_matrices(lam_re, lam_im, log_step, b_re, b_im, c_re, c_im):
    G, P, Hs = SSM_GROUPS, SSM_STATE, SSM_GROUP_WIDTH
    dt = jnp.exp(log_step)[:, :, None]
    mag = jnp.exp(lam_re * dt)
    ar, ai = mag * jnp.cos(lam_im * dt), mag * jnp.sin(lam_im * dt)
    er, ei = ar - 1.0, ai
    den = lam_re * lam_re + lam_im * lam_im
    cr, ci = (er * lam_re + ei * lam_im) / den, (ei * lam_re - er * lam_im) / den
    bbr = cr[..., None] * b_re - ci[..., None] * b_im
    bbi = cr[..., None] * b_im + ci[..., None] * b_re
    eye = jnp.eye(G, dtype=F32)
    blk_in = lambda m: jnp.einsum('gq,qph->ghqp', eye, m).reshape(G * Hs, G * P)
    bmat = [jnp.concatenate([blk_in(bbr[d]), blk_in(bbi[d])], axis=1).astype(BF16) for d in range(2)]
    blk_out = lambda m: jnp.einsum('gq,qhp->gpqh', eye, m).reshape(G * P, G * Hs)
    cmat = [jnp.concatenate([blk_out(c_re[d]), -blk_out(c_im[d])], axis=0).astype(BF16) for d in range(2)]
    lam = jnp.concatenate([ar.reshape(2, 1, G * P), ai.reshape(2, 1, G * P)], axis=2)
    lam = jnp.broadcast_to(lam, (2, S5_SEQS // 2, 2 * G * P))
    lam8 = jnp.stack([lam.reshape(S5_SEQS, 2 * G * P), lam[::-1].reshape(S5_SEQS, 2 * G * P)])
    return bmat[0], bmat[1], lam8, cmat[0], cmat[1]


def s5_scan(u_x, u_c, lam_re, lam_im, log_step, b_re, b_im, c_re, c_im):
    S = u_x.shape[0]
    half = S5_SEQS // 2
    L, Lc = u_x.shape[1] // half, u_c.shape[1] // half
    assert L % S5_CHUNK == 0 and Lc % S5_CHUNK == 0
    bf, bb, lam8, cf, cb = _s5_matrices(lam_re, lam_im, log_step, b_re, b_im, c_re, c_im)
    un = jnp.concatenate([u_x, u_c], axis=1)
    rows = S5_CHUNK * half
    nx, nc = L // S5_CHUNK, Lc // S5_CHUNK
    n = nx + nc
    fwd = pl.BlockSpec((S, rows, 128), lambda i: (0, jnp.where(i < nc, nx + i, i - nc), 0))
    bwd = pl.BlockSpec((S, rows, 128), lambda i: (0, n - 1 - i, 0))
    const = lambda shape: pl.BlockSpec(shape, lambda i: (0,) * len(shape))
    yf, yb = pl.pallas_call(
        _s5_kernel,
        grid=(n,),
        in_specs=[fwd, bwd, const((D_SSM, 2 * SSM_N)), const((D_SSM, 2 * SSM_N)),
                  const((2, S5_SEQS, 2 * SSM_N)), const((2 * SSM_N, D_SSM)), const((2 * SSM_N, D_SSM))],
        out_specs=[fwd, bwd],
        out_shape=[jax.ShapeDtypeStruct(un.shape, F32)] * 2,
        scratch_shapes=[pltpu.VMEM((rows, 2 * SSM_N), F32), pltpu.VMEM((rows, 2 * SSM_N), F32),
                        pltpu.VMEM((rows, 2 * SSM_N), BF16), pltpu.VMEM((rows, 2 * SSM_N), BF16),
                        pltpu.VMEM((S5_SEQS, 2 * SSM_N), F32)],
        compiler_params=_cparams(("arbitrary",), VMEM_LIMIT),
        name="s5_scan",
    )(un, un, bf, bb, lam8, cf, cb)
    return un, yf, yb


def _gelu_tanh(y):
    return 0.5 * y * (1.0 + jnp.tanh(math.sqrt(2.0 / math.pi) * (y + 0.044715 * (y * y * y))))


def _outproj_kernel(*refs, route):
    (x_ref, mod_ref, att_ref, hy_ref, yf_ref, yb_ref, u_ref, dsk_ref, wg_ref, bg_ref, wo_ref, g2_ref) = refs[:12]
    if route:
        wr_ref, x1_ref, h2_ref, rt_ref = refs[12:]
    else:
        x1_ref, h2_ref = refs[12:]
    B, tb, D = x_ref.shape
    rows = lambda a: a.reshape(B * tb, a.shape[-1])

    def scan_rows(ref):
        return jnp.concatenate(
            [jnp.concatenate([ref[s, pl.ds(b, tb, stride=B), :] for s in range(ref.shape[0])], axis=1)
             for b in range(B)], axis=0)

    y = scan_rows(yf_ref) + scan_rows(yb_ref) + scan_rows(u_ref) * dsk_ref[...]
    y = _gelu_tanh(y)
    ss = y * jax.nn.sigmoid(_dot(y.astype(BF16), wg_ref[...]) + bg_ref[...])
    wo = lambda lo, hi: wo_ref[lo:hi, :].astype(BF16)
    mix = (_dot(rows(att_ref[...]), wo(0, D_ATT))
           + _dot(rows(hy_ref[...]).astype(BF16), wo(D_ATT, D_ATT + D_HYENA))
           + _dot(ss.astype(BF16), wo(D_ATT + D_HYENA, D_MODEL)))
    h2s = []
    for b in range(B):
        x1 = x_ref[b] + mod_ref[b, 2:3, :] * mix[b * tb:(b + 1) * tb]
        x1_ref[b] = x1
        h2s.append(_rms_mod(x1, g2_ref[...], mod_ref[b, 3:4, :], mod_ref[b, 4:5, :]))
        h2_ref[b] = h2s[-1].astype(h2_ref.dtype)
    if route:
        h2 = jnp.concatenate(h2s, axis=0)
        h_hi = h2.astype(BF16)
        h_lo = (h2 - h_hi.astype(F32)).astype(BF16)
        logits = _dot_nt(wr_ref[0], h_hi) + _dot_nt(wr_ref[0], h_lo) + _dot_nt(wr_ref[1], h_hi)
        row = lax.broadcasted_iota(jnp.int32, (ROUTE_ROWS, 1), 0)
        lg = jnp.where(row < N_EXPERTS, logits, -jnp.inf)
        v1 = lg.max(axis=0, keepdims=True)
        i1 = jnp.where(lg == v1, row, ROUTE_ROWS).min(axis=0, keepdims=True)
        lg2 = jnp.where(row == i1, -jnp.inf, lg)
        v2 = lg2.max(axis=0, keepdims=True)
        i2 = jnp.where(lg2 == v2, row, ROUTE_ROWS).min(axis=0, keepdims=True)
        e = jnp.exp(v2 - v1)
        w1 = 1.0 / (1.0 + e)
        out_row = lax.broadcasted_iota(jnp.int32, (8, 1), 0)
        rt_ref[...] = jnp.where(out_row == 0, i1.astype(F32),
                                jnp.where(out_row == 1, i2.astype(F32),
                                          jnp.where(out_row == 2, w1, jnp.where(out_row == 3, e * w1, 0.0))))


def out_proj(x, mod, att, hy, yf, yb, u, scan_row0, d_skip, w_glu, b_glu, w_out_bf, g2, router_w=None):
    B, T, D = x.shape
    tb = TOK_TB
    route = router_w is not None
    S = u.shape[0]
    tok = lambda n: pl.BlockSpec((B, tb, n), lambda i: (0, i, 0))
    scan = pl.BlockSpec((S, tb * B, 128), lambda i: (0, i + scan_row0 // tb, 0))
    const = lambda shape: pl.BlockSpec(shape, lambda i: (0,) * len(shape))
    in_specs = [tok(D), const((B, 6, D)), tok(D_ATT), tok(D_HYENA),
                scan, scan, scan, const((1, D_SSM)), const((D_SSM, D_SSM)), const((1, D_SSM)),
                pl.BlockSpec((D, D), lambda i: (0, 0), pipeline_mode=pl.Buffered(1)), const((1, D))]
    args = [x, mod, att, hy, yf, yb, u, d_skip.reshape(1, D_SSM), w_glu.astype(BF16), b_glu.reshape(1, D_SSM),
            w_out_bf, g2.reshape(1, D)]
    out_specs = [tok(D), tok(D)]
    out_shape = [jax.ShapeDtypeStruct((B, T, D), F32), jax.ShapeDtypeStruct((B, T, D), BF16)]
    if route:
        in_specs.append(const((2, ROUTE_ROWS, D)))
        wr = jnp.pad(router_w.astype(F32).T, ((0, ROUTE_ROWS - N_EXPERTS), (0, 0)))
        wr_hi = wr.astype(BF16)
        args.append(jnp.stack([wr_hi, (wr - wr_hi.astype(F32)).astype(BF16)]))
        out_specs.append(pl.BlockSpec((None, 8, B * tb), lambda i: (i, 0, 0)))
        out_shape.append(jax.ShapeDtypeStruct((T // tb, 8, B * tb), F32))
    outs = pl.pallas_call(
        functools.partial(_outproj_kernel, route=route),
        grid=(T // tb,),
        in_specs=in_specs, out_specs=out_specs, out_shape=out_shape,
        compiler_params=_cparams(("parallel",), VMEM_LIMIT),
        name="out_proj",
    )(*args)
    outs = list(outs)
    if route:
        outs[2] = outs[2].reshape(T // tb, 8, B, tb).transpose(2, 0, 3, 1).reshape(B, T, 8)
    return outs


ROUTE_ROWS = 16


FF_CHUNK = 768
FF_VMEM_LIMIT = 60 * 1024 * 1024


def _swiglu_tile(h, w1_ref, w3_ref, w2_ref):
    acc = None
    for lo in range(0, D_FF, FF_CHUNK):
        hi = min(lo + FF_CHUNK, D_FF)
        a = _dot(h, w1_ref[:, lo:hi].astype(BF16))
        g = (a * jax.nn.sigmoid(a) * _dot(h, w3_ref[:, lo:hi].astype(BF16))).astype(BF16)
        part = _dot(g, w2_ref[lo:hi, :].astype(BF16))
        acc = part if acc is None else acc + part
    return acc


def _ffn_kernel(h_ref, x_ref, mod_ref, w1_ref, w3_ref, w2_ref, o_ref):
    o_ref[...] = x_ref[...] + mod_ref[5:6, :] * _swiglu_tile(h_ref[...], w1_ref, w3_ref, w2_ref)


def ffn_dense(h2, x1, mod, w1, w3, w2, tm=512):
    B, T, D = x1.shape
    tm = min(tm, T)
    tok = pl.BlockSpec((None, tm, D), lambda b, i: (b, i, 0))
    once = pl.Buffered(1)
    return pl.pallas_call(
        _ffn_kernel,
        grid=(B, T // tm),
        in_specs=[tok, tok, pl.BlockSpec((None, 6, D), lambda b, i: (b, 0, 0)),
                  pl.BlockSpec((D, D_FF), lambda b, i: (0, 0), pipeline_mode=once),
                  pl.BlockSpec((D, D_FF), lambda b, i: (0, 0), pipeline_mode=once),
                  pl.BlockSpec((D_FF, D), lambda b, i: (0, 0), pipeline_mode=once)],
        out_specs=tok,
        out_shape=jax.ShapeDtypeStruct((B, T, D), F32),
        compiler_params=_cparams(("parallel", "parallel"), FF_VMEM_LIMIT),
        name="ffn_dense",
    )(h2, x1, mod, w1, w3, w2)


MOE_TM = 512


def _moe_kernel(te_ref, tv_ref, h_ref, w1_ref, w3_ref, w2_ref, o_ref):
    valid = tv_ref[pl.program_id(0)] > 0

    @pl.when(valid)
    def _():
        o_ref[...] = _swiglu_tile(h_ref[...], w1_ref, w3_ref, w2_ref).astype(o_ref.dtype)

    @pl.when(jnp.logical_not(valid))
    def _():
        o_ref[...] = jnp.zeros_like(o_ref)


def moe_experts(hs, tile_expert, tile_valid, w1, w3, w2):
    P, D = hs.shape
    nt = P // MOE_TM
    tok = pl.BlockSpec((MOE_TM, D), lambda i, te, tv: (i, 0))
    return pl.pallas_call(
        _moe_kernel,
        grid_spec=pltpu.PrefetchScalarGridSpec(
            num_scalar_prefetch=2,
            grid=(nt,),
            in_specs=[tok,
                      pl.BlockSpec((None, D, D_FF), lambda i, te, tv: (te[i], 0, 0)),
                      pl.BlockSpec((None, D, D_FF), lambda i, te, tv: (te[i], 0, 0)),
                      pl.BlockSpec((None, D_FF, D), lambda i, te, tv: (te[i], 0, 0))],
            out_specs=tok),
        out_shape=jax.ShapeDtypeStruct((P, D), BF16),
        compiler_params=_cparams(("arbitrary",), FF_VMEM_LIMIT),
        name="moe_experts",
    )(tile_expert, tile_valid, hs, w1, w3, w2)


def _combine_kernel(x_ref, mod_ref, ya_ref, yb_ref, rt_ref, *rest):
    o_ref = rest[-1]
    wa = rt_ref[:, 2:3]
    wb = rt_ref[:, 3:4]
    o_ref[...] = x_ref[...] + mod_ref[5:6, :] * (wa * ya_ref[...] + wb * yb_ref[...])


def moe_combine(x1, mod, yab, rt, b0, after, tm=512):
    B, T, D = x1.shape
    gb = yab.shape[1]
    tok = lambda n: pl.BlockSpec((None, tm, n), lambda b, i: (b + b0, i, 0))
    choice = lambda c: pl.BlockSpec((None, None, tm, D), lambda b, i: (c, b, i, 0))
    in_specs = [tok(D), pl.BlockSpec((None, 6, D), lambda b, i: (b + b0, 0, 0)), choice(0), choice(1),
                tok(rt.shape[-1])]
    args = [x1, mod, yab, yab, rt]
    aliases = {0: 0}
    if after is not None:
        in_specs.append(pl.BlockSpec(memory_space=pl.ANY))
        args.append(after)
    return pl.pallas_call(
        _combine_kernel,
        grid=(gb, T // tm),
        in_specs=in_specs,
        out_specs=tok(D),
        out_shape=jax.ShapeDtypeStruct((B, T, D), F32),
        input_output_aliases=aliases,
        compiler_params=_cparams(("parallel", "parallel")),
        name="moe_combine",
    )(*args)


MOE_GROUPS = 2


def moe_layer(h2, x1, mod, rt, w1, w3, w2):
    B = x1.shape[0]
    gb = B // MOE_GROUPS
    weights = (w1.astype(BF16), w3.astype(BF16), w2.astype(BF16))
    groups = [_moe_group(h2, rt, b0, gb, *weights) for b0 in range(0, B, gb)]
    out = x1
    for g, (ys, slot) in enumerate(groups):
        yab = ys.at[slot].get(mode='promise_in_bounds').reshape(2, gb, *x1.shape[1:])
        after = groups[g + 1][0] if g + 1 < len(groups) else None
        out = moe_combine(out, mod, yab, rt, g * gb, after)
    return out


def _moe_group(h2, rt, b0, gb, w1, w3, w2):
    _, T, D = h2.shape
    n = gb * T
    rt = rt[b0:b0 + gb]
    e_flat = jnp.concatenate([rt[..., 0].reshape(n), rt[..., 1].reshape(n)]).astype(jnp.int32)
    order = jnp.argsort(e_flat, stable=True).astype(jnp.int32)
    rank = jnp.argsort(order).astype(jnp.int32)
    onehot = jax.nn.one_hot(e_flat, N_EXPERTS, dtype=jnp.int32)
    counts = jnp.sum(onehot, axis=0)
    padded = ((counts + MOE_TM - 1) // MOE_TM) * MOE_TM
    pend = jnp.cumsum(padded)
    shift = (pend - padded) - (jnp.cumsum(counts) - counts)
    P = 2 * n + N_EXPERTS * MOE_TM
    nt = P // MOE_TM
    tile_row = jnp.arange(nt, dtype=jnp.int32) * MOE_TM
    tile_expert = jnp.sum((tile_row[:, None] >= pend[None, :]).astype(jnp.int32), axis=1)
    tile_valid = (tile_expert < N_EXPERTS).astype(jnp.int32)
    tile_expert = jnp.minimum(tile_expert, N_EXPERTS - 1)
    slot = rank + jnp.sum(onehot * shift[None, :], axis=1)
    tile_shift = jnp.sum(jax.nn.one_hot(tile_expert, N_EXPERTS, dtype=jnp.int32) * shift[None, :], axis=1)
    row_rank = jnp.arange(P, dtype=jnp.int32) - jnp.repeat(tile_shift, MOE_TM)
    src = order[jnp.clip(row_rank, 0, 2 * n - 1)] % n
    take = lambda a, idx: a.at[idx].get(mode='promise_in_bounds')
    hs = take(h2.reshape(-1, D), src + b0 * T)
    ys = moe_experts(hs, tile_expert, tile_valid, w1, w3, w2)
    return ys, slot


def _hyena(hz, conv_w, conv_b, spec, bias):
    zc = short_conv(hz, conv_w, conv_b)
    y = gated_long_conv(zc, 0, zc, D_HYENA, bias[0], spec, 0)
    return gated_long_conv(y, 0, zc, 2 * D_HYENA, bias[1], spec, D_HYENA)


def _short_hyena_kernel(z_ref, taps_ref, bias_ref, fx_ref, ft_ref, gi_ref, o_ref):
    n = ft_ref.shape[1]
    z = z_ref[...]
    u = z[:, 0:D_HYENA].astype(F32)
    for o in range(HYENA_ORDER):
        s = _dot(ft_ref[...], taps_ref[o].astype(BF16))
        x = _dot(fx_ref[...], u.astype(BF16))
        xr, xi, sr, si = x[:n], x[n:], s[:n], s[n:]
        prod = jnp.concatenate([xr * sr - xi * si, xr * si + xi * sr], axis=0).astype(BF16)
        y = _dot(gi_ref[...], prod) * (1.0 / n)
        gate = z[:, (o + 1) * D_HYENA:(o + 2) * D_HYENA].astype(F32)
        u = gate * (y + u * bias_ref[o])
    o_ref[...] = u.astype(o_ref.dtype)


def short_hyena(hz, conv_w, conv_b, taps, bias):
    zc = short_conv(hz, conv_w, conv_b)
    B, L, C = zc.shape
    n = 2 * L
    taps = jnp.concatenate([taps[:L], taps[FFT_N - L:]], axis=0)
    taps = taps.reshape(n, HYENA_ORDER, D_HYENA).transpose(1, 0, 2)
    k = np.arange(n)
    th_x = 2.0 * np.pi * ((k[:, None] * np.arange(L)[None, :]) % n) / n
    th_t = 2.0 * np.pi * ((k[:, None] * k[None, :]) % n) / n
    fx = np.concatenate([np.cos(th_x), -np.sin(th_x)], axis=0)
    ft = np.concatenate([np.cos(th_t), -np.sin(th_t)], axis=0)
    gi = np.concatenate([np.cos(th_x.T), -np.sin(th_x.T)], axis=1)
    const = lambda shape: pl.BlockSpec(shape, lambda b: (0,) * len(shape))
    return pl.pallas_call(
        _short_hyena_kernel,
        grid=(B,),
        in_specs=[pl.BlockSpec((None, L, C), lambda b: (b, 0, 0)), const((HYENA_ORDER, n, D_HYENA)),
                  const((HYENA_ORDER, 1, D_HYENA)), const((2 * n, L)), const((2 * n, n)), const((L, 2 * n))],
        out_specs=pl.BlockSpec((None, L, D_HYENA), lambda b: (b, 0, 0)),
        out_shape=jax.ShapeDtypeStruct((B, L, D_HYENA), BF16),
        compiler_params=_cparams(("parallel",)),
        name="short_hyena",
    )(zc, taps, bias.reshape(HYENA_ORDER, 1, D_HYENA), _bf16_table(fx), _bf16_table(ft), _bf16_table(gi))


def _split_route(outs, mod):
    x1, h2 = outs[0], outs[1]
    return h2, x1, mod, (outs[2] if len(outs) > 2 else None)


def kernel(x, c, ctx, c_ctx, w_ada, b_ada, g_norm1, g_norm2, w_in, w_out, g_q, g_k, rpb, hy_conv_w, hy_conv_b, filt_w1, filt_b1, filt_w2, filt_b2, filt_w3, filt_b3, filt_freq, filt_w_out, hy_bias, lam_re, lam_im, log_step, b_re, b_im, c_re, c_im, d_skip, w_glu, b_glu, ffn_w1, ffn_w3, ffn_w2, router_w, moe_w1, moe_w3, moe_w2):
    B, L, D = x.shape
    Lc = ctx.shape[1]
    c8 = jnp.zeros((8, D), F32).at[:B].set(c).at[B].set(c_ctx)
    mod_all = adaln_mod(c8, w_ada, b_ada)
    for layer in range(DEPTH):
        last = layer == DEPTH - 1
        m = mod_all[layer].reshape(8, 6, D)
        mod_x = m[:B]
        mod_c = jnp.broadcast_to(m[B], (B, 6, D))
        w_l = w_in[layer]
        wo_l = w_out[layer]
        filt = (filt_w1[layer], filt_b1[layer], filt_w2[layer], filt_b2[layer],
                filt_w3[layer], filt_b3[layer], filt_freq[layer], filt_w_out[layer])
        ssm = (lam_re[layer], lam_im[layer], log_step[layer], b_re[layer], b_im[layer], c_re[layer], c_im[layer])

        q_c, k_c, v_c, hz_c, u_c = in_proj(ctx, mod_c, g_norm1[layer], w_l, g_q[layer], g_k[layer])
        q, k, v, hz, u = in_proj(x, mod_x, g_norm1[layer], w_l, g_q[layer], g_k[layer])
        att = neighbourhood_attention(q, k, v, k_c, v_c, rpb[layer])
        spec = filter_spectrum(hyena_filter_taps(L, *filt))
        hy = _hyena(hz, hy_conv_w[layer], hy_conv_b[layer], spec, hy_bias[layer])
        un, yf, yb = s5_scan(u, u_c, *ssm)

        tail = (d_skip[layer], w_glu[layer], b_glu[layer], wo_l, g_norm2[layer])
        i = layer // 2
        if layer % 2 == 0:
            ffw = (ffn_w1[i], ffn_w3[i], ffn_w2[i])
            mixer = lambda h2, x1, mod, rt: ffn_dense(h2, x1, mod, *ffw)
            rw = None
        else:
            mow = (moe_w1[i], moe_w3[i], moe_w2[i])
            mixer = lambda h2, x1, mod, rt: moe_layer(h2, x1, mod, rt, *mow)
            rw = router_w[i]
        x = mixer(*_split_route(out_proj(x, mod_x, att, hy, yf, yb, un, 0, *tail, router_w=rw), mod_x))

        if not last:
            att_c = context_attention(q_c, k_c, v_c)
            hy_c = short_hyena(hz_c, hy_conv_w[layer], hy_conv_b[layer], hyena_filter_taps(Lc, *filt),
                               hy_bias[layer])
            ctx = mixer(*_split_route(out_proj(ctx, mod_c, att_c, hy_c, yf, yb, un, L, *tail, router_w=rw), mod_c))
    return x
```
